```python
import math
import jax
import jax.numpy as jnp
from jax import lax
import numpy as np

D_MODEL = 1024
BATCH = 2
SEQ = 16384
DEPTH = 2

HEAD_DIM = 64
N_MIX_HEADS = D_MODEL // HEAD_DIM
GLA_HEADS = N_MIX_HEADS // 4
MOBA_HEADS = N_MIX_HEADS // 2
RWKV_HEADS = N_MIX_HEADS - GLA_HEADS - MOBA_HEADS
GLA_W = GLA_HEADS * HEAD_DIM
MOBA_W = MOBA_HEADS * HEAD_DIM
RWKV_W = RWKV_HEADS * HEAD_DIM
D_MIX = GLA_W + MOBA_W + RWKV_W

GLA_LOWRANK = 16
GLA_TAU = 16.0
GLA_CHUNK = 64

MOBA_BLOCK = 256
MOBA_TOPK = 3
Q_CHUNK = 128

RWKV_LORA = 32
RWKV_GN_EPS = 64e-5

N_BUCKETS = 32
MAX_DISTANCE = 4096
LN_EPS = 1e-5

DN_ALPHA = (2.0 * DEPTH) ** 0.25
DN_BETA = (8.0 * DEPTH) ** -0.25

GLA_COLS = 4 * GLA_W + GLA_LOWRANK
MOBA_COLS = 4 * MOBA_W
RWKV_COLS = 4 * RWKV_W + 2 * RWKV_LORA
D_IN = GLA_COLS + MOBA_COLS + RWKV_COLS

kernel_name = "hymba_style_gla_moba_rwkv7_deepnorm"


def _split(t, sizes):
    idx = [int(v) for v in np.cumsum(sizes)[:-1]]
    return jnp.split(t, idx, axis=-1)


def _to_heads(t, n_heads):
    b, s, _ = t.shape
    return t.reshape(b, s, n_heads, HEAD_DIM).transpose(0, 2, 1, 3)


def _from_heads(t):
    b, n, s, d = t.shape
    return t.transpose(0, 2, 1, 3).reshape(b, s, n * d)


def _layer_norm(x, w, b):
    xf = x.astype(jnp.float32)
    mu = jnp.mean(xf, axis=-1, keepdims=True)
    var = jnp.mean(jnp.square(xf - mu), axis=-1, keepdims=True)
    return (xf - mu) * lax.rsqrt(var + LN_EPS) * w + b


def _t5_bucket(rel):
    rel = jnp.maximum(rel, 0)
    max_exact = N_BUCKETS // 2
    rel_f = jnp.maximum(rel, 1).astype(jnp.float32)
    large = max_exact + (jnp.log(rel_f / max_exact) / math.log(MAX_DISTANCE / max_exact)
                         * (N_BUCKETS - max_exact)).astype(jnp.int32)
    large = jnp.minimum(large, N_BUCKETS - 1)
    return jnp.where(rel < max_exact, rel, large)


def _gla_chunked(q, k, v, log_a):
    b, h, s, dk = q.shape
    dv = v.shape[-1]
    n = s // GLA_CHUNK

    def chunks(t):
        return t.reshape(b, h, n, GLA_CHUNK, t.shape[-1]).transpose(2, 0, 1, 3, 4)

    causal = jnp.tril(jnp.ones((GLA_CHUNK, GLA_CHUNK), dtype=bool))[:, :, None]

    def step(state, inp):
        qc, kc, vc, gc = inp
        cum = jnp.cumsum(gc, axis=-2)
        o_inter = jnp.einsum('bhtk,bhkv->bhtv', qc * jnp.exp(cum), state)
        diff = cum[:, :, :, None, :] - cum[:, :, None, :, :]
        decay = jnp.exp(jnp.where(causal, diff, -jnp.inf))
        scores = jnp.einsum('bhtk,bhsk,bhtsk->bhts', qc, kc, decay)
        o_intra = jnp.einsum('bhts,bhsv->bhtv', scores, vc)
        last = cum[:, :, -1:, :]
        state = state * jnp.exp(last[:, :, 0, :])[..., None] + jnp.einsum(
            'bhsk,bhsv->bhkv', kc * jnp.exp(last - cum), vc)
        return state, o_inter + o_intra

    s0 = jnp.zeros((b, h, dk, dv), jnp.float32)
    _, out = lax.scan(step, s0, (chunks(q), chunks(k), chunks(v), chunks(log_a)))
    return out.transpose(1, 2, 0, 3, 4).reshape(b, h, s, dv)


def _gla_branch(hg, a_up, a_bias, norm_w):
    hg = hg.astype(jnp.float32)
    q, k, v, g, a_dn = _split(hg, [GLA_W] * 4 + [GLA_LOWRANK])
    log_a = jax.nn.log_sigmoid(a_dn @ a_up.astype(jnp.float32) + a_bias) / GLA_TAU
    o = _gla_chunked(_to_heads(q, GLA_HEADS) * HEAD_DIM ** -0.5, _to_heads(k, GLA_HEADS),
                     _to_heads(v, GLA_HEADS), _to_heads(log_a, GLA_HEADS))
    o = o * lax.rsqrt(jnp.mean(jnp.square(o), axis=-1, keepdims=True) + LN_EPS) * norm_w
    return _from_heads(o) * jax.nn.silu(g)


def _moba_attention(q, k, v, rel_bias):
    b, h, s, dh = q.shape
    nb = -(-s // MOBA_BLOCK)
    pad = nb * MOBA_BLOCK - s
    kp = jnp.pad(k, ((0, 0), (0, 0), (0, pad), (0, 0))).reshape(b, h, nb, MOBA_BLOCK, dh)
    vp = jnp.pad(v, ((0, 0), (0, 0), (0, pad), (0, 0))).reshape(b, h, nb, MOBA_BLOCK, dh)
    kbar = jnp.mean(kp, axis=3)
    top = min(MOBA_TOPK, nb)
    nq = s // Q_CHUNK
    q_chunks = q.reshape(b, h, nq, Q_CHUNK, dh).transpose(2, 0, 1, 3, 4)
    bias_tab = rel_bias.T.astype(jnp.float32)
    b_idx = jnp.arange(b)[:, None, None, None]
    h_idx = jnp.arange(h)[None, :, None, None]
    blk_ar = jnp.arange(MOBA_BLOCK)
    blocks = jnp.arange(nb)
    scale = dh ** -0.5

    def one_chunk(args):
        qi, ci = args
        q_pos = ci * Q_CHUNK + jnp.arange(Q_CHUNK)
        own = (ci * Q_CHUNK) // MOBA_BLOCK
        gate = jnp.einsum('bhqd,bhnd->bhqn', qi, kbar)
        gate = jnp.where(blocks < own, gate, -jnp.inf)
        _, idx = lax.top_k(gate, top)
        valid = idx < own
        k_sel = kp[b_idx, h_idx, idx]
        v_sel = vp[b_idx, h_idx, idx]
        kpos_sel = idx[..., None] * MOBA_BLOCK + blk_ar
        bias_sel = bias_tab[h_idx[..., None], _t5_bucket(q_pos[:, None, None] - kpos_sel)]
        s_sel = jnp.einsum('bhqd,bhqjkd->bhqjk', qi, k_sel) * scale + bias_sel
        s_sel = jnp.where(valid[..., None], s_sel, -jnp.inf)
        k_own = lax.dynamic_index_in_dim(kp, own, axis=2, keepdims=False)
        v_own = lax.dynamic_index_in_dim(vp, own, axis=2, keepdims=False)
        rel_own = q_pos[:, None] - (own * MOBA_BLOCK + blk_ar)[None, :]
        s_own = jnp.einsum('bhqd,bhkd->bhqk', qi, k_own) * scale + bias_tab[:, _t5_bucket(rel_own)]
        s_own = jnp.where(rel_own >= 0, s_own, -jnp.inf)
        logits = jnp.concatenate([s_sel.reshape(b, h, Q_CHUNK, top * MOBA_BLOCK), s_own], axis=-1)
        p = jax.nn.softmax(logits, axis=-1)
        p_sel = p[..., :top * MOBA_BLOCK].reshape(b, h, Q_CHUNK, top, MOBA_BLOCK)
        p_own = p[..., top * MOBA_BLOCK:]
        return (jnp.einsum('bhqjk,bhqjkd->bhqd', p_sel, v_sel)
                + jnp.einsum('bhqk,bhkd->bhqd', p_own, v_own))

    out = lax.map(one_chunk, (q_chunks, jnp.arange(nq)))
    return out.transpose(1, 2, 0, 3, 4).reshape(b, h, s, dh)


def _moba_branch(hm, rel_bias):
    hm = hm.astype(jnp.float32)
    q, k, v, g = _split(hm, [MOBA_W] * 4)
    o = _moba_attention(_to_heads(q, MOBA_HEADS), _to_heads(k, MOBA_HEADS),
                        _to_heads(v, MOBA_HEADS), rel_bias)
    return _from_heads(o) * jax.nn.silu(g)


def _rwkv7_scan(r, w, k, v, aa, bb):
    b, s, h, d = r.shape

    def step(state, inp):
        rt, wt, kt, vt, at, bt = inp
        sa = jnp.einsum('bhvk,bhk->bhv', state, at)
        state = state * wt[:, :, None, :] + sa[..., None] * bt[:, :, None, :] + vt[..., None] * kt[:, :, None, :]
        return state, jnp.einsum('bhvk,bhk->bhv', state, rt)

    xs = tuple(jnp.moveaxis(t, 1, 0) for t in (r, w, k, v, aa, bb))
    _, y = lax.scan(step, jnp.zeros((b, h, d, d), jnp.float32), xs)
    return jnp.moveaxis(y, 0, 1)


def _rwkv_branch(hr, mu, w_up, w0, a_up, a0, k_k, k_a, r_k, gn_w, gn_b):
    b, s, _ = hr.shape
    f32 = jnp.float32
    p = hr.astype(f32)
    p_prev = jnp.pad(p, ((0, 0), (1, 0), (0, 0)))[:, :-1]
    p = p + (p_prev - p) * mu
    r, k, v, g, w_dn, a_dn = _split(p, [RWKV_W] * 4 + [RWKV_LORA] * 2)
    d = w0 + jnp.tanh(w_dn) @ w_up.astype(f32)
    decay = jnp.exp(-jnp.exp(-jax.nn.softplus(-d) - 0.5))
    a = jax.nn.sigmoid(a0 + a_dn @ a_up.astype(f32))
    hs = lambda t: t.reshape(b, s, RWKV_HEADS, HEAD_DIM)
    hv = lambda t: t.reshape(RWKV_HEADS, HEAD_DIM)
    kk = hs(k * k_k)
    kk = kk / jnp.maximum(jnp.sqrt(jnp.sum(jnp.square(kk), axis=-1, keepdims=True)), 1e-12)
    k = k * (1.0 + (a - 1.0) * k_a)
    r_h, w_h, k_h, v_h, a_h = hs(r), hs(decay), hs(k), hs(v), hs(a)
    y = _rwkv7_scan(r_h, w_h, k_h, v_h, -kk, kk * a_h)
    ym = jnp.mean(y, axis=-1, keepdims=True)
    yv = jnp.mean(jnp.square(y - ym), axis=-1, keepdims=True)
    y = (y - ym) * lax.rsqrt(yv + RWKV_GN_EPS) * hv(gn_w) + hv(gn_b)
    y = y + jnp.sum(r_h * k_h * hv(r_k), axis=-1, keepdims=True) * v_h
    return y.reshape(b, s, RWKV_W) * jax.nn.silu(g)


def _layer(x, w_in, w_out, gla_a_up, gla_a_bias, gla_norm_w, rel_bias, mu, w_up, w0,
           a_up, a0, k_k, k_a, r_k, gn_w, gn_b, ln_w, ln_b):
    h = jnp.einsum('btd,de->bte', x, w_in)
    h_gla, h_moba, h_rwkv = _split(h, [GLA_COLS, MOBA_COLS, RWKV_COLS])
    o = jnp.concatenate([
        _gla_branch(h_gla, gla_a_up, gla_a_bias, gla_norm_w),
        _moba_branch(h_moba, rel_bias),
        _rwkv_branch(h_rwkv, mu, w_up, w0, a_up, a0, k_k, k_a, r_k, gn_w, gn_b),
    ], axis=-1)
    y = jnp.einsum('bte,ed->btd', o, w_out.astype(jnp.float32))
    return _layer_norm(DN_ALPHA * x.astype(jnp.float32) + y, ln_w, ln_b).astype(x.dtype)


def setup_inputs(seed: int = 0) -> dict:
    key = jax.random.key(seed)
    ks = jax.random.split(key, 20)
    L = DEPTH
    f32 = jnp.float32
    nrm = lambda k_, shp: jax.random.normal(k_, shp, f32)
    col_scale = jnp.ones((D_IN,), f32)
    for start, width in ((2 * GLA_W, GLA_W), (GLA_COLS + 2 * MOBA_W, MOBA_W),
                         (GLA_COLS + MOBA_COLS + 2 * RWKV_W, RWKV_W)):
        col_scale = col_scale.at[start:start + width].set(DN_BETA)
    return {
        "x": nrm(ks[0], (BATCH, SEQ, D_MODEL)),
        "w_in": nrm(ks[1], (L, D_MODEL, D_IN)) * D_MODEL ** -0.5 * col_scale,
        "w_out": nrm(ks[2], (L, D_MIX, D_MODEL)) * D_MIX ** -0.5 * DN_BETA,
        "gla_a_up": nrm(ks[3], (L, GLA_LOWRANK, GLA_W)) * GLA_LOWRANK ** -0.5,
        "gla_a_bias": nrm(ks[4], (L, GLA_W)) * 0.5,
        "gla_norm_w": 1.0 + 0.05 * nrm(ks[5], (L, HEAD_DIM)),
        "moba_rel_bias": nrm(ks[6], (N_BUCKETS, MOBA_HEADS)) * 0.5,
        "rwkv_mu": jax.random.uniform(ks[7], (L, RWKV_COLS), f32),
        "rwkv_w_up": nrm(ks[8], (L, RWKV_LORA, RWKV_W)) * 0.1,
        "rwkv_w0": jax.random.uniform(ks[9], (L, RWKV_W), f32, -6.0, 1.0),
        "rwkv_a_up": nrm(ks[10], (L, RWKV_LORA, RWKV_W)) * 0.5 * RWKV_LORA ** -0.5,
        "rwkv_a0": nrm(ks[11], (L, RWKV_W)) * 0.5,
        "rwkv_k_k": 0.85 + 0.05 * nrm(ks[12], (L, RWKV_W)),
        "rwkv_k_a": 1.0 + 0.05 * nrm(ks[13], (L, RWKV_W)),
        "rwkv_r_k": nrm(ks[14], (L, RWKV_W)) * 0.1,
        "rwkv_gn_w": 1.0 + 0.05 * nrm(ks[15], (L, RWKV_W)),
        "rwkv_gn_b": 0.02 * nrm(ks[16], (L, RWKV_W)),
        "ln_w": 1.0 + 0.05 * nrm(ks[17], (L, D_MODEL)),
        "ln_b": 0.02 * nrm(ks[18], (L, D_MODEL)),
    }


def reference(x, w_in, w_out, gla_a_up, gla_a_bias, gla_norm_w, moba_rel_bias, rwkv_mu,
              rwkv_w_up, rwkv_w0, rwkv_a_up, rwkv_a0, rwkv_k_k, rwkv_k_a, rwkv_r_k,
              rwkv_gn_w, rwkv_gn_b, ln_w, ln_b):
    for l in range(DEPTH):
        x = _layer(x, w_in[l], w_out[l], gla_a_up[l], gla_a_bias[l], gla_norm_w[l], moba_rel_bias,
                   rwkv_mu[l], rwkv_w_up[l], rwkv_w0[l], rwkv_a_up[l], rwkv_a0[l], rwkv_k_k[l],
                   rwkv_k_a[l], rwkv_r_k[l], rwkv_gn_w[l], rwkv_gn_b[l], ln_w[l], ln_b[l])
    return x
```

```python
import functools
import math

import jax
import jax.numpy as jnp
import numpy as np
from jax import lax
from jax.experimental import pallas as pl
from jax.experimental.pallas import tpu as pltpu

HEAD_DIM = 64
GLA_HEADS = 4
MOBA_HEADS = 8
RWKV_HEADS = 4
GLA_W = GLA_HEADS * HEAD_DIM
MOBA_W = MOBA_HEADS * HEAD_DIM
RWKV_W = RWKV_HEADS * HEAD_DIM
GLA_LOWRANK = 16
GLA_TAU = 16.0
RWKV_LORA = 32
RWKV_GN_EPS = 64e-5
MOBA_BLOCK = 256
MOBA_TOPK = 3
N_BUCKETS = 32
MAX_DISTANCE = 4096
LN_EPS = 1e-5
GLA_COLS = 4 * GLA_W + GLA_LOWRANK
MOBA_COLS = 4 * MOBA_W
RWKV_COLS = 4 * RWKV_W + 2 * RWKV_LORA
D_IN = GLA_COLS + MOBA_COLS + RWKV_COLS

LANE = 128
D_IN_PAD = -(-D_IN // LANE) * LANE
CHUNK = 64
SUB = 16
TIME_BLOCK = 512
ROW_BLOCK = 512
N_BIAS_TILES = (MAX_DISTANCE + MOBA_BLOCK - 1) // MOBA_BLOCK + 1
VMEM_LIMIT = 56 * 1024 * 1024

F32 = jnp.float32
BF16 = jnp.bfloat16
NEG_INF = float("-inf")


def _parts(a, n):
    out, r = [], a
    for i in range(n):
        p = r.astype(BF16)
        out.append(p)
        if i + 1 < n:
            r = r - p.astype(F32)
    return out


def _mm(a, b, dims=(((1,), (0,)), ((), ())), na=1, nb=1):
    ap = [a] if a.dtype == BF16 else _parts(a, na)
    bp = [b] if b.dtype == BF16 else _parts(b, nb)
    n = max(len(ap), len(bp))
    acc = None
    for i, x in enumerate(ap):
        for j, y in enumerate(bp):
            if i + j < n:
                t = lax.dot_general(x, y, dims, preferred_element_type=F32)
                acc = t if acc is None else acc + t
    return acc


_NT = (((1,), (1,)), ((), ()))
_TN = (((0,), (0,)), ((), ()))


def _softplus(x):
    return jnp.maximum(x, 0.0) + jnp.log(1.0 + jnp.exp(-jnp.abs(x)))


def _sigmoid(x):
    return 1.0 / (1.0 + jnp.exp(-x))


def _iota2(shape, dim):
    return lax.broadcasted_iota(jnp.int32, shape, dim)


def _proj_in_kernel(x_ref, w_ref, o_ref):
    o_ref[...] = jnp.dot(x_ref[...].astype(BF16), w_ref[...], preferred_element_type=F32)


def _proj_in(x2, w):
    m, d = x2.shape
    n = w.shape[1]
    n_split = 3
    tn = n // n_split
    return pl.pallas_call(
        _proj_in_kernel,
        grid=(n_split, m // ROW_BLOCK),
        in_specs=[pl.BlockSpec((ROW_BLOCK, d), lambda j, i: (i, 0)),
                  pl.BlockSpec((d, tn), lambda j, i: (0, j))],
        out_specs=pl.BlockSpec((ROW_BLOCK, tn), lambda j, i: (i, j)),
        out_shape=jax.ShapeDtypeStruct((m, n), F32),
        compiler_params=pltpu.CompilerParams(
            dimension_semantics=("parallel", "parallel"), vmem_limit_bytes=VMEM_LIMIT),
    )(x2, w)


def _gla_kernel(q_ref, k_ref, v_ref, g_ref, adn_ref, aup_ref, ab_ref, nw_ref, o_ref, st_ref):
    @pl.when(pl.program_id(2) == 0)
    def _():
        st_ref[...] = jnp.zeros_like(st_ref)

    z = _mm(adn_ref[0], aup_ref[0], na=2, nb=2) + ab_ref[0]
    la_all = -_softplus(-z) * (1.0 / GLA_TAU)
    row = _iota2((CHUNK, CHUNK), 0)
    col = _iota2((CHUNK, CHUNK), 1)
    tri = (row >= col)
    tri_b = tri.astype(BF16)
    anchor_b = (col <= (row // SUB) * SUB + (SUB - 1)).astype(BF16)
    nw = nw_ref[...]
    for c in range(TIME_BLOCK // CHUNK):
        sl = pl.ds(c * CHUNK, CHUNK)
        q = q_ref[0, 0, sl, :] * (HEAD_DIM ** -0.5)
        k = k_ref[0, 0, sl, :]
        v = v_ref[0, 0, sl, :]
        g = g_ref[0, 0, sl, :]
        la = la_all[c * CHUNK:(c + 1) * CHUNK]
        cum = _mm(tri_b, la, nb=3)
        k_anchor = _mm(anchor_b, la, nb=3)
        k_t = k * jnp.exp(k_anchor - cum)
        scores = jnp.zeros((CHUNK, CHUNK), F32)
        for j in range(CHUNK // SUB):
            a_j = cum[j * SUB + SUB - 1:j * SUB + SUB, :]
            q_j = q * jnp.exp(jnp.where(row >= j * SUB, cum - a_j, 0.0))
            in_grp = (row >= j * SUB) & (row < (j + 1) * SUB)
            k_j = jnp.where(in_grp, k_t, 0.0)
            scores = scores + _mm(q_j, k_j, _NT)
        scores = jnp.where(tri, scores, 0.0)
        st = st_ref[...]
        o = _mm(scores, v) + _mm(q * jnp.exp(cum), st, _NT)
        last = cum[CHUNK - 1:CHUNK, :]
        k_hat = k * jnp.exp(last - cum)
        st_ref[...] = st * jnp.exp(last) + _mm(v, k_hat, _TN)
        o = o * lax.rsqrt(jnp.mean(o * o, axis=-1, keepdims=True) + LN_EPS) * nw
        o_ref[0, 0, sl, :] = o * (g * _sigmoid(g))


def _gla(q, k, v, g, a_dn, a_up, a_bias, norm_w):
    b, h, t, d = q.shape
    hs = pl.BlockSpec((1, 1, TIME_BLOCK, d), lambda i, j, s: (i, j, s, 0))
    return pl.pallas_call(
        _gla_kernel,
        grid=(b, h, t // TIME_BLOCK),
        in_specs=[hs, hs, hs, hs,
                  pl.BlockSpec((1, TIME_BLOCK, GLA_LOWRANK), lambda i, j, s: (i, s, 0)),
                  pl.BlockSpec((1, GLA_LOWRANK, d), lambda i, j, s: (j, 0, 0)),
                  pl.BlockSpec((1, 1, d), lambda i, j, s: (j, 0, 0)),
                  pl.BlockSpec((1, d), lambda i, j, s: (0, 0))],
        out_specs=hs,
        out_shape=jax.ShapeDtypeStruct((b, h, t, d), F32),
        scratch_shapes=[pltpu.VMEM((d, d), F32)],
        compiler_params=pltpu.CompilerParams(
            dimension_semantics=("parallel", "parallel", "arbitrary"), vmem_limit_bytes=VMEM_LIMIT),
    )(q, k, v, g, a_dn, a_up, a_bias, norm_w)


def _kbar_kernel(k_ref, o_ref):
    nb = o_ref.shape[2]
    for j in range(nb):
        blk = k_ref[0, 0, j * MOBA_BLOCK:(j + 1) * MOBA_BLOCK, :]
        o_ref[0, 0, j:j + 1, :] = jnp.sum(blk, axis=0, keepdims=True) * (1.0 / MOBA_BLOCK)


def _moba_kbar(k):
    b, h, t, d = k.shape
    nb = t // MOBA_BLOCK
    return pl.pallas_call(
        _kbar_kernel,
        grid=(b, h),
        in_specs=[pl.BlockSpec((1, 1, t, d), lambda i, j: (i, j, 0, 0))],
        out_specs=pl.BlockSpec((1, 1, nb, d), lambda i, j: (i, j, 0, 0)),
        out_shape=jax.ShapeDtypeStruct((b, h, nb, d), F32),
        compiler_params=pltpu.CompilerParams(
            dimension_semantics=("parallel", "parallel"), vmem_limit_bytes=VMEM_LIMIT),
    )(k)


def _select_kernel(q_ref, kbar_ref, o_ref):
    own = pl.program_id(2)
    gate = _mm(q_ref[0, 0], kbar_ref[0, 0], _NT, na=2, nb=2)
    lane = _iota2(gate.shape, 1)
    nb = gate.shape[1]
    gate = jnp.where(lane < own, gate, NEG_INF)
    chosen = lane < 0
    for _ in range(min(MOBA_TOPK, nb)):
        best = jnp.max(gate, axis=1, keepdims=True)
        first = jnp.min(jnp.where(gate == best, lane, nb), axis=1, keepdims=True)
        hit = lane == first
        chosen = chosen | hit
        gate = jnp.where(hit, NEG_INF, gate)
    o_ref[0, 0] = (chosen & (lane < own)).astype(BF16)


def _moba_select(q, kbar):
    b, h, t, d = q.shape
    nb = kbar.shape[2]
    return pl.pallas_call(
        _select_kernel,
        grid=(b, h, t // MOBA_BLOCK),
        in_specs=[pl.BlockSpec((1, 1, MOBA_BLOCK, d), lambda i, j, s: (i, j, s, 0)),
                  pl.BlockSpec((1, 1, nb, d), lambda i, j, s: (i, j, 0, 0))],
        out_specs=pl.BlockSpec((1, 1, MOBA_BLOCK, nb), lambda i, j, s: (i, j, s, 0)),
        out_shape=jax.ShapeDtypeStruct((b, h, t, nb), BF16),
        compiler_params=pltpu.CompilerParams(
            dimension_semantics=("parallel", "parallel", "parallel"), vmem_limit_bytes=VMEM_LIMIT),
    )(q, kbar)


def _moba_attn_kernel(far_ref, q_ref, k_ref, v_ref, g_ref, sel_ref, bias_ref, o_ref):
    hd = pl.program_id(1)
    own = pl.program_id(2)
    nb = sel_ref.shape[3]
    q = (q_ref[0, 0] * (HEAD_DIM ** -0.5)).astype(BF16)
    sel = sel_ref[0, 0]
    far = far_ref[hd]
    row = _iota2((MOBA_BLOCK, MOBA_BLOCK), 0)
    col = _iota2((MOBA_BLOCK, MOBA_BLOCK), 1)
    blk_id = _iota2((nb, MOBA_BLOCK), 0)

    def kv(j):
        sl = pl.ds(pl.multiple_of(j * MOBA_BLOCK, MOBA_BLOCK), MOBA_BLOCK)
        return k_ref[0, 0, sl, :], v_ref[0, 0, sl, :]

    k0, v0 = kv(own)
    s = _mm(q, k0, _NT) + bias_ref[0, 0]
    s = jnp.where(row >= col, s, NEG_INF)
    m = jnp.max(s, axis=1, keepdims=True)
    p = jnp.exp(s - m)
    l = jnp.sum(p, axis=1, keepdims=True)
    acc = _mm(p.astype(BF16), v0)

    def body(j, carry):
        m, l, acc = carry
        kj, vj = kv(j)
        dist = own - j
        bias = jnp.where(dist < N_BIAS_TILES, bias_ref[0, jnp.minimum(dist, N_BIAS_TILES - 1)], far)
        picked = _mm(sel, (blk_id == j).astype(BF16)) > 0.5
        s = jnp.where(picked, _mm(q, kj, _NT) + bias, NEG_INF)
        m_new = jnp.maximum(m, jnp.max(s, axis=1, keepdims=True))
        alpha = jnp.exp(m - m_new)
        p = jnp.exp(s - m_new)
        l = alpha * l + jnp.sum(p, axis=1, keepdims=True)
        acc = alpha * acc + _mm(p.astype(BF16), vj)
        return m_new, l, acc

    m, l, acc = lax.fori_loop(0, own, body, (m, l, acc))
    g = g_ref[0, 0]
    o_ref[0, 0] = acc / l * (g * _sigmoid(g))


def _moba_attn(far, q, k, v, g, sel, bias):
    b, h, t, d = q.shape
    nb = t // MOBA_BLOCK
    qs = pl.BlockSpec((1, 1, MOBA_BLOCK, d), lambda i, j, s: (i, j, s, 0))
    full = pl.BlockSpec((1, 1, t, d), lambda i, j, s: (i, j, 0, 0))
    return pl.pallas_call(
        _moba_attn_kernel,
        grid=(b, h, nb),
        in_specs=[pl.BlockSpec(memory_space=pltpu.SMEM), qs, full, full, qs,
                  pl.BlockSpec((1, 1, MOBA_BLOCK, nb), lambda i, j, s: (i, j, s, 0)),
                  pl.BlockSpec((1, N_BIAS_TILES, MOBA_BLOCK, MOBA_BLOCK), lambda i, j, s: (j, 0, 0, 0))],
        out_specs=qs,
        out_shape=jax.ShapeDtypeStruct((b, h, t, d), F32),
        compiler_params=pltpu.CompilerParams(
            dimension_semantics=("parallel", "parallel", "arbitrary"), vmem_limit_bytes=VMEM_LIMIT),
    )(far, q, k, v, g, sel, bias)


def _t5_bucket(rel):
    rel = jnp.maximum(rel, 0)
    max_exact = N_BUCKETS // 2
    rel_f = jnp.maximum(rel, 1).astype(F32)
    large = max_exact + (jnp.log(rel_f / max_exact) / math.log(MAX_DISTANCE / max_exact)
                         * (N_BUCKETS - max_exact)).astype(jnp.int32)
    large = jnp.minimum(large, N_BUCKETS - 1)
    return jnp.where(rel < max_exact, rel, large)


def _bias_tiles(rel_bias):
    d = jnp.arange(N_BIAS_TILES)[:, None, None] * MOBA_BLOCK
    rel = d + jnp.arange(MOBA_BLOCK)[None, :, None] - jnp.arange(MOBA_BLOCK)[None, None, :]
    bucket = _t5_bucket(rel)
    tab = rel_bias.T.astype(F32)
    out = jnp.zeros((tab.shape[0],) + bucket.shape, F32)
    for bk in range(N_BUCKETS):
        out = jnp.where(bucket[None] == bk, tab[:, bk][:, None, None, None], out)
    return out


def _rwkv_prep_kernel(h_ref, hp_ref, mu_ref, wup_ref, w0_ref, aup_ref, a0_ref, kk_ref, ka_ref, rk_ref,
                      r_ref, lw_ref, k_ref, v_ref, a_ref, b_ref, bonus_ref, sg_ref):
    p = h_ref[0]
    p = p + (hp_ref[0] - p) * mu_ref[...]
    w = RWKV_W
    r, k, v, g = p[:, 0:w], p[:, w:2 * w], p[:, 2 * w:3 * w], p[:, 3 * w:4 * w]
    w_dn = p[:, 4 * w:4 * w + RWKV_LORA]
    a_dn = p[:, 4 * w + RWKV_LORA:4 * w + 2 * RWKV_LORA]
    d = w0_ref[...] + _mm(jnp.tanh(w_dn), wup_ref[...], na=2, nb=2)
    lw_ref[0] = -jnp.exp(-_softplus(-d) - 0.5)
    a = _sigmoid(a0_ref[...] + _mm(a_dn, aup_ref[...], na=2, nb=2))
    same_head = (_iota2((w, w), 0) // HEAD_DIM == _iota2((w, w), 1) // HEAD_DIM).astype(BF16)
    kk = k * kk_ref[...]
    kk = kk / jnp.maximum(jnp.sqrt(_mm(kk * kk, same_head, na=3)), 1e-12)
    k = k * (1.0 + (a - 1.0) * ka_ref[...])
    r_ref[0] = r
    k_ref[0] = k
    v_ref[0] = v
    a_ref[0] = -kk
    b_ref[0] = kk * a
    bonus_ref[0] = _mm(r * k * rk_ref[...], same_head, na=3) * v
    sg_ref[0] = g * _sigmoid(g)


def _rwkv_prep(hr, hr_prev, mu, w_up, w0, a_up, a0, k_k, k_a, r_k):
    b, t, c = hr.shape
    w = RWKV_W
    hs = pl.BlockSpec((1, ROW_BLOCK, c), lambda i, s: (i, s, 0))
    os_ = pl.BlockSpec((1, ROW_BLOCK, w), lambda i, s: (i, s, 0))
    vec = lambda n: pl.BlockSpec((1, n), lambda i, s: (0, 0))
    lora = pl.BlockSpec((RWKV_LORA, w), lambda i, s: (0, 0))
    return pl.pallas_call(
        _rwkv_prep_kernel,
        grid=(b, t // ROW_BLOCK),
        in_specs=[hs, hs, vec(c), lora, vec(w), lora, vec(w), vec(w), vec(w), vec(w)],
        out_specs=[os_] * 8,
        out_shape=[jax.ShapeDtypeStruct((b, t, w), F32)] * 8,
        compiler_params=pltpu.CompilerParams(
            dimension_semantics=("parallel", "parallel"), vmem_limit_bytes=VMEM_LIMIT),
    )(hr, hr_prev, mu, w_up, w0, a_up, a0, k_k, k_a, r_k)


def _rwkv_scan_kernel(r_ref, lw_ref, k_ref, v_ref, a_ref, b_ref, bonus_ref, sg_ref, gw_ref, gb_ref,
                      o_ref, st_ref):
    @pl.when(pl.program_id(2) == 0)
    def _():
        st_ref[...] = jnp.zeros_like(st_ref)

    row = _iota2((CHUNK, CHUNK), 0)
    col = _iota2((CHUNK, CHUNK), 1)
    tri_b = (row >= col).astype(BF16)
    eye = row == col
    hp = dict(na=2, nb=2)
    gw = gw_ref[0]
    gb = gb_ref[0]
    for c in range(TIME_BLOCK // CHUNK):
        sl = pl.ds(c * CHUNK, CHUNK)
        r = r_ref[0, 0, sl, :]
        lw = lw_ref[0, 0, sl, :]
        k = k_ref[0, 0, sl, :]
        v = v_ref[0, 0, sl, :]
        a = a_ref[0, 0, sl, :]
        b = b_ref[0, 0, sl, :]
        cum = _mm(tri_b, lw, nb=3)
        last = cum[CHUNK - 1:CHUNK, :]
        dec = jnp.exp(cum)
        inv = jnp.exp(-cum)
        to_end = jnp.exp(last - cum)
        r_t = r * dec
        a_t = a * jnp.exp(cum - lw)
        k_t = k * inv
        b_t = b * inv
        a_ab = jnp.where(row > col, _mm(a_t, b_t, _NT, **hp), 0.0)
        a_ak = jnp.where(row > col, _mm(a_t, k_t, _NT, **hp), 0.0)
        a_rb = jnp.where(row >= col, _mm(r_t, b_t, _NT, **hp), 0.0)
        a_rk = jnp.where(row >= col, _mm(r_t, k_t, _NT, **hp), 0.0)
        inv_t = jnp.where(eye, 1.0, a_ab)
        power = a_ab
        for _ in range(int(math.log2(CHUNK)) - 1):
            power = _mm(power, power, **hp)
            inv_t = inv_t + _mm(inv_t, power, **hp)
        w_mat = _mm(inv_t, a_t, **hp)
        u0 = _mm(inv_t, _mm(a_ak, v, **hp), **hp)
        y_w = r_t + _mm(a_rb, w_mat, **hp)
        y0 = _mm(a_rb, u0, **hp) + _mm(a_rk, v, **hp)
        b_end = b * to_end
        k_end = k * to_end
        m_mat = jnp.where(eye, jnp.exp(last), 0.0) + _mm(b_end, w_mat, _TN, **hp)
        n_mat = _mm(b_end, u0, _TN, **hp) + _mm(k_end, v, _TN, **hp)
        st = st_ref[...]
        y = _mm(y_w, st, **hp) + y0
        st_ref[...] = _mm(m_mat, st, **hp) + n_mat
        mean = jnp.mean(y, axis=-1, keepdims=True)
        var = jnp.mean(jnp.square(y - mean), axis=-1, keepdims=True)
        y = (y - mean) * lax.rsqrt(var + RWKV_GN_EPS) * gw + gb
        o_ref[0, 0, sl, :] = (y + bonus_ref[0, 0, sl, :]) * sg_ref[0, 0, sl, :]


def _rwkv_scan(r, lw, k, v, a, b, bonus, sg, gn_w, gn_b):
    bsz, h, t, d = r.shape
    hs = pl.BlockSpec((1, 1, TIME_BLOCK, d), lambda i, j, s: (i, j, s, 0))
    ps = pl.BlockSpec((1, 1, d), lambda i, j, s: (j, 0, 0))
    return pl.pallas_call(
        _rwkv_scan_kernel,
        grid=(bsz, h, t // TIME_BLOCK),
        in_specs=[hs] * 8 + [ps, ps],
        out_specs=hs,
        out_shape=jax.ShapeDtypeStruct((bsz, h, t, d), F32),
        scratch_shapes=[pltpu.VMEM((d, d), F32)],
        compiler_params=pltpu.CompilerParams(
            dimension_semantics=("parallel", "parallel", "arbitrary"), vmem_limit_bytes=VMEM_LIMIT),
    )(r, lw, k, v, a, b, bonus, sg, gn_w, gn_b)


def _proj_out_kernel(alpha, o_ref, w_ref, x_ref, lw_ref, lb_ref, y_ref):
    y = jnp.dot(o_ref[...].astype(BF16), w_ref[...], preferred_element_type=F32)
    z = alpha * x_ref[...] + y
    mu = jnp.mean(z, axis=-1, keepdims=True)
    var = jnp.mean(jnp.square(z - mu), axis=-1, keepdims=True)
    y_ref[...] = (z - mu) * lax.rsqrt(var + LN_EPS) * lw_ref[...] + lb_ref[...]


def _proj_out(o2, w, x2, ln_w, ln_b, alpha):
    m, d = x2.shape
    rs = pl.BlockSpec((ROW_BLOCK, d), lambda i: (i, 0))
    vs = pl.BlockSpec((1, d), lambda i: (0, 0))
    return pl.pallas_call(
        functools.partial(_proj_out_kernel, alpha),
        grid=(m // ROW_BLOCK,),
        in_specs=[pl.BlockSpec((ROW_BLOCK, o2.shape[1]), lambda i: (i, 0)),
                  pl.BlockSpec(w.shape, lambda i: (0, 0)), rs, vs, vs],
        out_specs=rs,
        out_shape=jax.ShapeDtypeStruct((m, d), F32),
        compiler_params=pltpu.CompilerParams(
            dimension_semantics=("parallel",), vmem_limit_bytes=VMEM_LIMIT),
    )(o2, w, x2, ln_w, ln_b)


def _to_heads(t, n):
    b, s, _ = t.shape
    return t.reshape(b, s, n, HEAD_DIM).transpose(0, 2, 1, 3)


def _from_heads(t):
    b, n, s, d = t.shape
    return t.transpose(0, 2, 1, 3).reshape(b, s, n * d)


def _gla_branch(hg, a_up, a_bias, norm_w):
    q, k, v, g = (_to_heads(hg[..., i * GLA_W:(i + 1) * GLA_W], GLA_HEADS) for i in range(4))
    a_dn = hg[..., 4 * GLA_W:]
    a_up_h = a_up.reshape(GLA_LOWRANK, GLA_HEADS, HEAD_DIM).transpose(1, 0, 2)
    a_bias_h = a_bias.reshape(GLA_HEADS, 1, HEAD_DIM)
    return _from_heads(_gla(q, k, v, g, a_dn, a_up_h, a_bias_h, norm_w.reshape(1, HEAD_DIM)))


def _moba_branch(hm, far, bias):
    q, k, v, g = (_to_heads(hm[..., i * MOBA_W:(i + 1) * MOBA_W], MOBA_HEADS) for i in range(4))
    sel = _moba_select(q, _moba_kbar(k))
    return _from_heads(_moba_attn(far, q, k.astype(BF16), v.astype(BF16), g, sel, bias))


def _rwkv_branch(hr, mu, w_up, w0, a_up, a0, k_k, k_a, r_k, gn_w, gn_b):
    hr_prev = jnp.pad(hr, ((0, 0), (1, 0), (0, 0)))[:, :-1]
    row = lambda p: p.reshape(1, -1)
    outs = _rwkv_prep(hr, hr_prev, row(mu), w_up, row(w0), a_up, row(a0), row(k_k), row(k_a), row(r_k))
    heads = [_to_heads(o, RWKV_HEADS) for o in outs]
    per_head = lambda p: p.reshape(RWKV_HEADS, 1, HEAD_DIM)
    return _from_heads(_rwkv_scan(*heads, per_head(gn_w), per_head(gn_b)))


def kernel(x, w_in, w_out, gla_a_up, gla_a_bias, gla_norm_w, moba_rel_bias, rwkv_mu, rwkv_w_up, rwkv_w0,
           rwkv_a_up, rwkv_a0, rwkv_k_k, rwkv_k_a, rwkv_r_k, rwkv_gn_w, rwkv_gn_b, ln_w, ln_b):
    bsz, t, d = x.shape
    depth = w_in.shape[0]
    alpha = (2.0 * depth) ** 0.25
    bias = _bias_tiles(moba_rel_bias)
    far = moba_rel_bias[N_BUCKETS - 1].astype(F32)
    x2 = x.reshape(bsz * t, d)
    for l in range(depth):
        w = jnp.pad(w_in[l], ((0, 0), (0, D_IN_PAD - D_IN))).astype(BF16)
        h = _proj_in(x2, w).reshape(bsz, t, D_IN_PAD)
        o = jnp.concatenate([
            _gla_branch(h[..., :GLA_COLS], gla_a_up[l], gla_a_bias[l], gla_norm_w[l]),
            _moba_branch(h[..., GLA_COLS:GLA_COLS + MOBA_COLS], far, bias),
            _rwkv_branch(h[..., GLA_COLS + MOBA_COLS:D_IN], rwkv_mu[l], rwkv_w_up[l], rwkv_w0[l],
                         rwkv_a_up[l], rwkv_a0[l], rwkv_k_k[l], rwkv_k_a[l], rwkv_r_k[l],
                         rwkv_gn_w[l], rwkv_gn_b[l]),
        ], axis=-1)
        x2 = _proj_out(o.reshape(bsz * t, d), w_out[l].astype(BF16), x2,
                       ln_w[l].reshape(1, d), ln_b[l].reshape(1, d), alpha)
    return x2.reshape(bsz, t, d)
```

```python
import functools
import math

import jax
import jax.numpy as jnp
import numpy as np
from jax import lax
from jax.experimental import pallas as pl
from jax.experimental.pallas import tpu as pltpu

HEAD_DIM = 64
GLA_HEADS = 4
MOBA_HEADS = 8
RWKV_HEADS = 4
GLA_W = GLA_HEADS * HEAD_DIM
MOBA_W = MOBA_HEADS * HEAD_DIM
RWKV_W = RWKV_HEADS * HEAD_DIM
GLA_LOWRANK = 16
GLA_TAU = 16.0
RWKV_LORA = 32
RWKV_GN_EPS = 64e-5
MOBA_BLOCK = 256
MOBA_TOPK = 3
N_BUCKETS = 32
MAX_DISTANCE = 4096
LN_EPS = 1e-5
GLA_COLS = 4 * GLA_W + GLA_LOWRANK
MOBA_COLS = 4 * MOBA_W
RWKV_COLS = 4 * RWKV_W + 2 * RWKV_LORA
D_IN = GLA_COLS + MOBA_COLS + RWKV_COLS

LANE = 128
D_IN_PAD = -(-D_IN // LANE) * LANE
CHUNK = 64
SUB = 16
TIME_BLOCK = 512
ROW_BLOCK = 512
N_BIAS_TILES = (MAX_DISTANCE + MOBA_BLOCK - 1) // MOBA_BLOCK + 1
VMEM_LIMIT = 56 * 1024 * 1024

F32 = jnp.float32
BF16 = jnp.bfloat16
NEG_INF = float("-inf")
MASK_BIG = 2.0 ** 100
SUBLANE_BF16 = 16


def _parts(a, n):
    out, r = [], a
    for i in range(n):
        p = r.astype(BF16)
        out.append(p)
        if i + 1 < n:
            r = r - p.astype(F32)
    return out


def _mm(a, b, dims=(((1,), (0,)), ((), ())), na=1, nb=1):
    ap = [a] if a.dtype == BF16 else _parts(a, na)
    bp = [b] if b.dtype == BF16 else _parts(b, nb)
    n = max(len(ap), len(bp))
    acc = None
    for i, x in enumerate(ap):
        for j, y in enumerate(bp):
            if i + j < n:
                t = lax.dot_general(x, y, dims, preferred_element_type=F32)
                acc = t if acc is None else acc + t
    return acc


_NT = (((1,), (1,)), ((), ()))
_TN = (((0,), (0,)), ((), ()))


def _softplus(x):
    return jnp.maximum(x, 0.0) + jnp.log(1.0 + jnp.exp(-jnp.abs(x)))


def _sigmoid(x):
    return 1.0 / (1.0 + jnp.exp(-x))


def _iota2(shape, dim):
    return lax.broadcasted_iota(jnp.int32, shape, dim)


def _proj_in_kernel(x_ref, w_ref, o_ref):
    o_ref[...] = jnp.dot(x_ref[...].astype(BF16), w_ref[...], preferred_element_type=F32)


def _proj_in(x2, w):
    m, d = x2.shape
    n = w.shape[1]
    n_split = 3
    tn = n // n_split
    return pl.pallas_call(
        _proj_in_kernel,
        grid=(n_split, m // ROW_BLOCK),
        in_specs=[pl.BlockSpec((ROW_BLOCK, d), lambda j, i: (i, 0)),
                  pl.BlockSpec((d, tn), lambda j, i: (0, j))],
        out_specs=pl.BlockSpec((ROW_BLOCK, tn), lambda j, i: (i, j)),
        out_shape=jax.ShapeDtypeStruct((m, n), F32),
        compiler_params=pltpu.CompilerParams(
            dimension_semantics=("parallel", "parallel"), vmem_limit_bytes=VMEM_LIMIT),
    )(x2, w)


def _gla_kernel(q_ref, k_ref, v_ref, g_ref, adn_ref, aup_ref, ab_ref, nw_ref, o_ref, st_ref):
    @pl.when(pl.program_id(2) == 0)
    def _():
        st_ref[...] = jnp.zeros_like(st_ref)

    z = _mm(adn_ref[0], aup_ref[0], na=2, nb=2) + ab_ref[0]
    la_all = -_softplus(-z) * (1.0 / GLA_TAU)
    row = _iota2((CHUNK, CHUNK), 0)
    col = _iota2((CHUNK, CHUNK), 1)
    tri = (row >= col)
    tri_b = tri.astype(BF16)
    anchor_b = (col <= (row // SUB) * SUB + (SUB - 1)).astype(BF16)
    nw = nw_ref[...]
    for c in range(TIME_BLOCK // CHUNK):
        sl = pl.ds(c * CHUNK, CHUNK)
        q = q_ref[0, 0, sl, :] * (HEAD_DIM ** -0.5)
        k = k_ref[0, 0, sl, :]
        v = v_ref[0, 0, sl, :]
        g = g_ref[0, 0, sl, :]
        la = la_all[c * CHUNK:(c + 1) * CHUNK]
        cum = _mm(tri_b, la, nb=3)
        k_anchor = _mm(anchor_b, la, nb=3)
        k_t = k * jnp.exp(k_anchor - cum)
        scores = jnp.zeros((CHUNK, CHUNK), F32)
        for j in range(CHUNK // SUB):
            a_j = cum[j * SUB + SUB - 1:j * SUB + SUB, :]
            q_j = q * jnp.exp(jnp.where(row >= j * SUB, cum - a_j, 0.0))
            in_grp = (row >= j * SUB) & (row < (j + 1) * SUB)
            k_j = jnp.where(in_grp, k_t, 0.0)
            scores = scores + _mm(q_j, k_j, _NT)
        scores = jnp.where(tri, scores, 0.0)
        st = st_ref[...]
        o = _mm(scores, v) + _mm(q * jnp.exp(cum), st, _NT)
        last = cum[CHUNK - 1:CHUNK, :]
        k_hat = k * jnp.exp(last - cum)
        st_ref[...] = st * jnp.exp(last) + _mm(v, k_hat, _TN)
        o = o * lax.rsqrt(jnp.mean(o * o, axis=-1, keepdims=True) + LN_EPS) * nw
        o_ref[0, 0, sl, :] = o * (g * _sigmoid(g))


def _gla(q, k, v, g, a_dn, a_up, a_bias, norm_w):
    b, h, t, d = q.shape
    hs = pl.BlockSpec((1, 1, TIME_BLOCK, d), lambda i, j, s: (i, j, s, 0))
    return pl.pallas_call(
        _gla_kernel,
        grid=(b, h, t // TIME_BLOCK),
        in_specs=[hs, hs, hs, hs,
                  pl.BlockSpec((1, TIME_BLOCK, GLA_LOWRANK), lambda i, j, s: (i, s, 0)),
                  pl.BlockSpec((1, GLA_LOWRANK, d), lambda i, j, s: (j, 0, 0)),
                  pl.BlockSpec((1, 1, d), lambda i, j, s: (j, 0, 0)),
                  pl.BlockSpec((1, d), lambda i, j, s: (0, 0))],
        out_specs=hs,
        out_shape=jax.ShapeDtypeStruct((b, h, t, d), F32),
        scratch_shapes=[pltpu.VMEM((d, d), F32)],
        compiler_params=pltpu.CompilerParams(
            dimension_semantics=("parallel", "parallel", "arbitrary"), vmem_limit_bytes=VMEM_LIMIT),
    )(q, k, v, g, a_dn, a_up, a_bias, norm_w)


def _kbar_kernel(k_ref, o_ref):
    nb = o_ref.shape[2]
    for j in range(nb):
        blk = k_ref[0, 0, j * MOBA_BLOCK:(j + 1) * MOBA_BLOCK, :]
        o_ref[0, 0, j:j + 1, :] = jnp.sum(blk, axis=0, keepdims=True) * (1.0 / MOBA_BLOCK)


def _moba_kbar(k):
    b, h, t, d = k.shape
    nb = t // MOBA_BLOCK
    return pl.pallas_call(
        _kbar_kernel,
        grid=(b, h),
        in_specs=[pl.BlockSpec((1, 1, t, d), lambda i, j: (i, j, 0, 0))],
        out_specs=pl.BlockSpec((1, 1, nb, d), lambda i, j: (i, j, 0, 0)),
        out_shape=jax.ShapeDtypeStruct((b, h, nb, d), F32),
        compiler_params=pltpu.CompilerParams(
            dimension_semantics=("parallel", "parallel"), vmem_limit_bytes=VMEM_LIMIT),
    )(k)


def _proj_t_kernel(x_ref, w_ref, o_ref):
    o_ref[0] = lax.dot_general(w_ref[...], x_ref[0].astype(BF16), _NT, preferred_element_type=F32)


def _proj_t(x, w_t):
    b, t, d = x.shape
    n = w_t.shape[0]
    return pl.pallas_call(
        _proj_t_kernel,
        grid=(b, t // ROW_BLOCK),
        in_specs=[pl.BlockSpec((1, ROW_BLOCK, d), lambda i, s: (i, s, 0)),
                  pl.BlockSpec((n, d), lambda i, s: (0, 0))],
        out_specs=pl.BlockSpec((1, n, ROW_BLOCK), lambda i, s: (i, 0, s)),
        out_shape=jax.ShapeDtypeStruct((b, n, t), F32),
        compiler_params=pltpu.CompilerParams(
            dimension_semantics=("parallel", "parallel"), vmem_limit_bytes=VMEM_LIMIT),
    )(x, w_t)


def _select_kernel(qt_ref, kbar_ref, pen_ref):
    own = pl.program_id(2)
    gate = _mm(kbar_ref[0, 0], qt_ref[0], na=2, nb=2)
    blk = _iota2(gate.shape, 0)
    nb = gate.shape[0]
    gate = jnp.where(blk < own, gate, NEG_INF)
    allowed = blk == own
    for _ in range(min(MOBA_TOPK, nb)):
        best = jnp.max(gate, axis=0, keepdims=True)
        first = jnp.min(jnp.where(gate == best, blk, nb), axis=0, keepdims=True)
        hit = blk == first
        allowed = allowed | (hit & (blk < own))
        gate = jnp.where(hit, NEG_INF, gate)
    pen_ref[0, 0] = jnp.where(allowed, 0.0, -MASK_BIG).astype(BF16)


def _moba_select(qt, kbar):
    b, _, t = qt.shape
    h, nb = kbar.shape[1], kbar.shape[2]
    return pl.pallas_call(
        _select_kernel,
        grid=(b, h, t // MOBA_BLOCK),
        in_specs=[pl.BlockSpec((1, HEAD_DIM, MOBA_BLOCK), lambda i, j, s: (i, j, s)),
                  pl.BlockSpec((1, 1, nb, HEAD_DIM), lambda i, j, s: (i, j, 0, 0))],
        out_specs=pl.BlockSpec((1, 1, nb, MOBA_BLOCK), lambda i, j, s: (i, j, 0, s)),
        out_shape=jax.ShapeDtypeStruct((b, h, nb, t), BF16),
        compiler_params=pltpu.CompilerParams(
            dimension_semantics=("parallel", "parallel", "parallel"), vmem_limit_bytes=VMEM_LIMIT),
    )(qt, kbar)


def _moba_attn_kernel(far_ref, qt_ref, pen_ref, k_ref, vt_ref, gt_ref, bias_ref, o_ref):
    hd = pl.program_id(1)
    own = pl.program_id(2)
    q_aug = jnp.concatenate([(qt_ref[0] * (HEAD_DIM ** -0.5)).astype(BF16), pen_ref[0, 0]], axis=0)
    far = far_ref[hd]

    def kv(j):
        sl = pl.ds(pl.multiple_of(j * MOBA_BLOCK, MOBA_BLOCK), MOBA_BLOCK)
        return k_ref[0, 0, sl, :], vt_ref[0, 0, j]

    def update(carry, s, shift, vj):
        m, acc = carry
        m_new = jnp.maximum(m, jnp.max(s, axis=0, keepdims=True) + shift)
        p = jnp.exp((s - (m_new - shift)).astype(BF16))
        return m_new, acc * jnp.exp(m - m_new) + _mm(vj, p)

    k0, v0 = kv(own)
    s = _mm(k0, q_aug) + bias_ref[0, 0]
    s = jnp.where(_iota2(s.shape, 0) <= _iota2(s.shape, 1), s, -MASK_BIG)
    m = jnp.max(s, axis=0, keepdims=True)
    acc = _mm(v0, jnp.exp((s - m).astype(BF16)))

    def far_body(j, carry):
        kj, vj = kv(j)
        return update(carry, _mm(kj, q_aug), far, vj)

    def near_body(j, carry):
        kj, vj = kv(j)
        return update(carry, _mm(kj, q_aug) + bias_ref[0, own - j], 0.0, vj)

    n_far = jnp.maximum(own - (N_BIAS_TILES - 1), 0)
    carry = lax.fori_loop(0, n_far, far_body, (m, acc))
    m, acc = lax.fori_loop(n_far, own, near_body, carry)
    g = gt_ref[0]
    o_ref[0] = acc[:HEAD_DIM] / acc[HEAD_DIM:HEAD_DIM + 1] * (g * _sigmoid(g))


def _moba_attn(far, ht, pen, k_aug, vt_aug, bias):
    b, _, t = ht.shape
    h, nb = pen.shape[1], pen.shape[2]
    qs = pl.BlockSpec((1, HEAD_DIM, MOBA_BLOCK), lambda i, j, s: (i, j, s))
    gs = pl.BlockSpec((1, HEAD_DIM, MOBA_BLOCK), lambda i, j, s: (i, 2 * h + j, s))
    return pl.pallas_call(
        _moba_attn_kernel,
        grid=(b, h, nb),
        in_specs=[pl.BlockSpec(memory_space=pltpu.SMEM), qs,
                  pl.BlockSpec((1, 1, nb, MOBA_BLOCK), lambda i, j, s: (i, j, 0, s)),
                  pl.BlockSpec((1, 1) + k_aug.shape[2:], lambda i, j, s: (i, j, 0, 0)),
                  pl.BlockSpec((1, 1) + vt_aug.shape[2:], lambda i, j, s: (i, j, 0, 0, 0)),
                  gs,
                  pl.BlockSpec((1, N_BIAS_TILES, MOBA_BLOCK, MOBA_BLOCK), lambda i, j, s: (j, 0, 0, 0))],
        out_specs=qs,
        out_shape=jax.ShapeDtypeStruct((b, h * HEAD_DIM, t), F32),
        compiler_params=pltpu.CompilerParams(
            dimension_semantics=("parallel", "parallel", "arbitrary"), vmem_limit_bytes=VMEM_LIMIT),
    )(far, ht, pen, k_aug, vt_aug, ht, bias)


def _t5_bucket(rel):
    rel = jnp.maximum(rel, 0)
    max_exact = N_BUCKETS // 2
    rel_f = jnp.maximum(rel, 1).astype(F32)
    large = max_exact + (jnp.log(rel_f / max_exact) / math.log(MAX_DISTANCE / max_exact)
                         * (N_BUCKETS - max_exact)).astype(jnp.int32)
    large = jnp.minimum(large, N_BUCKETS - 1)
    return jnp.where(rel < max_exact, rel, large)


def _bias_tiles(rel_bias):
    d = jnp.arange(N_BIAS_TILES)[:, None, None] * MOBA_BLOCK
    rel = d + jnp.arange(MOBA_BLOCK)[None, None, :] - jnp.arange(MOBA_BLOCK)[None, :, None]
    bucket = _t5_bucket(rel)
    tab = rel_bias.T.astype(F32)
    out = jnp.zeros((tab.shape[0],) + bucket.shape, F32)
    for bk in range(N_BUCKETS):
        out = jnp.where(bucket[None] == bk, tab[:, bk][:, None, None, None], out)
    return out


def _rwkv_prep_kernel(h_ref, hp_ref, mu_ref, wup_ref, w0_ref, aup_ref, a0_ref, kk_ref, ka_ref, rk_ref,
                      r_ref, lw_ref, k_ref, v_ref, a_ref, b_ref, bonus_ref, sg_ref):
    p = h_ref[0]
    p = p + (hp_ref[0] - p) * mu_ref[...]
    w = RWKV_W
    r, k, v, g = p[:, 0:w], p[:, w:2 * w], p[:, 2 * w:3 * w], p[:, 3 * w:4 * w]
    w_dn = p[:, 4 * w:4 * w + RWKV_LORA]
    a_dn = p[:, 4 * w + RWKV_LORA:4 * w + 2 * RWKV_LORA]
    d = w0_ref[...] + _mm(jnp.tanh(w_dn), wup_ref[...], na=2, nb=2)
    lw_ref[0] = -jnp.exp(-_softplus(-d) - 0.5)
    a = _sigmoid(a0_ref[...] + _mm(a_dn, aup_ref[...], na=2, nb=2))
    same_head = (_iota2((w, w), 0) // HEAD_DIM == _iota2((w, w), 1) // HEAD_DIM).astype(BF16)
    kk = k * kk_ref[...]
    kk = kk / jnp.maximum(jnp.sqrt(_mm(kk * kk, same_head, na=3)), 1e-12)
    k = k * (1.0 + (a - 1.0) * ka_ref[...])
    r_ref[0] = r
    k_ref[0] = k
    v_ref[0] = v
    a_ref[0] = -kk
    b_ref[0] = kk * a
    bonus_ref[0] = _mm(r * k * rk_ref[...], same_head, na=3) * v
    sg_ref[0] = g * _sigmoid(g)


def _rwkv_prep(hr, hr_prev, mu, w_up, w0, a_up, a0, k_k, k_a, r_k):
    b, t, c = hr.shape
    w = RWKV_W
    hs = pl.BlockSpec((1, ROW_BLOCK, c), lambda i, s: (i, s, 0))
    os_ = pl.BlockSpec((1, ROW_BLOCK, w), lambda i, s: (i, s, 0))
    vec = lambda n: pl.BlockSpec((1, n), lambda i, s: (0, 0))
    lora = pl.BlockSpec((RWKV_LORA, w), lambda i, s: (0, 0))
    return pl.pallas_call(
        _rwkv_prep_kernel,
        grid=(b, t // ROW_BLOCK),
        in_specs=[hs, hs, vec(c), lora, vec(w), lora, vec(w), vec(w), vec(w), vec(w)],
        out_specs=[os_] * 8,
        out_shape=[jax.ShapeDtypeStruct((b, t, w), F32)] * 8,
        compiler_params=pltpu.CompilerParams(
            dimension_semantics=("parallel", "parallel"), vmem_limit_bytes=VMEM_LIMIT),
    )(hr, hr_prev, mu, w_up, w0, a_up, a0, k_k, k_a, r_k)


def _rwkv_scan_kernel(r_ref, lw_ref, k_ref, v_ref, a_ref, b_ref, bonus_ref, sg_ref, gw_ref, gb_ref,
                      o_ref, st_ref):
    @pl.when(pl.program_id(2) == 0)
    def _():
        st_ref[...] = jnp.zeros_like(st_ref)

    row = _iota2((CHUNK, CHUNK), 0)
    col = _iota2((CHUNK, CHUNK), 1)
    tri_b = (row >= col).astype(BF16)
    eye = row == col
    hp = dict(na=2, nb=2)
    gw = gw_ref[0]
    gb = gb_ref[0]
    for c in range(TIME_BLOCK // CHUNK):
        sl = pl.ds(c * CHUNK, CHUNK)
        r = r_ref[0, 0, sl, :]
        lw = lw_ref[0, 0, sl, :]
        k = k_ref[0, 0, sl, :]
        v = v_ref[0, 0, sl, :]
        a = a_ref[0, 0, sl, :]
        b = b_ref[0, 0, sl, :]
        cum = _mm(tri_b, lw, nb=3)
        last = cum[CHUNK - 1:CHUNK, :]
        dec = jnp.exp(cum)
        inv = jnp.exp(-cum)
        to_end = jnp.exp(last - cum)
        r_t = r * dec
        a_t = a * jnp.exp(cum - lw)
        k_t = k * inv
        b_t = b * inv
        a_ab = jnp.where(row > col, _mm(a_t, b_t, _NT, **hp), 0.0)
        a_ak = jnp.where(row > col, _mm(a_t, k_t, _NT, **hp), 0.0)
        a_rb = jnp.where(row >= col, _mm(r_t, b_t, _NT, **hp), 0.0)
        a_rk = jnp.where(row >= col, _mm(r_t, k_t, _NT, **hp), 0.0)
        inv_t = jnp.where(eye, 1.0, a_ab)
        power = a_ab
        for _ in range(int(math.log2(CHUNK)) - 1):
            power = _mm(power, power, **hp)
            inv_t = inv_t + _mm(inv_t, power, **hp)
        w_mat = _mm(inv_t, a_t, **hp)
        u0 = _mm(inv_t, _mm(a_ak, v, **hp), **hp)
        y_w = r_t + _mm(a_rb, w_mat, **hp)
        y0 = _mm(a_rb, u0, **hp) + _mm(a_rk, v, **hp)
        b_end = b * to_end
        k_end = k * to_end
        m_mat = jnp.where(eye, jnp.exp(last), 0.0) + _mm(b_end, w_mat, _TN, **hp)
        n_mat = _mm(b_end, u0, _TN, **hp) + _mm(k_end, v, _TN, **hp)
        st = st_ref[...]
        y = _mm(y_w, st, **hp) + y0
        st_ref[...] = _mm(m_mat, st, **hp) + n_mat
        mean = jnp.mean(y, axis=-1, keepdims=True)
        var = jnp.mean(jnp.square(y - mean), axis=-1, keepdims=True)
        y = (y - mean) * lax.rsqrt(var + RWKV_GN_EPS) * gw + gb
        o_ref[0, 0, sl, :] = (y + bonus_ref[0, 0, sl, :]) * sg_ref[0, 0, sl, :]


def _rwkv_scan(r, lw, k, v, a, b, bonus, sg, gn_w, gn_b):
    bsz, h, t, d = r.shape
    hs = pl.BlockSpec((1, 1, TIME_BLOCK, d), lambda i, j, s: (i, j, s, 0))
    ps = pl.BlockSpec((1, 1, d), lambda i, j, s: (j, 0, 0))
    return pl.pallas_call(
        _rwkv_scan_kernel,
        grid=(bsz, h, t // TIME_BLOCK),
        in_specs=[hs] * 8 + [ps, ps],
        out_specs=hs,
        out_shape=jax.ShapeDtypeStruct((bsz, h, t, d), F32),
        scratch_shapes=[pltpu.VMEM((d, d), F32)],
        compiler_params=pltpu.CompilerParams(
            dimension_semantics=("parallel", "parallel", "arbitrary"), vmem_limit_bytes=VMEM_LIMIT),
    )(r, lw, k, v, a, b, bonus, sg, gn_w, gn_b)


def _proj_out_kernel(alpha, og_ref, ot_ref, or_ref, w_ref, x_ref, lw_ref, lb_ref, y_ref):
    y = _mm(og_ref[0].astype(BF16), w_ref[0:GLA_W, :])
    y = y + _mm(ot_ref[0].astype(BF16), w_ref[GLA_W:GLA_W + MOBA_W, :], _TN)
    y = y + _mm(or_ref[0].astype(BF16), w_ref[GLA_W + MOBA_W:, :])
    z = alpha * x_ref[0] + y
    mu = jnp.mean(z, axis=-1, keepdims=True)
    var = jnp.mean(jnp.square(z - mu), axis=-1, keepdims=True)
    y_ref[0] = (z - mu) * lax.rsqrt(var + LN_EPS) * lw_ref[...] + lb_ref[...]


def _proj_out(o_gla, ot_moba, o_rwkv, w, x, ln_w, ln_b, alpha):
    b, t, d = x.shape
    rs = lambda n: pl.BlockSpec((1, ROW_BLOCK, n), lambda i, s: (i, s, 0))
    vs = pl.BlockSpec((1, d), lambda i, s: (0, 0))
    return pl.pallas_call(
        functools.partial(_proj_out_kernel, alpha),
        grid=(b, t // ROW_BLOCK),
        in_specs=[rs(GLA_W), pl.BlockSpec((1, MOBA_W, ROW_BLOCK), lambda i, s: (i, 0, s)), rs(RWKV_W),
                  pl.BlockSpec(w.shape, lambda i, s: (0, 0)), rs(d), vs, vs],
        out_specs=rs(d),
        out_shape=jax.ShapeDtypeStruct((b, t, d), F32),
        compiler_params=pltpu.CompilerParams(
            dimension_semantics=("parallel", "parallel"), vmem_limit_bytes=VMEM_LIMIT),
    )(o_gla, ot_moba, o_rwkv, w, x, ln_w, ln_b)


def _to_heads(t, n):
    b, s, _ = t.shape
    return t.reshape(b, s, n, HEAD_DIM).transpose(0, 2, 1, 3)


def _from_heads(t):
    b, n, s, d = t.shape
    return t.transpose(0, 2, 1, 3).reshape(b, s, n * d)


def _gla_branch(hg, a_up, a_bias, norm_w):
    q, k, v, g = (_to_heads(hg[..., i * GLA_W:(i + 1) * GLA_W], GLA_HEADS) for i in range(4))
    a_dn = hg[..., 4 * GLA_W:]
    a_up_h = a_up.reshape(GLA_LOWRANK, GLA_HEADS, HEAD_DIM).transpose(1, 0, 2)
    a_bias_h = a_bias.reshape(GLA_HEADS, 1, HEAD_DIM)
    return _from_heads(_gla(q, k, v, g, a_dn, a_up_h, a_bias_h, norm_w.reshape(1, HEAD_DIM)))


def _moba_branch(x, w_l, k_nat, far, bias):
    bsz, t, _ = x.shape
    nb = t // MOBA_BLOCK
    cols = lambda i: w_l[:, GLA_COLS + i * MOBA_W:GLA_COLS + (i + 1) * MOBA_W]
    w_t = jnp.concatenate([cols(0), cols(2), cols(3)], axis=1).T.astype(BF16)
    ht = _proj_t(x, w_t)
    k = _to_heads(k_nat, MOBA_HEADS)
    onehot = (jnp.arange(t)[:, None] // MOBA_BLOCK == jnp.arange(nb)[None, :]).astype(BF16)
    k_aug = jnp.concatenate([k.astype(BF16), jnp.broadcast_to(onehot, (bsz, MOBA_HEADS, t, nb))], axis=-1)
    vt = ht[:, MOBA_W:2 * MOBA_W].reshape(bsz, MOBA_HEADS, HEAD_DIM, nb, MOBA_BLOCK).transpose(0, 1, 3, 2, 4)
    ones = jnp.ones((bsz, MOBA_HEADS, nb, 1, MOBA_BLOCK), F32)
    zeros = jnp.zeros((bsz, MOBA_HEADS, nb, SUBLANE_BF16 - 1, MOBA_BLOCK), F32)
    vt_aug = jnp.concatenate([vt, ones, zeros], axis=3).astype(BF16)
    pen = _moba_select(ht, _moba_kbar(k))
    return _moba_attn(far, ht, pen, k_aug, vt_aug, bias)


def _rwkv_branch(hr, mu, w_up, w0, a_up, a0, k_k, k_a, r_k, gn_w, gn_b):
    hr_prev = jnp.pad(hr, ((0, 0), (1, 0), (0, 0)))[:, :-1]
    row = lambda p: p.reshape(1, -1)
    outs = _rwkv_prep(hr, hr_prev, row(mu), w_up, row(w0), a_up, row(a0), row(k_k), row(k_a), row(r_k))
    heads = [_to_heads(o, RWKV_HEADS) for o in outs]
    per_head = lambda p: p.reshape(RWKV_HEADS, 1, HEAD_DIM)
    return _from_heads(_rwkv_scan(*heads, per_head(gn_w), per_head(gn_b)))


def kernel(x, w_in, w_out, gla_a_up, gla_a_bias, gla_norm_w, moba_rel_bias, rwkv_mu, rwkv_w_up, rwkv_w0,
           rwkv_a_up, rwkv_a0, rwkv_k_k, rwkv_k_a, rwkv_r_k, rwkv_gn_w, rwkv_gn_b, ln_w, ln_b):
    bsz, t, d = x.shape
    depth = w_in.shape[0]
    alpha = (2.0 * depth) ** 0.25
    bias = _bias_tiles(moba_rel_bias)
    far = moba_rel_bias[N_BUCKETS - 1].astype(F32)
    for l in range(depth):
        w = jnp.pad(w_in[l], ((0, 0), (0, D_IN_PAD - D_IN))).astype(BF16)
        h = _proj_in(x.reshape(bsz * t, d), w).reshape(bsz, t, D_IN_PAD)
        o_gla = _gla_branch(h[..., :GLA_COLS], gla_a_up[l], gla_a_bias[l], gla_norm_w[l])
        ot_moba = _moba_branch(x, w_in[l], h[..., GLA_COLS + MOBA_W:GLA_COLS + 2 * MOBA_W], far, bias)
        o_rwkv = _rwkv_branch(h[..., GLA_COLS + MOBA_COLS:D_IN], rwkv_mu[l], rwkv_w_up[l], rwkv_w0[l],
                              rwkv_a_up[l], rwkv_a0[l], rwkv_k_k[l], rwkv_k_a[l], rwkv_r_k[l],
                              rwkv_gn_w[l], rwkv_gn_b[l])
        x = _proj_out(o_gla, ot_moba, o_rwkv, w_out[l].astype(BF16), x,
                      ln_w[l].reshape(1, d), ln_b[l].reshape(1, d), alpha)
    return x
```

```python
import functools
import math

import jax
import jax.numpy as jnp
import numpy as np
from jax import lax
from jax.experimental import pallas as pl
from jax.experimental.pallas import tpu as pltpu

HEAD_DIM = 64
GLA_HEADS = 4
MOBA_HEADS = 8
RWKV_HEADS = 4
GLA_W = GLA_HEADS * HEAD_DIM
MOBA_W = MOBA_HEADS * HEAD_DIM
RWKV_W = RWKV_HEADS * HEAD_DIM
GLA_LOWRANK = 16
GLA_TAU = 16.0
RWKV_LORA = 32
RWKV_GN_EPS = 64e-5
MOBA_BLOCK = 256
MOBA_TOPK = 3
N_BUCKETS = 32
MAX_DISTANCE = 4096
LN_EPS = 1e-5
GLA_COLS = 4 * GLA_W + GLA_LOWRANK
MOBA_COLS = 4 * MOBA_W
RWKV_COLS = 4 * RWKV_W + 2 * RWKV_LORA
D_IN = GLA_COLS + MOBA_COLS + RWKV_COLS

LANE = 128
D_IN_PAD = -(-D_IN // LANE) * LANE
CHUNK = 64
SUB = 16
TIME_BLOCK = 512
ROW_BLOCK = 512
N_BIAS_TILES = (MAX_DISTANCE + MOBA_BLOCK - 1) // MOBA_BLOCK + 1
KV_GROUP = 4
assert (N_BIAS_TILES - 1) % KV_GROUP == 0
VMEM_LIMIT = 56 * 1024 * 1024

F32 = jnp.float32
BF16 = jnp.bfloat16
NEG_INF = float("-inf")
MASK_BIG = 2.0 ** 100
SUBLANE_BF16 = 16


def _parts(a, n):
    out, r = [], a
    for i in range(n):
        p = r.astype(BF16)
        out.append(p)
        if i + 1 < n:
            r = r - p.astype(F32)
    return out


def _mm(a, b, dims=(((1,), (0,)), ((), ())), na=1, nb=1):
    ap = [a] if a.dtype == BF16 else _parts(a, na)
    bp = [b] if b.dtype == BF16 else _parts(b, nb)
    n = max(len(ap), len(bp))
    acc = None
    for i, x in enumerate(ap):
        for j, y in enumerate(bp):
            if i + j < n:
                t = lax.dot_general(x, y, dims, preferred_element_type=F32)
                acc = t if acc is None else acc + t
    return acc


_NT = (((1,), (1,)), ((), ()))
_TN = (((0,), (0,)), ((), ()))


def _softplus(x):
    return jnp.maximum(x, 0.0) + jnp.log(1.0 + jnp.exp(-jnp.abs(x)))


def _sigmoid(x):
    return 1.0 / (1.0 + jnp.exp(-x))


def _iota2(shape, dim):
    return lax.broadcasted_iota(jnp.int32, shape, dim)


def _proj_in_kernel(x_ref, w_ref, o_ref):
    o_ref[...] = jnp.dot(x_ref[...].astype(BF16), w_ref[...], preferred_element_type=F32)


def _proj_in(x2, w):
    m, d = x2.shape
    n = w.shape[1]
    n_split = 3
    tn = n // n_split
    return pl.pallas_call(
        _proj_in_kernel,
        grid=(n_split, m // ROW_BLOCK),
        in_specs=[pl.BlockSpec((ROW_BLOCK, d), lambda j, i: (i, 0)),
                  pl.BlockSpec((d, tn), lambda j, i: (0, j))],
        out_specs=pl.BlockSpec((ROW_BLOCK, tn), lambda j, i: (i, j)),
        out_shape=jax.ShapeDtypeStruct((m, n), F32),
        compiler_params=pltpu.CompilerParams(
            dimension_semantics=("parallel", "parallel"), vmem_limit_bytes=VMEM_LIMIT),
    )(x2, w)


def _gla_kernel(q_ref, k_ref, v_ref, g_ref, adn_ref, aup_ref, ab_ref, nw_ref, o_ref, st_ref):
    @pl.when(pl.program_id(2) == 0)
    def _():
        st_ref[...] = jnp.zeros_like(st_ref)

    z = _mm(adn_ref[0], aup_ref[0], na=2, nb=2) + ab_ref[0]
    la_all = -_softplus(-z) * (1.0 / GLA_TAU)
    row = _iota2((CHUNK, CHUNK), 0)
    col = _iota2((CHUNK, CHUNK), 1)
    tri = (row >= col)
    tri_b = tri.astype(BF16)
    anchor_b = (col <= (row // SUB) * SUB + (SUB - 1)).astype(BF16)
    nw = nw_ref[...]
    for c in range(TIME_BLOCK // CHUNK):
        sl = pl.ds(c * CHUNK, CHUNK)
        q = q_ref[0, 0, sl, :] * (HEAD_DIM ** -0.5)
        k = k_ref[0, 0, sl, :]
        v = v_ref[0, 0, sl, :]
        g = g_ref[0, 0, sl, :]
        la = la_all[c * CHUNK:(c + 1) * CHUNK]
        cum = _mm(tri_b, la, nb=3)
        k_anchor = _mm(anchor_b, la, nb=3)
        k_t = k * jnp.exp(k_anchor - cum)
        scores = jnp.zeros((CHUNK, CHUNK), F32)
        for j in range(CHUNK // SUB):
            a_j = cum[j * SUB + SUB - 1:j * SUB + SUB, :]
            q_j = q * jnp.exp(jnp.where(row >= j * SUB, cum - a_j, 0.0))
            in_grp = (row >= j * SUB) & (row < (j + 1) * SUB)
            k_j = jnp.where(in_grp, k_t, 0.0)
            scores = scores + _mm(q_j, k_j, _NT)
        scores = jnp.where(tri, scores, 0.0)
        st = st_ref[...]
        o = _mm(scores, v) + _mm(q * jnp.exp(cum), st, _NT)
        last = cum[CHUNK - 1:CHUNK, :]
        k_hat = k * jnp.exp(last - cum)
        st_ref[...] = st * jnp.exp(last) + _mm(v, k_hat, _TN)
        o = o * lax.rsqrt(jnp.mean(o * o, axis=-1, keepdims=True) + LN_EPS) * nw
        o_ref[0, 0, sl, :] = o * (g * _sigmoid(g))


def _gla(q, k, v, g, a_dn, a_up, a_bias, norm_w):
    b, h, t, d = q.shape
    hs = pl.BlockSpec((1, 1, TIME_BLOCK, d), lambda i, j, s: (i, j, s, 0))
    return pl.pallas_call(
        _gla_kernel,
        grid=(b, h, t // TIME_BLOCK),
        in_specs=[hs, hs, hs, hs,
                  pl.BlockSpec((1, TIME_BLOCK, GLA_LOWRANK), lambda i, j, s: (i, s, 0)),
                  pl.BlockSpec((1, GLA_LOWRANK, d), lambda i, j, s: (j, 0, 0)),
                  pl.BlockSpec((1, 1, d), lambda i, j, s: (j, 0, 0)),
                  pl.BlockSpec((1, d), lambda i, j, s: (0, 0))],
        out_specs=hs,
        out_shape=jax.ShapeDtypeStruct((b, h, t, d), F32),
        scratch_shapes=[pltpu.VMEM((d, d), F32)],
        compiler_params=pltpu.CompilerParams(
            dimension_semantics=("parallel", "parallel", "arbitrary"), vmem_limit_bytes=VMEM_LIMIT),
    )(q, k, v, g, a_dn, a_up, a_bias, norm_w)


def _kbar_kernel(k_ref, o_ref):
    nb = o_ref.shape[2]
    for j in range(nb):
        blk = k_ref[0, 0, j * MOBA_BLOCK:(j + 1) * MOBA_BLOCK, :]
        o_ref[0, 0, j:j + 1, :] = jnp.sum(blk, axis=0, keepdims=True) * (1.0 / MOBA_BLOCK)


def _moba_kbar(k):
    b, h, t, d = k.shape
    nb = t // MOBA_BLOCK
    return pl.pallas_call(
        _kbar_kernel,
        grid=(b, h),
        in_specs=[pl.BlockSpec((1, 1, t, d), lambda i, j: (i, j, 0, 0))],
        out_specs=pl.BlockSpec((1, 1, nb, d), lambda i, j: (i, j, 0, 0)),
        out_shape=jax.ShapeDtypeStruct((b, h, nb, d), F32),
        compiler_params=pltpu.CompilerParams(
            dimension_semantics=("parallel", "parallel"), vmem_limit_bytes=VMEM_LIMIT),
    )(k)


def _proj_t_kernel(x_ref, w_ref, o_ref):
    o_ref[0] = lax.dot_general(w_ref[...], x_ref[0].astype(BF16), _NT, preferred_element_type=F32)


def _proj_t(x, w_t):
    b, t, d = x.shape
    n = w_t.shape[0]
    return pl.pallas_call(
        _proj_t_kernel,
        grid=(b, t // ROW_BLOCK),
        in_specs=[pl.BlockSpec((1, ROW_BLOCK, d), lambda i, s: (i, s, 0)),
                  pl.BlockSpec((n, d), lambda i, s: (0, 0))],
        out_specs=pl.BlockSpec((1, n, ROW_BLOCK), lambda i, s: (i, 0, s)),
        out_shape=jax.ShapeDtypeStruct((b, n, t), F32),
        compiler_params=pltpu.CompilerParams(
            dimension_semantics=("parallel", "parallel"), vmem_limit_bytes=VMEM_LIMIT),
    )(x, w_t)


def _select_kernel(qt_ref, kbar_ref, pen_ref):
    own = pl.program_id(2)
    gate = _mm(kbar_ref[0, 0], qt_ref[0], na=2, nb=2)
    blk = _iota2(gate.shape, 0)
    nb = gate.shape[0]
    gate = jnp.where(blk < own, gate, NEG_INF)
    allowed = blk == own
    for _ in range(min(MOBA_TOPK, nb)):
        best = jnp.max(gate, axis=0, keepdims=True)
        first = jnp.min(jnp.where(gate == best, blk, nb), axis=0, keepdims=True)
        hit = blk == first
        allowed = allowed | (hit & (blk < own))
        gate = jnp.where(hit, NEG_INF, gate)
    pen_ref[0, 0] = jnp.where(allowed, 0.0, -MASK_BIG).astype(BF16)


def _moba_select(qt, kbar):
    b, _, t = qt.shape
    h, nb = kbar.shape[1], kbar.shape[2]
    return pl.pallas_call(
        _select_kernel,
        grid=(b, h, t // MOBA_BLOCK),
        in_specs=[pl.BlockSpec((1, HEAD_DIM, MOBA_BLOCK), lambda i, j, s: (i, j, s)),
                  pl.BlockSpec((1, 1, nb, HEAD_DIM), lambda i, j, s: (i, j, 0, 0))],
        out_specs=pl.BlockSpec((1, 1, nb, MOBA_BLOCK), lambda i, j, s: (i, j, 0, s)),
        out_shape=jax.ShapeDtypeStruct((b, h, nb, t), BF16),
        compiler_params=pltpu.CompilerParams(
            dimension_semantics=("parallel", "parallel", "parallel"), vmem_limit_bytes=VMEM_LIMIT),
    )(qt, kbar)


def _moba_attn_kernel(far_ref, qt_ref, pen_ref, k_ref, vt_ref, gt_ref, bias_ref, o_ref, sa_ref, sb_ref):
    hd = pl.program_id(1)
    own = pl.program_id(2)
    nb = pen_ref.shape[2]
    q = (qt_ref[0] * (HEAD_DIM ** -0.5)).astype(BF16)
    pen = pen_ref[0, 0]
    blk = _iota2(pen.shape, 0)
    far = far_ref[hd]
    n_near = N_BIAS_TILES - 1
    n_far = jnp.maximum(own - n_near, 0)
    far_groups = (n_far + KV_GROUP - 1) // KV_GROUP
    n_groups = far_groups + (jnp.minimum(own, n_near) + KV_GROUP - 1) // KV_GROUP

    def first_block(u):
        hi_near = own - (u - far_groups) * KV_GROUP
        j0 = jnp.where(u < far_groups, u * KV_GROUP, hi_near - KV_GROUP)
        return jnp.clip(j0, 0, nb - KV_GROUP), hi_near

    def scores(u, dst):
        is_far = u < far_groups
        j0, hi_near = first_block(u)
        hi = jnp.where(u < n_groups, jnp.where(is_far, n_far, hi_near), 0)
        lo = jnp.where(is_far, 0, hi_near - KV_GROUP)
        in_range = (blk >= lo) & (blk < hi)
        q_aug = jnp.concatenate([q, jnp.where(in_range, pen, jnp.asarray(-MASK_BIG, BF16))], axis=0)
        for i in range(KV_GROUP):
            rows = pl.ds(pl.multiple_of((j0 + i) * MOBA_BLOCK, MOBA_BLOCK), MOBA_BLOCK)
            tile = jnp.where(is_far, N_BIAS_TILES, jnp.clip(own - (j0 + i), 0, n_near))
            dst[i * MOBA_BLOCK:(i + 1) * MOBA_BLOCK, :] = _mm(k_ref[0, 0, rows, :], q_aug) + bias_ref[0, tile]

    def consume(u, src, carry):
        m, acc = carry
        shift = jnp.where(u < far_groups, far, 0.0)
        j0, _ = first_block(u)
        m_new = jnp.maximum(m, jnp.max(src[...], axis=0, keepdims=True) + shift)
        acc = acc * jnp.exp(m - m_new)
        for i in range(KV_GROUP):
            s = src[i * MOBA_BLOCK:(i + 1) * MOBA_BLOCK, :]
            acc = acc + _mm(vt_ref[0, 0, j0 + i], jnp.exp((s - (m_new - shift)).astype(BF16)))
        return m_new, acc

    k0 = k_ref[0, 0, pl.ds(pl.multiple_of(own * MOBA_BLOCK, MOBA_BLOCK), MOBA_BLOCK), :]
    s = _mm(k0[:, :HEAD_DIM], q) + bias_ref[0, 0]
    s = jnp.where(_iota2(s.shape, 0) <= _iota2(s.shape, 1), s, -MASK_BIG)
    m = jnp.max(s, axis=0, keepdims=True)
    acc = _mm(vt_ref[0, 0, own], jnp.exp((s - m).astype(BF16)))

    scores(0, sa_ref)

    def pair(v, carry):
        u = 2 * v
        scores(u + 1, sb_ref)
        carry = consume(u, sa_ref, carry)
        scores(u + 2, sa_ref)
        return consume(u + 1, sb_ref, carry)

    m, acc = lax.fori_loop(0, (n_groups + 1) // 2, pair, (m, acc))
    g = gt_ref[0]
    o_ref[0] = acc[:HEAD_DIM] / acc[HEAD_DIM:HEAD_DIM + 1] * (g * _sigmoid(g))


def _moba_attn(far, ht, pen, k_aug, vt_aug, bias):
    b, _, t = ht.shape
    h, nb = pen.shape[1], pen.shape[2]
    qs = pl.BlockSpec((1, HEAD_DIM, MOBA_BLOCK), lambda i, j, s: (i, j, s))
    gs = pl.BlockSpec((1, HEAD_DIM, MOBA_BLOCK), lambda i, j, s: (i, 2 * h + j, s))
    return pl.pallas_call(
        _moba_attn_kernel,
        grid=(b, h, nb),
        in_specs=[pl.BlockSpec(memory_space=pltpu.SMEM), qs,
                  pl.BlockSpec((1, 1, nb, MOBA_BLOCK), lambda i, j, s: (i, j, 0, s)),
                  pl.BlockSpec((1, 1) + k_aug.shape[2:], lambda i, j, s: (i, j, 0, 0)),
                  pl.BlockSpec((1, 1) + vt_aug.shape[2:], lambda i, j, s: (i, j, 0, 0, 0)),
                  gs,
                  pl.BlockSpec((1, N_BIAS_TILES + 1, MOBA_BLOCK, MOBA_BLOCK), lambda i, j, s: (j, 0, 0, 0))],
        out_specs=qs,
        out_shape=jax.ShapeDtypeStruct((b, h * HEAD_DIM, t), F32),
        scratch_shapes=[pltpu.VMEM((KV_GROUP * MOBA_BLOCK, MOBA_BLOCK), F32)] * 2,
        compiler_params=pltpu.CompilerParams(
            dimension_semantics=("parallel", "parallel", "arbitrary"), vmem_limit_bytes=VMEM_LIMIT),
    )(far, ht, pen, k_aug, vt_aug, ht, bias)


def _t5_bucket(rel):
    rel = jnp.maximum(rel, 0)
    max_exact = N_BUCKETS // 2
    rel_f = jnp.maximum(rel, 1).astype(F32)
    large = max_exact + (jnp.log(rel_f / max_exact) / math.log(MAX_DISTANCE / max_exact)
                         * (N_BUCKETS - max_exact)).astype(jnp.int32)
    large = jnp.minimum(large, N_BUCKETS - 1)
    return jnp.where(rel < max_exact, rel, large)


def _bias_tiles(rel_bias):
    d = jnp.arange(N_BIAS_TILES)[:, None, None] * MOBA_BLOCK
    rel = d + jnp.arange(MOBA_BLOCK)[None, None, :] - jnp.arange(MOBA_BLOCK)[None, :, None]
    bucket = _t5_bucket(rel)
    tab = rel_bias.T.astype(F32)
    out = jnp.zeros((tab.shape[0],) + bucket.shape, F32)
    for bk in range(N_BUCKETS):
        out = jnp.where(bucket[None] == bk, tab[:, bk][:, None, None, None], out)
    return jnp.concatenate([out, jnp.zeros_like(out[:, :1])], axis=1)


def _rwkv_prep_kernel(h_ref, hp_ref, mu_ref, wup_ref, w0_ref, aup_ref, a0_ref, kk_ref, ka_ref, rk_ref,
                      r_ref, lw_ref, k_ref, v_ref, a_ref, b_ref, bonus_ref, sg_ref):
    p = h_ref[0]
    p = p + (hp_ref[0] - p) * mu_ref[...]
    w = RWKV_W
    r, k, v, g = p[:, 0:w], p[:, w:2 * w], p[:, 2 * w:3 * w], p[:, 3 * w:4 * w]
    w_dn = p[:, 4 * w:4 * w + RWKV_LORA]
    a_dn = p[:, 4 * w + RWKV_LORA:4 * w + 2 * RWKV_LORA]
    d = w0_ref[...] + _mm(jnp.tanh(w_dn), wup_ref[...], na=2, nb=2)
    lw_ref[0] = -jnp.exp(-_softplus(-d) - 0.5)
    a = _sigmoid(a0_ref[...] + _mm(a_dn, aup_ref[...], na=2, nb=2))
    same_head = (_iota2((w, w), 0) // HEAD_DIM == _iota2((w, w), 1) // HEAD_DIM).astype(BF16)
    kk = k * kk_ref[...]
    kk = kk / jnp.maximum(jnp.sqrt(_mm(kk * kk, same_head, na=3)), 1e-12)
    k = k * (1.0 + (a - 1.0) * ka_ref[...])
    r_ref[0] = r
    k_ref[0] = k
    v_ref[0] = v
    a_ref[0] = -kk
    b_ref[0] = kk * a
    bonus_ref[0] = _mm(r * k * rk_ref[...], same_head, na=3) * v
    sg_ref[0] = g * _sigmoid(g)


def _rwkv_prep(hr, hr_prev, mu, w_up, w0, a_up, a0, k_k, k_a, r_k):
    b, t, c = hr.shape
    w = RWKV_W
    hs = pl.BlockSpec((1, ROW_BLOCK, c), lambda i, s: (i, s, 0))
    os_ = pl.BlockSpec((1, ROW_BLOCK, w), lambda i, s: (i, s, 0))
    vec = lambda n: pl.BlockSpec((1, n), lambda i, s: (0, 0))
    lora = pl.BlockSpec((RWKV_LORA, w), lambda i, s: (0, 0))
    return pl.pallas_call(
        _rwkv_prep_kernel,
        grid=(b, t // ROW_BLOCK),
        in_specs=[hs, hs, vec(c), lora, vec(w), lora, vec(w), vec(w), vec(w), vec(w)],
        out_specs=[os_] * 8,
        out_shape=[jax.ShapeDtypeStruct((b, t, w), F32)] * 8,
        compiler_params=pltpu.CompilerParams(
            dimension_semantics=("parallel", "parallel"), vmem_limit_bytes=VMEM_LIMIT),
    )(hr, hr_prev, mu, w_up, w0, a_up, a0, k_k, k_a, r_k)


def _rwkv_scan_kernel(r_ref, lw_ref, k_ref, v_ref, a_ref, b_ref, bonus_ref, sg_ref, gw_ref, gb_ref,
                      o_ref, st_ref):
    @pl.when(pl.program_id(2) == 0)
    def _():
        st_ref[...] = jnp.zeros_like(st_ref)

    row = _iota2((CHUNK, CHUNK), 0)
    col = _iota2((CHUNK, CHUNK), 1)
    tri_b = (row >= col).astype(BF16)
    eye = row == col
    hp = dict(na=2, nb=2)
    gw = gw_ref[0]
    gb = gb_ref[0]
    for c in range(TIME_BLOCK // CHUNK):
        sl = pl.ds(c * CHUNK, CHUNK)
        r = r_ref[0, 0, sl, :]
        lw = lw_ref[0, 0, sl, :]
        k = k_ref[0, 0, sl, :]
        v = v_ref[0, 0, sl, :]
        a = a_ref[0, 0, sl, :]
        b = b_ref[0, 0, sl, :]
        cum = _mm(tri_b, lw, nb=3)
        last = cum[CHUNK - 1:CHUNK, :]
        dec = jnp.exp(cum)
        inv = jnp.exp(-cum)
        to_end = jnp.exp(last - cum)
        r_t = r * dec
        a_t = a * jnp.exp(cum - lw)
        k_t = k * inv
        b_t = b * inv
        a_ab = jnp.where(row > col, _mm(a_t, b_t, _NT, **hp), 0.0)
        a_ak = jnp.where(row > col, _mm(a_t, k_t, _NT, **hp), 0.0)
        a_rb = jnp.where(row >= col, _mm(r_t, b_t, _NT, **hp), 0.0)
        a_rk = jnp.where(row >= col, _mm(r_t, k_t, _NT, **hp), 0.0)
        inv_t = jnp.where(eye, 1.0, a_ab)
        power = a_ab
        for _ in range(int(math.log2(CHUNK)) - 1):
            power = _mm(power, power, **hp)
            inv_t = inv_t + _mm(inv_t, power, **hp)
        w_mat = _mm(inv_t, a_t, **hp)
        u0 = _mm(inv_t, _mm(a_ak, v, **hp), **hp)
        y_w = r_t + _mm(a_rb, w_mat, **hp)
        y0 = _mm(a_rb, u0, **hp) + _mm(a_rk, v, **hp)
        b_end = b * to_end
        k_end = k * to_end
        m_mat = jnp.where(eye, jnp.exp(last), 0.0) + _mm(b_end, w_mat, _TN, **hp)
        n_mat = _mm(b_end, u0, _TN, **hp) + _mm(k_end, v, _TN, **hp)
        st = st_ref[...]
        y = _mm(y_w, st, **hp) + y0
        st_ref[...] = _mm(m_mat, st, **hp) + n_mat
        mean = jnp.mean(y, axis=-1, keepdims=True)
        var = jnp.mean(jnp.square(y - mean), axis=-1, keepdims=True)
        y = (y - mean) * lax.rsqrt(var + RWKV_GN_EPS) * gw + gb
        o_ref[0, 0, sl, :] = (y + bonus_ref[0, 0, sl, :]) * sg_ref[0, 0, sl, :]


def _rwkv_scan(r, lw, k, v, a, b, bonus, sg, gn_w, gn_b):
    bsz, h, t, d = r.shape
    hs = pl.BlockSpec((1, 1, TIME_BLOCK, d), lambda i, j, s: (i, j, s, 0))
    ps = pl.BlockSpec((1, 1, d), lambda i, j, s: (j, 0, 0))
    return pl.pallas_call(
        _rwkv_scan_kernel,
        grid=(bsz, h, t // TIME_BLOCK),
        in_specs=[hs] * 8 + [ps, ps],
        out_specs=hs,
        out_shape=jax.ShapeDtypeStruct((bsz, h, t, d), F32),
        scratch_shapes=[pltpu.VMEM((d, d), F32)],
        compiler_params=pltpu.CompilerParams(
            dimension_semantics=("parallel", "parallel", "arbitrary"), vmem_limit_bytes=VMEM_LIMIT),
    )(r, lw, k, v, a, b, bonus, sg, gn_w, gn_b)


def _proj_out_kernel(alpha, og_ref, ot_ref, or_ref, w_ref, x_ref, lw_ref, lb_ref, y_ref):
    y = _mm(og_ref[0].astype(BF16), w_ref[0:GLA_W, :])
    y = y + _mm(ot_ref[0].astype(BF16), w_ref[GLA_W:GLA_W + MOBA_W, :], _TN)
    y = y + _mm(or_ref[0].astype(BF16), w_ref[GLA_W + MOBA_W:, :])
    z = alpha * x_ref[0] + y
    mu = jnp.mean(z, axis=-1, keepdims=True)
    var = jnp.mean(jnp.square(z - mu), axis=-1, keepdims=True)
    y_ref[0] = (z - mu) * lax.rsqrt(var + LN_EPS) * lw_ref[...] + lb_ref[...]


def _proj_out(o_gla, ot_moba, o_rwkv, w, x, ln_w, ln_b, alpha):
    b, t, d = x.shape
    rs = lambda n: pl.BlockSpec((1, ROW_BLOCK, n), lambda i, s: (i, s, 0))
    vs = pl.BlockSpec((1, d), lambda i, s: (0, 0))
    return pl.pallas_call(
        functools.partial(_proj_out_kernel, alpha),
        grid=(b, t // ROW_BLOCK),
        in_specs=[rs(GLA_W), pl.BlockSpec((1, MOBA_W, ROW_BLOCK), lambda i, s: (i, 0, s)), rs(RWKV_W),
                  pl.BlockSpec(w.shape, lambda i, s: (0, 0)), rs(d), vs, vs],
        out_specs=rs(d),
        out_shape=jax.ShapeDtypeStruct((b, t, d), F32),
        compiler_params=pltpu.CompilerParams(
            dimension_semantics=("parallel", "parallel"), vmem_limit_bytes=VMEM_LIMIT),
    )(o_gla, ot_moba, o_rwkv, w, x, ln_w, ln_b)


def _to_heads(t, n):
    b, s, _ = t.shape
    return t.reshape(b, s, n, HEAD_DIM).transpose(0, 2, 1, 3)


def _from_heads(t):
    b, n, s, d = t.shape
    return t.transpose(0, 2, 1, 3).reshape(b, s, n * d)


def _gla_branch(hg, a_up, a_bias, norm_w):
    q, k, v, g = (_to_heads(hg[..., i * GLA_W:(i + 1) * GLA_W], GLA_HEADS) for i in range(4))
    a_dn = hg[..., 4 * GLA_W:]
    a_up_h = a_up.reshape(GLA_LOWRANK, GLA_HEADS, HEAD_DIM).transpose(1, 0, 2)
    a_bias_h = a_bias.reshape(GLA_HEADS, 1, HEAD_DIM)
    return _from_heads(_gla(q, k, v, g, a_dn, a_up_h, a_bias_h, norm_w.reshape(1, HEAD_DIM)))


def _moba_branch(x, w_l, k_nat, far, bias):
    bsz, t, _ = x.shape
    nb = t // MOBA_BLOCK
    cols = lambda i: w_l[:, GLA_COLS + i * MOBA_W:GLA_COLS + (i + 1) * MOBA_W]
    w_t = jnp.concatenate([cols(0), cols(2), cols(3)], axis=1).T.astype(BF16)
    ht = _proj_t(x, w_t)
    k = _to_heads(k_nat, MOBA_HEADS)
    onehot = (jnp.arange(t)[:, None] // MOBA_BLOCK == jnp.arange(nb)[None, :]).astype(BF16)
    k_aug = jnp.concatenate([k.astype(BF16), jnp.broadcast_to(onehot, (bsz, MOBA_HEADS, t, nb))], axis=-1)
    vt = ht[:, MOBA_W:2 * MOBA_W].reshape(bsz, MOBA_HEADS, HEAD_DIM, nb, MOBA_BLOCK).transpose(0, 1, 3, 2, 4)
    ones = jnp.ones((bsz, MOBA_HEADS, nb, 1, MOBA_BLOCK), F32)
    zeros = jnp.zeros((bsz, MOBA_HEADS, nb, SUBLANE_BF16 - 1, MOBA_BLOCK), F32)
    vt_aug = jnp.concatenate([vt, ones, zeros], axis=3).astype(BF16)
    pen = _moba_select(ht, _moba_kbar(k))
    return _moba_attn(far, ht, pen, k_aug, vt_aug, bias)


def _rwkv_branch(hr, mu, w_up, w0, a_up, a0, k_k, k_a, r_k, gn_w, gn_b):
    hr_prev = jnp.pad(hr, ((0, 0), (1, 0), (0, 0)))[:, :-1]
    row = lambda p: p.reshape(1, -1)
    outs = _rwkv_prep(hr, hr_prev, row(mu), w_up, row(w0), a_up, row(a0), row(k_k), row(k_a), row(r_k))
    heads = [_to_heads(o, RWKV_HEADS) for o in outs]
    per_head = lambda p: p.reshape(RWKV_HEADS, 1, HEAD_DIM)
    return _from_heads(_rwkv_scan(*heads, per_head(gn_w), per_head(gn_b)))


def kernel(x, w_in, w_out, gla_a_up, gla_a_bias, gla_norm_w, moba_rel_bias, rwkv_mu, rwkv_w_up, rwkv_w0,
           rwkv_a_up, rwkv_a0, rwkv_k_k, rwkv_k_a, rwkv_r_k, rwkv_gn_w, rwkv_gn_b, ln_w, ln_b):
    bsz, t, d = x.shape
    depth = w_in.shape[0]
    alpha = (2.0 * depth) ** 0.25
    bias = _bias_tiles(moba_rel_bias)
    far = moba_rel_bias[N_BUCKETS - 1].astype(F32)
    for l in range(depth):
        w = jnp.pad(w_in[l], ((0, 0), (0, D_IN_PAD - D_IN))).astype(BF16)
        h = _proj_in(x.reshape(bsz * t, d), w).reshape(bsz, t, D_IN_PAD)
        o_gla = _gla_branch(h[..., :GLA_COLS], gla_a_up[l], gla_a_bias[l], gla_norm_w[l])
        ot_moba = _moba_branch(x, w_in[l], h[..., GLA_COLS + MOBA_W:GLA_COLS + 2 * MOBA_W], far, bias)
        o_rwkv = _rwkv_branch(h[..., GLA_COLS + MOBA_COLS:D_IN], rwkv_mu[l], rwkv_w_up[l], rwkv_w0[l],
                              rwkv_a_up[l], rwkv_a0[l], rwkv_k_k[l], rwkv_k_a[l], rwkv_r_k[l],
                              rwkv_gn_w[l], rwkv_gn_b[l])
        x = _proj_out(o_gla, ot_moba, o_rwkv, w_out[l].astype(BF16), x,
                      ln_w[l].reshape(1, d), ln_b[l].reshape(1, d), alpha)
    return x
```

```python
import functools
import math

import jax
import jax.numpy as jnp
import numpy as np
from jax import lax
from jax.experimental import pallas as pl
from jax.experimental.pallas import tpu as pltpu

HEAD_DIM = 64
GLA_HEADS = 4
MOBA_HEADS = 8
RWKV_HEADS = 4
GLA_W = GLA_HEADS * HEAD_DIM
MOBA_W = MOBA_HEADS * HEAD_DIM
RWKV_W = RWKV_HEADS * HEAD_DIM
GLA_LOWRANK = 16
GLA_TAU = 16.0
RWKV_LORA = 32
RWKV_GN_EPS = 64e-5
MOBA_BLOCK = 256
MOBA_TOPK = 3
N_BUCKETS = 32
MAX_DISTANCE = 4096
LN_EPS = 1e-5
GLA_COLS = 4 * GLA_W + GLA_LOWRANK
MOBA_COLS = 4 * MOBA_W
RWKV_COLS = 4 * RWKV_W + 2 * RWKV_LORA
D_IN = GLA_COLS + MOBA_COLS + RWKV_COLS

LANE = 128
D_IN_PAD = -(-D_IN // LANE) * LANE
CHUNK = 64
SUB = 16
TIME_BLOCK = 512
ROW_BLOCK = 512
N_BIAS_TILES = (MAX_DISTANCE + MOBA_BLOCK - 1) // MOBA_BLOCK + 1
KV_GROUP = 4
assert (N_BIAS_TILES - 1) % KV_GROUP == 0
VMEM_LIMIT = 56 * 1024 * 1024

F32 = jnp.float32
BF16 = jnp.bfloat16
NEG_INF = float("-inf")
MASK_BIG = 2.0 ** 100
SUBLANE_BF16 = 16


def _parts(a, n):
    out, r = [], a
    for i in range(n):
        p = r.astype(BF16)
        out.append(p)
        if i + 1 < n:
            r = r - p.astype(F32)
    return out


def _mm(a, b, dims=(((1,), (0,)), ((), ())), na=1, nb=1):
    ap = [a] if a.dtype == BF16 else _parts(a, na)
    bp = [b] if b.dtype == BF16 else _parts(b, nb)
    n = max(len(ap), len(bp))
    acc = None
    for i, x in enumerate(ap):
        for j, y in enumerate(bp):
            if i + j < n:
                t = lax.dot_general(x, y, dims, preferred_element_type=F32)
                acc = t if acc is None else acc + t
    return acc


_NT = (((1,), (1,)), ((), ()))
_TN = (((0,), (0,)), ((), ()))
_BNN = (((2,), (1,)), ((0,), (0,)))
_BNT = (((2,), (2,)), ((0,), (0,)))
_BTN = (((1,), (1,)), ((0,), (0,)))


def _softplus(x):
    return jnp.maximum(x, 0.0) + jnp.log(1.0 + jnp.exp(-jnp.abs(x)))


def _sigmoid(x):
    return 1.0 / (1.0 + jnp.exp(-x))


def _iota2(shape, dim):
    return lax.broadcasted_iota(jnp.int32, shape, dim)


def _proj_in_kernel(x_ref, w_ref, o_ref):
    o_ref[...] = jnp.dot(x_ref[...].astype(BF16), w_ref[...], preferred_element_type=F32)


def _proj_in(x2, w):
    m, d = x2.shape
    n = w.shape[1]
    n_split = 3
    tn = n // n_split
    return pl.pallas_call(
        _proj_in_kernel,
        grid=(n_split, m // ROW_BLOCK),
        in_specs=[pl.BlockSpec((ROW_BLOCK, d), lambda j, i: (i, 0)),
                  pl.BlockSpec((d, tn), lambda j, i: (0, j))],
        out_specs=pl.BlockSpec((ROW_BLOCK, tn), lambda j, i: (i, j)),
        out_shape=jax.ShapeDtypeStruct((m, n), F32),
        compiler_params=pltpu.CompilerParams(
            dimension_semantics=("parallel", "parallel"), vmem_limit_bytes=VMEM_LIMIT),
    )(x2, w)


def _gla_kernel(q_ref, k_ref, v_ref, g_ref, adn_ref, aup_ref, ab_ref, nw_ref, o_ref, st_ref):
    @pl.when(pl.program_id(2) == 0)
    def _():
        st_ref[...] = jnp.zeros_like(st_ref)

    nc = TIME_BLOCK // CHUNK
    shape3 = (nc, CHUNK, CHUNK)
    z = _mm(adn_ref[0], aup_ref[0], na=2, nb=2) + ab_ref[0]
    la = (-_softplus(-z) * (1.0 / GLA_TAU)).reshape(nc, CHUNK, HEAD_DIM)
    row = _iota2(shape3, 1)
    col = _iota2(shape3, 2)
    tri = (row >= col)
    anchor_b = (col <= (row // SUB) * SUB + (SUB - 1)).astype(BF16)
    chunks = lambda ref: ref[0, 0].reshape(nc, CHUNK, HEAD_DIM)
    q = chunks(q_ref) * (HEAD_DIM ** -0.5)
    k = chunks(k_ref)
    v = chunks(v_ref)
    cum = _mm(tri.astype(BF16), la, _BNN, nb=3)
    k_anchor = _mm(anchor_b, la, _BNN, nb=3)
    k_t = k * jnp.exp(k_anchor - cum)
    scores = jnp.zeros(shape3, F32)
    for j in range(CHUNK // SUB):
        a_j = cum[:, j * SUB + SUB - 1:j * SUB + SUB, :]
        q_j = q * jnp.exp(jnp.where(row >= j * SUB, cum - a_j, 0.0))
        in_grp = (row >= j * SUB) & (row < (j + 1) * SUB)
        scores = scores + _mm(q_j, jnp.where(in_grp, k_t, 0.0), _BNT)
    o_intra = _mm(jnp.where(tri, scores, 0.0), v, _BNN)
    last = cum[:, CHUNK - 1:CHUNK, :]
    n_mat = _mm(v, k * jnp.exp(last - cum), _BTN)
    q_dec = q * jnp.exp(cum)
    dec_last = jnp.exp(last)
    st = st_ref[...]
    o_inter = []
    for c in range(nc):
        o_inter.append(_mm(q_dec[c], st, _NT))
        st = st * dec_last[c] + n_mat[c]
    st_ref[...] = st
    o = o_intra.reshape(TIME_BLOCK, HEAD_DIM) + jnp.concatenate(o_inter, axis=0)
    o = o * lax.rsqrt(jnp.mean(o * o, axis=-1, keepdims=True) + LN_EPS) * nw_ref[...]
    g = g_ref[0, 0]
    o_ref[0, 0] = o * (g * _sigmoid(g))


def _gla(q, k, v, g, a_dn, a_up, a_bias, norm_w):
    b, h, t, d = q.shape
    hs = pl.BlockSpec((1, 1, TIME_BLOCK, d), lambda i, j, s: (i, j, s, 0))
    return pl.pallas_call(
        _gla_kernel,
        grid=(b, h, t // TIME_BLOCK),
        in_specs=[hs, hs, hs, hs,
                  pl.BlockSpec((1, TIME_BLOCK, GLA_LOWRANK), lambda i, j, s: (i, s, 0)),
                  pl.BlockSpec((1, GLA_LOWRANK, d), lambda i, j, s: (j, 0, 0)),
                  pl.BlockSpec((1, 1, d), lambda i, j, s: (j, 0, 0)),
                  pl.BlockSpec((1, d), lambda i, j, s: (0, 0))],
        out_specs=hs,
        out_shape=jax.ShapeDtypeStruct((b, h, t, d), F32),
        scratch_shapes=[pltpu.VMEM((d, d), F32)],
        compiler_params=pltpu.CompilerParams(
            dimension_semantics=("parallel", "parallel", "arbitrary"), vmem_limit_bytes=VMEM_LIMIT),
    )(q, k, v, g, a_dn, a_up, a_bias, norm_w)


def _kbar_kernel(k_ref, o_ref):
    nb = o_ref.shape[2]
    for j in range(nb):
        blk = k_ref[0, 0, j * MOBA_BLOCK:(j + 1) * MOBA_BLOCK, :]
        o_ref[0, 0, j:j + 1, :] = jnp.sum(blk, axis=0, keepdims=True) * (1.0 / MOBA_BLOCK)


def _moba_kbar(k):
    b, h, t, d = k.shape
    nb = t // MOBA_BLOCK
    return pl.pallas_call(
        _kbar_kernel,
        grid=(b, h),
        in_specs=[pl.BlockSpec((1, 1, t, d), lambda i, j: (i, j, 0, 0))],
        out_specs=pl.BlockSpec((1, 1, nb, d), lambda i, j: (i, j, 0, 0)),
        out_shape=jax.ShapeDtypeStruct((b, h, nb, d), F32),
        compiler_params=pltpu.CompilerParams(
            dimension_semantics=("parallel", "parallel"), vmem_limit_bytes=VMEM_LIMIT),
    )(k)


def _proj_t_kernel(x_ref, w_ref, o_ref):
    o_ref[0] = lax.dot_general(w_ref[...], x_ref[0].astype(BF16), _NT, preferred_element_type=F32)


def _proj_t(x, w_t):
    b, t, d = x.shape
    n = w_t.shape[0]
    return pl.pallas_call(
        _proj_t_kernel,
        grid=(b, t // ROW_BLOCK),
        in_specs=[pl.BlockSpec((1, ROW_BLOCK, d), lambda i, s: (i, s, 0)),
                  pl.BlockSpec((n, d), lambda i, s: (0, 0))],
        out_specs=pl.BlockSpec((1, n, ROW_BLOCK), lambda i, s: (i, 0, s)),
        out_shape=jax.ShapeDtypeStruct((b, n, t), F32),
        compiler_params=pltpu.CompilerParams(
            dimension_semantics=("parallel", "parallel"), vmem_limit_bytes=VMEM_LIMIT),
    )(x, w_t)


def _select_kernel(qt_ref, kbar_ref, pen_ref):
    own = pl.program_id(2)
    gate = _mm(kbar_ref[0, 0], qt_ref[0], na=2, nb=2)
    blk = _iota2(gate.shape, 0)
    nb = gate.shape[0]
    gate = jnp.where(blk < own, gate, NEG_INF)
    allowed = blk == own
    for _ in range(min(MOBA_TOPK, nb)):
        best = jnp.max(gate, axis=0, keepdims=True)
        first = jnp.min(jnp.where(gate == best, blk, nb), axis=0, keepdims=True)
        hit = blk == first
        allowed = allowed | (hit & (blk < own))
        gate = jnp.where(hit, NEG_INF, gate)
    pen_ref[0, 0] = jnp.where(allowed, 0.0, -MASK_BIG).astype(BF16)


def _moba_select(qt, kbar):
    b, _, t = qt.shape
    h, nb = kbar.shape[1], kbar.shape[2]
    return pl.pallas_call(
        _select_kernel,
        grid=(b, h, t // MOBA_BLOCK),
        in_specs=[pl.BlockSpec((1, HEAD_DIM, MOBA_BLOCK), lambda i, j, s: (i, j, s)),
                  pl.BlockSpec((1, 1, nb, HEAD_DIM), lambda i, j, s: (i, j, 0, 0))],
        out_specs=pl.BlockSpec((1, 1, nb, MOBA_BLOCK), lambda i, j, s: (i, j, 0, s)),
        out_shape=jax.ShapeDtypeStruct((b, h, nb, t), BF16),
        compiler_params=pltpu.CompilerParams(
            dimension_semantics=("parallel", "parallel", "parallel"), vmem_limit_bytes=VMEM_LIMIT),
    )(qt, kbar)


def _moba_attn_kernel(far_ref, qt_ref, pen_ref, k_ref, vt_ref, gt_ref, bias_ref, o_ref, sa_ref, sb_ref):
    hd = pl.program_id(1)
    own = pl.program_id(2)
    nb = pen_ref.shape[2]
    q = (qt_ref[0] * (HEAD_DIM ** -0.5)).astype(BF16)
    pen = pen_ref[0, 0]
    blk = _iota2(pen.shape, 0)
    far = far_ref[hd]
    n_near = N_BIAS_TILES - 1
    n_far = jnp.maximum(own - n_near, 0)
    far_groups = (n_far + KV_GROUP - 1) // KV_GROUP
    n_groups = far_groups + (jnp.minimum(own, n_near) + KV_GROUP - 1) // KV_GROUP

    def first_block(u):
        hi_near = own - (u - far_groups) * KV_GROUP
        j0 = jnp.where(u < far_groups, u * KV_GROUP, hi_near - KV_GROUP)
        return jnp.clip(j0, 0, nb - KV_GROUP), hi_near

    def scores(u, dst):
        is_far = u < far_groups
        j0, hi_near = first_block(u)
        hi = jnp.where(u < n_groups, jnp.where(is_far, n_far, hi_near), 0)
        lo = jnp.where(is_far, 0, hi_near - KV_GROUP)
        in_range = (blk >= lo) & (blk < hi)
        q_aug = jnp.concatenate([q, jnp.where(in_range, pen, jnp.asarray(-MASK_BIG, BF16))], axis=0)
        for i in range(KV_GROUP):
            rows = pl.ds(pl.multiple_of((j0 + i) * MOBA_BLOCK, MOBA_BLOCK), MOBA_BLOCK)
            tile = jnp.where(is_far, N_BIAS_TILES, jnp.clip(own - (j0 + i), 0, n_near))
            dst[i * MOBA_BLOCK:(i + 1) * MOBA_BLOCK, :] = _mm(k_ref[0, 0, rows, :], q_aug) + bias_ref[0, tile]

    def consume(u, src, carry):
        m, acc = carry
        shift = jnp.where(u < far_groups, far, 0.0)
        j0, _ = first_block(u)
        m_new = jnp.maximum(m, jnp.max(src[...], axis=0, keepdims=True) + shift)
        acc = acc * jnp.exp(m - m_new)
        for i in range(KV_GROUP):
            s = src[i * MOBA_BLOCK:(i + 1) * MOBA_BLOCK, :]
            acc = acc + _mm(vt_ref[0, 0, j0 + i], jnp.exp((s - (m_new - shift)).astype(BF16)))
        return m_new, acc

    k0 = k_ref[0, 0, pl.ds(pl.multiple_of(own * MOBA_BLOCK, MOBA_BLOCK), MOBA_BLOCK), :]
    s = _mm(k0[:, :HEAD_DIM], q) + bias_ref[0, 0]
    s = jnp.where(_iota2(s.shape, 0) <= _iota2(s.shape, 1), s, -MASK_BIG)
    m = jnp.max(s, axis=0, keepdims=True)
    acc = _mm(vt_ref[0, 0, own], jnp.exp((s - m).astype(BF16)))

    scores(0, sa_ref)

    def pair(v, carry):
        u = 2 * v
        scores(u + 1, sb_ref)
        carry = consume(u, sa_ref, carry)
        scores(u + 2, sa_ref)
        return consume(u + 1, sb_ref, carry)

    m, acc = lax.fori_loop(0, (n_groups + 1) // 2, pair, (m, acc))
    g = gt_ref[0]
    o_ref[0] = acc[:HEAD_DIM] / acc[HEAD_DIM:HEAD_DIM + 1] * (g * _sigmoid(g))


def _moba_attn(far, ht, pen, k_aug, vt_aug, bias):
    b, _, t = ht.shape
    h, nb = pen.shape[1], pen.shape[2]
    qs = pl.BlockSpec((1, HEAD_DIM, MOBA_BLOCK), lambda i, j, s: (i, j, s))
    gs = pl.BlockSpec((1, HEAD_DIM, MOBA_BLOCK), lambda i, j, s: (i, 2 * h + j, s))
    return pl.pallas_call(
        _moba_attn_kernel,
        grid=(b, h, nb),
        in_specs=[pl.BlockSpec(memory_space=pltpu.SMEM), qs,
                  pl.BlockSpec((1, 1, nb, MOBA_BLOCK), lambda i, j, s: (i, j, 0, s)),
                  pl.BlockSpec((1, 1) + k_aug.shape[2:], lambda i, j, s: (i, j, 0, 0)),
                  pl.BlockSpec((1, 1) + vt_aug.shape[2:], lambda i, j, s: (i, j, 0, 0, 0)),
                  gs,
                  pl.BlockSpec((1, N_BIAS_TILES + 1, MOBA_BLOCK, MOBA_BLOCK), lambda i, j, s: (j, 0, 0, 0))],
        out_specs=qs,
        out_shape=jax.ShapeDtypeStruct((b, h * HEAD_DIM, t), F32),
        scratch_shapes=[pltpu.VMEM((KV_GROUP * MOBA_BLOCK, MOBA_BLOCK), F32)] * 2,
        compiler_params=pltpu.CompilerParams(
            dimension_semantics=("parallel", "parallel", "arbitrary"), vmem_limit_bytes=VMEM_LIMIT),
    )(far, ht, pen, k_aug, vt_aug, ht, bias)


def _t5_bucket(rel):
    rel = jnp.maximum(rel, 0)
    max_exact = N_BUCKETS // 2
    rel_f = jnp.maximum(rel, 1).astype(F32)
    large = max_exact + (jnp.log(rel_f / max_exact) / math.log(MAX_DISTANCE / max_exact)
                         * (N_BUCKETS - max_exact)).astype(jnp.int32)
    large = jnp.minimum(large, N_BUCKETS - 1)
    return jnp.where(rel < max_exact, rel, large)


def _bias_tiles(rel_bias):
    d = jnp.arange(N_BIAS_TILES)[:, None, None] * MOBA_BLOCK
    rel = d + jnp.arange(MOBA_BLOCK)[None, None, :] - jnp.arange(MOBA_BLOCK)[None, :, None]
    bucket = _t5_bucket(rel)
    tab = rel_bias.T.astype(F32)
    out = jnp.zeros((tab.shape[0],) + bucket.shape, F32)
    for bk in range(N_BUCKETS):
        out = jnp.where(bucket[None] == bk, tab[:, bk][:, None, None, None], out)
    return jnp.concatenate([out, jnp.zeros_like(out[:, :1])], axis=1)


def _rwkv_prep_kernel(h_ref, hp_ref, mu_ref, wup_ref, w0_ref, aup_ref, a0_ref, kk_ref, ka_ref, rk_ref,
                      r_ref, lw_ref, k_ref, v_ref, a_ref, b_ref, bonus_ref, sg_ref):
    p = h_ref[0]
    p = p + (hp_ref[0] - p) * mu_ref[...]
    w = RWKV_W
    r, k, v, g = p[:, 0:w], p[:, w:2 * w], p[:, 2 * w:3 * w], p[:, 3 * w:4 * w]
    w_dn = p[:, 4 * w:4 * w + RWKV_LORA]
    a_dn = p[:, 4 * w + RWKV_LORA:4 * w + 2 * RWKV_LORA]
    d = w0_ref[...] + _mm(jnp.tanh(w_dn), wup_ref[...], na=2, nb=2)
    lw_ref[0] = -jnp.exp(-_softplus(-d) - 0.5)
    a = _sigmoid(a0_ref[...] + _mm(a_dn, aup_ref[...], na=2, nb=2))
    same_head = (_iota2((w, w), 0) // HEAD_DIM == _iota2((w, w), 1) // HEAD_DIM).astype(BF16)
    kk = k * kk_ref[...]
    kk = kk / jnp.maximum(jnp.sqrt(_mm(kk * kk, same_head, na=3)), 1e-12)
    k = k * (1.0 + (a - 1.0) * ka_ref[...])
    r_ref[0] = r
    k_ref[0] = k
    v_ref[0] = v
    a_ref[0] = -kk
    b_ref[0] = kk * a
    bonus_ref[0] = _mm(r * k * rk_ref[...], same_head, na=3) * v
    sg_ref[0] = g * _sigmoid(g)


def _rwkv_prep(hr, hr_prev, mu, w_up, w0, a_up, a0, k_k, k_a, r_k):
    b, t, c = hr.shape
    w = RWKV_W
    hs = pl.BlockSpec((1, ROW_BLOCK, c), lambda i, s: (i, s, 0))
    os_ = pl.BlockSpec((1, ROW_BLOCK, w), lambda i, s: (i, s, 0))
    vec = lambda n: pl.BlockSpec((1, n), lambda i, s: (0, 0))
    lora = pl.BlockSpec((RWKV_LORA, w), lambda i, s: (0, 0))
    return pl.pallas_call(
        _rwkv_prep_kernel,
        grid=(b, t // ROW_BLOCK),
        in_specs=[hs, hs, vec(c), lora, vec(w), lora, vec(w), vec(w), vec(w), vec(w)],
        out_specs=[os_] * 8,
        out_shape=[jax.ShapeDtypeStruct((b, t, w), F32)] * 8,
        compiler_params=pltpu.CompilerParams(
            dimension_semantics=("parallel", "parallel"), vmem_limit_bytes=VMEM_LIMIT),
    )(hr, hr_prev, mu, w_up, w0, a_up, a0, k_k, k_a, r_k)


def _rwkv_scan_kernel(r_ref, lw_ref, k_ref, v_ref, a_ref, b_ref, bonus_ref, sg_ref, gw_ref, gb_ref,
                      o_ref, st_ref):
    @pl.when(pl.program_id(2) == 0)
    def _():
        st_ref[...] = jnp.zeros_like(st_ref)

    nc = TIME_BLOCK // CHUNK
    shape3 = (nc, CHUNK, CHUNK)
    row = _iota2(shape3, 1)
    col = _iota2(shape3, 2)
    tri_b = (row >= col).astype(BF16)
    eye = row == col
    hp = dict()
    carry_prec = dict(na=2, nb=2)
    chunks = lambda ref: ref[0, 0].reshape(nc, CHUNK, HEAD_DIM)
    r, lw, k, v, a, b = (chunks(x) for x in (r_ref, lw_ref, k_ref, v_ref, a_ref, b_ref))
    cum = _mm(tri_b, lw, _BNN, nb=3)
    last = cum[:, CHUNK - 1:CHUNK, :]
    inv = jnp.exp(-cum)
    to_end = jnp.exp(last - cum)
    r_t = r * jnp.exp(cum)
    a_t = a * jnp.exp(cum - lw)
    k_t = k * inv
    b_t = b * inv
    a_ab = jnp.where(row > col, _mm(a_t, b_t, _BNT, **hp), 0.0)
    a_ak = jnp.where(row > col, _mm(a_t, k_t, _BNT, **hp), 0.0)
    a_rb = jnp.where(row >= col, _mm(r_t, b_t, _BNT, **hp), 0.0)
    a_rk = jnp.where(row >= col, _mm(r_t, k_t, _BNT, **hp), 0.0)
    inv_t = jnp.where(eye, 1.0, a_ab)
    power = a_ab
    for _ in range(int(math.log2(CHUNK)) - 1):
        power = _mm(power, power, _BNN, **hp)
        inv_t = inv_t + _mm(inv_t, power, _BNN, **hp)
    w_mat = _mm(inv_t, a_t, _BNN, **hp)
    u0 = _mm(inv_t, _mm(a_ak, v, _BNN, **hp), _BNN, **hp)
    y_w = r_t + _mm(a_rb, w_mat, _BNN, **hp)
    y0 = _mm(a_rb, u0, _BNN, **hp) + _mm(a_rk, v, _BNN, **hp)
    b_end = b * to_end
    k_end = k * to_end
    m_mat = jnp.where(eye, jnp.exp(last), 0.0) + _mm(b_end, w_mat, _BTN, **hp)
    n_mat = _mm(b_end, u0, _BTN, **hp) + _mm(k_end, v, _BTN, **hp)
    st = st_ref[...]
    ys = []
    for c in range(nc):
        ys.append(_mm(y_w[c], st, **hp) + y0[c])
        st = _mm(m_mat[c], st, **carry_prec) + n_mat[c]
    st_ref[...] = st
    y = jnp.concatenate(ys, axis=0)
    mean = jnp.mean(y, axis=-1, keepdims=True)
    var = jnp.mean(jnp.square(y - mean), axis=-1, keepdims=True)
    y = (y - mean) * lax.rsqrt(var + RWKV_GN_EPS) * gw_ref[0] + gb_ref[0]
    o_ref[0, 0] = (y + bonus_ref[0, 0]) * sg_ref[0, 0]


def _rwkv_scan(r, lw, k, v, a, b, bonus, sg, gn_w, gn_b):
    bsz, h, t, d = r.shape
    hs = pl.BlockSpec((1, 1, TIME_BLOCK, d), lambda i, j, s: (i, j, s, 0))
    ps = pl.BlockSpec((1, 1, d), lambda i, j, s: (j, 0, 0))
    return pl.pallas_call(
        _rwkv_scan_kernel,
        grid=(bsz, h, t // TIME_BLOCK),
        in_specs=[hs] * 8 + [ps, ps],
        out_specs=hs,
        out_shape=jax.ShapeDtypeStruct((bsz, h, t, d), F32),
        scratch_shapes=[pltpu.VMEM((d, d), F32)],
        compiler_params=pltpu.CompilerParams(
            dimension_semantics=("parallel", "parallel", "arbitrary"), vmem_limit_bytes=VMEM_LIMIT),
    )(r, lw, k, v, a, b, bonus, sg, gn_w, gn_b)


def _proj_out_kernel(alpha, og_ref, ot_ref, or_ref, w_ref, x_ref, lw_ref, lb_ref, y_ref):
    y = _mm(og_ref[0].astype(BF16), w_ref[0:GLA_W, :])
    y = y + _mm(ot_ref[0].astype(BF16), w_ref[GLA_W:GLA_W + MOBA_W, :], _TN)
    y = y + _mm(or_ref[0].astype(BF16), w_ref[GLA_W + MOBA_W:, :])
    z = alpha * x_ref[0] + y
    mu = jnp.mean(z, axis=-1, keepdims=True)
    var = jnp.mean(jnp.square(z - mu), axis=-1, keepdims=True)
    y_ref[0] = (z - mu) * lax.rsqrt(var + LN_EPS) * lw_ref[...] + lb_ref[...]


def _proj_out(o_gla, ot_moba, o_rwkv, w, x, ln_w, ln_b, alpha):
    b, t, d = x.shape
    rs = lambda n: pl.BlockSpec((1, ROW_BLOCK, n), lambda i, s: (i, s, 0))
    vs = pl.BlockSpec((1, d), lambda i, s: (0, 0))
    return pl.pallas_call(
        functools.partial(_proj_out_kernel, alpha),
        grid=(b, t // ROW_BLOCK),
        in_specs=[rs(GLA_W), pl.BlockSpec((1, MOBA_W, ROW_BLOCK), lambda i, s: (i, 0, s)), rs(RWKV_W),
                  pl.BlockSpec(w.shape, lambda i, s: (0, 0)), rs(d), vs, vs],
        out_specs=rs(d),
        out_shape=jax.ShapeDtypeStruct((b, t, d), F32),
        compiler_params=pltpu.CompilerParams(
            dimension_semantics=("parallel", "parallel"), vmem_limit_bytes=VMEM_LIMIT),
    )(o_gla, ot_moba, o_rwkv, w, x, ln_w, ln_b)


def _to_heads(t, n):
    b, s, _ = t.shape
    return t.reshape(b, s, n, HEAD_DIM).transpose(0, 2, 1, 3)


def _from_heads(t):
    b, n, s, d = t.shape
    return t.transpose(0, 2, 1, 3).reshape(b, s, n * d)


def _gla_branch(hg, a_up, a_bias, norm_w):
    q, k, v, g = (_to_heads(hg[..., i * GLA_W:(i + 1) * GLA_W], GLA_HEADS) for i in range(4))
    a_dn = hg[..., 4 * GLA_W:]
    a_up_h = a_up.reshape(GLA_LOWRANK, GLA_HEADS, HEAD_DIM).transpose(1, 0, 2)
    a_bias_h = a_bias.reshape(GLA_HEADS, 1, HEAD_DIM)
    return _from_heads(_gla(q, k, v, g, a_dn, a_up_h, a_bias_h, norm_w.reshape(1, HEAD_DIM)))


def _moba_branch(x, w_l, k_nat, far, bias):
    bsz, t, _ = x.shape
    nb = t // MOBA_BLOCK
    cols = lambda i: w_l[:, GLA_COLS + i * MOBA_W:GLA_COLS + (i + 1) * MOBA_W]
    w_t = jnp.concatenate([cols(0), cols(2), cols(3)], axis=1).T.astype(BF16)
    ht = _proj_t(x, w_t)
    k = _to_heads(k_nat, MOBA_HEADS)
    onehot = (jnp.arange(t)[:, None] // MOBA_BLOCK == jnp.arange(nb)[None, :]).astype(BF16)
    k_aug = jnp.concatenate([k.astype(BF16), jnp.broadcast_to(onehot, (bsz, MOBA_HEADS, t, nb))], axis=-1)
    vt = ht[:, MOBA_W:2 * MOBA_W].reshape(bsz, MOBA_HEADS, HEAD_DIM, nb, MOBA_BLOCK).transpose(0, 1, 3, 2, 4)
    ones = jnp.ones((bsz, MOBA_HEADS, nb, 1, MOBA_BLOCK), F32)
    zeros = jnp.zeros((bsz, MOBA_HEADS, nb, SUBLANE_BF16 - 1, MOBA_BLOCK), F32)
    vt_aug = jnp.concatenate([vt, ones, zeros], axis=3).astype(BF16)
    pen = _moba_select(ht, _moba_kbar(k))
    return _moba_attn(far, ht, pen, k_aug, vt_aug, bias)


def _rwkv_branch(hr, mu, w_up, w0, a_up, a0, k_k, k_a, r_k, gn_w, gn_b):
    hr_prev = jnp.pad(hr, ((0, 0), (1, 0), (0, 0)))[:, :-1]
    row = lambda p: p.reshape(1, -1)
    outs = _rwkv_prep(hr, hr_prev, row(mu), w_up, row(w0), a_up, row(a0), row(k_k), row(k_a), row(r_k))
    heads = [_to_heads(o, RWKV_HEADS) for o in outs]
    per_head = lambda p: p.reshape(RWKV_HEADS, 1, HEAD_DIM)
    return _from_heads(_rwkv_scan(*heads, per_head(gn_w), per_head(gn_b)))


def kernel(x, w_in, w_out, gla_a_up, gla_a_bias, gla_norm_w, moba_rel_bias, rwkv_mu, rwkv_w_up, rwkv_w0,
           rwkv_a_up, rwkv_a0, rwkv_k_k, rwkv_k_a, rwkv_r_k, rwkv_gn_w, rwkv_gn_b, ln_w, ln_b):
    bsz, t, d = x.shape
    depth = w_in.shape[0]
    alpha = (2.0 * depth) ** 0.25
    bias = _bias_tiles(moba_rel_bias)
    far = moba_rel_bias[N_BUCKETS - 1].astype(F32)
    for l in range(depth):
        w = jnp.pad(w_in[l], ((0, 0), (0, D_IN_PAD - D_IN))).astype(BF16)
        h = _proj_in(x.reshape(bsz * t, d), w).reshape(bsz, t, D_IN_PAD)
        o_gla = _gla_branch(h[..., :GLA_COLS], gla_a_up[l], gla_a_bias[l], gla_norm_w[l])
        ot_moba = _moba_branch(x, w_in[l], h[..., GLA_COLS + MOBA_W:GLA_COLS + 2 * MOBA_W], far, bias)
        o_rwkv = _rwkv_branch(h[..., GLA_COLS + MOBA_COLS:D_IN], rwkv_mu[l], rwkv_w_up[l], rwkv_w0[l],
                              rwkv_a_up[l], rwkv_a0[l], rwkv_k_k[l], rwkv_k_a[l], rwkv_r_k[l],
                              rwkv_gn_w[l], rwkv_gn_b[l])
        x = _proj_out(o_gla, ot_moba, o_rwkv, w_out[l].astype(BF16), x,
                      ln_w[l].reshape(1, d), ln_b[l].reshape(1, d), alpha)
    return x
```

```python
import functools
import math

import jax
import jax.numpy as jnp
from jax import lax
from jax.experimental import pallas as pl
from jax.experimental.pallas import tpu as pltpu

HEAD_DIM = 64
GLA_HEADS = 4
MOBA_HEADS = 8
RWKV_HEADS = 4
GLA_W = GLA_HEADS * HEAD_DIM
MOBA_W = MOBA_HEADS * HEAD_DIM
RWKV_W = RWKV_HEADS * HEAD_DIM
GLA_LOWRANK = 16
GLA_TAU = 16.0
RWKV_LORA = 32
RWKV_GN_EPS = 64e-5
MOBA_BLOCK = 256
MOBA_TOPK = 3
N_BUCKETS = 32
MAX_DISTANCE = 4096
LN_EPS = 1e-5
GLA_COLS = 4 * GLA_W + GLA_LOWRANK
MOBA_COLS = 4 * MOBA_W
RWKV_COLS = 4 * RWKV_W + 2 * RWKV_LORA

LANE = 128
SUBLANE_BF16 = 16
GLA_PAD = -(-GLA_COLS // LANE) * LANE
RWKV_PAD = -(-RWKV_COLS // LANE) * LANE
CHUNK = 64
SUB = 16
TIME_BLOCK = 512
ROW_BLOCK = 512
N_BIAS_TILES = (MAX_DISTANCE + MOBA_BLOCK - 1) // MOBA_BLOCK + 1
KV_GROUP = 4
assert (N_BIAS_TILES - 1) % KV_GROUP == 0
VMEM_LIMIT = 56 * 1024 * 1024

F32 = jnp.float32
BF16 = jnp.bfloat16
NEG_INF = float("-inf")
MASK_BIG = 2.0 ** 100


def _parts(a, n):
    out, r = [], a
    for i in range(n):
        p = r.astype(BF16)
        out.append(p)
        if i + 1 < n:
            r = r - p.astype(F32)
    return out


def _mm(a, b, dims=(((1,), (0,)), ((), ())), na=1, nb=1):
    ap = [a] if a.dtype == BF16 else _parts(a, na)
    bp = [b] if b.dtype == BF16 else _parts(b, nb)
    n = max(len(ap), len(bp))
    acc = None
    for i, x in enumerate(ap):
        for j, y in enumerate(bp):
            if i + j < n:
                t = lax.dot_general(x, y, dims, preferred_element_type=F32)
                acc = t if acc is None else acc + t
    return acc


_NT = (((1,), (1,)), ((), ()))
_TN = (((0,), (0,)), ((), ()))
_BNN = (((2,), (1,)), ((0,), (0,)))
_BNT = (((2,), (2,)), ((0,), (0,)))
_BTN = (((1,), (1,)), ((0,), (0,)))


def _softplus(x):
    return jnp.maximum(x, 0.0) + jnp.log(1.0 + jnp.exp(-jnp.abs(x)))


def _sigmoid(x):
    return 1.0 / (1.0 + jnp.exp(-x))


def _iota2(shape, dim):
    return lax.broadcasted_iota(jnp.int32, shape, dim)


def _split_heads(x, nc):
    n = x.shape[1] // HEAD_DIM
    return jnp.concatenate(
        [x[:, h * HEAD_DIM:(h + 1) * HEAD_DIM].reshape(nc, CHUNK, HEAD_DIM) for h in range(n)], axis=0)


def _merge_heads(x, n):
    nc = x.shape[0] // n
    return jnp.concatenate([x[h * nc:(h + 1) * nc].reshape(nc * CHUNK, HEAD_DIM) for h in range(n)], axis=1)


def _chunk_of(x, n, c):
    return x.reshape((n, x.shape[0] // n) + x.shape[1:])[:, c]


def _proj_in_kernel(x_ref, wg_ref, wr_ref, wk_ref, wt_ref, hg_ref, hr_ref, k_ref, ht_ref):
    xb = x_ref[0].astype(BF16)
    hg_ref[0] = jnp.dot(xb, wg_ref[...], preferred_element_type=F32)
    hr_ref[0] = jnp.dot(xb, wr_ref[...], preferred_element_type=F32)
    k_ref[0] = jnp.dot(xb, wk_ref[...], preferred_element_type=F32)
    ht_ref[0] = lax.dot_general(wt_ref[...], xb, _NT, preferred_element_type=F32)


def _proj_in(x, w_gla, w_rwkv, w_k, w_t):
    b, t, d = x.shape
    full = lambda w: pl.BlockSpec(w.shape, lambda i, s: (0, 0))
    rows = lambda n: pl.BlockSpec((1, ROW_BLOCK, n), lambda i, s: (i, s, 0))
    nt = w_t.shape[0]
    return pl.pallas_call(
        _proj_in_kernel,
        grid=(b, t // ROW_BLOCK),
        in_specs=[rows(d), full(w_gla), full(w_rwkv), full(w_k), full(w_t)],
        out_specs=[rows(w_gla.shape[1]), rows(w_rwkv.shape[1]), rows(w_k.shape[1]),
                   pl.BlockSpec((1, nt, ROW_BLOCK), lambda i, s: (i, 0, s))],
        out_shape=[jax.ShapeDtypeStruct((b, t, w_gla.shape[1]), F32),
                   jax.ShapeDtypeStruct((b, t, w_rwkv.shape[1]), F32),
                   jax.ShapeDtypeStruct((b, t, w_k.shape[1]), F32),
                   jax.ShapeDtypeStruct((b, nt, t), F32)],
        compiler_params=pltpu.CompilerParams(
            dimension_semantics=("parallel", "parallel"), vmem_limit_bytes=VMEM_LIMIT),
    )(x, w_gla, w_rwkv, w_k, w_t)


def _gla_kernel(h_ref, aup_ref, ab_ref, nw_ref, o_ref, st_ref):
    @pl.when(pl.program_id(1) == 0)
    def _():
        st_ref[...] = jnp.zeros_like(st_ref)

    nc = TIME_BLOCK // CHUNK
    nh = GLA_HEADS
    shape3 = (nh * nc, CHUNK, CHUNK)
    hb = h_ref[0]
    z = _mm(hb[:, 4 * GLA_W:4 * GLA_W + GLA_LOWRANK], aup_ref[...], na=2, nb=2) + ab_ref[...]
    la = _split_heads(-_softplus(-z) * (1.0 / GLA_TAU), nc)
    row = _iota2(shape3, 1)
    col = _iota2(shape3, 2)
    tri = (row >= col)
    anchor_b = (col <= (row // SUB) * SUB + (SUB - 1)).astype(BF16)
    q = _split_heads(hb[:, 0:GLA_W], nc) * (HEAD_DIM ** -0.5)
    k = _split_heads(hb[:, GLA_W:2 * GLA_W], nc)
    v = _split_heads(hb[:, 2 * GLA_W:3 * GLA_W], nc)
    cum = _mm(tri.astype(BF16), la, _BNN, nb=3)
    k_anchor = _mm(anchor_b, la, _BNN, nb=3)
    k_t = k * jnp.exp(k_anchor - cum)
    scores = jnp.zeros(shape3, F32)
    for j in range(CHUNK // SUB):
        a_j = cum[:, j * SUB + SUB - 1:j * SUB + SUB, :]
        q_j = q * jnp.exp(jnp.where(row >= j * SUB, cum - a_j, 0.0))
        in_grp = (row >= j * SUB) & (row < (j + 1) * SUB)
        scores = scores + _mm(q_j, jnp.where(in_grp, k_t, 0.0), _BNT)
    o_intra = _mm(jnp.where(tri, scores, 0.0), v, _BNN)
    last = cum[:, CHUNK - 1:CHUNK, :]
    n_mat = _mm(v, k * jnp.exp(last - cum), _BTN)
    q_dec = q * jnp.exp(cum)
    dec_last = jnp.exp(last)
    st = st_ref[...]
    o_inter = []
    for c in range(nc):
        o_inter.append(_mm(_chunk_of(q_dec, nh, c), st, _BNT))
        st = st * _chunk_of(dec_last, nh, c) + _chunk_of(n_mat, nh, c)
    st_ref[...] = st
    o = o_intra + jnp.stack(o_inter, axis=1).reshape(shape3)
    o = o * lax.rsqrt(jnp.mean(o * o, axis=-1, keepdims=True) + LN_EPS) * nw_ref[...]
    g = hb[:, 3 * GLA_W:4 * GLA_W]
    o_ref[0] = _merge_heads(o, nh) * (g * _sigmoid(g))


def _gla(h_gla, a_up, a_bias, norm_w):
    b, t, c = h_gla.shape
    vec = lambda a: pl.BlockSpec(a.shape, lambda i, s: (0, 0))
    return pl.pallas_call(
        _gla_kernel,
        grid=(b, t // TIME_BLOCK),
        in_specs=[pl.BlockSpec((1, TIME_BLOCK, c), lambda i, s: (i, s, 0)), vec(a_up), vec(a_bias), vec(norm_w)],
        out_specs=pl.BlockSpec((1, TIME_BLOCK, GLA_W), lambda i, s: (i, s, 0)),
        out_shape=jax.ShapeDtypeStruct((b, t, GLA_W), F32),
        scratch_shapes=[pltpu.VMEM((GLA_HEADS, HEAD_DIM, HEAD_DIM), F32)],
        compiler_params=pltpu.CompilerParams(
            dimension_semantics=("parallel", "arbitrary"), vmem_limit_bytes=VMEM_LIMIT),
    )(h_gla, a_up, a_bias, norm_w)


def _kbar_kernel(k_ref, o_ref):
    nb = o_ref.shape[2]
    for j in range(nb):
        blk = k_ref[0, 0, j * MOBA_BLOCK:(j + 1) * MOBA_BLOCK, :]
        o_ref[0, 0, j:j + 1, :] = jnp.sum(blk, axis=0, keepdims=True) * (1.0 / MOBA_BLOCK)


def _moba_kbar(k):
    b, h, t, d = k.shape
    nb = t // MOBA_BLOCK
    return pl.pallas_call(
        _kbar_kernel,
        grid=(b, h),
        in_specs=[pl.BlockSpec((1, 1, t, d), lambda i, j: (i, j, 0, 0))],
        out_specs=pl.BlockSpec((1, 1, nb, d), lambda i, j: (i, j, 0, 0)),
        out_shape=jax.ShapeDtypeStruct((b, h, nb, d), F32),
        compiler_params=pltpu.CompilerParams(
            dimension_semantics=("parallel", "parallel"), vmem_limit_bytes=VMEM_LIMIT),
    )(k)


def _block_penalty(q_t, kbar, own):
    gate = _mm(kbar, q_t, na=2, nb=2)
    blk = _iota2(gate.shape, 0)
    nb = gate.shape[0]
    gate = jnp.where(blk < own, gate, NEG_INF)
    allowed = blk < 0
    for _ in range(min(MOBA_TOPK, nb)):
        best = jnp.max(gate, axis=0, keepdims=True)
        first = jnp.min(jnp.where(gate == best, blk, nb), axis=0, keepdims=True)
        hit = blk == first
        allowed = allowed | (hit & (blk < own))
        gate = jnp.where(hit, NEG_INF, gate)
    return jnp.where(allowed, 0.0, -MASK_BIG).astype(BF16)


def _moba_attn_kernel(far_ref, qt_ref, kbar_ref, k_ref, vt_ref, gt_ref, bias_ref, o_ref, sa_ref, sb_ref):
    hd = pl.program_id(1)
    own = pl.program_id(2)
    nb = kbar_ref.shape[2]
    q = (qt_ref[0] * (HEAD_DIM ** -0.5)).astype(BF16)
    pen = _block_penalty(qt_ref[0], kbar_ref[0, 0], own)
    blk = _iota2(pen.shape, 0)
    far = far_ref[hd]
    n_near = N_BIAS_TILES - 1
    n_far = jnp.maximum(own - n_near, 0)
    far_groups = (n_far + KV_GROUP - 1) // KV_GROUP
    n_groups = far_groups + (jnp.minimum(own, n_near) + KV_GROUP - 1) // KV_GROUP

    def first_block(u):
        hi_near = own - (u - far_groups) * KV_GROUP
        j0 = jnp.where(u < far_groups, u * KV_GROUP, hi_near - KV_GROUP)
        return jnp.clip(j0, 0, nb - KV_GROUP), hi_near

    def scores(u, dst):
        is_far = u < far_groups
        j0, hi_near = first_block(u)
        hi = jnp.where(u < n_groups, jnp.where(is_far, n_far, hi_near), 0)
        lo = jnp.where(is_far, 0, hi_near - KV_GROUP)
        in_range = (blk >= lo) & (blk < hi)
        q_aug = jnp.concatenate([q, jnp.where(in_range, pen, jnp.asarray(-MASK_BIG, BF16))], axis=0)
        for i in range(KV_GROUP):
            rows = pl.ds(pl.multiple_of((j0 + i) * MOBA_BLOCK, MOBA_BLOCK), MOBA_BLOCK)
            tile = jnp.where(is_far, N_BIAS_TILES, jnp.clip(own - (j0 + i), 0, n_near))
            dst[i * MOBA_BLOCK:(i + 1) * MOBA_BLOCK, :] = _mm(k_ref[0, 0, rows, :], q_aug) + bias_ref[0, tile]

    def consume(u, src, carry):
        m, acc = carry
        shift = jnp.where(u < far_groups, far, 0.0)
        j0, _ = first_block(u)
        m_new = jnp.maximum(m, jnp.max(src[...], axis=0, keepdims=True) + shift)
        acc = acc * jnp.exp(m - m_new)
        for i in range(KV_GROUP):
            s = src[i * MOBA_BLOCK:(i + 1) * MOBA_BLOCK, :]
            acc = acc + _mm(vt_ref[0, 0, j0 + i], jnp.exp((s - (m_new - shift)).astype(BF16)))
        return m_new, acc

    k0 = k_ref[0, 0, pl.ds(pl.multiple_of(own * MOBA_BLOCK, MOBA_BLOCK), MOBA_BLOCK), :]
    s = _mm(k0[:, :HEAD_DIM], q) + bias_ref[0, 0]
    s = jnp.where(_iota2(s.shape, 0) <= _iota2(s.shape, 1), s, -MASK_BIG)
    m = jnp.max(s, axis=0, keepdims=True)
    acc = _mm(vt_ref[0, 0, own], jnp.exp((s - m).astype(BF16)))

    scores(0, sa_ref)

    def pair(v, carry):
        u = 2 * v
        scores(u + 1, sb_ref)
        carry = consume(u, sa_ref, carry)
        scores(u + 2, sa_ref)
        return consume(u + 1, sb_ref, carry)

    m, acc = lax.fori_loop(0, (n_groups + 1) // 2, pair, (m, acc))
    g = gt_ref[0]
    o_ref[0] = acc[:HEAD_DIM] / acc[HEAD_DIM:HEAD_DIM + 1] * (g * _sigmoid(g))


def _moba_attn(far, ht, kbar, k_aug, vt_aug, bias):
    b, _, t = ht.shape
    h, nb = kbar.shape[1], kbar.shape[2]
    qs = pl.BlockSpec((1, HEAD_DIM, MOBA_BLOCK), lambda i, j, s: (i, j, s))
    gs = pl.BlockSpec((1, HEAD_DIM, MOBA_BLOCK), lambda i, j, s: (i, 2 * h + j, s))
    return pl.pallas_call(
        _moba_attn_kernel,
        grid=(b, h, nb),
        in_specs=[pl.BlockSpec(memory_space=pltpu.SMEM), qs,
                  pl.BlockSpec((1, 1, nb, HEAD_DIM), lambda i, j, s: (i, j, 0, 0)),
                  pl.BlockSpec((1, 1) + k_aug.shape[2:], lambda i, j, s: (i, j, 0, 0)),
                  pl.BlockSpec((1, 1) + vt_aug.shape[2:], lambda i, j, s: (i, j, 0, 0, 0)),
                  gs,
                  pl.BlockSpec((1, N_BIAS_TILES + 1, MOBA_BLOCK, MOBA_BLOCK), lambda i, j, s: (j, 0, 0, 0))],
        out_specs=qs,
        out_shape=jax.ShapeDtypeStruct((b, h * HEAD_DIM, t), F32),
        scratch_shapes=[pltpu.VMEM((KV_GROUP * MOBA_BLOCK, MOBA_BLOCK), F32)] * 2,
        compiler_params=pltpu.CompilerParams(
            dimension_semantics=("parallel", "parallel", "arbitrary"), vmem_limit_bytes=VMEM_LIMIT),
    )(far, ht, kbar, k_aug, vt_aug, ht, bias)


def _t5_bucket(rel):
    rel = jnp.maximum(rel, 0)
    max_exact = N_BUCKETS // 2
    rel_f = jnp.maximum(rel, 1).astype(F32)
    large = max_exact + (jnp.log(rel_f / max_exact) / math.log(MAX_DISTANCE / max_exact)
                         * (N_BUCKETS - max_exact)).astype(jnp.int32)
    large = jnp.minimum(large, N_BUCKETS - 1)
    return jnp.where(rel < max_exact, rel, large)


def _bias_tiles(rel_bias):
    d = jnp.arange(N_BIAS_TILES)[:, None, None] * MOBA_BLOCK
    rel = d + jnp.arange(MOBA_BLOCK)[None, None, :] - jnp.arange(MOBA_BLOCK)[None, :, None]
    bucket = _t5_bucket(rel)
    tab = rel_bias.T.astype(F32)
    out = jnp.zeros((tab.shape[0],) + bucket.shape, F32)
    for bk in range(N_BUCKETS):
        out = jnp.where(bucket[None] == bk, tab[:, bk][:, None, None, None], out)
    return jnp.concatenate([out, jnp.zeros_like(out[:, :1])], axis=1)


def _rwkv_kernel(h_ref, mu_ref, wup_ref, w0_ref, aup_ref, a0_ref, kk_ref, ka_ref, rk_ref, gw_ref, gb_ref,
                 o_ref, st_ref, prev_ref):
    @pl.when(pl.program_id(1) == 0)
    def _():
        st_ref[...] = jnp.zeros_like(st_ref)
        prev_ref[...] = jnp.zeros_like(prev_ref)

    nc = TIME_BLOCK // CHUNK
    nh = RWKV_HEADS
    w = RWKV_W
    p = h_ref[0]
    shifted = pltpu.roll(p, 1, axis=0)
    p_prev = jnp.where(_iota2(p.shape, 0) == 0, prev_ref[...], shifted)
    prev_ref[...] = p[TIME_BLOCK - 1:TIME_BLOCK, :]
    p = p + (p_prev - p) * mu_ref[...]
    r, k, v, g = p[:, 0:w], p[:, w:2 * w], p[:, 2 * w:3 * w], p[:, 3 * w:4 * w]
    w_dn = p[:, 4 * w:4 * w + RWKV_LORA]
    a_dn = p[:, 4 * w + RWKV_LORA:4 * w + 2 * RWKV_LORA]
    d = w0_ref[...] + _mm(jnp.tanh(w_dn), wup_ref[...], na=2, nb=2)
    lw = -jnp.exp(-_softplus(-d) - 0.5)
    a = _sigmoid(a0_ref[...] + _mm(a_dn, aup_ref[...], na=2, nb=2))
    same_head = (_iota2((w, w), 0) // HEAD_DIM == _iota2((w, w), 1) // HEAD_DIM).astype(BF16)
    kk = k * kk_ref[...]
    kk = kk / jnp.maximum(jnp.sqrt(_mm(kk * kk, same_head, na=3)), 1e-12)
    k = k * (1.0 + (a - 1.0) * ka_ref[...])
    bonus = _mm(r * k * rk_ref[...], same_head, na=3) * v

    shape3 = (nh * nc, CHUNK, CHUNK)
    row = _iota2(shape3, 1)
    col = _iota2(shape3, 2)
    tri_b = (row >= col).astype(BF16)
    eye = row == col
    hp = dict()
    carry_prec = dict(na=2, nb=2)
    r, lw, k, v = (_split_heads(x, nc) for x in (r, lw, k, v))
    a_vec = _split_heads(-kk, nc)
    b_vec = _split_heads(kk * a, nc)
    cum = _mm(tri_b, lw, _BNN, nb=3)
    last = cum[:, CHUNK - 1:CHUNK, :]
    inv = jnp.exp(-cum)
    to_end = jnp.exp(last - cum)
    r_t = r * jnp.exp(cum)
    a_t = a_vec * jnp.exp(cum - lw)
    k_t = k * inv
    b_t = b_vec * inv
    a_ab = jnp.where(row > col, _mm(a_t, b_t, _BNT, **hp), 0.0)
    a_ak = jnp.where(row > col, _mm(a_t, k_t, _BNT, **hp), 0.0)
    a_rb = jnp.where(row >= col, _mm(r_t, b_t, _BNT, **hp), 0.0)
    a_rk = jnp.where(row >= col, _mm(r_t, k_t, _BNT, **hp), 0.0)
    inv_t = jnp.where(eye, 1.0, a_ab)
    power = a_ab
    for _ in range(int(math.log2(CHUNK)) - 1):
        power = _mm(power, power, _BNN, **hp)
        inv_t = inv_t + _mm(inv_t, power, _BNN, **hp)
    w_mat = _mm(inv_t, a_t, _BNN, **hp)
    u0 = _mm(inv_t, _mm(a_ak, v, _BNN, **hp), _BNN, **hp)
    y_w = r_t + _mm(a_rb, w_mat, _BNN, **hp)
    y0 = _mm(a_rb, u0, _BNN, **hp) + _mm(a_rk, v, _BNN, **hp)
    b_end = b_vec * to_end
    k_end = k * to_end
    m_mat = jnp.where(eye, jnp.exp(last), 0.0) + _mm(b_end, w_mat, _BTN, **hp)
    n_mat = _mm(b_end, u0, _BTN, **hp) + _mm(k_end, v, _BTN, **hp)
    st = st_ref[...]
    ys = []
    for c in range(nc):
        ys.append(_mm(_chunk_of(y_w, nh, c), st, _BNN, **hp))
        st = _mm(_chunk_of(m_mat, nh, c), st, _BNN, **carry_prec) + _chunk_of(n_mat, nh, c)
    st_ref[...] = st
    y = y0 + jnp.stack(ys, axis=1).reshape(shape3)
    mean = jnp.mean(y, axis=-1, keepdims=True)
    var = jnp.mean(jnp.square(y - mean), axis=-1, keepdims=True)
    y = _merge_heads((y - mean) * lax.rsqrt(var + RWKV_GN_EPS), nh) * gw_ref[...] + gb_ref[...]
    o_ref[0] = (y + bonus) * (g * _sigmoid(g))


def _rwkv(h_rwkv, mu, w_up, w0, a_up, a0, k_k, k_a, r_k, gn_w, gn_b):
    b, t, c = h_rwkv.shape
    full = lambda a: pl.BlockSpec(a.shape, lambda i, s: (0, 0))
    params = (mu, w_up, w0, a_up, a0, k_k, k_a, r_k, gn_w, gn_b)
    return pl.pallas_call(
        _rwkv_kernel,
        grid=(b, t // TIME_BLOCK),
        in_specs=[pl.BlockSpec((1, TIME_BLOCK, c), lambda i, s: (i, s, 0))] + [full(a) for a in params],
        out_specs=pl.BlockSpec((1, TIME_BLOCK, RWKV_W), lambda i, s: (i, s, 0)),
        out_shape=jax.ShapeDtypeStruct((b, t, RWKV_W), F32),
        scratch_shapes=[pltpu.VMEM((RWKV_HEADS, HEAD_DIM, HEAD_DIM), F32), pltpu.VMEM((1, c), F32)],
        compiler_params=pltpu.CompilerParams(
            dimension_semantics=("parallel", "arbitrary"), vmem_limit_bytes=VMEM_LIMIT),
    )(h_rwkv, *params)


def _proj_out_kernel(alpha, og_ref, ot_ref, or_ref, w_ref, x_ref, lw_ref, lb_ref, y_ref):
    y = _mm(og_ref[0].astype(BF16), w_ref[0:GLA_W, :])
    y = y + _mm(ot_ref[0].astype(BF16), w_ref[GLA_W:GLA_W + MOBA_W, :], _TN)
    y = y + _mm(or_ref[0].astype(BF16), w_ref[GLA_W + MOBA_W:, :])
    z = alpha * x_ref[0] + y
    mu = jnp.mean(z, axis=-1, keepdims=True)
    var = jnp.mean(jnp.square(z - mu), axis=-1, keepdims=True)
    y_ref[0] = (z - mu) * lax.rsqrt(var + LN_EPS) * lw_ref[...] + lb_ref[...]


def _proj_out(o_gla, ot_moba, o_rwkv, w, x, ln_w, ln_b, alpha):
    b, t, d = x.shape
    rs = lambda n: pl.BlockSpec((1, ROW_BLOCK, n), lambda i, s: (i, s, 0))
    vs = pl.BlockSpec((1, d), lambda i, s: (0, 0))
    return pl.pallas_call(
        functools.partial(_proj_out_kernel, alpha),
        grid=(b, t // ROW_BLOCK),
        in_specs=[rs(GLA_W), pl.BlockSpec((1, MOBA_W, ROW_BLOCK), lambda i, s: (i, 0, s)), rs(RWKV_W),
                  pl.BlockSpec(w.shape, lambda i, s: (0, 0)), rs(d), vs, vs],
        out_specs=rs(d),
        out_shape=jax.ShapeDtypeStruct((b, t, d), F32),
        compiler_params=pltpu.CompilerParams(
            dimension_semantics=("parallel", "parallel"), vmem_limit_bytes=VMEM_LIMIT),
    )(o_gla, ot_moba, o_rwkv, w, x, ln_w, ln_b)


def _pad_cols(a, n):
    return jnp.pad(a, ((0, 0), (0, n - a.shape[1])))


def _split_w_in(w_l):
    gla, moba = w_l[:, :GLA_COLS], w_l[:, GLA_COLS:GLA_COLS + MOBA_COLS]
    rwkv = w_l[:, GLA_COLS + MOBA_COLS:]
    mq, mk, mv, mg = (moba[:, i * MOBA_W:(i + 1) * MOBA_W] for i in range(4))
    w_t = jnp.concatenate([mq, mv, mg], axis=1).T
    return tuple(a.astype(BF16) for a in (_pad_cols(gla, GLA_PAD), _pad_cols(rwkv, RWKV_PAD), mk, w_t))


def _moba_operands(k_nat, ht):
    bsz, t, _ = k_nat.shape
    nb = t // MOBA_BLOCK
    k = k_nat.reshape(bsz, t, MOBA_HEADS, HEAD_DIM).transpose(0, 2, 1, 3)
    onehot = (jnp.arange(t)[:, None] // MOBA_BLOCK == jnp.arange(nb)[None, :]).astype(BF16)
    k_aug = jnp.concatenate([k.astype(BF16), jnp.broadcast_to(onehot, (bsz, MOBA_HEADS, t, nb))], axis=-1)
    vt = ht[:, MOBA_W:2 * MOBA_W].reshape(bsz, MOBA_HEADS, HEAD_DIM, nb, MOBA_BLOCK).transpose(0, 1, 3, 2, 4)
    ones = jnp.ones((bsz, MOBA_HEADS, nb, 1, MOBA_BLOCK), F32)
    zeros = jnp.zeros((bsz, MOBA_HEADS, nb, SUBLANE_BF16 - 1, MOBA_BLOCK), F32)
    vt_aug = jnp.concatenate([vt, ones, zeros], axis=3).astype(BF16)
    return k, k_aug, vt_aug


def _moba_branch(k_nat, ht, far, bias):
    k, k_aug, vt_aug = _moba_operands(k_nat, ht)
    return _moba_attn(far, ht, _moba_kbar(k), k_aug, vt_aug, bias)


def _gla_branch(h_gla, a_up, a_bias, norm_w):
    return _gla(h_gla, a_up, a_bias.reshape(1, GLA_W), norm_w.reshape(1, HEAD_DIM))


def _rwkv_branch(h_rwkv, mu, w_up, w0, a_up, a0, k_k, k_a, r_k, gn_w, gn_b):
    row = lambda p: p.reshape(1, -1)
    return _rwkv(h_rwkv, _pad_cols(row(mu), h_rwkv.shape[2]), w_up, row(w0), a_up, row(a0), row(k_k), row(k_a),
                 row(r_k), row(gn_w), row(gn_b))


def kernel(x, w_in, w_out, gla_a_up, gla_a_bias, gla_norm_w, moba_rel_bias, rwkv_mu, rwkv_w_up, rwkv_w0,
           rwkv_a_up, rwkv_a0, rwkv_k_k, rwkv_k_a, rwkv_r_k, rwkv_gn_w, rwkv_gn_b, ln_w, ln_b):
    bsz, t, d = x.shape
    depth = w_in.shape[0]
    alpha = (2.0 * depth) ** 0.25
    bias = _bias_tiles(moba_rel_bias)
    far = moba_rel_bias[N_BUCKETS - 1].astype(F32)
    for l in range(depth):
        h_gla, h_rwkv, k_nat, ht = _proj_in(x, *_split_w_in(w_in[l]))
        o_gla = _gla_branch(h_gla, gla_a_up[l], gla_a_bias[l], gla_norm_w[l])
        ot_moba = _moba_branch(k_nat, ht, far, bias)
        o_rwkv = _rwkv_branch(h_rwkv, rwkv_mu[l], rwkv_w_up[l], rwkv_w0[l], rwkv_a_up[l], rwkv_a0[l],
                              rwkv_k_k[l], rwkv_k_a[l], rwkv_r_k[l], rwkv_gn_w[l], rwkv_gn_b[l])
        x = _proj_out(o_gla, ot_moba, o_rwkv, w_out[l].astype(BF16), x,
                      ln_w[l].reshape(1, d), ln_b[l].reshape(1, d), alpha)
    return x
```

```python
import functools
import math

import jax
import jax.numpy as jnp
from jax import lax
from jax.experimental import pallas as pl
from jax.experimental.pallas import tpu as pltpu

HEAD_DIM = 64
GLA_HEADS = 4
MOBA_HEADS = 8
RWKV_HEADS = 4
GLA_W = GLA_HEADS * HEAD_DIM
MOBA_W = MOBA_HEADS * HEAD_DIM
RWKV_W = RWKV_HEADS * HEAD_DIM
GLA_LOWRANK = 16
GLA_TAU = 16.0
RWKV_LORA = 32
RWKV_GN_EPS = 64e-5
MOBA_BLOCK = 256
MOBA_TOPK = 3
N_BUCKETS = 32
MAX_DISTANCE = 4096
LN_EPS = 1e-5
GLA_COLS = 4 * GLA_W + GLA_LOWRANK
MOBA_COLS = 4 * MOBA_W
RWKV_COLS = 4 * RWKV_W + 2 * RWKV_LORA

LANE = 128
SUBLANE_BF16 = 16
GLA_PAD = -(-GLA_COLS // LANE) * LANE
RWKV_PAD = -(-RWKV_COLS // LANE) * LANE
CHUNK = 64
SUB = 16
TIME_BLOCK = 512
ROW_BLOCK = 512
N_BIAS_TILES = (MAX_DISTANCE + MOBA_BLOCK - 1) // MOBA_BLOCK + 1
KV_GROUP = 4
assert (N_BIAS_TILES - 1) % KV_GROUP == 0
VMEM_LIMIT = 56 * 1024 * 1024

F32 = jnp.float32
BF16 = jnp.bfloat16
NEG_INF = float("-inf")
MASK_BIG = 2.0 ** 100


def _parts(a, n):
    out, r = [], a
    for i in range(n):
        p = r.astype(BF16)
        out.append(p)
        if i + 1 < n:
            r = r - p.astype(F32)
    return out


def _mm(a, b, dims=(((1,), (0,)), ((), ())), na=1, nb=1):
    ap = [a] if a.dtype == BF16 else _parts(a, na)
    bp = [b] if b.dtype == BF16 else _parts(b, nb)
    n = max(len(ap), len(bp))
    acc = None
    for i, x in enumerate(ap):
        for j, y in enumerate(bp):
            if i + j < n:
                t = lax.dot_general(x, y, dims, preferred_element_type=F32)
                acc = t if acc is None else acc + t
    return acc


_NT = (((1,), (1,)), ((), ()))
_TN = (((0,), (0,)), ((), ()))
_BNN = (((2,), (1,)), ((0,), (0,)))
_BNT = (((2,), (2,)), ((0,), (0,)))
_BTN = (((1,), (1,)), ((0,), (0,)))


def _softplus(x):
    return jnp.maximum(x, 0.0) + jnp.log(1.0 + jnp.exp(-jnp.abs(x)))


def _sigmoid(x):
    return 1.0 / (1.0 + jnp.exp(-x))


def _iota2(shape, dim):
    return lax.broadcasted_iota(jnp.int32, shape, dim)


def _split_heads(x, nc):
    n = x.shape[1] // HEAD_DIM
    return jnp.concatenate(
        [x[:, h * HEAD_DIM:(h + 1) * HEAD_DIM].reshape(nc, CHUNK, HEAD_DIM) for h in range(n)], axis=0)


def _merge_heads(x, n):
    nc = x.shape[0] // n
    return jnp.concatenate([x[h * nc:(h + 1) * nc].reshape(nc * CHUNK, HEAD_DIM) for h in range(n)], axis=1)


def _chunk_of(x, n, c):
    return x.reshape((n, x.shape[0] // n) + x.shape[1:])[:, c]


def _proj_in_kernel(x_ref, wg_ref, wr_ref, wk_ref, wt_ref, hg_ref, hr_ref, k_ref, ht_ref):
    xb = x_ref[0].astype(BF16)
    hg_ref[0] = jnp.dot(xb, wg_ref[...], preferred_element_type=F32)
    hr_ref[0] = jnp.dot(xb, wr_ref[...], preferred_element_type=F32)
    k_ref[0] = jnp.dot(xb, wk_ref[...], preferred_element_type=F32)
    ht_ref[0] = lax.dot_general(wt_ref[...], xb, _NT, preferred_element_type=F32)


def _proj_in(x, w_gla, w_rwkv, w_k, w_t):
    b, t, d = x.shape
    full = lambda w: pl.BlockSpec(w.shape, lambda i, s: (0, 0))
    rows = lambda n: pl.BlockSpec((1, ROW_BLOCK, n), lambda i, s: (i, s, 0))
    nt = w_t.shape[0]
    return pl.pallas_call(
        _proj_in_kernel,
        grid=(b, t // ROW_BLOCK),
        in_specs=[rows(d), full(w_gla), full(w_rwkv), full(w_k), full(w_t)],
        out_specs=[rows(w_gla.shape[1]), rows(w_rwkv.shape[1]), rows(w_k.shape[1]),
                   pl.BlockSpec((1, nt, ROW_BLOCK), lambda i, s: (i, 0, s))],
        out_shape=[jax.ShapeDtypeStruct((b, t, w_gla.shape[1]), F32),
                   jax.ShapeDtypeStruct((b, t, w_rwkv.shape[1]), F32),
                   jax.ShapeDtypeStruct((b, t, w_k.shape[1]), F32),
                   jax.ShapeDtypeStruct((b, nt, t), F32)],
        compiler_params=pltpu.CompilerParams(
            dimension_semantics=("parallel", "parallel"), vmem_limit_bytes=VMEM_LIMIT),
    )(x, w_gla, w_rwkv, w_k, w_t)


def _gla_kernel(h_ref, aup_ref, ab_ref, nw_ref, o_ref, st_ref):
    @pl.when(pl.program_id(1) == 0)
    def _():
        st_ref[...] = jnp.zeros_like(st_ref)

    nc = TIME_BLOCK // CHUNK
    nh = GLA_HEADS
    shape3 = (nh * nc, CHUNK, CHUNK)
    hb = h_ref[0]
    z = _mm(hb[:, 4 * GLA_W:4 * GLA_W + GLA_LOWRANK], aup_ref[...], na=2, nb=2) + ab_ref[...]
    la = _split_heads(-_softplus(-z) * (1.0 / GLA_TAU), nc)
    row = _iota2(shape3, 1)
    col = _iota2(shape3, 2)
    tri = (row >= col)
    anchor_b = (col <= (row // SUB) * SUB + (SUB - 1)).astype(BF16)
    q = _split_heads(hb[:, 0:GLA_W], nc) * (HEAD_DIM ** -0.5)
    k = _split_heads(hb[:, GLA_W:2 * GLA_W], nc)
    v = _split_heads(hb[:, 2 * GLA_W:3 * GLA_W], nc)
    cum = _mm(tri.astype(BF16), la, _BNN, nb=3)
    k_anchor = _mm(anchor_b, la, _BNN, nb=3)
    k_t = k * jnp.exp(k_anchor - cum)
    scores = jnp.zeros(shape3, F32)
    for j in range(CHUNK // SUB):
        a_j = cum[:, j * SUB + SUB - 1:j * SUB + SUB, :]
        q_j = q * jnp.exp(jnp.where(row >= j * SUB, cum - a_j, 0.0))
        in_grp = (row >= j * SUB) & (row < (j + 1) * SUB)
        scores = scores + _mm(q_j, jnp.where(in_grp, k_t, 0.0), _BNT)
    o_intra = _mm(jnp.where(tri, scores, 0.0), v, _BNN)
    last = cum[:, CHUNK - 1:CHUNK, :]
    n_mat = _mm(v, k * jnp.exp(last - cum), _BTN)
    q_dec = q * jnp.exp(cum)
    dec_last = jnp.exp(last)
    st = st_ref[...]
    o_inter = []
    for c in range(nc):
        o_inter.append(_mm(_chunk_of(q_dec, nh, c), st, _BNT))
        st = st * _chunk_of(dec_last, nh, c) + _chunk_of(n_mat, nh, c)
    st_ref[...] = st
    o = o_intra + jnp.stack(o_inter, axis=1).reshape(shape3)
    o = o * lax.rsqrt(jnp.mean(o * o, axis=-1, keepdims=True) + LN_EPS) * nw_ref[...]
    g = hb[:, 3 * GLA_W:4 * GLA_W]
    o_ref[0] = _merge_heads(o, nh) * (g * _sigmoid(g))


def _gla(h_gla, a_up, a_bias, norm_w):
    b, t, c = h_gla.shape
    vec = lambda a: pl.BlockSpec(a.shape, lambda i, s: (0, 0))
    return pl.pallas_call(
        _gla_kernel,
        grid=(b, t // TIME_BLOCK),
        in_specs=[pl.BlockSpec((1, TIME_BLOCK, c), lambda i, s: (i, s, 0)), vec(a_up), vec(a_bias), vec(norm_w)],
        out_specs=pl.BlockSpec((1, TIME_BLOCK, GLA_W), lambda i, s: (i, s, 0)),
        out_shape=jax.ShapeDtypeStruct((b, t, GLA_W), F32),
        scratch_shapes=[pltpu.VMEM((GLA_HEADS, HEAD_DIM, HEAD_DIM), F32)],
        compiler_params=pltpu.CompilerParams(
            dimension_semantics=("parallel", "arbitrary"), vmem_limit_bytes=VMEM_LIMIT),
    )(h_gla, a_up, a_bias, norm_w)


def _kbar_kernel(k_ref, o_ref):
    nb = o_ref.shape[2]
    for j in range(nb):
        blk = k_ref[0, 0, j * MOBA_BLOCK:(j + 1) * MOBA_BLOCK, :]
        o_ref[0, 0, j:j + 1, :] = jnp.sum(blk, axis=0, keepdims=True) * (1.0 / MOBA_BLOCK)


def _moba_kbar(k):
    b, h, t, d = k.shape
    nb = t // MOBA_BLOCK
    return pl.pallas_call(
        _kbar_kernel,
        grid=(b, h),
        in_specs=[pl.BlockSpec((1, 1, t, d), lambda i, j: (i, j, 0, 0))],
        out_specs=pl.BlockSpec((1, 1, nb, d), lambda i, j: (i, j, 0, 0)),
        out_shape=jax.ShapeDtypeStruct((b, h, nb, d), F32),
        compiler_params=pltpu.CompilerParams(
            dimension_semantics=("parallel", "parallel"), vmem_limit_bytes=VMEM_LIMIT),
    )(k)


def _block_penalty(q_t, kbar, own):
    gate = _mm(kbar, q_t, na=2, nb=2)
    blk = _iota2(gate.shape, 0)
    nb = gate.shape[0]
    gate = jnp.where(blk < own, gate, NEG_INF)
    allowed = blk < 0
    for _ in range(min(MOBA_TOPK, nb)):
        best = jnp.max(gate, axis=0, keepdims=True)
        first = jnp.min(jnp.where(gate == best, blk, nb), axis=0, keepdims=True)
        hit = blk == first
        allowed = allowed | (hit & (blk < own))
        gate = jnp.where(hit, NEG_INF, gate)
    return jnp.where(allowed, 0.0, -MASK_BIG).astype(BF16)


def _moba_attn_kernel(far_ref, qt_ref, kbar_ref, k_ref, vt_ref, gt_ref, bias_ref, o_ref,
                      sa_ref, sb_ref, ma_ref, mb_ref):
    hd = pl.program_id(1)
    own = pl.program_id(2)
    nb = kbar_ref.shape[2]
    q = (qt_ref[0] * (HEAD_DIM ** -0.5)).astype(BF16)
    pen = _block_penalty(qt_ref[0], kbar_ref[0, 0], own)
    blk = _iota2(pen.shape, 0)
    far = far_ref[hd]
    n_near = N_BIAS_TILES - 1
    n_far = jnp.maximum(own - n_near, 0)
    far_groups = (n_far + KV_GROUP - 1) // KV_GROUP
    n_groups = far_groups + (jnp.minimum(own, n_near) + KV_GROUP - 1) // KV_GROUP

    def first_block(u):
        hi_near = own - (u - far_groups) * KV_GROUP
        j0 = jnp.where(u < far_groups, u * KV_GROUP, hi_near - KV_GROUP)
        return jnp.clip(j0, 0, nb - KV_GROUP), hi_near

    def scores(u, dst, dst_max):
        is_far = u < far_groups
        j0, hi_near = first_block(u)
        hi = jnp.where(u < n_groups, jnp.where(is_far, n_far, hi_near), 0)
        lo = jnp.where(is_far, 0, hi_near - KV_GROUP)
        in_range = (blk >= lo) & (blk < hi)
        q_aug = jnp.concatenate([q, jnp.where(in_range, pen, jnp.asarray(-MASK_BIG, BF16))], axis=0)
        top = None
        for i in range(KV_GROUP):
            rows = pl.ds(pl.multiple_of((j0 + i) * MOBA_BLOCK, MOBA_BLOCK), MOBA_BLOCK)
            tile = jnp.where(is_far, N_BIAS_TILES, jnp.clip(own - (j0 + i), 0, n_near))
            s = _mm(k_ref[0, 0, rows, :], q_aug).astype(BF16) + bias_ref[0, tile]
            dst[i * MOBA_BLOCK:(i + 1) * MOBA_BLOCK, :] = s
            tile_top = jnp.max(s, axis=0, keepdims=True)
            top = tile_top if top is None else jnp.maximum(top, tile_top)
        dst_max[...] = top.astype(F32)

    def consume(u, src, src_max, carry):
        m, acc = carry
        shift = jnp.where(u < far_groups, far, 0.0)
        j0, _ = first_block(u)
        ref_b = jnp.maximum(m - shift, src_max[...]).astype(BF16)
        m_new = ref_b.astype(F32) + shift
        acc = acc * jnp.exp(m - m_new)
        for i in range(KV_GROUP):
            s = src[i * MOBA_BLOCK:(i + 1) * MOBA_BLOCK, :]
            acc = acc + _mm(vt_ref[0, 0, j0 + i], jnp.exp(s - ref_b))
        return m_new, acc

    k0 = k_ref[0, 0, pl.ds(pl.multiple_of(own * MOBA_BLOCK, MOBA_BLOCK), MOBA_BLOCK), :]
    s = _mm(k0[:, :HEAD_DIM], q) + bias_ref[0, 0]
    s = jnp.where(_iota2(s.shape, 0) <= _iota2(s.shape, 1), s, -MASK_BIG)
    m = jnp.max(s, axis=0, keepdims=True)
    acc = _mm(vt_ref[0, 0, own], jnp.exp((s - m).astype(BF16)))

    scores(0, sa_ref, ma_ref)

    def pair(v, carry):
        u = 2 * v
        scores(u + 1, sb_ref, mb_ref)
        carry = consume(u, sa_ref, ma_ref, carry)
        scores(u + 2, sa_ref, ma_ref)
        return consume(u + 1, sb_ref, mb_ref, carry)

    m, acc = lax.fori_loop(0, (n_groups + 1) // 2, pair, (m, acc))
    g = gt_ref[0]
    o_ref[0] = acc[:HEAD_DIM] / acc[HEAD_DIM:HEAD_DIM + 1] * (g * _sigmoid(g))


def _moba_attn(far, ht, kbar, k_aug, vt_aug, bias):
    b, _, t = ht.shape
    h, nb = kbar.shape[1], kbar.shape[2]
    qs = pl.BlockSpec((1, HEAD_DIM, MOBA_BLOCK), lambda i, j, s: (i, j, s))
    gs = pl.BlockSpec((1, HEAD_DIM, MOBA_BLOCK), lambda i, j, s: (i, 2 * h + j, s))
    return pl.pallas_call(
        _moba_attn_kernel,
        grid=(b, h, nb),
        in_specs=[pl.BlockSpec(memory_space=pltpu.SMEM), qs,
                  pl.BlockSpec((1, 1, nb, HEAD_DIM), lambda i, j, s: (i, j, 0, 0)),
                  pl.BlockSpec((1, 1) + k_aug.shape[2:], lambda i, j, s: (i, j, 0, 0)),
                  pl.BlockSpec((1, 1) + vt_aug.shape[2:], lambda i, j, s: (i, j, 0, 0, 0)),
                  gs,
                  pl.BlockSpec((1, N_BIAS_TILES + 1, MOBA_BLOCK, MOBA_BLOCK), lambda i, j, s: (j, 0, 0, 0))],
        out_specs=qs,
        out_shape=jax.ShapeDtypeStruct((b, h * HEAD_DIM, t), F32),
        scratch_shapes=[pltpu.VMEM((KV_GROUP * MOBA_BLOCK, MOBA_BLOCK), BF16)] * 2
        + [pltpu.VMEM((1, MOBA_BLOCK), F32)] * 2,
        compiler_params=pltpu.CompilerParams(
            dimension_semantics=("parallel", "parallel", "arbitrary"), vmem_limit_bytes=VMEM_LIMIT),
    )(far, ht, kbar, k_aug, vt_aug, ht, bias)


def _t5_bucket(rel):
    rel = jnp.maximum(rel, 0)
    max_exact = N_BUCKETS // 2
    rel_f = jnp.maximum(rel, 1).astype(F32)
    large = max_exact + (jnp.log(rel_f / max_exact) / math.log(MAX_DISTANCE / max_exact)
                         * (N_BUCKETS - max_exact)).astype(jnp.int32)
    large = jnp.minimum(large, N_BUCKETS - 1)
    return jnp.where(rel < max_exact, rel, large)


def _bias_tiles(rel_bias):
    d = jnp.arange(N_BIAS_TILES)[:, None, None] * MOBA_BLOCK
    rel = d + jnp.arange(MOBA_BLOCK)[None, None, :] - jnp.arange(MOBA_BLOCK)[None, :, None]
    bucket = _t5_bucket(rel)
    tab = rel_bias.T.astype(F32)
    out = jnp.zeros((tab.shape[0],) + bucket.shape, F32)
    for bk in range(N_BUCKETS):
        out = jnp.where(bucket[None] == bk, tab[:, bk][:, None, None, None], out)
    return jnp.concatenate([out, jnp.zeros_like(out[:, :1])], axis=1).astype(BF16)


def _rwkv_kernel(h_ref, mu_ref, wup_ref, w0_ref, aup_ref, a0_ref, kk_ref, ka_ref, rk_ref, gw_ref, gb_ref,
                 o_ref, st_ref, prev_ref):
    @pl.when(pl.program_id(1) == 0)
    def _():
        st_ref[...] = jnp.zeros_like(st_ref)
        prev_ref[...] = jnp.zeros_like(prev_ref)

    nc = TIME_BLOCK // CHUNK
    nh = RWKV_HEADS
    w = RWKV_W
    p = h_ref[0]
    shifted = pltpu.roll(p, 1, axis=0)
    p_prev = jnp.where(_iota2(p.shape, 0) == 0, prev_ref[...], shifted)
    prev_ref[...] = p[TIME_BLOCK - 1:TIME_BLOCK, :]
    p = p + (p_prev - p) * mu_ref[...]
    r, k, v, g = p[:, 0:w], p[:, w:2 * w], p[:, 2 * w:3 * w], p[:, 3 * w:4 * w]
    w_dn = p[:, 4 * w:4 * w + RWKV_LORA]
    a_dn = p[:, 4 * w + RWKV_LORA:4 * w + 2 * RWKV_LORA]
    d = w0_ref[...] + _mm(jnp.tanh(w_dn), wup_ref[...], na=2, nb=2)
    lw = -jnp.exp(-_softplus(-d) - 0.5)
    a = _sigmoid(a0_ref[...] + _mm(a_dn, aup_ref[...], na=2, nb=2))
    same_head = (_iota2((w, w), 0) // HEAD_DIM == _iota2((w, w), 1) // HEAD_DIM).astype(BF16)
    kk = k * kk_ref[...]
    kk = kk / jnp.maximum(jnp.sqrt(_mm(kk * kk, same_head, na=3)), 1e-12)
    k = k * (1.0 + (a - 1.0) * ka_ref[...])
    bonus = _mm(r * k * rk_ref[...], same_head, na=3) * v

    shape3 = (nh * nc, CHUNK, CHUNK)
    row = _iota2(shape3, 1)
    col = _iota2(shape3, 2)
    tri_b = (row >= col).astype(BF16)
    eye = row == col
    hp = dict()
    carry_prec = dict(na=2, nb=2)
    r, lw, k, v = (_split_heads(x, nc) for x in (r, lw, k, v))
    a_vec = _split_heads(-kk, nc)
    b_vec = _split_heads(kk * a, nc)
    cum = _mm(tri_b, lw, _BNN, nb=3)
    last = cum[:, CHUNK - 1:CHUNK, :]
    inv = jnp.exp(-cum)
    to_end = jnp.exp(last - cum)
    r_t = r * jnp.exp(cum)
    a_t = a_vec * jnp.exp(cum - lw)
    k_t = k * inv
    b_t = b_vec * inv
    a_ab = jnp.where(row > col, _mm(a_t, b_t, _BNT, **hp), 0.0)
    a_ak = jnp.where(row > col, _mm(a_t, k_t, _BNT, **hp), 0.0)
    a_rb = jnp.where(row >= col, _mm(r_t, b_t, _BNT, **hp), 0.0)
    a_rk = jnp.where(row >= col, _mm(r_t, k_t, _BNT, **hp), 0.0)
    inv_t = jnp.where(eye, 1.0, a_ab)
    power = a_ab
    for _ in range(int(math.log2(CHUNK)) - 1):
        power = _mm(power, power, _BNN, **hp)
        inv_t = inv_t + _mm(inv_t, power, _BNN, **hp)
    w_mat = _mm(inv_t, a_t, _BNN, **hp)
    u0 = _mm(inv_t, _mm(a_ak, v, _BNN, **hp), _BNN, **hp)
    y_w = r_t + _mm(a_rb, w_mat, _BNN, **hp)
    y0 = _mm(a_rb, u0, _BNN, **hp) + _mm(a_rk, v, _BNN, **hp)
    b_end = b_vec * to_end
    k_end = k * to_end
    m_mat = jnp.where(eye, jnp.exp(last), 0.0) + _mm(b_end, w_mat, _BTN, **hp)
    n_mat = _mm(b_end, u0, _BTN, **hp) + _mm(k_end, v, _BTN, **hp)
    st = st_ref[...]
    ys = []
    for c in range(nc):
        ys.append(_mm(_chunk_of(y_w, nh, c), st, _BNN, **hp))
        st = _mm(_chunk_of(m_mat, nh, c), st, _BNN, **carry_prec) + _chunk_of(n_mat, nh, c)
    st_ref[...] = st
    y = y0 + jnp.stack(ys, axis=1).reshape(shape3)
    mean = jnp.mean(y, axis=-1, keepdims=True)
    var = jnp.mean(jnp.square(y - mean), axis=-1, keepdims=True)
    y = _merge_heads((y - mean) * lax.rsqrt(var + RWKV_GN_EPS), nh) * gw_ref[...] + gb_ref[...]
    o_ref[0] = (y + bonus) * (g * _sigmoid(g))


def _rwkv(h_rwkv, mu, w_up, w0, a_up, a0, k_k, k_a, r_k, gn_w, gn_b):
    b, t, c = h_rwkv.shape
    full = lambda a: pl.BlockSpec(a.shape, lambda i, s: (0, 0))
    params = (mu, w_up, w0, a_up, a0, k_k, k_a, r_k, gn_w, gn_b)
    return pl.pallas_call(
        _rwkv_kernel,
        grid=(b, t // TIME_BLOCK),
        in_specs=[pl.BlockSpec((1, TIME_BLOCK, c), lambda i, s: (i, s, 0))] + [full(a) for a in params],
        out_specs=pl.BlockSpec((1, TIME_BLOCK, RWKV_W), lambda i, s: (i, s, 0)),
        out_shape=jax.ShapeDtypeStruct((b, t, RWKV_W), F32),
        scratch_shapes=[pltpu.VMEM((RWKV_HEADS, HEAD_DIM, HEAD_DIM), F32), pltpu.VMEM((1, c), F32)],
        compiler_params=pltpu.CompilerParams(
            dimension_semantics=("parallel", "arbitrary"), vmem_limit_bytes=VMEM_LIMIT),
    )(h_rwkv, *params)


def _proj_out_kernel(alpha, og_ref, ot_ref, or_ref, w_ref, x_ref, lw_ref, lb_ref, y_ref):
    y = _mm(og_ref[0].astype(BF16), w_ref[0:GLA_W, :])
    y = y + _mm(ot_ref[0].astype(BF16), w_ref[GLA_W:GLA_W + MOBA_W, :], _TN)
    y = y + _mm(or_ref[0].astype(BF16), w_ref[GLA_W + MOBA_W:, :])
    z = alpha * x_ref[0] + y
    mu = jnp.mean(z, axis=-1, keepdims=True)
    var = jnp.mean(jnp.square(z - mu), axis=-1, keepdims=True)
    y_ref[0] = (z - mu) * lax.rsqrt(var + LN_EPS) * lw_ref[...] + lb_ref[...]


def _proj_out(o_gla, ot_moba, o_rwkv, w, x, ln_w, ln_b, alpha):
    b, t, d = x.shape
    rs = lambda n: pl.BlockSpec((1, ROW_BLOCK, n), lambda i, s: (i, s, 0))
    vs = pl.BlockSpec((1, d), lambda i, s: (0, 0))
    return pl.pallas_call(
        functools.partial(_proj_out_kernel, alpha),
        grid=(b, t // ROW_BLOCK),
        in_specs=[rs(GLA_W), pl.BlockSpec((1, MOBA_W, ROW_BLOCK), lambda i, s: (i, 0, s)), rs(RWKV_W),
                  pl.BlockSpec(w.shape, lambda i, s: (0, 0)), rs(d), vs, vs],
        out_specs=rs(d),
        out_shape=jax.ShapeDtypeStruct((b, t, d), F32),
        compiler_params=pltpu.CompilerParams(
            dimension_semantics=("parallel", "parallel"), vmem_limit_bytes=VMEM_LIMIT),
    )(o_gla, ot_moba, o_rwkv, w, x, ln_w, ln_b)


def _pad_cols(a, n):
    return jnp.pad(a, ((0, 0), (0, n - a.shape[1])))


def _split_w_in(w_l):
    gla, moba = w_l[:, :GLA_COLS], w_l[:, GLA_COLS:GLA_COLS + MOBA_COLS]
    rwkv = w_l[:, GLA_COLS + MOBA_COLS:]
    mq, mk, mv, mg = (moba[:, i * MOBA_W:(i + 1) * MOBA_W] for i in range(4))
    w_t = jnp.concatenate([mq, mv, mg], axis=1).T
    return tuple(a.astype(BF16) for a in (_pad_cols(gla, GLA_PAD), _pad_cols(rwkv, RWKV_PAD), mk, w_t))


def _moba_operands(k_nat, ht):
    bsz, t, _ = k_nat.shape
    nb = t // MOBA_BLOCK
    k = k_nat.reshape(bsz, t, MOBA_HEADS, HEAD_DIM).transpose(0, 2, 1, 3)
    onehot = (jnp.arange(t)[:, None] // MOBA_BLOCK == jnp.arange(nb)[None, :]).astype(BF16)
    k_aug = jnp.concatenate([k.astype(BF16), jnp.broadcast_to(onehot, (bsz, MOBA_HEADS, t, nb))], axis=-1)
    vt = ht[:, MOBA_W:2 * MOBA_W].reshape(bsz, MOBA_HEADS, HEAD_DIM, nb, MOBA_BLOCK).transpose(0, 1, 3, 2, 4)
    ones = jnp.ones((bsz, MOBA_HEADS, nb, 1, MOBA_BLOCK), F32)
    zeros = jnp.zeros((bsz, MOBA_HEADS, nb, SUBLANE_BF16 - 1, MOBA_BLOCK), F32)
    vt_aug = jnp.concatenate([vt, ones, zeros], axis=3).astype(BF16)
    return k, k_aug, vt_aug


def _moba_branch(k_nat, ht, far, bias):
    k, k_aug, vt_aug = _moba_operands(k_nat, ht)
    return _moba_attn(far, ht, _moba_kbar(k), k_aug, vt_aug, bias)


def _gla_branch(h_gla, a_up, a_bias, norm_w):
    return _gla(h_gla, a_up, a_bias.reshape(1, GLA_W), norm_w.reshape(1, HEAD_DIM))


def _rwkv_branch(h_rwkv, mu, w_up, w0, a_up, a0, k_k, k_a, r_k, gn_w, gn_b):
    row = lambda p: p.reshape(1, -1)
    return _rwkv(h_rwkv, _pad_cols(row(mu), h_rwkv.shape[2]), w_up, row(w0), a_up, row(a0), row(k_k), row(k_a),
                 row(r_k), row(gn_w), row(gn_b))


def kernel(x, w_in, w_out, gla_a_up, gla_a_bias, gla_norm_w, moba_rel_bias, rwkv_mu, rwkv_w_up, rwkv_w0,
           rwkv_a_up, rwkv_a0, rwkv_k_k, rwkv_k_a, rwkv_r_k, rwkv_gn_w, rwkv_gn_b, ln_w, ln_b):
    bsz, t, d = x.shape
    depth = w_in.shape[0]
    alpha = (2.0 * depth) ** 0.25
    bias = _bias_tiles(moba_rel_bias)
    far = moba_rel_bias[N_BUCKETS - 1].astype(F32)
    for l in range(depth):
        h_gla, h_rwkv, k_nat, ht = _proj_in(x, *_split_w_in(w_in[l]))
        o_gla = _gla_branch(h_gla, gla_a_up[l], gla_a_bias[l], gla_norm_w[l])
        ot_moba = _moba_branch(k_nat, ht, far, bias)
        o_rwkv = _rwkv_branch(h_rwkv, rwkv_mu[l], rwkv_w_up[l], rwkv_w0[l], rwkv_a_up[l], rwkv_a0[l],
                              rwkv_k_k[l], rwkv_k_a[l], rwkv_r_k[l], rwkv_gn_w[l], rwkv_gn_b[l])
        x = _proj_out(o_gla, ot_moba, o_rwkv, w_out[l].astype(BF16), x,
                      ln_w[l].reshape(1, d), ln_b[l].reshape(1, d), alpha)
    return x
```

```python
import functools
import math

import jax
import jax.numpy as jnp
from jax import lax
from jax.experimental import pallas as pl
from jax.experimental.pallas import tpu as pltpu

HEAD_DIM = 64
GLA_HEADS = 4
MOBA_HEADS = 8
RWKV_HEADS = 4
GLA_W = GLA_HEADS * HEAD_DIM
MOBA_W = MOBA_HEADS * HEAD_DIM
RWKV_W = RWKV_HEADS * HEAD_DIM
GLA_LOWRANK = 16
GLA_TAU = 16.0
RWKV_LORA = 32
RWKV_GN_EPS = 64e-5
MOBA_BLOCK = 256
MOBA_TOPK = 3
N_BUCKETS = 32
MAX_DISTANCE = 4096
LN_EPS = 1e-5
GLA_COLS = 4 * GLA_W + GLA_LOWRANK
MOBA_COLS = 4 * MOBA_W
RWKV_COLS = 4 * RWKV_W + 2 * RWKV_LORA

LANE = 128
SUBLANE_BF16 = 16
GLA_PAD = -(-GLA_COLS // LANE) * LANE
RWKV_PAD = -(-RWKV_COLS // LANE) * LANE
CHUNK = 64
SUB = 16
TIME_BLOCK = 512
ROW_BLOCK = 512
N_BIAS_TILES = (MAX_DISTANCE + MOBA_BLOCK - 1) // MOBA_BLOCK + 1
KV_GROUP = 4
MOBA_HEADS_PER_STEP = 2
assert (N_BIAS_TILES - 1) % KV_GROUP == 0
VMEM_LIMIT = 56 * 1024 * 1024

F32 = jnp.float32
BF16 = jnp.bfloat16
NEG_INF = float("-inf")
MASK_BIG = 2.0 ** 100


def _parts(a, n):
    out, r = [], a
    for i in range(n):
        p = r.astype(BF16)
        out.append(p)
        if i + 1 < n:
            r = r - p.astype(F32)
    return out


def _mm(a, b, dims=(((1,), (0,)), ((), ())), na=1, nb=1):
    ap = [a] if a.dtype == BF16 else _parts(a, na)
    bp = [b] if b.dtype == BF16 else _parts(b, nb)
    n = max(len(ap), len(bp))
    acc = None
    for i, x in enumerate(ap):
        for j, y in enumerate(bp):
            if i + j < n:
                t = lax.dot_general(x, y, dims, preferred_element_type=F32)
                acc = t if acc is None else acc + t
    return acc


_NT = (((1,), (1,)), ((), ()))
_TN = (((0,), (0,)), ((), ()))
_BNN = (((2,), (1,)), ((0,), (0,)))
_BNT = (((2,), (2,)), ((0,), (0,)))
_BTN = (((1,), (1,)), ((0,), (0,)))


def _softplus(x):
    return jnp.maximum(x, 0.0) + jnp.log(1.0 + jnp.exp(-jnp.abs(x)))


def _sigmoid(x):
    return 1.0 / (1.0 + jnp.exp(-x))


def _iota2(shape, dim):
    return lax.broadcasted_iota(jnp.int32, shape, dim)


def _split_heads(x, nc):
    n = x.shape[1] // HEAD_DIM
    return jnp.concatenate(
        [x[:, h * HEAD_DIM:(h + 1) * HEAD_DIM].reshape(nc, CHUNK, HEAD_DIM) for h in range(n)], axis=0)


def _merge_heads(x, n):
    nc = x.shape[0] // n
    return jnp.concatenate([x[h * nc:(h + 1) * nc].reshape(nc * CHUNK, HEAD_DIM) for h in range(n)], axis=1)


def _chunk_of(x, n, c):
    return x.reshape((n, x.shape[0] // n) + x.shape[1:])[:, c]


def _proj_in_kernel(x_ref, wg_ref, wr_ref, wk_ref, wt_ref, hg_ref, hr_ref, k_ref, ht_ref):
    xb = x_ref[0].astype(BF16)
    hg_ref[0] = jnp.dot(xb, wg_ref[...], preferred_element_type=F32)
    hr_ref[0] = jnp.dot(xb, wr_ref[...], preferred_element_type=F32)
    k_ref[0] = jnp.dot(xb, wk_ref[...], preferred_element_type=F32)
    ht_ref[0] = lax.dot_general(wt_ref[...], xb, _NT, preferred_element_type=F32)


def _proj_in(x, w_gla, w_rwkv, w_k, w_t):
    b, t, d = x.shape
    full = lambda w: pl.BlockSpec(w.shape, lambda i, s: (0, 0))
    rows = lambda n: pl.BlockSpec((1, ROW_BLOCK, n), lambda i, s: (i, s, 0))
    nt = w_t.shape[0]
    return pl.pallas_call(
        _proj_in_kernel,
        grid=(b, t // ROW_BLOCK),
        in_specs=[rows(d), full(w_gla), full(w_rwkv), full(w_k), full(w_t)],
        out_specs=[rows(w_gla.shape[1]), rows(w_rwkv.shape[1]), rows(w_k.shape[1]),
                   pl.BlockSpec((1, nt, ROW_BLOCK), lambda i, s: (i, 0, s))],
        out_shape=[jax.ShapeDtypeStruct((b, t, w_gla.shape[1]), F32),
                   jax.ShapeDtypeStruct((b, t, w_rwkv.shape[1]), F32),
                   jax.ShapeDtypeStruct((b, t, w_k.shape[1]), F32),
                   jax.ShapeDtypeStruct((b, nt, t), F32)],
        compiler_params=pltpu.CompilerParams(
            dimension_semantics=("parallel", "parallel"), vmem_limit_bytes=VMEM_LIMIT),
    )(x, w_gla, w_rwkv, w_k, w_t)


def _gla_kernel(h_ref, aup_ref, ab_ref, nw_ref, o_ref, st_ref):
    @pl.when(pl.program_id(1) == 0)
    def _():
        st_ref[...] = jnp.zeros_like(st_ref)

    nc = TIME_BLOCK // CHUNK
    nh = GLA_HEADS
    shape3 = (nh * nc, CHUNK, CHUNK)
    hb = h_ref[0]
    z = _mm(hb[:, 4 * GLA_W:4 * GLA_W + GLA_LOWRANK], aup_ref[...], na=2, nb=2) + ab_ref[...]
    la = _split_heads(-_softplus(-z) * (1.0 / GLA_TAU), nc)
    row = _iota2(shape3, 1)
    col = _iota2(shape3, 2)
    tri = (row >= col)
    anchor_b = (col <= (row // SUB) * SUB + (SUB - 1)).astype(BF16)
    q = _split_heads(hb[:, 0:GLA_W], nc) * (HEAD_DIM ** -0.5)
    k = _split_heads(hb[:, GLA_W:2 * GLA_W], nc)
    v = _split_heads(hb[:, 2 * GLA_W:3 * GLA_W], nc)
    cum = _mm(tri.astype(BF16), la, _BNN, nb=3)
    k_anchor = _mm(anchor_b, la, _BNN, nb=3)
    k_t = k * jnp.exp(k_anchor - cum)
    scores = jnp.zeros(shape3, F32)
    for j in range(CHUNK // SUB):
        a_j = cum[:, j * SUB + SUB - 1:j * SUB + SUB, :]
        q_j = q * jnp.exp(jnp.where(row >= j * SUB, cum - a_j, 0.0))
        in_grp = (row >= j * SUB) & (row < (j + 1) * SUB)
        scores = scores + _mm(q_j, jnp.where(in_grp, k_t, 0.0), _BNT)
    o_intra = _mm(jnp.where(tri, scores, 0.0), v, _BNN)
    last = cum[:, CHUNK - 1:CHUNK, :]
    n_mat = _mm(v, k * jnp.exp(last - cum), _BTN)
    q_dec = q * jnp.exp(cum)
    dec_last = jnp.exp(last)
    st = st_ref[...]
    o_inter = []
    for c in range(nc):
        o_inter.append(_mm(_chunk_of(q_dec, nh, c), st, _BNT))
        st = st * _chunk_of(dec_last, nh, c) + _chunk_of(n_mat, nh, c)
    st_ref[...] = st
    o = o_intra + jnp.stack(o_inter, axis=1).reshape(shape3)
    o = o * lax.rsqrt(jnp.mean(o * o, axis=-1, keepdims=True) + LN_EPS) * nw_ref[...]
    g = hb[:, 3 * GLA_W:4 * GLA_W]
    o_ref[0] = _merge_heads(o, nh) * (g * _sigmoid(g))


def _gla(h_gla, a_up, a_bias, norm_w):
    b, t, c = h_gla.shape
    vec = lambda a: pl.BlockSpec(a.shape, lambda i, s: (0, 0))
    return pl.pallas_call(
        _gla_kernel,
        grid=(b, t // TIME_BLOCK),
        in_specs=[pl.BlockSpec((1, TIME_BLOCK, c), lambda i, s: (i, s, 0)), vec(a_up), vec(a_bias), vec(norm_w)],
        out_specs=pl.BlockSpec((1, TIME_BLOCK, GLA_W), lambda i, s: (i, s, 0)),
        out_shape=jax.ShapeDtypeStruct((b, t, GLA_W), F32),
        scratch_shapes=[pltpu.VMEM((GLA_HEADS, HEAD_DIM, HEAD_DIM), F32)],
        compiler_params=pltpu.CompilerParams(
            dimension_semantics=("parallel", "arbitrary"), vmem_limit_bytes=VMEM_LIMIT),
    )(h_gla, a_up, a_bias, norm_w)


def _kbar_kernel(k_ref, o_ref):
    nb = o_ref.shape[2]
    for j in range(nb):
        blk = k_ref[0, 0, j * MOBA_BLOCK:(j + 1) * MOBA_BLOCK, :]
        o_ref[0, 0, j:j + 1, :] = jnp.sum(blk, axis=0, keepdims=True) * (1.0 / MOBA_BLOCK)


def _moba_kbar(k):
    b, h, t, d = k.shape
    nb = t // MOBA_BLOCK
    return pl.pallas_call(
        _kbar_kernel,
        grid=(b, h),
        in_specs=[pl.BlockSpec((1, 1, t, d), lambda i, j: (i, j, 0, 0))],
        out_specs=pl.BlockSpec((1, 1, nb, d), lambda i, j: (i, j, 0, 0)),
        out_shape=jax.ShapeDtypeStruct((b, h, nb, d), F32),
        compiler_params=pltpu.CompilerParams(
            dimension_semantics=("parallel", "parallel"), vmem_limit_bytes=VMEM_LIMIT),
    )(k)


def _block_penalty(q_t, kbar, own):
    gate = _mm(kbar, q_t, na=2, nb=2)
    blk = _iota2(gate.shape, 0)
    nb = gate.shape[0]
    gate = jnp.where(blk < own, gate, NEG_INF)
    allowed = blk < 0
    for _ in range(min(MOBA_TOPK, nb)):
        best = jnp.max(gate, axis=0, keepdims=True)
        first = jnp.min(jnp.where(gate == best, blk, nb), axis=0, keepdims=True)
        hit = blk == first
        allowed = allowed | (hit & (blk < own))
        gate = jnp.where(hit, NEG_INF, gate)
    return jnp.where(allowed, 0.0, -MASK_BIG).astype(BF16)


def _moba_attn_kernel(qt_ref, kbar_ref, k_ref, vt_ref, gt_ref, bias_ref, o_ref, *scratch):
    hps = MOBA_HEADS_PER_STEP
    own = pl.program_id(2)
    nb = kbar_ref.shape[2]
    n_near = N_BIAS_TILES - 1
    n_far = jnp.maximum(own - n_near, 0)
    far_groups = (n_far + KV_GROUP - 1) // KV_GROUP
    n_groups = far_groups + (jnp.minimum(own, n_near) + KV_GROUP - 1) // KV_GROUP
    head_rows = lambda hh: slice(hh * HEAD_DIM, (hh + 1) * HEAD_DIM)
    q = [(qt_ref[0, head_rows(hh), :] * (HEAD_DIM ** -0.5)).astype(BF16) for hh in range(hps)]
    pen = [_block_penalty(qt_ref[0, head_rows(hh), :], kbar_ref[0, hh], own) for hh in range(hps)]
    blk = _iota2(pen[0].shape, 0)
    sa, sb, ma, mb = (scratch[i * hps:(i + 1) * hps] for i in range(4))

    def first_block(u):
        hi_near = own - (u - far_groups) * KV_GROUP
        j0 = jnp.where(u < far_groups, u * KV_GROUP, hi_near - KV_GROUP)
        return jnp.clip(j0, 0, nb - KV_GROUP), hi_near

    def scores(u, hh, dst, dst_max):
        is_far = u < far_groups
        j0, hi_near = first_block(u)
        hi = jnp.where(u < n_groups, jnp.where(is_far, n_far, hi_near), 0)
        lo = jnp.where(is_far, 0, hi_near - KV_GROUP)
        in_range = (blk >= lo) & (blk < hi)
        q_aug = jnp.concatenate([q[hh], jnp.where(in_range, pen[hh], jnp.asarray(-MASK_BIG, BF16))], axis=0)
        top = None
        for i in range(KV_GROUP):
            rows = pl.ds(pl.multiple_of((j0 + i) * MOBA_BLOCK, MOBA_BLOCK), MOBA_BLOCK)
            tile = jnp.where(is_far, N_BIAS_TILES, jnp.clip(own - (j0 + i), 0, n_near))
            s = _mm(k_ref[0, hh, rows, :], q_aug).astype(BF16) + bias_ref[hh, tile]
            dst[i * MOBA_BLOCK:(i + 1) * MOBA_BLOCK, :] = s
            tile_top = jnp.max(s, axis=0, keepdims=True)
            top = tile_top if top is None else jnp.maximum(top, tile_top)
        dst_max[...] = top.astype(F32)

    def consume(u, hh, src, src_max, carry):
        m, acc = carry
        j0, _ = first_block(u)
        ref_b = jnp.maximum(m, src_max[...]).astype(BF16)
        m_new = ref_b.astype(F32)
        acc = acc * jnp.exp(m - m_new)
        for i in range(KV_GROUP):
            s = src[i * MOBA_BLOCK:(i + 1) * MOBA_BLOCK, :]
            acc = acc + _mm(vt_ref[0, hh, j0 + i], jnp.exp(s - ref_b))
        return m_new, acc

    def own_block(hh):
        k0 = k_ref[0, hh, pl.ds(pl.multiple_of(own * MOBA_BLOCK, MOBA_BLOCK), MOBA_BLOCK), :]
        s = _mm(k0[:, :HEAD_DIM], q[hh]) + bias_ref[hh, 0]
        s = jnp.where(_iota2(s.shape, 0) <= _iota2(s.shape, 1), s, -MASK_BIG)
        m = jnp.max(s, axis=0, keepdims=True)
        return m, _mm(vt_ref[0, hh, own], jnp.exp((s - m).astype(BF16)))

    carry = tuple(own_block(hh) for hh in range(hps))
    for hh in range(hps):
        scores(0, hh, sa[hh], ma[hh])

    def pair(v, carry):
        u = 2 * v
        for hh in range(hps):
            scores(u + 1, hh, sb[hh], mb[hh])
        carry = tuple(consume(u, hh, sa[hh], ma[hh], carry[hh]) for hh in range(hps))
        for hh in range(hps):
            scores(u + 2, hh, sa[hh], ma[hh])
        return tuple(consume(u + 1, hh, sb[hh], mb[hh], carry[hh]) for hh in range(hps))

    carry = lax.fori_loop(0, (n_groups + 1) // 2, pair, carry)
    for hh in range(hps):
        _, acc = carry[hh]
        g = gt_ref[0, head_rows(hh), :]
        o_ref[0, head_rows(hh), :] = acc[:HEAD_DIM] / acc[HEAD_DIM:HEAD_DIM + 1] * (g * _sigmoid(g))


def _moba_attn(ht, kbar, k_aug, vt_aug, bias):
    b, _, t = ht.shape
    h, nb = kbar.shape[1], kbar.shape[2]
    hps = MOBA_HEADS_PER_STEP
    rows = hps * HEAD_DIM
    qs = pl.BlockSpec((1, rows, MOBA_BLOCK), lambda i, j, s: (i, j, s))
    gs = pl.BlockSpec((1, rows, MOBA_BLOCK), lambda i, j, s: (i, 2 * (h // hps) + j, s))
    heads = lambda a: pl.BlockSpec((1, hps) + a.shape[2:], lambda i, j, s: (i, j) + (0,) * (a.ndim - 2))
    return pl.pallas_call(
        _moba_attn_kernel,
        grid=(b, h // hps, nb),
        in_specs=[qs, heads(kbar), heads(k_aug), heads(vt_aug), gs,
                  pl.BlockSpec((hps,) + bias.shape[1:], lambda i, j, s: (j, 0, 0, 0))],
        out_specs=qs,
        out_shape=jax.ShapeDtypeStruct((b, h * HEAD_DIM, t), F32),
        scratch_shapes=[pltpu.VMEM((KV_GROUP * MOBA_BLOCK, MOBA_BLOCK), BF16)] * (2 * hps)
        + [pltpu.VMEM((1, MOBA_BLOCK), F32)] * (2 * hps),
        compiler_params=pltpu.CompilerParams(
            dimension_semantics=("parallel", "parallel", "arbitrary"), vmem_limit_bytes=VMEM_LIMIT),
    )(ht, kbar, k_aug, vt_aug, ht, bias)


def _t5_bucket(rel):
    rel = jnp.maximum(rel, 0)
    max_exact = N_BUCKETS // 2
    rel_f = jnp.maximum(rel, 1).astype(F32)
    large = max_exact + (jnp.log(rel_f / max_exact) / math.log(MAX_DISTANCE / max_exact)
                         * (N_BUCKETS - max_exact)).astype(jnp.int32)
    large = jnp.minimum(large, N_BUCKETS - 1)
    return jnp.where(rel < max_exact, rel, large)


def _bias_tiles(rel_bias):
    d = jnp.arange(N_BIAS_TILES)[:, None, None] * MOBA_BLOCK
    rel = d + jnp.arange(MOBA_BLOCK)[None, None, :] - jnp.arange(MOBA_BLOCK)[None, :, None]
    bucket = _t5_bucket(rel)
    tab = rel_bias.T.astype(F32)
    out = jnp.zeros((tab.shape[0],) + bucket.shape, F32)
    for bk in range(N_BUCKETS):
        out = jnp.where(bucket[None] == bk, tab[:, bk][:, None, None, None], out)
    far = jnp.broadcast_to(tab[:, N_BUCKETS - 1][:, None, None, None], out[:, :1].shape)
    return jnp.concatenate([out, far], axis=1).astype(BF16)


def _rwkv_kernel(h_ref, mu_ref, wup_ref, w0_ref, aup_ref, a0_ref, kk_ref, ka_ref, rk_ref, gw_ref, gb_ref,
                 o_ref, st_ref, prev_ref):
    @pl.when(pl.program_id(1) == 0)
    def _():
        st_ref[...] = jnp.zeros_like(st_ref)
        prev_ref[...] = jnp.zeros_like(prev_ref)

    nc = TIME_BLOCK // CHUNK
    nh = RWKV_HEADS
    w = RWKV_W
    p = h_ref[0]
    shifted = pltpu.roll(p, 1, axis=0)
    p_prev = jnp.where(_iota2(p.shape, 0) == 0, prev_ref[...], shifted)
    prev_ref[...] = p[TIME_BLOCK - 1:TIME_BLOCK, :]
    p = p + (p_prev - p) * mu_ref[...]
    r, k, v, g = p[:, 0:w], p[:, w:2 * w], p[:, 2 * w:3 * w], p[:, 3 * w:4 * w]
    w_dn = p[:, 4 * w:4 * w + RWKV_LORA]
    a_dn = p[:, 4 * w + RWKV_LORA:4 * w + 2 * RWKV_LORA]
    d = w0_ref[...] + _mm(jnp.tanh(w_dn), wup_ref[...], na=2, nb=2)
    lw = -jnp.exp(-_softplus(-d) - 0.5)
    a = _sigmoid(a0_ref[...] + _mm(a_dn, aup_ref[...], na=2, nb=2))
    same_head = (_iota2((w, w), 0) // HEAD_DIM == _iota2((w, w), 1) // HEAD_DIM).astype(BF16)
    kk = k * kk_ref[...]
    kk = kk / jnp.maximum(jnp.sqrt(_mm(kk * kk, same_head, na=3)), 1e-12)
    k = k * (1.0 + (a - 1.0) * ka_ref[...])
    bonus = _mm(r * k * rk_ref[...], same_head, na=3) * v

    shape3 = (nh * nc, CHUNK, CHUNK)
    row = _iota2(shape3, 1)
    col = _iota2(shape3, 2)
    tri_b = (row >= col).astype(BF16)
    eye = row == col
    hp = dict()
    carry_prec = dict(na=2, nb=2)
    r, lw, k, v = (_split_heads(x, nc) for x in (r, lw, k, v))
    a_vec = _split_heads(-kk, nc)
    b_vec = _split_heads(kk * a, nc)
    cum = _mm(tri_b, lw, _BNN, nb=3)
    last = cum[:, CHUNK - 1:CHUNK, :]
    inv = jnp.exp(-cum)
    to_end = jnp.exp(last - cum)
    r_t = r * jnp.exp(cum)
    a_t = a_vec * jnp.exp(cum - lw)
    k_t = k * inv
    b_t = b_vec * inv
    a_ab = jnp.where(row > col, _mm(a_t, b_t, _BNT, **hp), 0.0)
    a_ak = jnp.where(row > col, _mm(a_t, k_t, _BNT, **hp), 0.0)
    a_rb = jnp.where(row >= col, _mm(r_t, b_t, _BNT, **hp), 0.0)
    a_rk = jnp.where(row >= col, _mm(r_t, k_t, _BNT, **hp), 0.0)
    inv_t = jnp.where(eye, 1.0, a_ab)
    power = a_ab
    for _ in range(int(math.log2(CHUNK)) - 1):
        power = _mm(power, power, _BNN, **hp)
        inv_t = inv_t + _mm(inv_t, power, _BNN, **hp)
    w_mat = _mm(inv_t, a_t, _BNN, **hp)
    u0 = _mm(inv_t, _mm(a_ak, v, _BNN, **hp), _BNN, **hp)
    y_w = r_t + _mm(a_rb, w_mat, _BNN, **hp)
    y0 = _mm(a_rb, u0, _BNN, **hp) + _mm(a_rk, v, _BNN, **hp)
    b_end = b_vec * to_end
    k_end = k * to_end
    m_mat = jnp.where(eye, jnp.exp(last), 0.0) + _mm(b_end, w_mat, _BTN, **hp)
    n_mat = _mm(b_end, u0, _BTN, **hp) + _mm(k_end, v, _BTN, **hp)
    st = st_ref[...]
    ys = []
    for c in range(nc):
        ys.append(_mm(_chunk_of(y_w, nh, c), st, _BNN, **hp))
        st = _mm(_chunk_of(m_mat, nh, c), st, _BNN, **carry_prec) + _chunk_of(n_mat, nh, c)
    st_ref[...] = st
    y = y0 + jnp.stack(ys, axis=1).reshape(shape3)
    mean = jnp.mean(y, axis=-1, keepdims=True)
    var = jnp.mean(jnp.square(y - mean), axis=-1, keepdims=True)
    y = _merge_heads((y - mean) * lax.rsqrt(var + RWKV_GN_EPS), nh) * gw_ref[...] + gb_ref[...]
    o_ref[0] = (y + bonus) * (g * _sigmoid(g))


def _rwkv(h_rwkv, mu, w_up, w0, a_up, a0, k_k, k_a, r_k, gn_w, gn_b):
    b, t, c = h_rwkv.shape
    full = lambda a: pl.BlockSpec(a.shape, lambda i, s: (0, 0))
    params = (mu, w_up, w0, a_up, a0, k_k, k_a, r_k, gn_w, gn_b)
    return pl.pallas_call(
        _rwkv_kernel,
        grid=(b, t // TIME_BLOCK),
        in_specs=[pl.BlockSpec((1, TIME_BLOCK, c), lambda i, s: (i, s, 0))] + [full(a) for a in params],
        out_specs=pl.BlockSpec((1, TIME_BLOCK, RWKV_W), lambda i, s: (i, s, 0)),
        out_shape=jax.ShapeDtypeStruct((b, t, RWKV_W), F32),
        scratch_shapes=[pltpu.VMEM((RWKV_HEADS, HEAD_DIM, HEAD_DIM), F32), pltpu.VMEM((1, c), F32)],
        compiler_params=pltpu.CompilerParams(
            dimension_semantics=("parallel", "arbitrary"), vmem_limit_bytes=VMEM_LIMIT),
    )(h_rwkv, *params)


def _proj_out_kernel(alpha, og_ref, ot_ref, or_ref, w_ref, x_ref, lw_ref, lb_ref, y_ref):
    y = _mm(og_ref[0].astype(BF16), w_ref[0:GLA_W, :])
    y = y + _mm(ot_ref[0].astype(BF16), w_ref[GLA_W:GLA_W + MOBA_W, :], _TN)
    y = y + _mm(or_ref[0].astype(BF16), w_ref[GLA_W + MOBA_W:, :])
    z = alpha * x_ref[0] + y
    mu = jnp.mean(z, axis=-1, keepdims=True)
    var = jnp.mean(jnp.square(z - mu), axis=-1, keepdims=True)
    y_ref[0] = (z - mu) * lax.rsqrt(var + LN_EPS) * lw_ref[...] + lb_ref[...]


def _proj_out(o_gla, ot_moba, o_rwkv, w, x, ln_w, ln_b, alpha):
    b, t, d = x.shape
    rs = lambda n: pl.BlockSpec((1, ROW_BLOCK, n), lambda i, s: (i, s, 0))
    vs = pl.BlockSpec((1, d), lambda i, s: (0, 0))
    return pl.pallas_call(
        functools.partial(_proj_out_kernel, alpha),
        grid=(b, t // ROW_BLOCK),
        in_specs=[rs(GLA_W), pl.BlockSpec((1, MOBA_W, ROW_BLOCK), lambda i, s: (i, 0, s)), rs(RWKV_W),
                  pl.BlockSpec(w.shape, lambda i, s: (0, 0)), rs(d), vs, vs],
        out_specs=rs(d),
        out_shape=jax.ShapeDtypeStruct((b, t, d), F32),
        compiler_params=pltpu.CompilerParams(
            dimension_semantics=("parallel", "parallel"), vmem_limit_bytes=VMEM_LIMIT),
    )(o_gla, ot_moba, o_rwkv, w, x, ln_w, ln_b)


def _pad_cols(a, n):
    return jnp.pad(a, ((0, 0), (0, n - a.shape[1])))


def _split_w_in(w_l):
    gla, moba = w_l[:, :GLA_COLS], w_l[:, GLA_COLS:GLA_COLS + MOBA_COLS]
    rwkv = w_l[:, GLA_COLS + MOBA_COLS:]
    mq, mk, mv, mg = (moba[:, i * MOBA_W:(i + 1) * MOBA_W] for i in range(4))
    w_t = jnp.concatenate([mq, mv, mg], axis=1).T
    return tuple(a.astype(BF16) for a in (_pad_cols(gla, GLA_PAD), _pad_cols(rwkv, RWKV_PAD), mk, w_t))


def _moba_operands(k_nat, ht):
    bsz, t, _ = k_nat.shape
    nb = t // MOBA_BLOCK
    k = k_nat.reshape(bsz, t, MOBA_HEADS, HEAD_DIM).transpose(0, 2, 1, 3)
    onehot = (jnp.arange(t)[:, None] // MOBA_BLOCK == jnp.arange(nb)[None, :]).astype(BF16)
    k_aug = jnp.concatenate([k.astype(BF16), jnp.broadcast_to(onehot, (bsz, MOBA_HEADS, t, nb))], axis=-1)
    vt = ht[:, MOBA_W:2 * MOBA_W].reshape(bsz, MOBA_HEADS, HEAD_DIM, nb, MOBA_BLOCK).transpose(0, 1, 3, 2, 4)
    ones = jnp.ones((bsz, MOBA_HEADS, nb, 1, MOBA_BLOCK), F32)
    zeros = jnp.zeros((bsz, MOBA_HEADS, nb, SUBLANE_BF16 - 1, MOBA_BLOCK), F32)
    vt_aug = jnp.concatenate([vt, ones, zeros], axis=3).astype(BF16)
    return k, k_aug, vt_aug


def _moba_branch(k_nat, ht, bias):
    k, k_aug, vt_aug = _moba_operands(k_nat, ht)
    return _moba_attn(ht, _moba_kbar(k), k_aug, vt_aug, bias)


def _gla_branch(h_gla, a_up, a_bias, norm_w):
    return _gla(h_gla, a_up, a_bias.reshape(1, GLA_W), norm_w.reshape(1, HEAD_DIM))


def _rwkv_branch(h_rwkv, mu, w_up, w0, a_up, a0, k_k, k_a, r_k, gn_w, gn_b):
    row = lambda p: p.reshape(1, -1)
    return _rwkv(h_rwkv, _pad_cols(row(mu), h_rwkv.shape[2]), w_up, row(w0), a_up, row(a0), row(k_k), row(k_a),
                 row(r_k), row(gn_w), row(gn_b))


def kernel(x, w_in, w_out, gla_a_up, gla_a_bias, gla_norm_w, moba_rel_bias, rwkv_mu, rwkv_w_up, rwkv_w0,
           rwkv_a_up, rwkv_a0, rwkv_k_k, rwkv_k_a, rwkv_r_k, rwkv_gn_w, rwkv_gn_b, ln_w, ln_b):
    bsz, t, d = x.shape
    depth = w_in.shape[0]
    alpha = (2.0 * depth) ** 0.25
    bias = _bias_tiles(moba_rel_bias)
    for l in range(depth):
        h_gla, h_rwkv, k_nat, ht = _proj_in(x, *_split_w_in(w_in[l]))
        o_gla = _gla_branch(h_gla, gla_a_up[l], gla_a_bias[l], gla_norm_w[l])
        ot_moba = _moba_branch(k_nat, ht, bias)
        o_rwkv = _rwkv_branch(h_rwkv, rwkv_mu[l], rwkv_w_up[l], rwkv_w0[l], rwkv_a_up[l], rwkv_a0[l],
                              rwkv_k_k[l], rwkv_k_a[l], rwkv_r_k[l], rwkv_gn_w[l], rwkv_gn_b[l])
        x = _proj_out(o_gla, ot_moba, o_rwkv, w_out[l].astype(BF16), x,
                      ln_w[l].reshape(1, d), ln_b[l].reshape(1, d), alpha)
    return x
```

```python
import functools
import math

import jax
import jax.numpy as jnp
from jax import lax
from jax.experimental import pallas as pl
from jax.experimental.pallas import tpu as pltpu

HEAD_DIM = 64
GLA_HEADS = 4
MOBA_HEADS = 8
RWKV_HEADS = 4
GLA_W = GLA_HEADS * HEAD_DIM
MOBA_W = MOBA_HEADS * HEAD_DIM
RWKV_W = RWKV_HEADS * HEAD_DIM
GLA_LOWRANK = 16
GLA_TAU = 16.0
RWKV_LORA = 32
RWKV_GN_EPS = 64e-5
MOBA_BLOCK = 256
MOBA_TOPK = 3
N_BUCKETS = 32
MAX_DISTANCE = 4096
LN_EPS = 1e-5
GLA_COLS = 4 * GLA_W + GLA_LOWRANK
MOBA_COLS = 4 * MOBA_W
RWKV_COLS = 4 * RWKV_W + 2 * RWKV_LORA

LANE = 128
SUBLANE_BF16 = 16
GLA_PAD = -(-GLA_COLS // LANE) * LANE
RWKV_PAD = -(-RWKV_COLS // LANE) * LANE
CHUNK = 64
SUB = 16
TIME_BLOCK = 512
ROW_BLOCK = 512
N_BIAS_TILES = (MAX_DISTANCE + MOBA_BLOCK - 1) // MOBA_BLOCK + 1
KV_GROUP = 4
MOBA_HEADS_PER_STEP = 2
assert (N_BIAS_TILES - 1) % KV_GROUP == 0
VMEM_LIMIT = 56 * 1024 * 1024

F32 = jnp.float32
BF16 = jnp.bfloat16
NEG_INF = float("-inf")
MASK_BIG = 2.0 ** 100


def _parts(a, n):
    out, r = [], a
    for i in range(n):
        p = r.astype(BF16)
        out.append(p)
        if i + 1 < n:
            r = r - p.astype(F32)
    return out


def _mm(a, b, dims=(((1,), (0,)), ((), ())), na=1, nb=1):
    ap = [a] if a.dtype == BF16 else _parts(a, na)
    bp = [b] if b.dtype == BF16 else _parts(b, nb)
    n = max(len(ap), len(bp))
    acc = None
    for i, x in enumerate(ap):
        for j, y in enumerate(bp):
            if i + j < n:
                t = lax.dot_general(x, y, dims, preferred_element_type=F32)
                acc = t if acc is None else acc + t
    return acc


_NT = (((1,), (1,)), ((), ()))
_TN = (((0,), (0,)), ((), ()))
_BNN = (((2,), (1,)), ((0,), (0,)))
_BNT = (((2,), (2,)), ((0,), (0,)))
_BTN = (((1,), (1,)), ((0,), (0,)))


def _softplus(x):
    return jnp.maximum(x, 0.0) + jnp.log(1.0 + jnp.exp(-jnp.abs(x)))


def _sigmoid(x):
    return 1.0 / (1.0 + jnp.exp(-x))


def _iota2(shape, dim):
    return lax.broadcasted_iota(jnp.int32, shape, dim)


def _split_heads(x, nc):
    n = x.shape[1] // HEAD_DIM
    return jnp.concatenate(
        [x[:, h * HEAD_DIM:(h + 1) * HEAD_DIM].reshape(nc, CHUNK, HEAD_DIM) for h in range(n)], axis=0)


def _merge_heads(x, n):
    nc = x.shape[0] // n
    return jnp.concatenate([x[h * nc:(h + 1) * nc].reshape(nc * CHUNK, HEAD_DIM) for h in range(n)], axis=1)


def _chunk_of(x, n, c):
    return x.reshape((n, x.shape[0] // n) + x.shape[1:])[:, c]


def _proj_in_kernel(x_ref, wg_ref, wr_ref, wk_ref, wt_ref, wv_ref,
                    hg_ref, hr_ref, ht_ref, kaug_ref, kbar_ref, vt_ref):
    nb = kaug_ref.shape[3] - HEAD_DIM
    blocks = ROW_BLOCK // MOBA_BLOCK
    xb = x_ref[0].astype(BF16)
    hg_ref[0] = jnp.dot(xb, wg_ref[...], preferred_element_type=F32)
    hr_ref[0] = jnp.dot(xb, wr_ref[...], preferred_element_type=F32)
    ht_ref[0] = lax.dot_general(wt_ref[...], xb, _NT, preferred_element_type=F32)
    k = jnp.dot(xb, wk_ref[...], preferred_element_type=F32)
    first = pl.program_id(1) * blocks
    onehot = (_iota2((ROW_BLOCK, nb), 1) == first + _iota2((ROW_BLOCK, nb), 0) // MOBA_BLOCK).astype(BF16)
    for h in range(MOBA_HEADS):
        kaug_ref[0, h] = jnp.concatenate([k[:, h * HEAD_DIM:(h + 1) * HEAD_DIM].astype(BF16), onehot], axis=1)
    for i in range(blocks):
        kbar_ref[0, 0, i:i + 1, :] = jnp.sum(k[i * MOBA_BLOCK:(i + 1) * MOBA_BLOCK], axis=0,
                                             keepdims=True) * (1.0 / MOBA_BLOCK)
    vt = lax.dot_general(wv_ref[...], xb, _NT, preferred_element_type=F32)
    pad = jnp.concatenate([jnp.ones((1, MOBA_BLOCK), F32), jnp.zeros((SUBLANE_BF16 - 1, MOBA_BLOCK), F32)], axis=0)
    for h in range(MOBA_HEADS):
        for i in range(blocks):
            tile = vt[h * HEAD_DIM:(h + 1) * HEAD_DIM, i * MOBA_BLOCK:(i + 1) * MOBA_BLOCK]
            vt_ref[0, h, i] = jnp.concatenate([tile, pad], axis=0).astype(BF16)


def _proj_in(x, w_gla, w_rwkv, w_k, w_t, w_v):
    b, t, d = x.shape
    nb = t // MOBA_BLOCK
    blocks = ROW_BLOCK // MOBA_BLOCK
    full = lambda w: pl.BlockSpec(w.shape, lambda i, s: (0, 0))
    rows = lambda n: pl.BlockSpec((1, ROW_BLOCK, n), lambda i, s: (i, s, 0))
    nt = w_t.shape[0]
    vrows = HEAD_DIM + SUBLANE_BF16
    return pl.pallas_call(
        _proj_in_kernel,
        grid=(b, t // ROW_BLOCK),
        in_specs=[rows(d), full(w_gla), full(w_rwkv), full(w_k), full(w_t), full(w_v)],
        out_specs=[rows(w_gla.shape[1]), rows(w_rwkv.shape[1]),
                   pl.BlockSpec((1, nt, ROW_BLOCK), lambda i, s: (i, 0, s)),
                   pl.BlockSpec((1, MOBA_HEADS, ROW_BLOCK, HEAD_DIM + nb), lambda i, s: (i, 0, s, 0)),
                   pl.BlockSpec((1, 1, blocks, MOBA_W), lambda i, s: (i, s, 0, 0)),
                   pl.BlockSpec((1, MOBA_HEADS, blocks, vrows, MOBA_BLOCK), lambda i, s: (i, 0, s, 0, 0))],
        out_shape=[jax.ShapeDtypeStruct((b, t, w_gla.shape[1]), F32),
                   jax.ShapeDtypeStruct((b, t, w_rwkv.shape[1]), F32),
                   jax.ShapeDtypeStruct((b, nt, t), F32),
                   jax.ShapeDtypeStruct((b, MOBA_HEADS, t, HEAD_DIM + nb), BF16),
                   jax.ShapeDtypeStruct((b, nb // blocks, blocks, MOBA_W), F32),
                   jax.ShapeDtypeStruct((b, MOBA_HEADS, nb, vrows, MOBA_BLOCK), BF16)],
        compiler_params=pltpu.CompilerParams(
            dimension_semantics=("parallel", "parallel"), vmem_limit_bytes=VMEM_LIMIT),
    )(x, w_gla, w_rwkv, w_k, w_t, w_v)


def _gla_kernel(h_ref, aup_ref, ab_ref, nw_ref, o_ref, st_ref):
    @pl.when(pl.program_id(1) == 0)
    def _():
        st_ref[...] = jnp.zeros_like(st_ref)

    nc = TIME_BLOCK // CHUNK
    nh = GLA_HEADS
    shape3 = (nh * nc, CHUNK, CHUNK)
    hb = h_ref[0]
    z = _mm(hb[:, 4 * GLA_W:4 * GLA_W + GLA_LOWRANK], aup_ref[...], na=2, nb=2) + ab_ref[...]
    la = _split_heads(-_softplus(-z) * (1.0 / GLA_TAU), nc)
    row = _iota2(shape3, 1)
    col = _iota2(shape3, 2)
    tri = (row >= col)
    anchor_b = (col <= (row // SUB) * SUB + (SUB - 1)).astype(BF16)
    q = _split_heads(hb[:, 0:GLA_W], nc) * (HEAD_DIM ** -0.5)
    k = _split_heads(hb[:, GLA_W:2 * GLA_W], nc)
    v = _split_heads(hb[:, 2 * GLA_W:3 * GLA_W], nc)
    cum = _mm(tri.astype(BF16), la, _BNN, nb=3)
    k_anchor = _mm(anchor_b, la, _BNN, nb=3)
    k_t = k * jnp.exp(k_anchor - cum)
    scores = jnp.zeros(shape3, F32)
    for j in range(CHUNK // SUB):
        a_j = cum[:, j * SUB + SUB - 1:j * SUB + SUB, :]
        q_j = q * jnp.exp(jnp.where(row >= j * SUB, cum - a_j, 0.0))
        in_grp = (row >= j * SUB) & (row < (j + 1) * SUB)
        scores = scores + _mm(q_j, jnp.where(in_grp, k_t, 0.0), _BNT)
    o_intra = _mm(jnp.where(tri, scores, 0.0), v, _BNN)
    last = cum[:, CHUNK - 1:CHUNK, :]
    n_mat = _mm(v, k * jnp.exp(last - cum), _BTN)
    q_dec = q * jnp.exp(cum)
    dec_last = jnp.exp(last)
    st = st_ref[...]
    o_inter = []
    for c in range(nc):
        o_inter.append(_mm(_chunk_of(q_dec, nh, c), st, _BNT))
        st = st * _chunk_of(dec_last, nh, c) + _chunk_of(n_mat, nh, c)
    st_ref[...] = st
    o = o_intra + jnp.stack(o_inter, axis=1).reshape(shape3)
    o = o * lax.rsqrt(jnp.mean(o * o, axis=-1, keepdims=True) + LN_EPS) * nw_ref[...]
    g = hb[:, 3 * GLA_W:4 * GLA_W]
    o_ref[0] = _merge_heads(o, nh) * (g * _sigmoid(g))


def _gla(h_gla, a_up, a_bias, norm_w):
    b, t, c = h_gla.shape
    vec = lambda a: pl.BlockSpec(a.shape, lambda i, s: (0, 0))
    return pl.pallas_call(
        _gla_kernel,
        grid=(b, t // TIME_BLOCK),
        in_specs=[pl.BlockSpec((1, TIME_BLOCK, c), lambda i, s: (i, s, 0)), vec(a_up), vec(a_bias), vec(norm_w)],
        out_specs=pl.BlockSpec((1, TIME_BLOCK, GLA_W), lambda i, s: (i, s, 0)),
        out_shape=jax.ShapeDtypeStruct((b, t, GLA_W), F32),
        scratch_shapes=[pltpu.VMEM((GLA_HEADS, HEAD_DIM, HEAD_DIM), F32)],
        compiler_params=pltpu.CompilerParams(
            dimension_semantics=("parallel", "arbitrary"), vmem_limit_bytes=VMEM_LIMIT),
    )(h_gla, a_up, a_bias, norm_w)


def _block_penalty(q_t, kbar, own):
    gate = _mm(kbar, q_t, na=2, nb=2)
    blk = _iota2(gate.shape, 0)
    nb = gate.shape[0]
    gate = jnp.where(blk < own, gate, NEG_INF)
    allowed = blk < 0
    for _ in range(min(MOBA_TOPK, nb)):
        best = jnp.max(gate, axis=0, keepdims=True)
        first = jnp.min(jnp.where(gate == best, blk, nb), axis=0, keepdims=True)
        hit = blk == first
        allowed = allowed | (hit & (blk < own))
        gate = jnp.where(hit, NEG_INF, gate)
    return jnp.where(allowed, 0.0, -MASK_BIG).astype(BF16)


def _moba_attn_kernel(qt_ref, kbar_ref, k_ref, vt_ref, gt_ref, bias_ref, o_ref, *scratch):
    hps = MOBA_HEADS_PER_STEP
    own = pl.program_id(2)
    nb = kbar_ref.shape[1]
    n_near = N_BIAS_TILES - 1
    n_far = jnp.maximum(own - n_near, 0)
    far_groups = (n_far + KV_GROUP - 1) // KV_GROUP
    n_groups = far_groups + (jnp.minimum(own, n_near) + KV_GROUP - 1) // KV_GROUP
    head_rows = lambda hh: slice(hh * HEAD_DIM, (hh + 1) * HEAD_DIM)
    q = [(qt_ref[0, head_rows(hh), :] * (HEAD_DIM ** -0.5)).astype(BF16) for hh in range(hps)]
    pen = [_block_penalty(qt_ref[0, head_rows(hh), :], kbar_ref[0][:, head_rows(hh)], own) for hh in range(hps)]
    blk = _iota2(pen[0].shape, 0)
    sa, sb, ma, mb = (scratch[i * hps:(i + 1) * hps] for i in range(4))

    def first_block(u):
        hi_near = own - (u - far_groups) * KV_GROUP
        j0 = jnp.where(u < far_groups, u * KV_GROUP, hi_near - KV_GROUP)
        return jnp.clip(j0, 0, nb - KV_GROUP), hi_near

    def scores(u, hh, dst, dst_max):
        is_far = u < far_groups
        j0, hi_near = first_block(u)
        hi = jnp.where(u < n_groups, jnp.where(is_far, n_far, hi_near), 0)
        lo = jnp.where(is_far, 0, hi_near - KV_GROUP)
        in_range = (blk >= lo) & (blk < hi)
        q_aug = jnp.concatenate([q[hh], jnp.where(in_range, pen[hh], jnp.asarray(-MASK_BIG, BF16))], axis=0)
        top = None
        for i in range(KV_GROUP):
            rows = pl.ds(pl.multiple_of((j0 + i) * MOBA_BLOCK, MOBA_BLOCK), MOBA_BLOCK)
            tile = jnp.where(is_far, N_BIAS_TILES, jnp.clip(own - (j0 + i), 0, n_near))
            s = _mm(k_ref[0, hh, rows, :], q_aug).astype(BF16) + bias_ref[hh, tile]
            dst[i * MOBA_BLOCK:(i + 1) * MOBA_BLOCK, :] = s
            tile_top = jnp.max(s, axis=0, keepdims=True)
            top = tile_top if top is None else jnp.maximum(top, tile_top)
        dst_max[...] = top.astype(F32)

    def consume(u, hh, src, src_max, carry):
        m, acc = carry
        j0, _ = first_block(u)
        ref_b = jnp.maximum(m, src_max[...]).astype(BF16)
        m_new = ref_b.astype(F32)
        acc = acc * jnp.exp(m - m_new)
        for i in range(KV_GROUP):
            s = src[i * MOBA_BLOCK:(i + 1) * MOBA_BLOCK, :]
            acc = acc + _mm(vt_ref[0, hh, j0 + i], jnp.exp(s - ref_b))
        return m_new, acc

    def own_block(hh):
        k0 = k_ref[0, hh, pl.ds(pl.multiple_of(own * MOBA_BLOCK, MOBA_BLOCK), MOBA_BLOCK), :]
        s = _mm(k0[:, :HEAD_DIM], q[hh]) + bias_ref[hh, 0]
        s = jnp.where(_iota2(s.shape, 0) <= _iota2(s.shape, 1), s, -MASK_BIG)
        m = jnp.max(s, axis=0, keepdims=True)
        return m, _mm(vt_ref[0, hh, own], jnp.exp((s - m).astype(BF16)))

    carry = tuple(own_block(hh) for hh in range(hps))
    for hh in range(hps):
        scores(0, hh, sa[hh], ma[hh])

    def pair(v, carry):
        u = 2 * v
        for hh in range(hps):
            scores(u + 1, hh, sb[hh], mb[hh])
        carry = tuple(consume(u, hh, sa[hh], ma[hh], carry[hh]) for hh in range(hps))
        for hh in range(hps):
            scores(u + 2, hh, sa[hh], ma[hh])
        return tuple(consume(u + 1, hh, sb[hh], mb[hh], carry[hh]) for hh in range(hps))

    carry = lax.fori_loop(0, (n_groups + 1) // 2, pair, carry)
    for hh in range(hps):
        _, acc = carry[hh]
        g = gt_ref[0, head_rows(hh), :]
        o_ref[0, head_rows(hh), :] = acc[:HEAD_DIM] / acc[HEAD_DIM:HEAD_DIM + 1] * (g * _sigmoid(g))


def _moba_attn(ht, kbar, k_aug, vt_aug, bias):
    b, _, t = ht.shape
    h, nb = k_aug.shape[1], kbar.shape[1]
    hps = MOBA_HEADS_PER_STEP
    rows = hps * HEAD_DIM
    qs = pl.BlockSpec((1, rows, MOBA_BLOCK), lambda i, j, s: (i, j, s))
    gs = pl.BlockSpec((1, rows, MOBA_BLOCK), lambda i, j, s: (i, h // hps + j, s))
    heads = lambda a: pl.BlockSpec((1, hps) + a.shape[2:], lambda i, j, s: (i, j) + (0,) * (a.ndim - 2))
    return pl.pallas_call(
        _moba_attn_kernel,
        grid=(b, h // hps, nb),
        in_specs=[qs, pl.BlockSpec((1, nb, rows), lambda i, j, s: (i, 0, j)), heads(k_aug), heads(vt_aug), gs,
                  pl.BlockSpec((hps,) + bias.shape[1:], lambda i, j, s: (j, 0, 0, 0))],
        out_specs=qs,
        out_shape=jax.ShapeDtypeStruct((b, h * HEAD_DIM, t), F32),
        scratch_shapes=[pltpu.VMEM((KV_GROUP * MOBA_BLOCK, MOBA_BLOCK), BF16)] * (2 * hps)
        + [pltpu.VMEM((1, MOBA_BLOCK), F32)] * (2 * hps),
        compiler_params=pltpu.CompilerParams(
            dimension_semantics=("parallel", "parallel", "arbitrary"), vmem_limit_bytes=VMEM_LIMIT),
    )(ht, kbar, k_aug, vt_aug, ht, bias)


def _t5_bucket(rel):
    rel = jnp.maximum(rel, 0)
    max_exact = N_BUCKETS // 2
    rel_f = jnp.maximum(rel, 1).astype(F32)
    large = max_exact + (jnp.log(rel_f / max_exact) / math.log(MAX_DISTANCE / max_exact)
                         * (N_BUCKETS - max_exact)).astype(jnp.int32)
    large = jnp.minimum(large, N_BUCKETS - 1)
    return jnp.where(rel < max_exact, rel, large)


def _bias_tiles(rel_bias):
    d = jnp.arange(N_BIAS_TILES)[:, None, None] * MOBA_BLOCK
    rel = d + jnp.arange(MOBA_BLOCK)[None, None, :] - jnp.arange(MOBA_BLOCK)[None, :, None]
    bucket = _t5_bucket(rel)
    tab = rel_bias.T.astype(F32)
    out = jnp.zeros((tab.shape[0],) + bucket.shape, F32)
    for bk in range(N_BUCKETS):
        out = jnp.where(bucket[None] == bk, tab[:, bk][:, None, None, None], out)
    far = jnp.broadcast_to(tab[:, N_BUCKETS - 1][:, None, None, None], out[:, :1].shape)
    return jnp.concatenate([out, far], axis=1).astype(BF16)


def _rwkv_kernel(h_ref, mu_ref, wup_ref, w0_ref, aup_ref, a0_ref, kk_ref, ka_ref, rk_ref, gw_ref, gb_ref,
                 o_ref, st_ref, prev_ref):
    @pl.when(pl.program_id(1) == 0)
    def _():
        st_ref[...] = jnp.zeros_like(st_ref)
        prev_ref[...] = jnp.zeros_like(prev_ref)

    nc = TIME_BLOCK // CHUNK
    nh = RWKV_HEADS
    w = RWKV_W
    p = h_ref[0]
    shifted = pltpu.roll(p, 1, axis=0)
    p_prev = jnp.where(_iota2(p.shape, 0) == 0, prev_ref[...], shifted)
    prev_ref[...] = p[TIME_BLOCK - 1:TIME_BLOCK, :]
    p = p + (p_prev - p) * mu_ref[...]
    r, k, v, g = p[:, 0:w], p[:, w:2 * w], p[:, 2 * w:3 * w], p[:, 3 * w:4 * w]
    w_dn = p[:, 4 * w:4 * w + RWKV_LORA]
    a_dn = p[:, 4 * w + RWKV_LORA:4 * w + 2 * RWKV_LORA]
    d = w0_ref[...] + _mm(jnp.tanh(w_dn), wup_ref[...], na=2, nb=2)
    lw = -jnp.exp(-_softplus(-d) - 0.5)
    a = _sigmoid(a0_ref[...] + _mm(a_dn, aup_ref[...], na=2, nb=2))
    same_head = (_iota2((w, w), 0) // HEAD_DIM == _iota2((w, w), 1) // HEAD_DIM).astype(BF16)
    kk = k * kk_ref[...]
    kk = kk / jnp.maximum(jnp.sqrt(_mm(kk * kk, same_head, na=3)), 1e-12)
    k = k * (1.0 + (a - 1.0) * ka_ref[...])
    bonus = _mm(r * k * rk_ref[...], same_head, na=3) * v

    shape3 = (nh * nc, CHUNK, CHUNK)
    row = _iota2(shape3, 1)
    col = _iota2(shape3, 2)
    tri_b = (row >= col).astype(BF16)
    eye = row == col
    hp = dict()
    carry_prec = dict(na=2, nb=2)
    r, lw, k, v = (_split_heads(x, nc) for x in (r, lw, k, v))
    a_vec = _split_heads(-kk, nc)
    b_vec = _split_heads(kk * a, nc)
    cum = _mm(tri_b, lw, _BNN, nb=3)
    last = cum[:, CHUNK - 1:CHUNK, :]
    inv = jnp.exp(-cum)
    to_end = jnp.exp(last - cum)
    r_t = r * jnp.exp(cum)
    a_t = a_vec * jnp.exp(cum - lw)
    k_t = k * inv
    b_t = b_vec * inv
    a_ab = jnp.where(row > col, _mm(a_t, b_t, _BNT, **hp), 0.0)
    a_ak = jnp.where(row > col, _mm(a_t, k_t, _BNT, **hp), 0.0)
    a_rb = jnp.where(row >= col, _mm(r_t, b_t, _BNT, **hp), 0.0)
    a_rk = jnp.where(row >= col, _mm(r_t, k_t, _BNT, **hp), 0.0)
    inv_t = jnp.where(eye, 1.0, a_ab)
    power = a_ab
    for _ in range(int(math.log2(CHUNK)) - 1):
        power = _mm(power, power, _BNN, **hp)
        inv_t = inv_t + _mm(inv_t, power, _BNN, **hp)
    w_mat = _mm(inv_t, a_t, _BNN, **hp)
    u0 = _mm(inv_t, _mm(a_ak, v, _BNN, **hp), _BNN, **hp)
    y_w = r_t + _mm(a_rb, w_mat, _BNN, **hp)
    y0 = _mm(a_rb, u0, _BNN, **hp) + _mm(a_rk, v, _BNN, **hp)
    b_end = b_vec * to_end
    k_end = k * to_end
    m_mat = jnp.where(eye, jnp.exp(last), 0.0) + _mm(b_end, w_mat, _BTN, **hp)
    n_mat = _mm(b_end, u0, _BTN, **hp) + _mm(k_end, v, _BTN, **hp)
    st = st_ref[...]
    ys = []
    for c in range(nc):
        ys.append(_mm(_chunk_of(y_w, nh, c), st, _BNN, **hp))
        st = _mm(_chunk_of(m_mat, nh, c), st, _BNN, **carry_prec) + _chunk_of(n_mat, nh, c)
    st_ref[...] = st
    y = y0 + jnp.stack(ys, axis=1).reshape(shape3)
    mean = jnp.mean(y, axis=-1, keepdims=True)
    var = jnp.mean(jnp.square(y - mean), axis=-1, keepdims=True)
    y = _merge_heads((y - mean) * lax.rsqrt(var + RWKV_GN_EPS), nh) * gw_ref[...] + gb_ref[...]
    o_ref[0] = (y + bonus) * (g * _sigmoid(g))


def _rwkv(h_rwkv, mu, w_up, w0, a_up, a0, k_k, k_a, r_k, gn_w, gn_b):
    b, t, c = h_rwkv.shape
    full = lambda a: pl.BlockSpec(a.shape, lambda i, s: (0, 0))
    params = (mu, w_up, w0, a_up, a0, k_k, k_a, r_k, gn_w, gn_b)
    return pl.pallas_call(
        _rwkv_kernel,
        grid=(b, t // TIME_BLOCK),
        in_specs=[pl.BlockSpec((1, TIME_BLOCK, c), lambda i, s: (i, s, 0))] + [full(a) for a in params],
        out_specs=pl.BlockSpec((1, TIME_BLOCK, RWKV_W), lambda i, s: (i, s, 0)),
        out_shape=jax.ShapeDtypeStruct((b, t, RWKV_W), F32),
        scratch_shapes=[pltpu.VMEM((RWKV_HEADS, HEAD_DIM, HEAD_DIM), F32), pltpu.VMEM((1, c), F32)],
        compiler_params=pltpu.CompilerParams(
            dimension_semantics=("parallel", "arbitrary"), vmem_limit_bytes=VMEM_LIMIT),
    )(h_rwkv, *params)


def _proj_out_kernel(alpha, og_ref, ot_ref, or_ref, w_ref, x_ref, lw_ref, lb_ref, y_ref):
    y = _mm(og_ref[0].astype(BF16), w_ref[0:GLA_W, :])
    y = y + _mm(ot_ref[0].astype(BF16), w_ref[GLA_W:GLA_W + MOBA_W, :], _TN)
    y = y + _mm(or_ref[0].astype(BF16), w_ref[GLA_W + MOBA_W:, :])
    z = alpha * x_ref[0] + y
    mu = jnp.mean(z, axis=-1, keepdims=True)
    var = jnp.mean(jnp.square(z - mu), axis=-1, keepdims=True)
    y_ref[0] = (z - mu) * lax.rsqrt(var + LN_EPS) * lw_ref[...] + lb_ref[...]


def _proj_out(o_gla, ot_moba, o_rwkv, w, x, ln_w, ln_b, alpha):
    b, t, d = x.shape
    rs = lambda n: pl.BlockSpec((1, ROW_BLOCK, n), lambda i, s: (i, s, 0))
    vs = pl.BlockSpec((1, d), lambda i, s: (0, 0))
    return pl.pallas_call(
        functools.partial(_proj_out_kernel, alpha),
        grid=(b, t // ROW_BLOCK),
        in_specs=[rs(GLA_W), pl.BlockSpec((1, MOBA_W, ROW_BLOCK), lambda i, s: (i, 0, s)), rs(RWKV_W),
                  pl.BlockSpec(w.shape, lambda i, s: (0, 0)), rs(d), vs, vs],
        out_specs=rs(d),
        out_shape=jax.ShapeDtypeStruct((b, t, d), F32),
        compiler_params=pltpu.CompilerParams(
            dimension_semantics=("parallel", "parallel"), vmem_limit_bytes=VMEM_LIMIT),
    )(o_gla, ot_moba, o_rwkv, w, x, ln_w, ln_b)


def _pad_cols(a, n):
    return jnp.pad(a, ((0, 0), (0, n - a.shape[1])))


def _split_w_in(w_l):
    gla, moba = w_l[:, :GLA_COLS], w_l[:, GLA_COLS:GLA_COLS + MOBA_COLS]
    rwkv = w_l[:, GLA_COLS + MOBA_COLS:]
    mq, mk, mv, mg = (moba[:, i * MOBA_W:(i + 1) * MOBA_W] for i in range(4))
    w_t = jnp.concatenate([mq, mg], axis=1).T
    return tuple(a.astype(BF16) for a in (_pad_cols(gla, GLA_PAD), _pad_cols(rwkv, RWKV_PAD), mk, w_t, mv.T))


def _moba_branch(ht, k_aug, kbar, vt_aug, bias):
    return _moba_attn(ht, kbar.reshape(kbar.shape[0], -1, MOBA_W), k_aug, vt_aug, bias)


def _gla_branch(h_gla, a_up, a_bias, norm_w):
    return _gla(h_gla, a_up, a_bias.reshape(1, GLA_W), norm_w.reshape(1, HEAD_DIM))


def _rwkv_branch(h_rwkv, mu, w_up, w0, a_up, a0, k_k, k_a, r_k, gn_w, gn_b):
    row = lambda p: p.reshape(1, -1)
    return _rwkv(h_rwkv, _pad_cols(row(mu), h_rwkv.shape[2]), w_up, row(w0), a_up, row(a0), row(k_k), row(k_a),
                 row(r_k), row(gn_w), row(gn_b))


def kernel(x, w_in, w_out, gla_a_up, gla_a_bias, gla_norm_w, moba_rel_bias, rwkv_mu, rwkv_w_up, rwkv_w0,
           rwkv_a_up, rwkv_a0, rwkv_k_k, rwkv_k_a, rwkv_r_k, rwkv_gn_w, rwkv_gn_b, ln_w, ln_b):
    bsz, t, d = x.shape
    depth = w_in.shape[0]
    alpha = (2.0 * depth) ** 0.25
    bias = _bias_tiles(moba_rel_bias)
    for l in range(depth):
        h_gla, h_rwkv, ht, k_aug, kbar, vt_aug = _proj_in(x, *_split_w_in(w_in[l]))
        o_gla = _gla_branch(h_gla, gla_a_up[l], gla_a_bias[l], gla_norm_w[l])
        ot_moba = _moba_branch(ht, k_aug, kbar, vt_aug, bias)
        o_rwkv = _rwkv_branch(h_rwkv, rwkv_mu[l], rwkv_w_up[l], rwkv_w0[l], rwkv_a_up[l], rwkv_a0[l],
                              rwkv_k_k[l], rwkv_k_a[l], rwkv_r_k[l], rwkv_gn_w[l], rwkv_gn_b[l])
        x = _proj_out(o_gla, ot_moba, o_rwkv, w_out[l].astype(BF16), x,
                      ln_w[l].reshape(1, d), ln_b[l].reshape(1, d), alpha)
    return x
```

```python
import functools
import math

import jax
import jax.numpy as jnp
from jax import lax
from jax.experimental import pallas as pl
from jax.experimental.pallas import tpu as pltpu

HEAD_DIM = 64
GLA_HEADS = 4
MOBA_HEADS = 8
RWKV_HEADS = 4
GLA_W = GLA_HEADS * HEAD_DIM
MOBA_W = MOBA_HEADS * HEAD_DIM
RWKV_W = RWKV_HEADS * HEAD_DIM
GLA_LOWRANK = 16
GLA_TAU = 16.0
RWKV_LORA = 32
RWKV_GN_EPS = 64e-5
MOBA_BLOCK = 256
MOBA_TOPK = 3
N_BUCKETS = 32
MAX_DISTANCE = 4096
LN_EPS = 1e-5
GLA_COLS = 4 * GLA_W + GLA_LOWRANK
MOBA_COLS = 4 * MOBA_W
RWKV_COLS = 4 * RWKV_W + 2 * RWKV_LORA

LANE = 128
SUBLANE_BF16 = 16
GLA_PAD = -(-GLA_COLS // LANE) * LANE
RWKV_PAD = -(-RWKV_COLS // LANE) * LANE
CHUNK = 64
SUB = 16
TIME_BLOCK = 512
ROW_BLOCK = 512
N_BIAS_TILES = (MAX_DISTANCE + MOBA_BLOCK - 1) // MOBA_BLOCK + 1
KV_GROUP = 4
MOBA_HEADS_PER_STEP = 4
assert (N_BIAS_TILES - 1) % KV_GROUP == 0
VMEM_LIMIT = 56 * 1024 * 1024

F32 = jnp.float32
BF16 = jnp.bfloat16
NEG_INF = float("-inf")
MASK_BIG = 2.0 ** 100


def _parts(a, n):
    out, r = [], a
    for i in range(n):
        p = r.astype(BF16)
        out.append(p)
        if i + 1 < n:
            r = r - p.astype(F32)
    return out


def _mm(a, b, dims=(((1,), (0,)), ((), ())), na=1, nb=1):
    ap = [a] if a.dtype == BF16 else _parts(a, na)
    bp = [b] if b.dtype == BF16 else _parts(b, nb)
    n = max(len(ap), len(bp))
    acc = None
    for i, x in enumerate(ap):
        for j, y in enumerate(bp):
            if i + j < n:
                t = lax.dot_general(x, y, dims, preferred_element_type=F32)
                acc = t if acc is None else acc + t
    return acc


_NT = (((1,), (1,)), ((), ()))
_TN = (((0,), (0,)), ((), ()))
_BNN = (((2,), (1,)), ((0,), (0,)))
_BNT = (((2,), (2,)), ((0,), (0,)))
_BTN = (((1,), (1,)), ((0,), (0,)))


def _softplus(x):
    return jnp.maximum(x, 0.0) + jnp.log(1.0 + jnp.exp(-jnp.abs(x)))


def _sigmoid(x):
    return 1.0 / (1.0 + jnp.exp(-x))


def _iota2(shape, dim):
    return lax.broadcasted_iota(jnp.int32, shape, dim)


def _split_heads(x, nc):
    n = x.shape[1] // HEAD_DIM
    return jnp.concatenate(
        [x[:, h * HEAD_DIM:(h + 1) * HEAD_DIM].reshape(nc, CHUNK, HEAD_DIM) for h in range(n)], axis=0)


def _merge_heads(x, n):
    nc = x.shape[0] // n
    return jnp.concatenate([x[h * nc:(h + 1) * nc].reshape(nc * CHUNK, HEAD_DIM) for h in range(n)], axis=1)


def _chunk_of(x, n, c):
    return x.reshape((n, x.shape[0] // n) + x.shape[1:])[:, c]


def _proj_in_kernel(x_ref, wg_ref, wr_ref, wk_ref, wt_ref, wv_ref,
                    hg_ref, hr_ref, ht_ref, kaug_ref, kbar_ref, vt_ref):
    nb = kaug_ref.shape[3] - HEAD_DIM
    blocks = ROW_BLOCK // MOBA_BLOCK
    xb = x_ref[0].astype(BF16)
    hg_ref[0] = jnp.dot(xb, wg_ref[...], preferred_element_type=F32)
    hr_ref[0] = jnp.dot(xb, wr_ref[...], preferred_element_type=F32)
    ht_ref[0] = lax.dot_general(wt_ref[...], xb, _NT, preferred_element_type=F32)
    k = jnp.dot(xb, wk_ref[...], preferred_element_type=F32)
    first = pl.program_id(1) * blocks
    onehot = (_iota2((ROW_BLOCK, nb), 1) == first + _iota2((ROW_BLOCK, nb), 0) // MOBA_BLOCK).astype(BF16)
    for h in range(MOBA_HEADS):
        kaug_ref[0, h] = jnp.concatenate([k[:, h * HEAD_DIM:(h + 1) * HEAD_DIM].astype(BF16), onehot], axis=1)
    for i in range(blocks):
        kbar_ref[0, 0, i:i + 1, :] = jnp.sum(k[i * MOBA_BLOCK:(i + 1) * MOBA_BLOCK], axis=0,
                                             keepdims=True) * (1.0 / MOBA_BLOCK)
    vt = lax.dot_general(wv_ref[...], xb, _NT, preferred_element_type=F32)
    pad = jnp.concatenate([jnp.ones((1, MOBA_BLOCK), F32), jnp.zeros((SUBLANE_BF16 - 1, MOBA_BLOCK), F32)], axis=0)
    for h in range(MOBA_HEADS):
        for i in range(blocks):
            tile = vt[h * HEAD_DIM:(h + 1) * HEAD_DIM, i * MOBA_BLOCK:(i + 1) * MOBA_BLOCK]
            vt_ref[0, h, i] = jnp.concatenate([tile, pad], axis=0).astype(BF16)


def _proj_in(x, w_gla, w_rwkv, w_k, w_t, w_v):
    b, t, d = x.shape
    nb = t // MOBA_BLOCK
    blocks = ROW_BLOCK // MOBA_BLOCK
    full = lambda w: pl.BlockSpec(w.shape, lambda i, s: (0, 0))
    rows = lambda n: pl.BlockSpec((1, ROW_BLOCK, n), lambda i, s: (i, s, 0))
    nt = w_t.shape[0]
    vrows = HEAD_DIM + SUBLANE_BF16
    return pl.pallas_call(
        _proj_in_kernel,
        grid=(b, t // ROW_BLOCK),
        in_specs=[rows(d), full(w_gla), full(w_rwkv), full(w_k), full(w_t), full(w_v)],
        out_specs=[rows(w_gla.shape[1]), rows(w_rwkv.shape[1]),
                   pl.BlockSpec((1, nt, ROW_BLOCK), lambda i, s: (i, 0, s)),
                   pl.BlockSpec((1, MOBA_HEADS, ROW_BLOCK, HEAD_DIM + nb), lambda i, s: (i, 0, s, 0)),
                   pl.BlockSpec((1, 1, blocks, MOBA_W), lambda i, s: (i, s, 0, 0)),
                   pl.BlockSpec((1, MOBA_HEADS, blocks, vrows, MOBA_BLOCK), lambda i, s: (i, 0, s, 0, 0))],
        out_shape=[jax.ShapeDtypeStruct((b, t, w_gla.shape[1]), F32),
                   jax.ShapeDtypeStruct((b, t, w_rwkv.shape[1]), F32),
                   jax.ShapeDtypeStruct((b, nt, t), F32),
                   jax.ShapeDtypeStruct((b, MOBA_HEADS, t, HEAD_DIM + nb), BF16),
                   jax.ShapeDtypeStruct((b, nb // blocks, blocks, MOBA_W), F32),
                   jax.ShapeDtypeStruct((b, MOBA_HEADS, nb, vrows, MOBA_BLOCK), BF16)],
        compiler_params=pltpu.CompilerParams(
            dimension_semantics=("parallel", "parallel"), vmem_limit_bytes=VMEM_LIMIT),
    )(x, w_gla, w_rwkv, w_k, w_t, w_v)


def _gla_kernel(h_ref, aup_ref, ab_ref, nw_ref, o_ref, st_ref):
    @pl.when(pl.program_id(1) == 0)
    def _():
        st_ref[...] = jnp.zeros_like(st_ref)

    nc = TIME_BLOCK // CHUNK
    nh = GLA_HEADS
    shape3 = (nh * nc, CHUNK, CHUNK)
    hb = h_ref[0]
    z = _mm(hb[:, 4 * GLA_W:4 * GLA_W + GLA_LOWRANK], aup_ref[...], na=2, nb=2) + ab_ref[...]
    la = _split_heads(-_softplus(-z) * (1.0 / GLA_TAU), nc)
    row = _iota2(shape3, 1)
    col = _iota2(shape3, 2)
    tri = (row >= col)
    anchor_b = (col <= (row // SUB) * SUB + (SUB - 1)).astype(BF16)
    q = _split_heads(hb[:, 0:GLA_W], nc) * (HEAD_DIM ** -0.5)
    k = _split_heads(hb[:, GLA_W:2 * GLA_W], nc)
    v = _split_heads(hb[:, 2 * GLA_W:3 * GLA_W], nc)
    cum = _mm(tri.astype(BF16), la, _BNN, nb=3)
    k_anchor = _mm(anchor_b, la, _BNN, nb=3)
    k_t = k * jnp.exp(k_anchor - cum)
    scores = jnp.zeros(shape3, F32)
    for j in range(CHUNK // SUB):
        a_j = cum[:, j * SUB + SUB - 1:j * SUB + SUB, :]
        q_j = q * jnp.exp(jnp.where(row >= j * SUB, cum - a_j, 0.0))
        in_grp = (row >= j * SUB) & (row < (j + 1) * SUB)
        scores = scores + _mm(q_j, jnp.where(in_grp, k_t, 0.0), _BNT)
    o_intra = _mm(jnp.where(tri, scores, 0.0), v, _BNN)
    last = cum[:, CHUNK - 1:CHUNK, :]
    n_mat = _mm(v, k * jnp.exp(last - cum), _BTN)
    q_dec = q * jnp.exp(cum)
    dec_last = jnp.exp(last)
    st = st_ref[...]
    o_inter = []
    for c in range(nc):
        o_inter.append(_mm(_chunk_of(q_dec, nh, c), st, _BNT))
        st = st * _chunk_of(dec_last, nh, c) + _chunk_of(n_mat, nh, c)
    st_ref[...] = st
    o = o_intra + jnp.stack(o_inter, axis=1).reshape(shape3)
    o = o * lax.rsqrt(jnp.mean(o * o, axis=-1, keepdims=True) + LN_EPS) * nw_ref[...]
    g = hb[:, 3 * GLA_W:4 * GLA_W]
    o_ref[0] = _merge_heads(o, nh) * (g * _sigmoid(g))


def _gla(h_gla, a_up, a_bias, norm_w):
    b, t, c = h_gla.shape
    vec = lambda a: pl.BlockSpec(a.shape, lambda i, s: (0, 0))
    return pl.pallas_call(
        _gla_kernel,
        grid=(b, t // TIME_BLOCK),
        in_specs=[pl.BlockSpec((1, TIME_BLOCK, c), lambda i, s: (i, s, 0)), vec(a_up), vec(a_bias), vec(norm_w)],
        out_specs=pl.BlockSpec((1, TIME_BLOCK, GLA_W), lambda i, s: (i, s, 0)),
        out_shape=jax.ShapeDtypeStruct((b, t, GLA_W), F32),
        scratch_shapes=[pltpu.VMEM((GLA_HEADS, HEAD_DIM, HEAD_DIM), F32)],
        compiler_params=pltpu.CompilerParams(
            dimension_semantics=("parallel", "arbitrary"), vmem_limit_bytes=VMEM_LIMIT),
    )(h_gla, a_up, a_bias, norm_w)


def _block_penalty(q_t, kbar, own):
    gate = _mm(kbar, q_t, na=2, nb=2)
    blk = _iota2(gate.shape, 0)
    nb = gate.shape[0]
    gate = jnp.where(blk < own, gate, NEG_INF)
    allowed = blk < 0
    for _ in range(min(MOBA_TOPK, nb)):
        best = jnp.max(gate, axis=0, keepdims=True)
        first = jnp.min(jnp.where(gate == best, blk, nb), axis=0, keepdims=True)
        hit = blk == first
        allowed = allowed | (hit & (blk < own))
        gate = jnp.where(hit, NEG_INF, gate)
    return jnp.where(allowed, 0.0, -MASK_BIG).astype(BF16)


def _moba_attn_kernel(qt_ref, kbar_ref, k_ref, vt_ref, gt_ref, bias_ref, o_ref, *scratch):
    hps = MOBA_HEADS_PER_STEP
    own = pl.program_id(2)
    nb = kbar_ref.shape[1]
    n_near = N_BIAS_TILES - 1
    n_far = jnp.maximum(own - n_near, 0)
    far_groups = (n_far + KV_GROUP - 1) // KV_GROUP
    n_groups = far_groups + (jnp.minimum(own, n_near) + KV_GROUP - 1) // KV_GROUP
    head_rows = lambda hh: slice(hh * HEAD_DIM, (hh + 1) * HEAD_DIM)
    q = [(qt_ref[0, head_rows(hh), :] * (HEAD_DIM ** -0.5)).astype(BF16) for hh in range(hps)]
    pen = [_block_penalty(qt_ref[0, head_rows(hh), :], kbar_ref[0][:, head_rows(hh)], own) for hh in range(hps)]
    blk = _iota2(pen[0].shape, 0)
    sa, sb, ma, mb = (scratch[i * hps:(i + 1) * hps] for i in range(4))

    def first_block(u):
        hi_near = own - (u - far_groups) * KV_GROUP
        j0 = jnp.where(u < far_groups, u * KV_GROUP, hi_near - KV_GROUP)
        return jnp.clip(j0, 0, nb - KV_GROUP), hi_near

    def scores(u, hh, dst, dst_max):
        is_far = u < far_groups
        j0, hi_near = first_block(u)
        hi = jnp.where(u < n_groups, jnp.where(is_far, n_far, hi_near), 0)
        lo = jnp.where(is_far, 0, hi_near - KV_GROUP)
        in_range = (blk >= lo) & (blk < hi)
        q_aug = jnp.concatenate([q[hh], jnp.where(in_range, pen[hh], jnp.asarray(-MASK_BIG, BF16))], axis=0)
        top = None
        for i in range(KV_GROUP):
            rows = pl.ds(pl.multiple_of((j0 + i) * MOBA_BLOCK, MOBA_BLOCK), MOBA_BLOCK)
            tile = jnp.where(is_far, N_BIAS_TILES, jnp.clip(own - (j0 + i), 0, n_near))
            s = _mm(k_ref[0, hh, rows, :], q_aug).astype(BF16) + bias_ref[hh, tile]
            dst[i * MOBA_BLOCK:(i + 1) * MOBA_BLOCK, :] = s
            tile_top = jnp.max(s, axis=0, keepdims=True)
            top = tile_top if top is None else jnp.maximum(top, tile_top)
        dst_max[...] = top.astype(F32)

    def consume(u, hh, src, src_max, carry):
        m, acc = carry
        j0, _ = first_block(u)
        ref_b = jnp.maximum(m, src_max[...]).astype(BF16)
        m_new = ref_b.astype(F32)
        acc = acc * jnp.exp(m - m_new)
        for i in range(KV_GROUP):
            s = src[i * MOBA_BLOCK:(i + 1) * MOBA_BLOCK, :]
            acc = acc + _mm(vt_ref[0, hh, j0 + i], jnp.exp(s - ref_b))
        return m_new, acc

    def own_block(hh):
        k0 = k_ref[0, hh, pl.ds(pl.multiple_of(own * MOBA_BLOCK, MOBA_BLOCK), MOBA_BLOCK), :]
        s = _mm(k0[:, :HEAD_DIM], q[hh]) + bias_ref[hh, 0]
        s = jnp.where(_iota2(s.shape, 0) <= _iota2(s.shape, 1), s, -MASK_BIG)
        m = jnp.max(s, axis=0, keepdims=True)
        return m, _mm(vt_ref[0, hh, own], jnp.exp((s - m).astype(BF16)))

    carry = tuple(own_block(hh) for hh in range(hps))
    for hh in range(hps):
        scores(0, hh, sa[hh], ma[hh])

    def pair(v, carry):
        u = 2 * v
        for hh in range(hps):
            scores(u + 1, hh, sb[hh], mb[hh])
        carry = tuple(consume(u, hh, sa[hh], ma[hh], carry[hh]) for hh in range(hps))
        for hh in range(hps):
            scores(u + 2, hh, sa[hh], ma[hh])
        return tuple(consume(u + 1, hh, sb[hh], mb[hh], carry[hh]) for hh in range(hps))

    carry = lax.fori_loop(0, (n_groups + 1) // 2, pair, carry)
    for hh in range(hps):
        _, acc = carry[hh]
        g = gt_ref[0, head_rows(hh), :]
        o_ref[0, head_rows(hh), :] = acc[:HEAD_DIM] / acc[HEAD_DIM:HEAD_DIM + 1] * (g * _sigmoid(g))


def _moba_attn(ht, kbar, k_aug, vt_aug, bias):
    b, _, t = ht.shape
    h, nb = k_aug.shape[1], kbar.shape[1]
    hps = MOBA_HEADS_PER_STEP
    rows = hps * HEAD_DIM
    qs = pl.BlockSpec((1, rows, MOBA_BLOCK), lambda i, j, s: (i, j, s))
    gs = pl.BlockSpec((1, rows, MOBA_BLOCK), lambda i, j, s: (i, h // hps + j, s))
    once = pl.Buffered(1)
    heads = lambda a: pl.BlockSpec((1, hps) + a.shape[2:], lambda i, j, s: (i, j) + (0,) * (a.ndim - 2),
                                   pipeline_mode=once)
    return pl.pallas_call(
        _moba_attn_kernel,
        grid=(b, h // hps, nb),
        in_specs=[qs, pl.BlockSpec((1, nb, rows), lambda i, j, s: (i, 0, j)), heads(k_aug), heads(vt_aug), gs,
                  pl.BlockSpec((hps,) + bias.shape[1:], lambda i, j, s: (j, 0, 0, 0), pipeline_mode=once)],
        out_specs=qs,
        out_shape=jax.ShapeDtypeStruct((b, h * HEAD_DIM, t), F32),
        scratch_shapes=[pltpu.VMEM((KV_GROUP * MOBA_BLOCK, MOBA_BLOCK), BF16)] * (2 * hps)
        + [pltpu.VMEM((1, MOBA_BLOCK), F32)] * (2 * hps),
        compiler_params=pltpu.CompilerParams(
            dimension_semantics=("parallel", "parallel", "arbitrary"), vmem_limit_bytes=VMEM_LIMIT),
    )(ht, kbar, k_aug, vt_aug, ht, bias)


def _t5_bucket(rel):
    rel = jnp.maximum(rel, 0)
    max_exact = N_BUCKETS // 2
    rel_f = jnp.maximum(rel, 1).astype(F32)
    large = max_exact + (jnp.log(rel_f / max_exact) / math.log(MAX_DISTANCE / max_exact)
                         * (N_BUCKETS - max_exact)).astype(jnp.int32)
    large = jnp.minimum(large, N_BUCKETS - 1)
    return jnp.where(rel < max_exact, rel, large)


def _bias_tiles(rel_bias):
    d = jnp.arange(N_BIAS_TILES)[:, None, None] * MOBA_BLOCK
    rel = d + jnp.arange(MOBA_BLOCK)[None, None, :] - jnp.arange(MOBA_BLOCK)[None, :, None]
    bucket = _t5_bucket(rel)
    tab = rel_bias.T.astype(F32)
    out = jnp.zeros((tab.shape[0],) + bucket.shape, F32)
    for bk in range(N_BUCKETS):
        out = jnp.where(bucket[None] == bk, tab[:, bk][:, None, None, None], out)
    far = jnp.broadcast_to(tab[:, N_BUCKETS - 1][:, None, None, None], out[:, :1].shape)
    return jnp.concatenate([out, far], axis=1).astype(BF16)


def _rwkv_kernel(h_ref, mu_ref, wup_ref, w0_ref, aup_ref, a0_ref, kk_ref, ka_ref, rk_ref, gw_ref, gb_ref,
                 o_ref, st_ref, prev_ref):
    @pl.when(pl.program_id(1) == 0)
    def _():
        st_ref[...] = jnp.zeros_like(st_ref)
        prev_ref[...] = jnp.zeros_like(prev_ref)

    nc = TIME_BLOCK // CHUNK
    nh = RWKV_HEADS
    w = RWKV_W
    p = h_ref[0]
    shifted = pltpu.roll(p, 1, axis=0)
    p_prev = jnp.where(_iota2(p.shape, 0) == 0, prev_ref[...], shifted)
    prev_ref[...] = p[TIME_BLOCK - 1:TIME_BLOCK, :]
    p = p + (p_prev - p) * mu_ref[...]
    r, k, v, g = p[:, 0:w], p[:, w:2 * w], p[:, 2 * w:3 * w], p[:, 3 * w:4 * w]
    w_dn = p[:, 4 * w:4 * w + RWKV_LORA]
    a_dn = p[:, 4 * w + RWKV_LORA:4 * w + 2 * RWKV_LORA]
    d = w0_ref[...] + _mm(jnp.tanh(w_dn), wup_ref[...], na=2, nb=2)
    lw = -jnp.exp(-_softplus(-d) - 0.5)
    a = _sigmoid(a0_ref[...] + _mm(a_dn, aup_ref[...], na=2, nb=2))
    same_head = (_iota2((w, w), 0) // HEAD_DIM == _iota2((w, w), 1) // HEAD_DIM).astype(BF16)
    kk = k * kk_ref[...]
    kk = kk / jnp.maximum(jnp.sqrt(_mm(kk * kk, same_head, na=3)), 1e-12)
    k = k * (1.0 + (a - 1.0) * ka_ref[...])
    bonus = _mm(r * k * rk_ref[...], same_head, na=3) * v

    shape3 = (nh * nc, CHUNK, CHUNK)
    row = _iota2(shape3, 1)
    col = _iota2(shape3, 2)
    tri_b = (row >= col).astype(BF16)
    eye = row == col
    hp = dict()
    carry_prec = dict(na=2, nb=2)
    r, lw, k, v = (_split_heads(x, nc) for x in (r, lw, k, v))
    a_vec = _split_heads(-kk, nc)
    b_vec = _split_heads(kk * a, nc)
    cum = _mm(tri_b, lw, _BNN, nb=3)
    last = cum[:, CHUNK - 1:CHUNK, :]
    inv = jnp.exp(-cum)
    to_end = jnp.exp(last - cum)
    r_t = r * jnp.exp(cum)
    a_t = a_vec * jnp.exp(cum - lw)
    k_t = k * inv
    b_t = b_vec * inv
    a_ab = jnp.where(row > col, _mm(a_t, b_t, _BNT, **hp), 0.0)
    a_ak = jnp.where(row > col, _mm(a_t, k_t, _BNT, **hp), 0.0)
    a_rb = jnp.where(row >= col, _mm(r_t, b_t, _BNT, **hp), 0.0)
    a_rk = jnp.where(row >= col, _mm(r_t, k_t, _BNT, **hp), 0.0)
    inv_t = jnp.where(eye, 1.0, a_ab)
    power = a_ab
    for _ in range(int(math.log2(CHUNK)) - 1):
        power = _mm(power, power, _BNN, **hp)
        inv_t = inv_t + _mm(inv_t, power, _BNN, **hp)
    w_mat = _mm(inv_t, a_t, _BNN, **hp)
    u0 = _mm(inv_t, _mm(a_ak, v, _BNN, **hp), _BNN, **hp)
    y_w = r_t + _mm(a_rb, w_mat, _BNN, **hp)
    y0 = _mm(a_rb, u0, _BNN, **hp) + _mm(a_rk, v, _BNN, **hp)
    b_end = b_vec * to_end
    k_end = k * to_end
    m_mat = jnp.where(eye, jnp.exp(last), 0.0) + _mm(b_end, w_mat, _BTN, **hp)
    n_mat = _mm(b_end, u0, _BTN, **hp) + _mm(k_end, v, _BTN, **hp)
    st = st_ref[...]
    ys = []
    for c in range(nc):
        ys.append(_mm(_chunk_of(y_w, nh, c), st, _BNN, **hp))
        st = _mm(_chunk_of(m_mat, nh, c), st, _BNN, **carry_prec) + _chunk_of(n_mat, nh, c)
    st_ref[...] = st
    y = y0 + jnp.stack(ys, axis=1).reshape(shape3)
    mean = jnp.mean(y, axis=-1, keepdims=True)
    var = jnp.mean(jnp.square(y - mean), axis=-1, keepdims=True)
    y = _merge_heads((y - mean) * lax.rsqrt(var + RWKV_GN_EPS), nh) * gw_ref[...] + gb_ref[...]
    o_ref[0] = (y + bonus) * (g * _sigmoid(g))


def _rwkv(h_rwkv, mu, w_up, w0, a_up, a0, k_k, k_a, r_k, gn_w, gn_b):
    b, t, c = h_rwkv.shape
    full = lambda a: pl.BlockSpec(a.shape, lambda i, s: (0, 0))
    params = (mu, w_up, w0, a_up, a0, k_k, k_a, r_k, gn_w, gn_b)
    return pl.pallas_call(
        _rwkv_kernel,
        grid=(b, t // TIME_BLOCK),
        in_specs=[pl.BlockSpec((1, TIME_BLOCK, c), lambda i, s: (i, s, 0))] + [full(a) for a in params],
        out_specs=pl.BlockSpec((1, TIME_BLOCK, RWKV_W), lambda i, s: (i, s, 0)),
        out_shape=jax.ShapeDtypeStruct((b, t, RWKV_W), F32),
        scratch_shapes=[pltpu.VMEM((RWKV_HEADS, HEAD_DIM, HEAD_DIM), F32), pltpu.VMEM((1, c), F32)],
        compiler_params=pltpu.CompilerParams(
            dimension_semantics=("parallel", "arbitrary"), vmem_limit_bytes=VMEM_LIMIT),
    )(h_rwkv, *params)


def _proj_out_kernel(alpha, og_ref, ot_ref, or_ref, w_ref, x_ref, lw_ref, lb_ref, y_ref):
    y = _mm(og_ref[0].astype(BF16), w_ref[0:GLA_W, :])
    y = y + _mm(ot_ref[0].astype(BF16), w_ref[GLA_W:GLA_W + MOBA_W, :], _TN)
    y = y + _mm(or_ref[0].astype(BF16), w_ref[GLA_W + MOBA_W:, :])
    z = alpha * x_ref[0] + y
    mu = jnp.mean(z, axis=-1, keepdims=True)
    var = jnp.mean(jnp.square(z - mu), axis=-1, keepdims=True)
    y_ref[0] = (z - mu) * lax.rsqrt(var + LN_EPS) * lw_ref[...] + lb_ref[...]


def _proj_out(o_gla, ot_moba, o_rwkv, w, x, ln_w, ln_b, alpha):
    b, t, d = x.shape
    rs = lambda n: pl.BlockSpec((1, ROW_BLOCK, n), lambda i, s: (i, s, 0))
    vs = pl.BlockSpec((1, d), lambda i, s: (0, 0))
    return pl.pallas_call(
        functools.partial(_proj_out_kernel, alpha),
        grid=(b, t // ROW_BLOCK),
        in_specs=[rs(GLA_W), pl.BlockSpec((1, MOBA_W, ROW_BLOCK), lambda i, s: (i, 0, s)), rs(RWKV_W),
                  pl.BlockSpec(w.shape, lambda i, s: (0, 0)), rs(d), vs, vs],
        out_specs=rs(d),
        out_shape=jax.ShapeDtypeStruct((b, t, d), F32),
        compiler_params=pltpu.CompilerParams(
            dimension_semantics=("parallel", "parallel"), vmem_limit_bytes=VMEM_LIMIT),
    )(o_gla, ot_moba, o_rwkv, w, x, ln_w, ln_b)


def _pad_cols(a, n):
    return jnp.pad(a, ((0, 0), (0, n - a.shape[1])))


def _split_w_in(w_l):
    gla, moba = w_l[:, :GLA_COLS], w_l[:, GLA_COLS:GLA_COLS + MOBA_COLS]
    rwkv = w_l[:, GLA_COLS + MOBA_COLS:]
    mq, mk, mv, mg = (moba[:, i * MOBA_W:(i + 1) * MOBA_W] for i in range(4))
    w_t = jnp.concatenate([mq, mg], axis=1).T
    return tuple(a.astype(BF16) for a in (_pad_cols(gla, GLA_PAD), _pad_cols(rwkv, RWKV_PAD), mk, w_t, mv.T))


def _moba_branch(ht, k_aug, kbar, vt_aug, bias):
    return _moba_attn(ht, kbar.reshape(kbar.shape[0], -1, MOBA_W), k_aug, vt_aug, bias)


def _gla_branch(h_gla, a_up, a_bias, norm_w):
    return _gla(h_gla, a_up, a_bias.reshape(1, GLA_W), norm_w.reshape(1, HEAD_DIM))


def _rwkv_branch(h_rwkv, mu, w_up, w0, a_up, a0, k_k, k_a, r_k, gn_w, gn_b):
    row = lambda p: p.reshape(1, -1)
    return _rwkv(h_rwkv, _pad_cols(row(mu), h_rwkv.shape[2]), w_up, row(w0), a_up, row(a0), row(k_k), row(k_a),
                 row(r_k), row(gn_w), row(gn_b))


def kernel(x, w_in, w_out, gla_a_up, gla_a_bias, gla_norm_w, moba_rel_bias, rwkv_mu, rwkv_w_up, rwkv_w0,
           rwkv_a_up, rwkv_a0, rwkv_k_k, rwkv_k_a, rwkv_r_k, rwkv_gn_w, rwkv_gn_b, ln_w, ln_b):
    bsz, t, d = x.shape
    depth = w_in.shape[0]
    alpha = (2.0 * depth) ** 0.25
    bias = _bias_tiles(moba_rel_bias)
    for l in range(depth):
        h_gla, h_rwkv, ht, k_aug, kbar, vt_aug = _proj_in(x, *_split_w_in(w_in[l]))
        o_gla = _gla_branch(h_gla, gla_a_up[l], gla_a_bias[l], gla_norm_w[l])
        ot_moba = _moba_branch(ht, k_aug, kbar, vt_aug, bias)
        o_rwkv = _rwkv_branch(h_rwkv, rwkv_mu[l], rwkv_w_up[l], rwkv_w0[l], rwkv_a_up[l], rwkv_a0[l],
                              rwkv_k_k[l], rwkv_k_a[l], rwkv_r_k[l], rwkv_gn_w[l], rwkv_gn_b[l])
        x = _proj_out(o_gla, ot_moba, o_rwkv, w_out[l].astype(BF16), x,
                      ln_w[l].reshape(1, d), ln_b[l].reshape(1, d), alpha)
    return x
```

```python
import functools
import math

import jax
import jax.numpy as jnp
from jax import lax
from jax.experimental import pallas as pl
from jax.experimental.pallas import tpu as pltpu

HEAD_DIM = 64
GLA_HEADS = 4
MOBA_HEADS = 8
RWKV_HEADS = 4
GLA_W = GLA_HEADS * HEAD_DIM
MOBA_W = MOBA_HEADS * HEAD_DIM
RWKV_W = RWKV_HEADS * HEAD_DIM
GLA_LOWRANK = 16
GLA_TAU = 16.0
RWKV_LORA = 32
RWKV_GN_EPS = 64e-5
MOBA_BLOCK = 256
MOBA_TOPK = 3
N_BUCKETS = 32
MAX_DISTANCE = 4096
LN_EPS = 1e-5
GLA_COLS = 4 * GLA_W + GLA_LOWRANK
MOBA_COLS = 4 * MOBA_W
RWKV_COLS = 4 * RWKV_W + 2 * RWKV_LORA

LANE = 128
SUBLANE_BF16 = 16
GLA_PAD = -(-GLA_COLS // LANE) * LANE
RWKV_PAD = -(-RWKV_COLS // LANE) * LANE
CHUNK = 64
SUB = 16
GLA_TIME_BLOCK = 1024
RWKV_TIME_BLOCK = 512
ROW_BLOCK = 512
N_BIAS_TILES = (MAX_DISTANCE + MOBA_BLOCK - 1) // MOBA_BLOCK + 1
KV_GROUP = 4
MOBA_HEADS_PER_STEP = 4
assert (N_BIAS_TILES - 1) % KV_GROUP == 0
VMEM_LIMIT = 56 * 1024 * 1024

F32 = jnp.float32
BF16 = jnp.bfloat16
NEG_INF = float("-inf")
MASK_BIG = 2.0 ** 100


def _parts(a, n):
    out, r = [], a
    for i in range(n):
        p = r.astype(BF16)
        out.append(p)
        if i + 1 < n:
            r = r - p.astype(F32)
    return out


def _mm(a, b, dims=(((1,), (0,)), ((), ())), na=1, nb=1):
    ap = [a] if a.dtype == BF16 else _parts(a, na)
    bp = [b] if b.dtype == BF16 else _parts(b, nb)
    n = max(len(ap), len(bp))
    acc = None
    for i, x in enumerate(ap):
        for j, y in enumerate(bp):
            if i + j < n:
                t = lax.dot_general(x, y, dims, preferred_element_type=F32)
                acc = t if acc is None else acc + t
    return acc


_NT = (((1,), (1,)), ((), ()))
_TN = (((0,), (0,)), ((), ()))
_BNN = (((2,), (1,)), ((0,), (0,)))
_BNT = (((2,), (2,)), ((0,), (0,)))
_BTN = (((1,), (1,)), ((0,), (0,)))


def _softplus(x):
    return jnp.maximum(x, 0.0) + jnp.log(1.0 + jnp.exp(-jnp.abs(x)))


def _sigmoid(x):
    return 1.0 / (1.0 + jnp.exp(-x))


def _iota2(shape, dim):
    return lax.broadcasted_iota(jnp.int32, shape, dim)


def _split_heads(x, nc):
    n = x.shape[1] // HEAD_DIM
    return jnp.concatenate(
        [x[:, h * HEAD_DIM:(h + 1) * HEAD_DIM].reshape(nc, CHUNK, HEAD_DIM) for h in range(n)], axis=0)


def _merge_heads(x, n):
    nc = x.shape[0] // n
    return jnp.concatenate([x[h * nc:(h + 1) * nc].reshape(nc * CHUNK, HEAD_DIM) for h in range(n)], axis=1)


def _chunk_of(x, n, c):
    return x.reshape((n, x.shape[0] // n) + x.shape[1:])[:, c]


def _proj_in_kernel(x_ref, wg_ref, wr_ref, wk_ref, wt_ref, wv_ref,
                    hg_ref, hr_ref, ht_ref, kaug_ref, kbar_ref, vt_ref):
    nb = kaug_ref.shape[3] - HEAD_DIM
    blocks = ROW_BLOCK // MOBA_BLOCK
    xb = x_ref[0].astype(BF16)
    hg_ref[0] = jnp.dot(xb, wg_ref[...], preferred_element_type=F32)
    hr_ref[0] = jnp.dot(xb, wr_ref[...], preferred_element_type=F32)
    ht_ref[0] = lax.dot_general(wt_ref[...], xb, _NT, preferred_element_type=F32)
    k = jnp.dot(xb, wk_ref[...], preferred_element_type=F32)
    first = pl.program_id(1) * blocks
    onehot = (_iota2((ROW_BLOCK, nb), 1) == first + _iota2((ROW_BLOCK, nb), 0) // MOBA_BLOCK).astype(BF16)
    for h in range(MOBA_HEADS):
        kaug_ref[0, h] = jnp.concatenate([k[:, h * HEAD_DIM:(h + 1) * HEAD_DIM].astype(BF16), onehot], axis=1)
    for i in range(blocks):
        kbar_ref[0, 0, i:i + 1, :] = jnp.sum(k[i * MOBA_BLOCK:(i + 1) * MOBA_BLOCK], axis=0,
                                             keepdims=True) * (1.0 / MOBA_BLOCK)
    vt = lax.dot_general(wv_ref[...], xb, _NT, preferred_element_type=F32)
    pad = jnp.concatenate([jnp.ones((1, MOBA_BLOCK), F32), jnp.zeros((SUBLANE_BF16 - 1, MOBA_BLOCK), F32)], axis=0)
    for h in range(MOBA_HEADS):
        for i in range(blocks):
            tile = vt[h * HEAD_DIM:(h + 1) * HEAD_DIM, i * MOBA_BLOCK:(i + 1) * MOBA_BLOCK]
            vt_ref[0, h, i] = jnp.concatenate([tile, pad], axis=0).astype(BF16)


def _proj_in(x, w_gla, w_rwkv, w_k, w_t, w_v):
    b, t, d = x.shape
    nb = t // MOBA_BLOCK
    blocks = ROW_BLOCK // MOBA_BLOCK
    full = lambda w: pl.BlockSpec(w.shape, lambda i, s: (0, 0))
    rows = lambda n: pl.BlockSpec((1, ROW_BLOCK, n), lambda i, s: (i, s, 0))
    nt = w_t.shape[0]
    vrows = HEAD_DIM + SUBLANE_BF16
    return pl.pallas_call(
        _proj_in_kernel,
        grid=(b, t // ROW_BLOCK),
        in_specs=[rows(d), full(w_gla), full(w_rwkv), full(w_k), full(w_t), full(w_v)],
        out_specs=[rows(w_gla.shape[1]), rows(w_rwkv.shape[1]),
                   pl.BlockSpec((1, nt, ROW_BLOCK), lambda i, s: (i, 0, s)),
                   pl.BlockSpec((1, MOBA_HEADS, ROW_BLOCK, HEAD_DIM + nb), lambda i, s: (i, 0, s, 0)),
                   pl.BlockSpec((1, 1, blocks, MOBA_W), lambda i, s: (i, s, 0, 0)),
                   pl.BlockSpec((1, MOBA_HEADS, blocks, vrows, MOBA_BLOCK), lambda i, s: (i, 0, s, 0, 0))],
        out_shape=[jax.ShapeDtypeStruct((b, t, w_gla.shape[1]), F32),
                   jax.ShapeDtypeStruct((b, t, w_rwkv.shape[1]), F32),
                   jax.ShapeDtypeStruct((b, nt, t), F32),
                   jax.ShapeDtypeStruct((b, MOBA_HEADS, t, HEAD_DIM + nb), BF16),
                   jax.ShapeDtypeStruct((b, nb // blocks, blocks, MOBA_W), F32),
                   jax.ShapeDtypeStruct((b, MOBA_HEADS, nb, vrows, MOBA_BLOCK), BF16)],
        compiler_params=pltpu.CompilerParams(
            dimension_semantics=("parallel", "parallel"), vmem_limit_bytes=VMEM_LIMIT),
    )(x, w_gla, w_rwkv, w_k, w_t, w_v)


def _gla_kernel(h_ref, aup_ref, ab_ref, nw_ref, o_ref, st_ref):
    @pl.when(pl.program_id(1) == 0)
    def _():
        st_ref[...] = jnp.zeros_like(st_ref)

    nc = GLA_TIME_BLOCK // CHUNK
    nh = GLA_HEADS
    shape3 = (nh * nc, CHUNK, CHUNK)
    hb = h_ref[0]
    z = _mm(hb[:, 4 * GLA_W:4 * GLA_W + GLA_LOWRANK], aup_ref[...], na=2, nb=2) + ab_ref[...]
    la = _split_heads(-_softplus(-z) * (1.0 / GLA_TAU), nc)
    row = _iota2(shape3, 1)
    col = _iota2(shape3, 2)
    tri = (row >= col)
    anchor_b = (col <= (row // SUB) * SUB + (SUB - 1)).astype(BF16)
    q = _split_heads(hb[:, 0:GLA_W], nc) * (HEAD_DIM ** -0.5)
    k = _split_heads(hb[:, GLA_W:2 * GLA_W], nc)
    v = _split_heads(hb[:, 2 * GLA_W:3 * GLA_W], nc)
    cum = _mm(tri.astype(BF16), la, _BNN, nb=3)
    k_anchor = _mm(anchor_b, la, _BNN, nb=3)
    k_t = k * jnp.exp(k_anchor - cum)
    scores = jnp.zeros(shape3, F32)
    for j in range(CHUNK // SUB):
        a_j = cum[:, j * SUB + SUB - 1:j * SUB + SUB, :]
        q_j = q * jnp.exp(jnp.where(row >= j * SUB, cum - a_j, 0.0))
        in_grp = (row >= j * SUB) & (row < (j + 1) * SUB)
        scores = scores + _mm(q_j, jnp.where(in_grp, k_t, 0.0), _BNT)
    o_intra = _mm(jnp.where(tri, scores, 0.0), v, _BNN)
    last = cum[:, CHUNK - 1:CHUNK, :]
    n_mat = _mm(v, k * jnp.exp(last - cum), _BTN)
    q_dec = q * jnp.exp(cum)
    dec_last = jnp.exp(last)
    st = st_ref[...]
    o_inter = []
    for c in range(nc):
        o_inter.append(_mm(_chunk_of(q_dec, nh, c), st, _BNT))
        st = st * _chunk_of(dec_last, nh, c) + _chunk_of(n_mat, nh, c)
    st_ref[...] = st
    o = o_intra + jnp.stack(o_inter, axis=1).reshape(shape3)
    o = o * lax.rsqrt(jnp.mean(o * o, axis=-1, keepdims=True) + LN_EPS) * nw_ref[...]
    g = hb[:, 3 * GLA_W:4 * GLA_W]
    o_ref[0] = _merge_heads(o, nh) * (g * _sigmoid(g))


def _gla(h_gla, a_up, a_bias, norm_w):
    b, t, c = h_gla.shape
    vec = lambda a: pl.BlockSpec(a.shape, lambda i, s: (0, 0))
    return pl.pallas_call(
        _gla_kernel,
        grid=(b, t // GLA_TIME_BLOCK),
        in_specs=[pl.BlockSpec((1, GLA_TIME_BLOCK, c), lambda i, s: (i, s, 0)), vec(a_up), vec(a_bias),
                  vec(norm_w)],
        out_specs=pl.BlockSpec((1, GLA_TIME_BLOCK, GLA_W), lambda i, s: (i, s, 0)),
        out_shape=jax.ShapeDtypeStruct((b, t, GLA_W), F32),
        scratch_shapes=[pltpu.VMEM((GLA_HEADS, HEAD_DIM, HEAD_DIM), F32)],
        compiler_params=pltpu.CompilerParams(
            dimension_semantics=("parallel", "arbitrary"), vmem_limit_bytes=VMEM_LIMIT),
    )(h_gla, a_up, a_bias, norm_w)


def _block_penalty(q_t, kbar, own):
    gate = _mm(kbar, q_t, na=2, nb=2)
    blk = _iota2(gate.shape, 0)
    nb = gate.shape[0]
    gate = jnp.where(blk < own, gate, NEG_INF)
    allowed = blk < 0
    for _ in range(min(MOBA_TOPK, nb)):
        best = jnp.max(gate, axis=0, keepdims=True)
        first = jnp.min(jnp.where(gate == best, blk, nb), axis=0, keepdims=True)
        hit = blk == first
        allowed = allowed | (hit & (blk < own))
        gate = jnp.where(hit, NEG_INF, gate)
    return jnp.where(allowed, 0.0, -MASK_BIG).astype(BF16)


def _moba_attn_kernel(qt_ref, kbar_ref, k_ref, vt_ref, gt_ref, bias_ref, o_ref, *scratch):
    hps = MOBA_HEADS_PER_STEP
    own = pl.program_id(2)
    nb = kbar_ref.shape[1]
    n_near = N_BIAS_TILES - 1
    n_far = jnp.maximum(own - n_near, 0)
    far_groups = (n_far + KV_GROUP - 1) // KV_GROUP
    n_groups = far_groups + (jnp.minimum(own, n_near) + KV_GROUP - 1) // KV_GROUP
    head_rows = lambda hh: slice(hh * HEAD_DIM, (hh + 1) * HEAD_DIM)
    q = [(qt_ref[0, head_rows(hh), :] * (HEAD_DIM ** -0.5)).astype(BF16) for hh in range(hps)]
    pen = [_block_penalty(qt_ref[0, head_rows(hh), :], kbar_ref[0][:, head_rows(hh)], own) for hh in range(hps)]
    blk = _iota2(pen[0].shape, 0)
    sa, sb, ma, mb = (scratch[i * hps:(i + 1) * hps] for i in range(4))

    def first_block(u):
        hi_near = own - (u - far_groups) * KV_GROUP
        j0 = jnp.where(u < far_groups, u * KV_GROUP, hi_near - KV_GROUP)
        return jnp.clip(j0, 0, nb - KV_GROUP), hi_near

    def scores(u, hh, dst, dst_max):
        is_far = u < far_groups
        j0, hi_near = first_block(u)
        hi = jnp.where(u < n_groups, jnp.where(is_far, n_far, hi_near), 0)
        lo = jnp.where(is_far, 0, hi_near - KV_GROUP)
        in_range = (blk >= lo) & (blk < hi)
        q_aug = jnp.concatenate([q[hh], jnp.where(in_range, pen[hh], jnp.asarray(-MASK_BIG, BF16))], axis=0)
        top = None
        for i in range(KV_GROUP):
            rows = pl.ds(pl.multiple_of((j0 + i) * MOBA_BLOCK, MOBA_BLOCK), MOBA_BLOCK)
            tile = jnp.where(is_far, N_BIAS_TILES, jnp.clip(own - (j0 + i), 0, n_near))
            s = _mm(k_ref[0, hh, rows, :], q_aug).astype(BF16) + bias_ref[hh, tile]
            dst[i * MOBA_BLOCK:(i + 1) * MOBA_BLOCK, :] = s
            tile_top = jnp.max(s, axis=0, keepdims=True)
            top = tile_top if top is None else jnp.maximum(top, tile_top)
        dst_max[...] = top.astype(F32)

    def consume(u, hh, src, src_max, carry):
        m, acc = carry
        j0, _ = first_block(u)
        ref_b = jnp.maximum(m, src_max[...]).astype(BF16)
        m_new = ref_b.astype(F32)
        acc = acc * jnp.exp(m - m_new)
        for i in range(KV_GROUP):
            s = src[i * MOBA_BLOCK:(i + 1) * MOBA_BLOCK, :]
            acc = acc + _mm(vt_ref[0, hh, j0 + i], jnp.exp(s - ref_b))
        return m_new, acc

    def own_block(hh):
        k0 = k_ref[0, hh, pl.ds(pl.multiple_of(own * MOBA_BLOCK, MOBA_BLOCK), MOBA_BLOCK), :]
        s = _mm(k0[:, :HEAD_DIM], q[hh]) + bias_ref[hh, 0]
        s = jnp.where(_iota2(s.shape, 0) <= _iota2(s.shape, 1), s, -MASK_BIG)
        m = jnp.max(s, axis=0, keepdims=True)
        return m, _mm(vt_ref[0, hh, own], jnp.exp((s - m).astype(BF16)))

    carry = tuple(own_block(hh) for hh in range(hps))
    for hh in range(hps):
        scores(0, hh, sa[hh], ma[hh])

    def pair(v, carry):
        u = 2 * v
        for hh in range(hps):
            scores(u + 1, hh, sb[hh], mb[hh])
        carry = tuple(consume(u, hh, sa[hh], ma[hh], carry[hh]) for hh in range(hps))
        for hh in range(hps):
            scores(u + 2, hh, sa[hh], ma[hh])
        return tuple(consume(u + 1, hh, sb[hh], mb[hh], carry[hh]) for hh in range(hps))

    n_pairs = jnp.maximum(n_groups - 1, 0) // 2
    carry = lax.fori_loop(0, n_pairs, pair, carry)
    u_last = 2 * n_pairs

    def last_two(carry):
        for hh in range(hps):
            scores(u_last + 1, hh, sb[hh], mb[hh])
        carry = tuple(consume(u_last, hh, sa[hh], ma[hh], carry[hh]) for hh in range(hps))
        return tuple(consume(u_last + 1, hh, sb[hh], mb[hh], carry[hh]) for hh in range(hps))

    def last_one(carry):
        return tuple(consume(u_last, hh, sa[hh], ma[hh], carry[hh]) for hh in range(hps))

    left = n_groups - u_last
    carry = lax.cond(left == 2, last_two, lambda c: c, carry)
    carry = lax.cond(left == 1, last_one, lambda c: c, carry)
    for hh in range(hps):
        _, acc = carry[hh]
        g = gt_ref[0, head_rows(hh), :]
        o_ref[0, head_rows(hh), :] = acc[:HEAD_DIM] / acc[HEAD_DIM:HEAD_DIM + 1] * (g * _sigmoid(g))


def _moba_attn(ht, kbar, k_aug, vt_aug, bias):
    b, _, t = ht.shape
    h, nb = k_aug.shape[1], kbar.shape[1]
    hps = MOBA_HEADS_PER_STEP
    rows = hps * HEAD_DIM
    qs = pl.BlockSpec((1, rows, MOBA_BLOCK), lambda i, j, s: (i, j, s))
    gs = pl.BlockSpec((1, rows, MOBA_BLOCK), lambda i, j, s: (i, h // hps + j, s))
    once = pl.Buffered(1)
    heads = lambda a: pl.BlockSpec((1, hps) + a.shape[2:], lambda i, j, s: (i, j) + (0,) * (a.ndim - 2),
                                   pipeline_mode=once)
    return pl.pallas_call(
        _moba_attn_kernel,
        grid=(b, h // hps, nb),
        in_specs=[qs, pl.BlockSpec((1, nb, rows), lambda i, j, s: (i, 0, j)), heads(k_aug), heads(vt_aug), gs,
                  pl.BlockSpec((hps,) + bias.shape[1:], lambda i, j, s: (j, 0, 0, 0), pipeline_mode=once)],
        out_specs=qs,
        out_shape=jax.ShapeDtypeStruct((b, h * HEAD_DIM, t), F32),
        scratch_shapes=[pltpu.VMEM((KV_GROUP * MOBA_BLOCK, MOBA_BLOCK), BF16)] * (2 * hps)
        + [pltpu.VMEM((1, MOBA_BLOCK), F32)] * (2 * hps),
        compiler_params=pltpu.CompilerParams(
            dimension_semantics=("parallel", "parallel", "arbitrary"), vmem_limit_bytes=VMEM_LIMIT),
    )(ht, kbar, k_aug, vt_aug, ht, bias)


def _t5_bucket(rel):
    rel = jnp.maximum(rel, 0)
    max_exact = N_BUCKETS // 2
    rel_f = jnp.maximum(rel, 1).astype(F32)
    large = max_exact + (jnp.log(rel_f / max_exact) / math.log(MAX_DISTANCE / max_exact)
                         * (N_BUCKETS - max_exact)).astype(jnp.int32)
    large = jnp.minimum(large, N_BUCKETS - 1)
    return jnp.where(rel < max_exact, rel, large)


def _bias_tiles(rel_bias):
    d = jnp.arange(N_BIAS_TILES)[:, None, None] * MOBA_BLOCK
    rel = d + jnp.arange(MOBA_BLOCK)[None, None, :] - jnp.arange(MOBA_BLOCK)[None, :, None]
    bucket = _t5_bucket(rel)
    tab = rel_bias.T.astype(F32)
    out = jnp.zeros((tab.shape[0],) + bucket.shape, F32)
    for bk in range(N_BUCKETS):
        out = jnp.where(bucket[None] == bk, tab[:, bk][:, None, None, None], out)
    far = jnp.broadcast_to(tab[:, N_BUCKETS - 1][:, None, None, None], out[:, :1].shape)
    return jnp.concatenate([out, far], axis=1).astype(BF16)


def _rwkv_kernel(h_ref, mu_ref, wup_ref, w0_ref, aup_ref, a0_ref, kk_ref, ka_ref, rk_ref, gw_ref, gb_ref,
                 o_ref, st_ref, prev_ref):
    @pl.when(pl.program_id(1) == 0)
    def _():
        st_ref[...] = jnp.zeros_like(st_ref)
        prev_ref[...] = jnp.zeros_like(prev_ref)

    nc = RWKV_TIME_BLOCK // CHUNK
    nh = RWKV_HEADS
    w = RWKV_W
    p = h_ref[0]
    shifted = pltpu.roll(p, 1, axis=0)
    p_prev = jnp.where(_iota2(p.shape, 0) == 0, prev_ref[...], shifted)
    prev_ref[...] = p[RWKV_TIME_BLOCK - 1:RWKV_TIME_BLOCK, :]
    p = p + (p_prev - p) * mu_ref[...]
    r, k, v, g = p[:, 0:w], p[:, w:2 * w], p[:, 2 * w:3 * w], p[:, 3 * w:4 * w]
    w_dn = p[:, 4 * w:4 * w + RWKV_LORA]
    a_dn = p[:, 4 * w + RWKV_LORA:4 * w + 2 * RWKV_LORA]
    d = w0_ref[...] + _mm(jnp.tanh(w_dn), wup_ref[...], na=2, nb=2)
    lw = -jnp.exp(-_softplus(-d) - 0.5)
    a = _sigmoid(a0_ref[...] + _mm(a_dn, aup_ref[...], na=2, nb=2))
    same_head = (_iota2((w, w), 0) // HEAD_DIM == _iota2((w, w), 1) // HEAD_DIM).astype(BF16)
    kk = k * kk_ref[...]
    kk = kk / jnp.maximum(jnp.sqrt(_mm(kk * kk, same_head, na=3)), 1e-12)
    k = k * (1.0 + (a - 1.0) * ka_ref[...])
    bonus = _mm(r * k * rk_ref[...], same_head, na=3) * v

    shape3 = (nh * nc, CHUNK, CHUNK)
    row = _iota2(shape3, 1)
    col = _iota2(shape3, 2)
    tri_b = (row >= col).astype(BF16)
    eye = row == col
    hp = dict()
    carry_prec = dict(na=2, nb=2)
    r, lw, k, v = (_split_heads(x, nc) for x in (r, lw, k, v))
    a_vec = _split_heads(-kk, nc)
    b_vec = _split_heads(kk * a, nc)
    cum = _mm(tri_b, lw, _BNN, nb=3)
    last = cum[:, CHUNK - 1:CHUNK, :]
    inv = jnp.exp(-cum)
    to_end = jnp.exp(last - cum)
    r_t = r * jnp.exp(cum)
    a_t = a_vec * jnp.exp(cum - lw)
    k_t = k * inv
    b_t = b_vec * inv
    a_ab = jnp.where(row > col, _mm(a_t, b_t, _BNT, **hp), 0.0)
    a_ak = jnp.where(row > col, _mm(a_t, k_t, _BNT, **hp), 0.0)
    a_rb = jnp.where(row >= col, _mm(r_t, b_t, _BNT, **hp), 0.0)
    a_rk = jnp.where(row >= col, _mm(r_t, k_t, _BNT, **hp), 0.0)
    inv_t = jnp.where(eye, 1.0, a_ab)
    power = a_ab
    for _ in range(int(math.log2(CHUNK)) - 1):
        power = _mm(power, power, _BNN, **hp)
        inv_t = inv_t + _mm(inv_t, power, _BNN, **hp)
    w_mat = _mm(inv_t, a_t, _BNN, **hp)
    u0 = _mm(inv_t, _mm(a_ak, v, _BNN, **hp), _BNN, **hp)
    y_w = r_t + _mm(a_rb, w_mat, _BNN, **hp)
    y0 = _mm(a_rb, u0, _BNN, **hp) + _mm(a_rk, v, _BNN, **hp)
    b_end = b_vec * to_end
    k_end = k * to_end
    m_mat = jnp.where(eye, jnp.exp(last), 0.0) + _mm(b_end, w_mat, _BTN, **hp)
    n_mat = _mm(b_end, u0, _BTN, **hp) + _mm(k_end, v, _BTN, **hp)
    st = st_ref[...]
    ys = []
    for c in range(nc):
        ys.append(_mm(_chunk_of(y_w, nh, c), st, _BNN, **hp))
        st = _mm(_chunk_of(m_mat, nh, c), st, _BNN, **carry_prec) + _chunk_of(n_mat, nh, c)
    st_ref[...] = st
    y = y0 + jnp.stack(ys, axis=1).reshape(shape3)
    mean = jnp.mean(y, axis=-1, keepdims=True)
    var = jnp.mean(jnp.square(y - mean), axis=-1, keepdims=True)
    y = _merge_heads((y - mean) * lax.rsqrt(var + RWKV_GN_EPS), nh) * gw_ref[...] + gb_ref[...]
    o_ref[0] = (y + bonus) * (g * _sigmoid(g))


def _rwkv(h_rwkv, mu, w_up, w0, a_up, a0, k_k, k_a, r_k, gn_w, gn_b):
    b, t, c = h_rwkv.shape
    full = lambda a: pl.BlockSpec(a.shape, lambda i, s: (0, 0))
    params = (mu, w_up, w0, a_up, a0, k_k, k_a, r_k, gn_w, gn_b)
    return pl.pallas_call(
        _rwkv_kernel,
        grid=(b, t // RWKV_TIME_BLOCK),
        in_specs=[pl.BlockSpec((1, RWKV_TIME_BLOCK, c), lambda i, s: (i, s, 0))] + [full(a) for a in params],
        out_specs=pl.BlockSpec((1, RWKV_TIME_BLOCK, RWKV_W), lambda i, s: (i, s, 0)),
        out_shape=jax.ShapeDtypeStruct((b, t, RWKV_W), F32),
        scratch_shapes=[pltpu.VMEM((RWKV_HEADS, HEAD_DIM, HEAD_DIM), F32), pltpu.VMEM((1, c), F32)],
        compiler_params=pltpu.CompilerParams(
            dimension_semantics=("parallel", "arbitrary"), vmem_limit_bytes=VMEM_LIMIT),
    )(h_rwkv, *params)


def _proj_out_kernel(alpha, og_ref, ot_ref, or_ref, w_ref, x_ref, lw_ref, lb_ref, y_ref):
    y = _mm(og_ref[0].astype(BF16), w_ref[0:GLA_W, :])
    y = y + _mm(ot_ref[0].astype(BF16), w_ref[GLA_W:GLA_W + MOBA_W, :], _TN)
    y = y + _mm(or_ref[0].astype(BF16), w_ref[GLA_W + MOBA_W:, :])
    z = alpha * x_ref[0] + y
    mu = jnp.mean(z, axis=-1, keepdims=True)
    var = jnp.mean(jnp.square(z - mu), axis=-1, keepdims=True)
    y_ref[0] = (z - mu) * lax.rsqrt(var + LN_EPS) * lw_ref[...] + lb_ref[...]


def _proj_out(o_gla, ot_moba, o_rwkv, w, x, ln_w, ln_b, alpha):
    b, t, d = x.shape
    rs = lambda n: pl.BlockSpec((1, ROW_BLOCK, n), lambda i, s: (i, s, 0))
    vs = pl.BlockSpec((1, d), lambda i, s: (0, 0))
    return pl.pallas_call(
        functools.partial(_proj_out_kernel, alpha),
        grid=(b, t // ROW_BLOCK),
        in_specs=[rs(GLA_W), pl.BlockSpec((1, MOBA_W, ROW_BLOCK), lambda i, s: (i, 0, s)), rs(RWKV_W),
                  pl.BlockSpec(w.shape, lambda i, s: (0, 0)), rs(d), vs, vs],
        out_specs=rs(d),
        out_shape=jax.ShapeDtypeStruct((b, t, d), F32),
        compiler_params=pltpu.CompilerParams(
            dimension_semantics=("parallel", "parallel"), vmem_limit_bytes=VMEM_LIMIT),
    )(o_gla, ot_moba, o_rwkv, w, x, ln_w, ln_b)


def _pad_cols(a, n):
    return jnp.pad(a, ((0, 0), (0, n - a.shape[1])))


def _split_w_in(w_l):
    gla, moba = w_l[:, :GLA_COLS], w_l[:, GLA_COLS:GLA_COLS + MOBA_COLS]
    rwkv = w_l[:, GLA_COLS + MOBA_COLS:]
    mq, mk, mv, mg = (moba[:, i * MOBA_W:(i + 1) * MOBA_W] for i in range(4))
    w_t = jnp.concatenate([mq, mg], axis=1).T
    return tuple(a.astype(BF16) for a in (_pad_cols(gla, GLA_PAD), _pad_cols(rwkv, RWKV_PAD), mk, w_t, mv.T))


def _moba_branch(ht, k_aug, kbar, vt_aug, bias):
    return _moba_attn(ht, kbar.reshape(kbar.shape[0], -1, MOBA_W), k_aug, vt_aug, bias)


def _gla_branch(h_gla, a_up, a_bias, norm_w):
    return _gla(h_gla, a_up, a_bias.reshape(1, GLA_W), norm_w.reshape(1, HEAD_DIM))


def _rwkv_branch(h_rwkv, mu, w_up, w0, a_up, a0, k_k, k_a, r_k, gn_w, gn_b):
    row = lambda p: p.reshape(1, -1)
    return _rwkv(h_rwkv, _pad_cols(row(mu), h_rwkv.shape[2]), w_up, row(w0), a_up, row(a0), row(k_k), row(k_a),
                 row(r_k), row(gn_w), row(gn_b))


def kernel(x, w_in, w_out, gla_a_up, gla_a_bias, gla_norm_w, moba_rel_bias, rwkv_mu, rwkv_w_up, rwkv_w0,
           rwkv_a_up, rwkv_a0, rwkv_k_k, rwkv_k_a, rwkv_r_k, rwkv_gn_w, rwkv_gn_b, ln_w, ln_b):
    bsz, t, d = x.shape
    depth = w_in.shape[0]
    alpha = (2.0 * depth) ** 0.25
    bias = _bias_tiles(moba_rel_bias)
    for l in range(depth):
        h_gla, h_rwkv, ht, k_aug, kbar, vt_aug = _proj_in(x, *_split_w_in(w_in[l]))
        o_gla = _gla_branch(h_gla, gla_a_up[l], gla_a_bias[l], gla_norm_w[l])
        ot_moba = _moba_branch(ht, k_aug, kbar, vt_aug, bias)
        o_rwkv = _rwkv_branch(h_rwkv, rwkv_mu[l], rwkv_w_up[l], rwkv_w0[l], rwkv_a_up[l], rwkv_a0[l],
                              rwkv_k_k[l], rwkv_k_a[l], rwkv_r_k[l], rwkv_gn_w[l], rwkv_gn_b[l])
        x = _proj_out(o_gla, ot_moba, o_rwkv, w_out[l].astype(BF16), x,
                      ln_w[l].reshape(1, d), ln_b[l].reshape(1, d), alpha)
    return x
```

```python
import functools
import math

import jax
import jax.numpy as jnp
from jax import lax
from jax.experimental import pallas as pl
from jax.experimental.pallas import tpu as pltpu

HEAD_DIM = 64
GLA_HEADS = 4
MOBA_HEADS = 8
RWKV_HEADS = 4
GLA_W = GLA_HEADS * HEAD_DIM
MOBA_W = MOBA_HEADS * HEAD_DIM
RWKV_W = RWKV_HEADS * HEAD_DIM
GLA_LOWRANK = 16
GLA_TAU = 16.0
RWKV_LORA = 32
RWKV_GN_EPS = 64e-5
MOBA_BLOCK = 256
MOBA_TOPK = 3
N_BUCKETS = 32
MAX_DISTANCE = 4096
LN_EPS = 1e-5
GLA_COLS = 4 * GLA_W + GLA_LOWRANK
MOBA_COLS = 4 * MOBA_W
RWKV_COLS = 4 * RWKV_W + 2 * RWKV_LORA

LANE = 128
SUBLANE_BF16 = 16
GLA_PAD = -(-GLA_COLS // LANE) * LANE
RWKV_PAD = -(-RWKV_COLS // LANE) * LANE
CHUNK = 64
SUB = 16
GLA_TIME_BLOCK = 1024
RWKV_TIME_BLOCK = 512
ROW_BLOCK = 512
N_BIAS_TILES = (MAX_DISTANCE + MOBA_BLOCK - 1) // MOBA_BLOCK + 1
KV_GROUP = 4
MOBA_HEADS_PER_STEP = 4
NEAR_BLOCKS = -(-N_BIAS_TILES // KV_GROUP) * KV_GROUP
VMEM_LIMIT = 56 * 1024 * 1024

F32 = jnp.float32
BF16 = jnp.bfloat16
NEG_INF = float("-inf")
MASK_BIG = 2.0 ** 100


def _parts(a, n):
    out, r = [], a
    for i in range(n):
        p = r.astype(BF16)
        out.append(p)
        if i + 1 < n:
            r = r - p.astype(F32)
    return out


def _mm(a, b, dims=(((1,), (0,)), ((), ())), na=1, nb=1):
    ap = [a] if a.dtype == BF16 else _parts(a, na)
    bp = [b] if b.dtype == BF16 else _parts(b, nb)
    n = max(len(ap), len(bp))
    acc = None
    for i, x in enumerate(ap):
        for j, y in enumerate(bp):
            if i + j < n:
                t = lax.dot_general(x, y, dims, preferred_element_type=F32)
                acc = t if acc is None else acc + t
    return acc


_NT = (((1,), (1,)), ((), ()))
_TN = (((0,), (0,)), ((), ()))
_BNN = (((2,), (1,)), ((0,), (0,)))
_BNT = (((2,), (2,)), ((0,), (0,)))
_BTN = (((1,), (1,)), ((0,), (0,)))


def _softplus(x):
    return jnp.maximum(x, 0.0) + jnp.log(1.0 + jnp.exp(-jnp.abs(x)))


def _sigmoid(x):
    return 1.0 / (1.0 + jnp.exp(-x))


def _iota2(shape, dim):
    return lax.broadcasted_iota(jnp.int32, shape, dim)


def _split_heads(x, nc):
    n = x.shape[1] // HEAD_DIM
    return jnp.concatenate(
        [x[:, h * HEAD_DIM:(h + 1) * HEAD_DIM].reshape(nc, CHUNK, HEAD_DIM) for h in range(n)], axis=0)


def _merge_heads(x, n):
    nc = x.shape[0] // n
    return jnp.concatenate([x[h * nc:(h + 1) * nc].reshape(nc * CHUNK, HEAD_DIM) for h in range(n)], axis=1)


def _chunk_of(x, n, c):
    return x.reshape((n, x.shape[0] // n) + x.shape[1:])[:, c]


def _proj_in_kernel(x_ref, wg_ref, wr_ref, wk_ref, wt_ref, wv_ref,
                    hg_ref, hr_ref, ht_ref, kaug_ref, kbar_ref, vt_ref):
    nb = kaug_ref.shape[3] - HEAD_DIM
    blocks = ROW_BLOCK // MOBA_BLOCK
    xb = x_ref[0].astype(BF16)
    hg_ref[0] = jnp.dot(xb, wg_ref[...], preferred_element_type=F32)
    hr_ref[0] = jnp.dot(xb, wr_ref[...], preferred_element_type=F32)
    ht_ref[0] = lax.dot_general(wt_ref[...], xb, _NT, preferred_element_type=F32)
    k = jnp.dot(xb, wk_ref[...], preferred_element_type=F32)
    first = pl.program_id(1) * blocks
    onehot = (_iota2((ROW_BLOCK, nb), 1) == first + _iota2((ROW_BLOCK, nb), 0) // MOBA_BLOCK).astype(BF16)
    for h in range(MOBA_HEADS):
        kaug_ref[0, h] = jnp.concatenate([k[:, h * HEAD_DIM:(h + 1) * HEAD_DIM].astype(BF16), onehot], axis=1)
    for i in range(blocks):
        kbar_ref[0, 0, i:i + 1, :] = jnp.sum(k[i * MOBA_BLOCK:(i + 1) * MOBA_BLOCK], axis=0,
                                             keepdims=True) * (1.0 / MOBA_BLOCK)
    vt = lax.dot_general(wv_ref[...], xb, _NT, preferred_element_type=F32)
    pad = jnp.concatenate([jnp.ones((1, MOBA_BLOCK), F32), jnp.zeros((SUBLANE_BF16 - 1, MOBA_BLOCK), F32)], axis=0)
    for h in range(MOBA_HEADS):
        for i in range(blocks):
            tile = vt[h * HEAD_DIM:(h + 1) * HEAD_DIM, i * MOBA_BLOCK:(i + 1) * MOBA_BLOCK]
            vt_ref[0, h, i] = jnp.concatenate([tile, pad], axis=0).astype(BF16)


def _proj_in(x, w_gla, w_rwkv, w_k, w_t, w_v):
    b, t, d = x.shape
    nb = t // MOBA_BLOCK
    blocks = ROW_BLOCK // MOBA_BLOCK
    full = lambda w: pl.BlockSpec(w.shape, lambda i, s: (0, 0))
    rows = lambda n: pl.BlockSpec((1, ROW_BLOCK, n), lambda i, s: (i, s, 0))
    nt = w_t.shape[0]
    vrows = HEAD_DIM + SUBLANE_BF16
    return pl.pallas_call(
        _proj_in_kernel,
        grid=(b, t // ROW_BLOCK),
        in_specs=[rows(d), full(w_gla), full(w_rwkv), full(w_k), full(w_t), full(w_v)],
        out_specs=[rows(w_gla.shape[1]), rows(w_rwkv.shape[1]),
                   pl.BlockSpec((1, nt, ROW_BLOCK), lambda i, s: (i, 0, s)),
                   pl.BlockSpec((1, MOBA_HEADS, ROW_BLOCK, HEAD_DIM + nb), lambda i, s: (i, 0, s, 0)),
                   pl.BlockSpec((1, 1, blocks, MOBA_W), lambda i, s: (i, s, 0, 0)),
                   pl.BlockSpec((1, MOBA_HEADS, blocks, vrows, MOBA_BLOCK), lambda i, s: (i, 0, s, 0, 0))],
        out_shape=[jax.ShapeDtypeStruct((b, t, w_gla.shape[1]), F32),
                   jax.ShapeDtypeStruct((b, t, w_rwkv.shape[1]), F32),
                   jax.ShapeDtypeStruct((b, nt, t), F32),
                   jax.ShapeDtypeStruct((b, MOBA_HEADS, t, HEAD_DIM + nb), BF16),
                   jax.ShapeDtypeStruct((b, nb // blocks, blocks, MOBA_W), F32),
                   jax.ShapeDtypeStruct((b, MOBA_HEADS, nb, vrows, MOBA_BLOCK), BF16)],
        compiler_params=pltpu.CompilerParams(
            dimension_semantics=("parallel", "parallel"), vmem_limit_bytes=VMEM_LIMIT),
    )(x, w_gla, w_rwkv, w_k, w_t, w_v)


def _gla_kernel(h_ref, aup_ref, ab_ref, nw_ref, o_ref, st_ref):
    @pl.when(pl.program_id(1) == 0)
    def _():
        st_ref[...] = jnp.zeros_like(st_ref)

    nc = GLA_TIME_BLOCK // CHUNK
    nh = GLA_HEADS
    shape3 = (nh * nc, CHUNK, CHUNK)
    hb = h_ref[0]
    z = _mm(hb[:, 4 * GLA_W:4 * GLA_W + GLA_LOWRANK], aup_ref[...], na=2, nb=2) + ab_ref[...]
    la = _split_heads(-_softplus(-z) * (1.0 / GLA_TAU), nc)
    row = _iota2(shape3, 1)
    col = _iota2(shape3, 2)
    tri = (row >= col)
    anchor_b = (col <= (row // SUB) * SUB + (SUB - 1)).astype(BF16)
    q = _split_heads(hb[:, 0:GLA_W], nc) * (HEAD_DIM ** -0.5)
    k = _split_heads(hb[:, GLA_W:2 * GLA_W], nc)
    v = _split_heads(hb[:, 2 * GLA_W:3 * GLA_W], nc)
    cum = _mm(tri.astype(BF16), la, _BNN, nb=3)
    k_anchor = _mm(anchor_b, la, _BNN, nb=3)
    k_t = k * jnp.exp(k_anchor - cum)
    scores = jnp.zeros(shape3, F32)
    for j in range(CHUNK // SUB):
        a_j = cum[:, j * SUB + SUB - 1:j * SUB + SUB, :]
        q_j = q * jnp.exp(jnp.where(row >= j * SUB, cum - a_j, 0.0))
        in_grp = (row >= j * SUB) & (row < (j + 1) * SUB)
        scores = scores + _mm(q_j, jnp.where(in_grp, k_t, 0.0), _BNT)
    o_intra = _mm(jnp.where(tri, scores, 0.0), v, _BNN)
    last = cum[:, CHUNK - 1:CHUNK, :]
    n_mat = _mm(v, k * jnp.exp(last - cum), _BTN)
    q_dec = q * jnp.exp(cum)
    dec_last = jnp.exp(last)
    st = st_ref[...]
    o_inter = []
    for c in range(nc):
        o_inter.append(_mm(_chunk_of(q_dec, nh, c), st, _BNT))
        st = st * _chunk_of(dec_last, nh, c) + _chunk_of(n_mat, nh, c)
    st_ref[...] = st
    o = o_intra + jnp.stack(o_inter, axis=1).reshape(shape3)
    o = o * lax.rsqrt(jnp.mean(o * o, axis=-1, keepdims=True) + LN_EPS) * nw_ref[...]
    g = hb[:, 3 * GLA_W:4 * GLA_W]
    o_ref[0] = _merge_heads(o, nh) * (g * _sigmoid(g))


def _gla(h_gla, a_up, a_bias, norm_w):
    b, t, c = h_gla.shape
    vec = lambda a: pl.BlockSpec(a.shape, lambda i, s: (0, 0))
    return pl.pallas_call(
        _gla_kernel,
        grid=(b, t // GLA_TIME_BLOCK),
        in_specs=[pl.BlockSpec((1, GLA_TIME_BLOCK, c), lambda i, s: (i, s, 0)), vec(a_up), vec(a_bias),
                  vec(norm_w)],
        out_specs=pl.BlockSpec((1, GLA_TIME_BLOCK, GLA_W), lambda i, s: (i, s, 0)),
        out_shape=jax.ShapeDtypeStruct((b, t, GLA_W), F32),
        scratch_shapes=[pltpu.VMEM((GLA_HEADS, HEAD_DIM, HEAD_DIM), F32)],
        compiler_params=pltpu.CompilerParams(
            dimension_semantics=("parallel", "arbitrary"), vmem_limit_bytes=VMEM_LIMIT),
    )(h_gla, a_up, a_bias, norm_w)


def _block_penalty(q_t, kbar, own):
    gate = _mm(kbar, q_t, na=2, nb=2)
    blk = _iota2(gate.shape, 0)
    nb = gate.shape[0]
    gate = jnp.where(blk < own, gate, NEG_INF)
    allowed = blk == own
    for _ in range(min(MOBA_TOPK, nb)):
        best = jnp.max(gate, axis=0, keepdims=True)
        first = jnp.min(jnp.where(gate == best, blk, nb), axis=0, keepdims=True)
        hit = blk == first
        allowed = allowed | (hit & (blk < own))
        gate = jnp.where(hit, NEG_INF, gate)
    return jnp.where(allowed, 0.0, -MASK_BIG).astype(BF16)


def _moba_attn_kernel(qt_ref, kbar_ref, k_ref, vt_ref, gt_ref, bias_ref, o_ref, *scratch):
    hps = MOBA_HEADS_PER_STEP
    own = pl.program_id(2)
    nb = kbar_ref.shape[1]
    n_far = jnp.maximum(own + 1 - NEAR_BLOCKS, 0)
    near_groups = (jnp.minimum(own + 1, NEAR_BLOCKS) + KV_GROUP - 1) // KV_GROUP
    n_groups = near_groups + (n_far + KV_GROUP - 1) // KV_GROUP
    head_rows = lambda hh: slice(hh * HEAD_DIM, (hh + 1) * HEAD_DIM)
    q = [(qt_ref[0, head_rows(hh), :] * (HEAD_DIM ** -0.5)).astype(BF16) for hh in range(hps)]
    pen = [_block_penalty(qt_ref[0, head_rows(hh), :], kbar_ref[0][:, head_rows(hh)], own) for hh in range(hps)]
    blk = _iota2(pen[0].shape, 0)
    sa, sb, ma, mb = (scratch[i * hps:(i + 1) * hps] for i in range(4))

    def first_block(u):
        hi_near = own + 1 - u * KV_GROUP
        j0 = jnp.where(u < near_groups, hi_near - KV_GROUP, (u - near_groups) * KV_GROUP)
        return jnp.clip(j0, 0, nb - KV_GROUP), hi_near

    def scores(u, hh, dst, dst_max):
        is_near = u < near_groups
        j0, hi_near = first_block(u)
        hi = jnp.where(is_near, hi_near, n_far)
        lo = jnp.where(is_near, hi_near - KV_GROUP, 0)
        in_range = (blk >= lo) & (blk < hi)
        q_aug = jnp.concatenate([q[hh], jnp.where(in_range, pen[hh], jnp.asarray(-MASK_BIG, BF16))], axis=0)
        top = None
        for i in range(KV_GROUP):
            rows = pl.ds(pl.multiple_of((j0 + i) * MOBA_BLOCK, MOBA_BLOCK), MOBA_BLOCK)
            tile = jnp.clip(own - (j0 + i), 0, N_BIAS_TILES)
            s = _mm(k_ref[0, hh, rows, :], q_aug).astype(BF16) + bias_ref[hh, tile]
            dst[i * MOBA_BLOCK:(i + 1) * MOBA_BLOCK, :] = s
            tile_top = jnp.max(s, axis=0, keepdims=True)
            top = tile_top if top is None else jnp.maximum(top, tile_top)
        dst_max[...] = top.astype(F32)

    def consume(u, hh, src, src_max, carry):
        m, acc = carry
        j0, _ = first_block(u)
        ref_b = jnp.maximum(m, src_max[...]).astype(BF16)
        m_new = ref_b.astype(F32)
        acc = acc * jnp.exp(m - m_new)
        for i in range(KV_GROUP):
            s = src[i * MOBA_BLOCK:(i + 1) * MOBA_BLOCK, :]
            acc = acc + _mm(vt_ref[0, hh, j0 + i], jnp.exp(s - ref_b))
        return m_new, acc

    carry = tuple((jnp.full((1, MOBA_BLOCK), -MASK_BIG, F32), jnp.zeros((vt_ref.shape[3], MOBA_BLOCK), F32))
                  for _ in range(hps))
    for hh in range(hps):
        scores(0, hh, sa[hh], ma[hh])

    def pair(v, carry):
        u = 2 * v
        for hh in range(hps):
            scores(u + 1, hh, sb[hh], mb[hh])
        carry = tuple(consume(u, hh, sa[hh], ma[hh], carry[hh]) for hh in range(hps))
        for hh in range(hps):
            scores(u + 2, hh, sa[hh], ma[hh])
        return tuple(consume(u + 1, hh, sb[hh], mb[hh], carry[hh]) for hh in range(hps))

    n_pairs = (n_groups - 1) // 2
    carry = lax.fori_loop(0, n_pairs, pair, carry)
    u_last = 2 * n_pairs

    def last_two(carry):
        for hh in range(hps):
            scores(u_last + 1, hh, sb[hh], mb[hh])
        carry = tuple(consume(u_last, hh, sa[hh], ma[hh], carry[hh]) for hh in range(hps))
        return tuple(consume(u_last + 1, hh, sb[hh], mb[hh], carry[hh]) for hh in range(hps))

    def last_one(carry):
        return tuple(consume(u_last, hh, sa[hh], ma[hh], carry[hh]) for hh in range(hps))

    carry = lax.cond(n_groups - u_last == 2, last_two, last_one, carry)
    for hh in range(hps):
        _, acc = carry[hh]
        g = gt_ref[0, head_rows(hh), :]
        o_ref[0, head_rows(hh), :] = acc[:HEAD_DIM] / acc[HEAD_DIM:HEAD_DIM + 1] * (g * _sigmoid(g))


def _moba_attn(ht, kbar, k_aug, vt_aug, bias):
    b, _, t = ht.shape
    h, nb = k_aug.shape[1], kbar.shape[1]
    hps = MOBA_HEADS_PER_STEP
    rows = hps * HEAD_DIM
    qs = pl.BlockSpec((1, rows, MOBA_BLOCK), lambda i, j, s: (i, j, s))
    gs = pl.BlockSpec((1, rows, MOBA_BLOCK), lambda i, j, s: (i, h // hps + j, s))
    once = pl.Buffered(1)
    heads = lambda a: pl.BlockSpec((1, hps) + a.shape[2:], lambda i, j, s: (i, j) + (0,) * (a.ndim - 2),
                                   pipeline_mode=once)
    return pl.pallas_call(
        _moba_attn_kernel,
        grid=(b, h // hps, nb),
        in_specs=[qs, pl.BlockSpec((1, nb, rows), lambda i, j, s: (i, 0, j)), heads(k_aug), heads(vt_aug), gs,
                  pl.BlockSpec((hps,) + bias.shape[1:], lambda i, j, s: (j, 0, 0, 0), pipeline_mode=once)],
        out_specs=qs,
        out_shape=jax.ShapeDtypeStruct((b, h * HEAD_DIM, t), F32),
        scratch_shapes=[pltpu.VMEM((KV_GROUP * MOBA_BLOCK, MOBA_BLOCK), BF16)] * (2 * hps)
        + [pltpu.VMEM((1, MOBA_BLOCK), F32)] * (2 * hps),
        compiler_params=pltpu.CompilerParams(
            dimension_semantics=("parallel", "parallel", "arbitrary"), vmem_limit_bytes=VMEM_LIMIT),
    )(ht, kbar, k_aug, vt_aug, ht, bias)


def _t5_bucket(rel):
    rel = jnp.maximum(rel, 0)
    max_exact = N_BUCKETS // 2
    rel_f = jnp.maximum(rel, 1).astype(F32)
    large = max_exact + (jnp.log(rel_f / max_exact) / math.log(MAX_DISTANCE / max_exact)
                         * (N_BUCKETS - max_exact)).astype(jnp.int32)
    large = jnp.minimum(large, N_BUCKETS - 1)
    return jnp.where(rel < max_exact, rel, large)


def _bias_tiles(rel_bias):
    d = jnp.arange(N_BIAS_TILES)[:, None, None] * MOBA_BLOCK
    rel = d + jnp.arange(MOBA_BLOCK)[None, None, :] - jnp.arange(MOBA_BLOCK)[None, :, None]
    bucket = _t5_bucket(rel)
    tab = rel_bias.T.astype(F32)
    out = jnp.zeros((tab.shape[0],) + bucket.shape, F32)
    for bk in range(N_BUCKETS):
        out = jnp.where(bucket[None] == bk, tab[:, bk][:, None, None, None], out)
    out = jnp.where(rel[None] < 0, -MASK_BIG, out)
    far =jnp.broadcast_to(tab[:, N_BUCKETS - 1][:, None, None, None], out[:, :1].shape)
    return jnp.concatenate([out, far], axis=1).astype(BF16)


def _rwkv_kernel(h_ref, mu_ref, wup_ref, w0_ref, aup_ref, a0_ref, kk_ref, ka_ref, rk_ref, gw_ref, gb_ref,
                 o_ref, st_ref, prev_ref):
    @pl.when(pl.program_id(1) == 0)
    def _():
        st_ref[...] = jnp.zeros_like(st_ref)
        prev_ref[...] = jnp.zeros_like(prev_ref)

    nc = RWKV_TIME_BLOCK // CHUNK
    nh = RWKV_HEADS
    w = RWKV_W
    p = h_ref[0]
    shifted = pltpu.roll(p, 1, axis=0)
    p_prev = jnp.where(_iota2(p.shape, 0) == 0, prev_ref[...], shifted)
    prev_ref[...] = p[RWKV_TIME_BLOCK - 1:RWKV_TIME_BLOCK, :]
    p = p + (p_prev - p) * mu_ref[...]
    r, k, v, g = p[:, 0:w], p[:, w:2 * w], p[:, 2 * w:3 * w], p[:, 3 * w:4 * w]
    w_dn = p[:, 4 * w:4 * w + RWKV_LORA]
    a_dn = p[:, 4 * w + RWKV_LORA:4 * w + 2 * RWKV_LORA]
    d = w0_ref[...] + _mm(jnp.tanh(w_dn), wup_ref[...], na=2, nb=2)
    lw = -jnp.exp(-_softplus(-d) - 0.5)
    a = _sigmoid(a0_ref[...] + _mm(a_dn, aup_ref[...], na=2, nb=2))
    same_head = (_iota2((w, w), 0) // HEAD_DIM == _iota2((w, w), 1) // HEAD_DIM).astype(BF16)
    kk = k * kk_ref[...]
    kk = kk / jnp.maximum(jnp.sqrt(_mm(kk * kk, same_head, na=3)), 1e-12)
    k = k * (1.0 + (a - 1.0) * ka_ref[...])
    bonus = _mm(r * k * rk_ref[...], same_head, na=3) * v

    shape3 = (nh * nc, CHUNK, CHUNK)
    row = _iota2(shape3, 1)
    col = _iota2(shape3, 2)
    tri_b = (row >= col).astype(BF16)
    eye = row == col
    hp = dict()
    carry_prec = dict(na=2, nb=2)
    r, lw, k, v = (_split_heads(x, nc) for x in (r, lw, k, v))
    a_vec = _split_heads(-kk, nc)
    b_vec = _split_heads(kk * a, nc)
    cum = _mm(tri_b, lw, _BNN, nb=3)
    last = cum[:, CHUNK - 1:CHUNK, :]
    inv = jnp.exp(-cum)
    to_end = jnp.exp(last - cum)
    r_t = r * jnp.exp(cum)
    a_t = a_vec * jnp.exp(cum - lw)
    k_t = k * inv
    b_t = b_vec * inv
    a_ab = jnp.where(row > col, _mm(a_t, b_t, _BNT, **hp), 0.0)
    a_ak = jnp.where(row > col, _mm(a_t, k_t, _BNT, **hp), 0.0)
    a_rb = jnp.where(row >= col, _mm(r_t, b_t, _BNT, **hp), 0.0)
    a_rk = jnp.where(row >= col, _mm(r_t, k_t, _BNT, **hp), 0.0)
    inv_t = jnp.where(eye, 1.0, a_ab)
    power = a_ab
    for _ in range(int(math.log2(CHUNK)) - 1):
        power = _mm(power, power, _BNN, **hp)
        inv_t = inv_t + _mm(inv_t, power, _BNN, **hp)
    w_mat = _mm(inv_t, a_t, _BNN, **hp)
    u0 = _mm(inv_t, _mm(a_ak, v, _BNN, **hp), _BNN, **hp)
    y_w = r_t + _mm(a_rb, w_mat, _BNN, **hp)
    y0 = _mm(a_rb, u0, _BNN, **hp) + _mm(a_rk, v, _BNN, **hp)
    b_end = b_vec * to_end
    k_end = k * to_end
    m_mat = jnp.where(eye, jnp.exp(last), 0.0) + _mm(b_end, w_mat, _BTN, **hp)
    n_mat = _mm(b_end, u0, _BTN, **hp) + _mm(k_end, v, _BTN, **hp)
    st = st_ref[...]
    ys = []
    for c in range(nc):
        ys.append(_mm(_chunk_of(y_w, nh, c), st, _BNN, **hp))
        st = _mm(_chunk_of(m_mat, nh, c), st, _BNN, **carry_prec) + _chunk_of(n_mat, nh, c)
    st_ref[...] = st
    y = y0 + jnp.stack(ys, axis=1).reshape(shape3)
    mean = jnp.mean(y, axis=-1, keepdims=True)
    var = jnp.mean(jnp.square(y - mean), axis=-1, keepdims=True)
    y = _merge_heads((y - mean) * lax.rsqrt(var + RWKV_GN_EPS), nh) * gw_ref[...] + gb_ref[...]
    o_ref[0] = (y + bonus) * (g * _sigmoid(g))


def _rwkv(h_rwkv, mu, w_up, w0, a_up, a0, k_k, k_a, r_k, gn_w, gn_b):
    b, t, c = h_rwkv.shape
    full = lambda a: pl.BlockSpec(a.shape, lambda i, s: (0, 0))
    params = (mu, w_up, w0, a_up, a0, k_k, k_a, r_k, gn_w, gn_b)
    return pl.pallas_call(
        _rwkv_kernel,
        grid=(b, t // RWKV_TIME_BLOCK),
        in_specs=[pl.BlockSpec((1, RWKV_TIME_BLOCK, c), lambda i, s: (i, s, 0))] + [full(a) for a in params],
        out_specs=pl.BlockSpec((1, RWKV_TIME_BLOCK, RWKV_W), lambda i, s: (i, s, 0)),
        out_shape=jax.ShapeDtypeStruct((b, t, RWKV_W), F32),
        scratch_shapes=[pltpu.VMEM((RWKV_HEADS, HEAD_DIM, HEAD_DIM), F32), pltpu.VMEM((1, c), F32)],
        compiler_params=pltpu.CompilerParams(
            dimension_semantics=("parallel", "arbitrary"), vmem_limit_bytes=VMEM_LIMIT),
    )(h_rwkv, *params)


def _proj_out_kernel(alpha, og_ref, ot_ref, or_ref, w_ref, x_ref, lw_ref, lb_ref, y_ref):
    y = _mm(og_ref[0].astype(BF16), w_ref[0:GLA_W, :])
    y = y + _mm(ot_ref[0].astype(BF16), w_ref[GLA_W:GLA_W + MOBA_W, :], _TN)
    y = y + _mm(or_ref[0].astype(BF16), w_ref[GLA_W + MOBA_W:, :])
    z = alpha * x_ref[0] + y
    mu = jnp.mean(z, axis=-1, keepdims=True)
    var = jnp.mean(jnp.square(z - mu), axis=-1, keepdims=True)
    y_ref[0] = (z - mu) * lax.rsqrt(var + LN_EPS) * lw_ref[...] + lb_ref[...]


def _proj_out(o_gla, ot_moba, o_rwkv, w, x, ln_w, ln_b, alpha):
    b, t, d = x.shape
    rs = lambda n: pl.BlockSpec((1, ROW_BLOCK, n), lambda i, s: (i, s, 0))
    vs = pl.BlockSpec((1, d), lambda i, s: (0, 0))
    return pl.pallas_call(
        functools.partial(_proj_out_kernel, alpha),
        grid=(b, t // ROW_BLOCK),
        in_specs=[rs(GLA_W), pl.BlockSpec((1, MOBA_W, ROW_BLOCK), lambda i, s: (i, 0, s)), rs(RWKV_W),
                  pl.BlockSpec(w.shape, lambda i, s: (0, 0)), rs(d), vs, vs],
        out_specs=rs(d),
        out_shape=jax.ShapeDtypeStruct((b, t, d), F32),
        compiler_params=pltpu.CompilerParams(
            dimension_semantics=("parallel", "parallel"), vmem_limit_bytes=VMEM_LIMIT),
    )(o_gla, ot_moba, o_rwkv, w, x, ln_w, ln_b)


def _pad_cols(a, n):
    return jnp.pad(a, ((0, 0), (0, n - a.shape[1])))


def _split_w_in(w_l):
    gla, moba = w_l[:, :GLA_COLS], w_l[:, GLA_COLS:GLA_COLS + MOBA_COLS]
    rwkv = w_l[:, GLA_COLS + MOBA_COLS:]
    mq, mk, mv, mg = (moba[:, i * MOBA_W:(i + 1) * MOBA_W] for i in range(4))
    w_t = jnp.concatenate([mq, mg], axis=1).T
    return tuple(a.astype(BF16) for a in (_pad_cols(gla, GLA_PAD), _pad_cols(rwkv, RWKV_PAD), mk, w_t, mv.T))


def _moba_branch(ht, k_aug, kbar, vt_aug, bias):
    return _moba_attn(ht, kbar.reshape(kbar.shape[0], -1, MOBA_W), k_aug, vt_aug, bias)


def _gla_branch(h_gla, a_up, a_bias, norm_w):
    return _gla(h_gla, a_up, a_bias.reshape(1, GLA_W), norm_w.reshape(1, HEAD_DIM))


def _rwkv_branch(h_rwkv, mu, w_up, w0, a_up, a0, k_k, k_a, r_k, gn_w, gn_b):
    row = lambda p: p.reshape(1, -1)
    return _rwkv(h_rwkv, _pad_cols(row(mu), h_rwkv.shape[2]), w_up, row(w0), a_up, row(a0), row(k_k), row(k_a),
                 row(r_k), row(gn_w), row(gn_b))


def kernel(x, w_in, w_out, gla_a_up, gla_a_bias, gla_norm_w, moba_rel_bias, rwkv_mu, rwkv_w_up, rwkv_w0,
           rwkv_a_up, rwkv_a0, rwkv_k_k, rwkv_k_a, rwkv_r_k, rwkv_gn_w, rwkv_gn_b, ln_w, ln_b):
    bsz, t, d = x.shape
    depth = w_in.shape[0]
    alpha = (2.0 * depth) ** 0.25
    bias = _bias_tiles(moba_rel_bias)
    for l in range(depth):
        h_gla, h_rwkv, ht, k_aug, kbar, vt_aug = _proj_in(x, *_split_w_in(w_in[l]))
        o_gla = _gla_branch(h_gla, gla_a_up[l], gla_a_bias[l], gla_norm_w[l])
        ot_moba = _moba_branch(ht, k_aug, kbar, vt_aug, bias)
        o_rwkv = _rwkv_branch(h_rwkv, rwkv_mu[l], rwkv_w_up[l], rwkv_w0[l], rwkv_a_up[l], rwkv_a0[l],
                              rwkv_k_k[l], rwkv_k_a[l], rwkv_r_k[l], rwkv_gn_w[l], rwkv_gn_b[l])
        x = _proj_out(o_gla, ot_moba, o_rwkv, w_out[l].astype(BF16), x,
                      ln_w[l].reshape(1, d), ln_b[l].reshape(1, d), alpha)
    return x
```

```python
import functools
import math

import jax
import jax.numpy as jnp
from jax import lax
from jax.experimental import pallas as pl
from jax.experimental.pallas import tpu as pltpu

HEAD_DIM = 64
GLA_HEADS = 4
MOBA_HEADS = 8
RWKV_HEADS = 4
GLA_W = GLA_HEADS * HEAD_DIM
MOBA_W = MOBA_HEADS * HEAD_DIM
RWKV_W = RWKV_HEADS * HEAD_DIM
GLA_LOWRANK = 16
GLA_TAU = 16.0
RWKV_LORA = 32
RWKV_GN_EPS = 64e-5
MOBA_BLOCK = 256
MOBA_TOPK = 3
N_BUCKETS = 32
MAX_DISTANCE = 4096
LN_EPS = 1e-5
GLA_COLS = 4 * GLA_W + GLA_LOWRANK
MOBA_COLS = 4 * MOBA_W
RWKV_COLS = 4 * RWKV_W + 2 * RWKV_LORA

LANE = 128
SUBLANE_BF16 = 16
GLA_PAD = -(-GLA_COLS // LANE) * LANE
RWKV_PAD = -(-RWKV_COLS // LANE) * LANE
CHUNK = 64
SUB = 16
GLA_TIME_BLOCK = 1024
RWKV_TIME_BLOCK = 512
ROW_BLOCK = 512
N_BIAS_TILES = (MAX_DISTANCE + MOBA_BLOCK - 1) // MOBA_BLOCK + 1
KV_GROUP = 4
MOBA_HEADS_PER_STEP = 4
NEAR_BLOCKS = -(-N_BIAS_TILES // KV_GROUP) * KV_GROUP
VMEM_LIMIT = 56 * 1024 * 1024

F32 = jnp.float32
BF16 = jnp.bfloat16
NEG_INF = float("-inf")
MASK_BIG = 2.0 ** 100


def _parts(a, n):
    out, r = [], a
    for i in range(n):
        p = r.astype(BF16)
        out.append(p)
        if i + 1 < n:
            r = r - p.astype(F32)
    return out


def _mm(a, b, dims=(((1,), (0,)), ((), ())), na=1, nb=1):
    ap = [a] if a.dtype == BF16 else _parts(a, na)
    bp = [b] if b.dtype == BF16 else _parts(b, nb)
    n = max(len(ap), len(bp))
    acc = None
    for i, x in enumerate(ap):
        for j, y in enumerate(bp):
            if i + j < n:
                t = lax.dot_general(x, y, dims, preferred_element_type=F32)
                acc = t if acc is None else acc + t
    return acc


_NT = (((1,), (1,)), ((), ()))
_TN = (((0,), (0,)), ((), ()))
_BNN = (((2,), (1,)), ((0,), (0,)))
_BNT = (((2,), (2,)), ((0,), (0,)))
_BTN = (((1,), (1,)), ((0,), (0,)))


def _softplus(x):
    return jnp.maximum(x, 0.0) + jnp.log(1.0 + jnp.exp(-jnp.abs(x)))


def _sigmoid(x):
    return 1.0 / (1.0 + jnp.exp(-x))


def _iota2(shape, dim):
    return lax.broadcasted_iota(jnp.int32, shape, dim)


def _split_heads(x, nc):
    n = x.shape[1] // HEAD_DIM
    return jnp.concatenate(
        [x[:, h * HEAD_DIM:(h + 1) * HEAD_DIM].reshape(nc, CHUNK, HEAD_DIM) for h in range(n)], axis=0)


def _merge_heads(x, n):
    nc = x.shape[0] // n
    return jnp.concatenate([x[h * nc:(h + 1) * nc].reshape(nc * CHUNK, HEAD_DIM) for h in range(n)], axis=1)


def _chunk_of(x, n, c):
    return x.reshape((n, x.shape[0] // n) + x.shape[1:])[:, c]


def _proj_in_kernel(x_ref, wg_ref, wr_ref, wk_ref, wt_ref, wv_ref,
                    hg_ref, hr_ref, ht_ref, kaug_ref, kbar_ref, vt_ref):
    nb = kaug_ref.shape[3] - HEAD_DIM
    blocks = ROW_BLOCK // MOBA_BLOCK
    xb = x_ref[0].astype(BF16)
    hg_ref[0] = jnp.dot(xb, wg_ref[...], preferred_element_type=F32)
    hr_ref[0] = jnp.dot(xb, wr_ref[...], preferred_element_type=F32)
    ht_ref[0] = lax.dot_general(wt_ref[...], xb, _NT, preferred_element_type=F32)
    k = jnp.dot(xb, wk_ref[...], preferred_element_type=F32)
    first = pl.program_id(1) * blocks
    onehot = (_iota2((ROW_BLOCK, nb), 1) == first + _iota2((ROW_BLOCK, nb), 0) // MOBA_BLOCK).astype(BF16)
    for h in range(MOBA_HEADS):
        kaug_ref[0, h] = jnp.concatenate([k[:, h * HEAD_DIM:(h + 1) * HEAD_DIM].astype(BF16), onehot], axis=1)
    for i in range(blocks):
        kbar_ref[0, 0, i:i + 1, :] = jnp.sum(k[i * MOBA_BLOCK:(i + 1) * MOBA_BLOCK], axis=0,
                                             keepdims=True) * (1.0 / MOBA_BLOCK)
    vt = lax.dot_general(wv_ref[...], xb, _NT, preferred_element_type=F32)
    pad = jnp.concatenate([jnp.ones((1, MOBA_BLOCK), F32), jnp.zeros((SUBLANE_BF16 - 1, MOBA_BLOCK), F32)], axis=0)
    for h in range(MOBA_HEADS):
        for i in range(blocks):
            tile = vt[h * HEAD_DIM:(h + 1) * HEAD_DIM, i * MOBA_BLOCK:(i + 1) * MOBA_BLOCK]
            vt_ref[0, h, i] = jnp.concatenate([tile, pad], axis=0).astype(BF16)


def _proj_in(x, w_gla, w_rwkv, w_k, w_t, w_v):
    b, t, d = x.shape
    nb = t // MOBA_BLOCK
    blocks = ROW_BLOCK // MOBA_BLOCK
    full = lambda w: pl.BlockSpec(w.shape, lambda i, s: (0, 0))
    rows = lambda n: pl.BlockSpec((1, ROW_BLOCK, n), lambda i, s: (i, s, 0))
    nt = w_t.shape[0]
    vrows = HEAD_DIM + SUBLANE_BF16
    return pl.pallas_call(
        _proj_in_kernel,
        grid=(b, t // ROW_BLOCK),
        in_specs=[rows(d), full(w_gla), full(w_rwkv), full(w_k), full(w_t), full(w_v)],
        out_specs=[rows(w_gla.shape[1]), rows(w_rwkv.shape[1]),
                   pl.BlockSpec((1, nt, ROW_BLOCK), lambda i, s: (i, 0, s)),
                   pl.BlockSpec((1, MOBA_HEADS, ROW_BLOCK, HEAD_DIM + nb), lambda i, s: (i, 0, s, 0)),
                   pl.BlockSpec((1, 1, blocks, MOBA_W), lambda i, s: (i, s, 0, 0)),
                   pl.BlockSpec((1, MOBA_HEADS, blocks, vrows, MOBA_BLOCK), lambda i, s: (i, 0, s, 0, 0))],
        out_shape=[jax.ShapeDtypeStruct((b, t, w_gla.shape[1]), F32),
                   jax.ShapeDtypeStruct((b, t, w_rwkv.shape[1]), F32),
                   jax.ShapeDtypeStruct((b, nt, t), F32),
                   jax.ShapeDtypeStruct((b, MOBA_HEADS, t, HEAD_DIM + nb), BF16),
                   jax.ShapeDtypeStruct((b, nb // blocks, blocks, MOBA_W), F32),
                   jax.ShapeDtypeStruct((b, MOBA_HEADS, nb, vrows, MOBA_BLOCK), BF16)],
        compiler_params=pltpu.CompilerParams(
            dimension_semantics=("parallel", "parallel"), vmem_limit_bytes=VMEM_LIMIT),
    )(x, w_gla, w_rwkv, w_k, w_t, w_v)


def _gla_kernel(h_ref, aup_ref, ab_ref, nw_ref, o_ref, st_ref):
    @pl.when(pl.program_id(1) == 0)
    def _():
        st_ref[...] = jnp.zeros_like(st_ref)

    nc = GLA_TIME_BLOCK // CHUNK
    nh = GLA_HEADS
    shape3 = (nh * nc, CHUNK, CHUNK)
    hb = h_ref[0]
    z = _mm(hb[:, 4 * GLA_W:4 * GLA_W + GLA_LOWRANK], aup_ref[...], na=2, nb=2) + ab_ref[...]
    la = _split_heads(-_softplus(-z) * (1.0 / GLA_TAU), nc)
    row = _iota2(shape3, 1)
    col = _iota2(shape3, 2)
    tri = (row >= col)
    anchor_b = (col <= (row // SUB) * SUB + (SUB - 1)).astype(BF16)
    q = _split_heads(hb[:, 0:GLA_W], nc) * (HEAD_DIM ** -0.5)
    k = _split_heads(hb[:, GLA_W:2 * GLA_W], nc)
    v = _split_heads(hb[:, 2 * GLA_W:3 * GLA_W], nc)
    cum = _mm(tri.astype(BF16), la, _BNN, nb=3)
    k_anchor = _mm(anchor_b, la, _BNN, nb=3)
    k_t = k * jnp.exp(k_anchor - cum)
    scores = jnp.zeros(shape3, F32)
    for j in range(CHUNK // SUB):
        a_j = cum[:, j * SUB + SUB - 1:j * SUB + SUB, :]
        q_j = q * jnp.exp(jnp.where(row >= j * SUB, cum - a_j, 0.0))
        in_grp = (row >= j * SUB) & (row < (j + 1) * SUB)
        scores = scores + _mm(q_j, jnp.where(in_grp, k_t, 0.0), _BNT)
    o_intra = _mm(jnp.where(tri, scores, 0.0), v, _BNN)
    last = cum[:, CHUNK - 1:CHUNK, :]
    n_mat = _mm(v, k * jnp.exp(last - cum), _BTN)
    q_dec = q * jnp.exp(cum)
    dec_last = jnp.exp(last)
    st = st_ref[...]
    o_inter = []
    for c in range(nc):
        o_inter.append(_mm(_chunk_of(q_dec, nh, c), st, _BNT))
        st = st * _chunk_of(dec_last, nh, c) + _chunk_of(n_mat, nh, c)
    st_ref[...] = st
    o = o_intra + jnp.stack(o_inter, axis=1).reshape(shape3)
    o = o * lax.rsqrt(jnp.mean(o * o, axis=-1, keepdims=True) + LN_EPS) * nw_ref[...]
    g = hb[:, 3 * GLA_W:4 * GLA_W]
    o_ref[0] = _merge_heads(o, nh) * (g * _sigmoid(g))


def _gla(h_gla, a_up, a_bias, norm_w):
    b, t, c = h_gla.shape
    vec = lambda a: pl.BlockSpec(a.shape, lambda i, s: (0, 0))
    return pl.pallas_call(
        _gla_kernel,
        grid=(b, t // GLA_TIME_BLOCK),
        in_specs=[pl.BlockSpec((1, GLA_TIME_BLOCK, c), lambda i, s: (i, s, 0)), vec(a_up), vec(a_bias),
                  vec(norm_w)],
        out_specs=pl.BlockSpec((1, GLA_TIME_BLOCK, GLA_W), lambda i, s: (i, s, 0)),
        out_shape=jax.ShapeDtypeStruct((b, t, GLA_W), F32),
        scratch_shapes=[pltpu.VMEM((GLA_HEADS, HEAD_DIM, HEAD_DIM), F32)],
        compiler_params=pltpu.CompilerParams(
            dimension_semantics=("parallel", "arbitrary"), vmem_limit_bytes=VMEM_LIMIT),
    )(h_gla, a_up, a_bias, norm_w)


def _block_penalty(q_t, kbar, own):
    gate = _mm(kbar, q_t, na=2, nb=2)
    blk = _iota2(gate.shape, 0)
    nb = gate.shape[0]
    gate = jnp.where(blk < own, gate, NEG_INF)
    allowed = blk == own
    for _ in range(min(MOBA_TOPK, nb)):
        best = jnp.max(gate, axis=0, keepdims=True)
        first = jnp.min(jnp.where(gate == best, blk, nb), axis=0, keepdims=True)
        hit = blk == first
        allowed = allowed | (hit & (blk < own))
        gate = jnp.where(hit, NEG_INF, gate)
    return jnp.where(allowed, 0.0, -MASK_BIG).astype(BF16)


def _moba_attn_kernel(qt_ref, kbar_ref, k_ref, vt_ref, gt_ref, bias_ref, o_ref, *scratch):
    hps = MOBA_HEADS_PER_STEP
    own = pl.program_id(2)
    nb = kbar_ref.shape[1]
    n_far = jnp.maximum(own + 1 - NEAR_BLOCKS, 0)
    near_groups = (jnp.minimum(own + 1, NEAR_BLOCKS) + KV_GROUP - 1) // KV_GROUP
    n_groups = near_groups + (n_far + KV_GROUP - 1) // KV_GROUP
    head_rows = lambda hh: slice(hh * HEAD_DIM, (hh + 1) * HEAD_DIM)
    q = [(qt_ref[0, head_rows(hh), :] * (HEAD_DIM ** -0.5)).astype(BF16) for hh in range(hps)]
    pen = [_block_penalty(qt_ref[0, head_rows(hh), :], kbar_ref[0][:, head_rows(hh)], own) for hh in range(hps)]
    blk = _iota2(pen[0].shape, 0)
    sa, sb, ma, mb = (scratch[i * hps:(i + 1) * hps] for i in range(4))

    def first_block(u):
        hi_near = own + 1 - u * KV_GROUP
        j0 = jnp.where(u < near_groups, hi_near - KV_GROUP, (u - near_groups) * KV_GROUP)
        return jnp.clip(j0, 0, nb - KV_GROUP), hi_near

    def scores(u, hh, dst, dst_max):
        is_near = u < near_groups
        j0, hi_near = first_block(u)
        hi = jnp.where(is_near, hi_near, n_far)
        lo = jnp.where(is_near, hi_near - KV_GROUP, 0)
        in_range = (blk >= lo) & (blk < hi)
        q_aug = jnp.concatenate([q[hh], jnp.where(in_range, pen[hh], jnp.asarray(-MASK_BIG, BF16))], axis=0)
        top = None
        for i in range(KV_GROUP):
            rows = pl.ds(pl.multiple_of((j0 + i) * MOBA_BLOCK, MOBA_BLOCK), MOBA_BLOCK)
            tile = jnp.clip(own - (j0 + i), 0, N_BIAS_TILES)
            s = _mm(k_ref[0, hh, rows, :], q_aug).astype(BF16) + bias_ref[hh, tile]
            dst[i * MOBA_BLOCK:(i + 1) * MOBA_BLOCK, :] = s
            tile_top = jnp.max(s, axis=0, keepdims=True)
            top = tile_top if top is None else jnp.maximum(top, tile_top)
        dst_max[...] = top.astype(F32)

    def consume(u, hh, src, src_max, carry):
        m, acc = carry
        j0, _ = first_block(u)
        ref_b = jnp.maximum(m, src_max[...]).astype(BF16)
        m_new = ref_b.astype(F32)
        acc = acc * jnp.exp(m - m_new)
        for i in range(KV_GROUP):
            s = src[i * MOBA_BLOCK:(i + 1) * MOBA_BLOCK, :]
            acc = acc + _mm(vt_ref[0, hh, j0 + i], jnp.exp(s - ref_b))
        return m_new, acc

    carry = tuple((jnp.full((1, MOBA_BLOCK), -MASK_BIG, F32), jnp.zeros((vt_ref.shape[3], MOBA_BLOCK), F32))
                  for _ in range(hps))
    for hh in range(hps):
        scores(0, hh, sa[hh], ma[hh])

    def pair(v, carry):
        u = 2 * v
        carry = list(carry)
        for hh in range(hps):
            scores(u + 1, hh, sb[hh], mb[hh])
            carry[hh] = consume(u, hh, sa[hh], ma[hh], carry[hh])
        for hh in range(hps):
            scores(u + 2, hh, sa[hh], ma[hh])
            carry[hh] = consume(u + 1, hh, sb[hh], mb[hh], carry[hh])
        return tuple(carry)

    n_pairs = (n_groups - 1) // 2
    carry = lax.fori_loop(0, n_pairs, pair, carry)
    u_last = 2 * n_pairs

    def last_two(carry):
        carry = list(carry)
        for hh in range(hps):
            scores(u_last + 1, hh, sb[hh], mb[hh])
            carry[hh] = consume(u_last, hh, sa[hh], ma[hh], carry[hh])
        return tuple(consume(u_last + 1, hh, sb[hh], mb[hh], carry[hh]) for hh in range(hps))

    def last_one(carry):
        return tuple(consume(u_last, hh, sa[hh], ma[hh], carry[hh]) for hh in range(hps))

    carry = lax.cond(n_groups - u_last == 2, last_two, last_one, carry)
    for hh in range(hps):
        _, acc = carry[hh]
        g = gt_ref[0, head_rows(hh), :]
        o_ref[0, head_rows(hh), :] = acc[:HEAD_DIM] / acc[HEAD_DIM:HEAD_DIM + 1] * (g * _sigmoid(g))


def _moba_attn(ht, kbar, k_aug, vt_aug, bias):
    b, _, t = ht.shape
    h, nb = k_aug.shape[1], kbar.shape[1]
    hps = MOBA_HEADS_PER_STEP
    rows = hps * HEAD_DIM
    qs = pl.BlockSpec((1, rows, MOBA_BLOCK), lambda i, j, s: (i, j, s))
    gs = pl.BlockSpec((1, rows, MOBA_BLOCK), lambda i, j, s: (i, h // hps + j, s))
    once = pl.Buffered(1)
    heads = lambda a: pl.BlockSpec((1, hps) + a.shape[2:], lambda i, j, s: (i, j) + (0,) * (a.ndim - 2),
                                   pipeline_mode=once)
    return pl.pallas_call(
        _moba_attn_kernel,
        grid=(b, h // hps, nb),
        in_specs=[qs, pl.BlockSpec((1, nb, rows), lambda i, j, s: (i, 0, j)), heads(k_aug), heads(vt_aug), gs,
                  pl.BlockSpec((hps,) + bias.shape[1:], lambda i, j, s: (j, 0, 0, 0), pipeline_mode=once)],
        out_specs=qs,
        out_shape=jax.ShapeDtypeStruct((b, h * HEAD_DIM, t), F32),
        scratch_shapes=[pltpu.VMEM((KV_GROUP * MOBA_BLOCK, MOBA_BLOCK), BF16)] * (2 * hps)
        + [pltpu.VMEM((1, MOBA_BLOCK), F32)] * (2 * hps),
        compiler_params=pltpu.CompilerParams(
            dimension_semantics=("parallel", "parallel", "arbitrary"), vmem_limit_bytes=VMEM_LIMIT),
    )(ht, kbar, k_aug, vt_aug, ht, bias)


def _t5_bucket(rel):
    rel = jnp.maximum(rel, 0)
    max_exact = N_BUCKETS // 2
    rel_f = jnp.maximum(rel, 1).astype(F32)
    large = max_exact + (jnp.log(rel_f / max_exact) / math.log(MAX_DISTANCE / max_exact)
                         * (N_BUCKETS - max_exact)).astype(jnp.int32)
    large = jnp.minimum(large, N_BUCKETS - 1)
    return jnp.where(rel < max_exact, rel, large)


def _bias_tiles(rel_bias):
    d = jnp.arange(N_BIAS_TILES)[:, None, None] * MOBA_BLOCK
    rel = d + jnp.arange(MOBA_BLOCK)[None, None, :] - jnp.arange(MOBA_BLOCK)[None, :, None]
    bucket = _t5_bucket(rel)
    tab = rel_bias.T.astype(F32)
    out = jnp.zeros((tab.shape[0],) + bucket.shape, F32)
    for bk in range(N_BUCKETS):
        out = jnp.where(bucket[None] == bk, tab[:, bk][:, None, None, None], out)
    out = jnp.where(rel[None] < 0, -MASK_BIG, out)
    far =jnp.broadcast_to(tab[:, N_BUCKETS - 1][:, None, None, None], out[:, :1].shape)
    return jnp.concatenate([out, far], axis=1).astype(BF16)


def _rwkv_kernel(h_ref, mu_ref, wup_ref, w0_ref, aup_ref, a0_ref, kk_ref, ka_ref, rk_ref, gw_ref, gb_ref,
                 o_ref, st_ref, prev_ref):
    @pl.when(pl.program_id(1) == 0)
    def _():
        st_ref[...] = jnp.zeros_like(st_ref)
        prev_ref[...] = jnp.zeros_like(prev_ref)

    nc = RWKV_TIME_BLOCK // CHUNK
    nh = RWKV_HEADS
    w = RWKV_W
    p = h_ref[0]
    shifted = pltpu.roll(p, 1, axis=0)
    p_prev = jnp.where(_iota2(p.shape, 0) == 0, prev_ref[...], shifted)
    prev_ref[...] = p[RWKV_TIME_BLOCK - 1:RWKV_TIME_BLOCK, :]
    p = p + (p_prev - p) * mu_ref[...]
    r, k, v, g = p[:, 0:w], p[:, w:2 * w], p[:, 2 * w:3 * w], p[:, 3 * w:4 * w]
    w_dn = p[:, 4 * w:4 * w + RWKV_LORA]
    a_dn = p[:, 4 * w + RWKV_LORA:4 * w + 2 * RWKV_LORA]
    d = w0_ref[...] + _mm(jnp.tanh(w_dn), wup_ref[...], na=2, nb=2)
    lw = -jnp.exp(-_softplus(-d) - 0.5)
    a = _sigmoid(a0_ref[...] + _mm(a_dn, aup_ref[...], na=2, nb=2))
    same_head = (_iota2((w, w), 0) // HEAD_DIM == _iota2((w, w), 1) // HEAD_DIM).astype(BF16)
    kk = k * kk_ref[...]
    kk = kk / jnp.maximum(jnp.sqrt(_mm(kk * kk, same_head, na=3)), 1e-12)
    k = k * (1.0 + (a - 1.0) * ka_ref[...])
    bonus = _mm(r * k * rk_ref[...], same_head, na=3) * v

    shape3 = (nh * nc, CHUNK, CHUNK)
    row = _iota2(shape3, 1)
    col = _iota2(shape3, 2)
    tri_b = (row >= col).astype(BF16)
    eye = row == col
    hp = dict()
    carry_prec = dict(na=2, nb=2)
    r, lw, k, v = (_split_heads(x, nc) for x in (r, lw, k, v))
    a_vec = _split_heads(-kk, nc)
    b_vec = _split_heads(kk * a, nc)
    cum = _mm(tri_b, lw, _BNN, nb=3)
    last = cum[:, CHUNK - 1:CHUNK, :]
    inv = jnp.exp(-cum)
    to_end = jnp.exp(last - cum)
    r_t = r * jnp.exp(cum)
    a_t = a_vec * jnp.exp(cum - lw)
    k_t = k * inv
    b_t = b_vec * inv
    a_ab = jnp.where(row > col, _mm(a_t, b_t, _BNT, **hp), 0.0)
    a_ak = jnp.where(row > col, _mm(a_t, k_t, _BNT, **hp), 0.0)
    a_rb = jnp.where(row >= col, _mm(r_t, b_t, _BNT, **hp), 0.0)
    a_rk = jnp.where(row >= col, _mm(r_t, k_t, _BNT, **hp), 0.0)
    inv_t = jnp.where(eye, 1.0, a_ab)
    power = a_ab
    for _ in range(int(math.log2(CHUNK)) - 1):
        power = _mm(power, power, _BNN, **hp)
        inv_t = inv_t + _mm(inv_t, power, _BNN, **hp)
    w_mat = _mm(inv_t, a_t, _BNN, **hp)
    u0 = _mm(inv_t, _mm(a_ak, v, _BNN, **hp), _BNN, **hp)
    y_w = r_t + _mm(a_rb, w_mat, _BNN, **hp)
    y0 = _mm(a_rb, u0, _BNN, **hp) + _mm(a_rk, v, _BNN, **hp)
    b_end = b_vec * to_end
    k_end = k * to_end
    m_mat = jnp.where(eye, jnp.exp(last), 0.0) + _mm(b_end, w_mat, _BTN, **hp)
    n_mat = _mm(b_end, u0, _BTN, **hp) + _mm(k_end, v, _BTN, **hp)
    st = st_ref[...]
    ys = []
    for c in range(nc):
        ys.append(_mm(_chunk_of(y_w, nh, c), st, _BNN, **hp))
        st = _mm(_chunk_of(m_mat, nh, c), st, _BNN, **carry_prec) + _chunk_of(n_mat, nh, c)
    st_ref[...] = st
    y = y0 + jnp.stack(ys, axis=1).reshape(shape3)
    mean = jnp.mean(y, axis=-1, keepdims=True)
    var = jnp.mean(jnp.square(y - mean), axis=-1, keepdims=True)
    y = _merge_heads((y - mean) * lax.rsqrt(var + RWKV_GN_EPS), nh) * gw_ref[...] + gb_ref[...]
    o_ref[0] = (y + bonus) * (g * _sigmoid(g))


def _rwkv(h_rwkv, mu, w_up, w0, a_up, a0, k_k, k_a, r_k, gn_w, gn_b):
    b, t, c = h_rwkv.shape
    full = lambda a: pl.BlockSpec(a.shape, lambda i, s: (0, 0))
    params = (mu, w_up, w0, a_up, a0, k_k, k_a, r_k, gn_w, gn_b)
    return pl.pallas_call(
        _rwkv_kernel,
        grid=(b, t // RWKV_TIME_BLOCK),
        in_specs=[pl.BlockSpec((1, RWKV_TIME_BLOCK, c), lambda i, s: (i, s, 0))] + [full(a) for a in params],
        out_specs=pl.BlockSpec((1, RWKV_TIME_BLOCK, RWKV_W), lambda i, s: (i, s, 0)),
        out_shape=jax.ShapeDtypeStruct((b, t, RWKV_W), F32),
        scratch_shapes=[pltpu.VMEM((RWKV_HEADS, HEAD_DIM, HEAD_DIM), F32), pltpu.VMEM((1, c), F32)],
        compiler_params=pltpu.CompilerParams(
            dimension_semantics=("parallel", "arbitrary"), vmem_limit_bytes=VMEM_LIMIT),
    )(h_rwkv, *params)


def _proj_out_kernel(alpha, og_ref, ot_ref, or_ref, w_ref, x_ref, lw_ref, lb_ref, y_ref):
    y = _mm(og_ref[0].astype(BF16), w_ref[0:GLA_W, :])
    y = y + _mm(ot_ref[0].astype(BF16), w_ref[GLA_W:GLA_W + MOBA_W, :], _TN)
    y = y + _mm(or_ref[0].astype(BF16), w_ref[GLA_W + MOBA_W:, :])
    z = alpha * x_ref[0] + y
    mu = jnp.mean(z, axis=-1, keepdims=True)
    var = jnp.mean(jnp.square(z - mu), axis=-1, keepdims=True)
    y_ref[0] = (z - mu) * lax.rsqrt(var + LN_EPS) * lw_ref[...] + lb_ref[...]


def _proj_out(o_gla, ot_moba, o_rwkv, w, x, ln_w, ln_b, alpha):
    b, t, d = x.shape
    rs = lambda n: pl.BlockSpec((1, ROW_BLOCK, n), lambda i, s: (i, s, 0))
    vs = pl.BlockSpec((1, d), lambda i, s: (0, 0))
    return pl.pallas_call(
        functools.partial(_proj_out_kernel, alpha),
        grid=(b, t // ROW_BLOCK),
        in_specs=[rs(GLA_W), pl.BlockSpec((1, MOBA_W, ROW_BLOCK), lambda i, s: (i, 0, s)), rs(RWKV_W),
                  pl.BlockSpec(w.shape, lambda i, s: (0, 0)), rs(d), vs, vs],
        out_specs=rs(d),
        out_shape=jax.ShapeDtypeStruct((b, t, d), F32),
        compiler_params=pltpu.CompilerParams(
            dimension_semantics=("parallel", "parallel"), vmem_limit_bytes=VMEM_LIMIT),
    )(o_gla, ot_moba, o_rwkv, w, x, ln_w, ln_b)


def _pad_cols(a, n):
    return jnp.pad(a, ((0, 0), (0, n - a.shape[1])))


def _split_w_in(w_l):
    gla, moba = w_l[:, :GLA_COLS], w_l[:, GLA_COLS:GLA_COLS + MOBA_COLS]
    rwkv = w_l[:, GLA_COLS + MOBA_COLS:]
    mq, mk, mv, mg = (moba[:, i * MOBA_W:(i + 1) * MOBA_W] for i in range(4))
    w_t = jnp.concatenate([mq, mg], axis=1).T
    return tuple(a.astype(BF16) for a in (_pad_cols(gla, GLA_PAD), _pad_cols(rwkv, RWKV_PAD), mk, w_t, mv.T))


def _moba_branch(ht, k_aug, kbar, vt_aug, bias):
    return _moba_attn(ht, kbar.reshape(kbar.shape[0], -1, MOBA_W), k_aug, vt_aug, bias)


def _gla_branch(h_gla, a_up, a_bias, norm_w):
    return _gla(h_gla, a_up, a_bias.reshape(1, GLA_W), norm_w.reshape(1, HEAD_DIM))


def _rwkv_branch(h_rwkv, mu, w_up, w0, a_up, a0, k_k, k_a, r_k, gn_w, gn_b):
    row = lambda p: p.reshape(1, -1)
    return _rwkv(h_rwkv, _pad_cols(row(mu), h_rwkv.shape[2]), w_up, row(w0), a_up, row(a0), row(k_k), row(k_a),
                 row(r_k), row(gn_w), row(gn_b))


def kernel(x, w_in, w_out, gla_a_up, gla_a_bias, gla_norm_w, moba_rel_bias, rwkv_mu, rwkv_w_up, rwkv_w0,
           rwkv_a_up, rwkv_a0, rwkv_k_k, rwkv_k_a, rwkv_r_k, rwkv_gn_w, rwkv_gn_b, ln_w, ln_b):
    bsz, t, d = x.shape
    depth = w_in.shape[0]
    alpha = (2.0 * depth) ** 0.25
    bias = _bias_tiles(moba_rel_bias)
    for l in range(depth):
        h_gla, h_rwkv, ht, k_aug, kbar, vt_aug = _proj_in(x, *_split_w_in(w_in[l]))
        o_gla = _gla_branch(h_gla, gla_a_up[l], gla_a_bias[l], gla_norm_w[l])
        ot_moba = _moba_branch(ht, k_aug, kbar, vt_aug, bias)
        o_rwkv = _rwkv_branch(h_rwkv, rwkv_mu[l], rwkv_w_up[l], rwkv_w0[l], rwkv_a_up[l], rwkv_a0[l],
                              rwkv_k_k[l], rwkv_k_a[l], rwkv_r_k[l], rwkv_gn_w[l], rwkv_gn_b[l])
        x = _proj_out(o_gla, ot_moba, o_rwkv, w_out[l].astype(BF16), x,
                      ln_w[l].reshape(1, d), ln_b[l].reshape(1, d), alpha)
    return x
```

```python
import functools
import math

import jax
import jax.numpy as jnp
from jax import lax
from jax.experimental import pallas as pl
from jax.experimental.pallas import tpu as pltpu

HEAD_DIM = 64
GLA_HEADS = 4
MOBA_HEADS = 8
RWKV_HEADS = 4
GLA_W = GLA_HEADS * HEAD_DIM
MOBA_W = MOBA_HEADS * HEAD_DIM
RWKV_W = RWKV_HEADS * HEAD_DIM
GLA_LOWRANK = 16
GLA_TAU = 16.0
RWKV_LORA = 32
RWKV_GN_EPS = 64e-5
MOBA_BLOCK = 256
MOBA_TOPK = 3
N_BUCKETS = 32
MAX_DISTANCE = 4096
LN_EPS = 1e-5
GLA_COLS = 4 * GLA_W + GLA_LOWRANK
MOBA_COLS = 4 * MOBA_W
RWKV_COLS = 4 * RWKV_W + 2 * RWKV_LORA

LANE = 128
SUBLANE_BF16 = 16
GLA_PAD = -(-GLA_COLS // LANE) * LANE
RWKV_PAD = -(-RWKV_COLS // LANE) * LANE
CHUNK = 64
SUB = 16
GLA_TIME_BLOCK = 1024
RWKV_TIME_BLOCK = 512
ROW_BLOCK = 512
N_BIAS_TILES = (MAX_DISTANCE + MOBA_BLOCK - 1) // MOBA_BLOCK + 1
KV_GROUP = 4
MOBA_HEADS_PER_STEP = 4
NEAR_BLOCKS = -(-N_BIAS_TILES // KV_GROUP) * KV_GROUP
VMEM_LIMIT = 56 * 1024 * 1024

F32 = jnp.float32
BF16 = jnp.bfloat16
NEG_INF = float("-inf")
MASK_BIG = 2.0 ** 100


def _parts(a, n):
    out, r = [], a
    for i in range(n):
        p = r.astype(BF16)
        out.append(p)
        if i + 1 < n:
            r = r - p.astype(F32)
    return out


def _mm(a, b, dims=(((1,), (0,)), ((), ())), na=1, nb=1):
    ap = [a] if a.dtype == BF16 else _parts(a, na)
    bp = [b] if b.dtype == BF16 else _parts(b, nb)
    n = max(len(ap), len(bp))
    acc = None
    for i, x in enumerate(ap):
        for j, y in enumerate(bp):
            if i + j < n:
                t = lax.dot_general(x, y, dims, preferred_element_type=F32)
                acc = t if acc is None else acc + t
    return acc


_NT = (((1,), (1,)), ((), ()))
_TN = (((0,), (0,)), ((), ()))
_BNN = (((2,), (1,)), ((0,), (0,)))
_BNT = (((2,), (2,)), ((0,), (0,)))
_BTN = (((1,), (1,)), ((0,), (0,)))


def _softplus(x):
    return jnp.maximum(x, 0.0) + jnp.log(1.0 + jnp.exp(-jnp.abs(x)))


def _sigmoid(x):
    return 1.0 / (1.0 + jnp.exp(-x))


def _iota2(shape, dim):
    return lax.broadcasted_iota(jnp.int32, shape, dim)


def _split_heads(x, nc):
    n = x.shape[1] // HEAD_DIM
    return jnp.concatenate(
        [x[:, h * HEAD_DIM:(h + 1) * HEAD_DIM].reshape(nc, CHUNK, HEAD_DIM) for h in range(n)], axis=0)


def _merge_heads(x, n):
    nc = x.shape[0] // n
    return jnp.concatenate([x[h * nc:(h + 1) * nc].reshape(nc * CHUNK, HEAD_DIM) for h in range(n)], axis=1)


def _chunk_of(x, n, c):
    return x.reshape((n, x.shape[0] // n) + x.shape[1:])[:, c]


def _proj_in_kernel(x_ref, wg_ref, wr_ref, wk_ref, wt_ref, wv_ref,
                    hg_ref, hr_ref, ht_ref, kaug_ref, kbar_ref, vt_ref):
    nb = kaug_ref.shape[3] - HEAD_DIM
    blocks = ROW_BLOCK // MOBA_BLOCK
    xb = x_ref[0].astype(BF16)
    hg_ref[0] = jnp.dot(xb, wg_ref[...], preferred_element_type=F32)
    hr_ref[0] = jnp.dot(xb, wr_ref[...], preferred_element_type=F32)
    ht_ref[0] = lax.dot_general(wt_ref[...], xb, _NT, preferred_element_type=F32)
    k = jnp.dot(xb, wk_ref[...], preferred_element_type=F32)
    first = pl.program_id(1) * blocks
    onehot = (_iota2((ROW_BLOCK, nb), 1) == first + _iota2((ROW_BLOCK, nb), 0) // MOBA_BLOCK).astype(BF16)
    for h in range(MOBA_HEADS):
        kaug_ref[0, h] = jnp.concatenate([k[:, h * HEAD_DIM:(h + 1) * HEAD_DIM].astype(BF16), onehot], axis=1)
    for i in range(blocks):
        kbar_ref[0, 0, i:i + 1, :] = jnp.sum(k[i * MOBA_BLOCK:(i + 1) * MOBA_BLOCK], axis=0,
                                             keepdims=True) * (1.0 / MOBA_BLOCK)
    vt = lax.dot_general(wv_ref[...], xb, _NT, preferred_element_type=F32)
    pad = jnp.concatenate([jnp.ones((1, MOBA_BLOCK), F32), jnp.zeros((SUBLANE_BF16 - 1, MOBA_BLOCK), F32)], axis=0)
    for h in range(MOBA_HEADS):
        for i in range(blocks):
            tile = vt[h * HEAD_DIM:(h + 1) * HEAD_DIM, i * MOBA_BLOCK:(i + 1) * MOBA_BLOCK]
            vt_ref[0, h, i] = jnp.concatenate([tile, pad], axis=0).astype(BF16)


def _proj_in(x, w_gla, w_rwkv, w_k, w_t, w_v):
    b, t, d = x.shape
    nb = t // MOBA_BLOCK
    blocks = ROW_BLOCK // MOBA_BLOCK
    full = lambda w: pl.BlockSpec(w.shape, lambda i, s: (0, 0))
    rows = lambda n: pl.BlockSpec((1, ROW_BLOCK, n), lambda i, s: (i, s, 0))
    nt = w_t.shape[0]
    vrows = HEAD_DIM + SUBLANE_BF16
    return pl.pallas_call(
        _proj_in_kernel,
        grid=(b, t // ROW_BLOCK),
        in_specs=[rows(d), full(w_gla), full(w_rwkv), full(w_k), full(w_t), full(w_v)],
        out_specs=[rows(w_gla.shape[1]), rows(w_rwkv.shape[1]),
                   pl.BlockSpec((1, nt, ROW_BLOCK), lambda i, s: (i, 0, s)),
                   pl.BlockSpec((1, MOBA_HEADS, ROW_BLOCK, HEAD_DIM + nb), lambda i, s: (i, 0, s, 0)),
                   pl.BlockSpec((1, 1, blocks, MOBA_W), lambda i, s: (i, s, 0, 0)),
                   pl.BlockSpec((1, MOBA_HEADS, blocks, vrows, MOBA_BLOCK), lambda i, s: (i, 0, s, 0, 0))],
        out_shape=[jax.ShapeDtypeStruct((b, t, w_gla.shape[1]), F32),
                   jax.ShapeDtypeStruct((b, t, w_rwkv.shape[1]), F32),
                   jax.ShapeDtypeStruct((b, nt, t), F32),
                   jax.ShapeDtypeStruct((b, MOBA_HEADS, t, HEAD_DIM + nb), BF16),
                   jax.ShapeDtypeStruct((b, nb // blocks, blocks, MOBA_W), F32),
                   jax.ShapeDtypeStruct((b, MOBA_HEADS, nb, vrows, MOBA_BLOCK), BF16)],
        compiler_params=pltpu.CompilerParams(
            dimension_semantics=("parallel", "parallel"), vmem_limit_bytes=VMEM_LIMIT),
    )(x, w_gla, w_rwkv, w_k, w_t, w_v)


def _gla_kernel(h_ref, aup_ref, ab_ref, nw_ref, o_ref, st_ref):
    @pl.when(pl.program_id(1) == 0)
    def _():
        st_ref[...] = jnp.zeros_like(st_ref)

    nc = GLA_TIME_BLOCK // CHUNK
    nh = GLA_HEADS
    shape3 = (nh * nc, CHUNK, CHUNK)
    hb = h_ref[0]
    z = _mm(hb[:, 4 * GLA_W:4 * GLA_W + GLA_LOWRANK], aup_ref[...], na=2, nb=2) + ab_ref[...]
    la = _split_heads(-_softplus(-z) * (1.0 / GLA_TAU), nc)
    row = _iota2(shape3, 1)
    col = _iota2(shape3, 2)
    tri = (row >= col)
    anchor_b = (col <= (row // SUB) * SUB + (SUB - 1)).astype(BF16)
    q = _split_heads(hb[:, 0:GLA_W], nc) * (HEAD_DIM ** -0.5)
    k = _split_heads(hb[:, GLA_W:2 * GLA_W], nc)
    v = _split_heads(hb[:, 2 * GLA_W:3 * GLA_W], nc)
    cum = _mm(tri.astype(BF16), la, _BNN, nb=3)
    k_anchor = _mm(anchor_b, la, _BNN, nb=3)
    k_t = k * jnp.exp(k_anchor - cum)
    scores = jnp.zeros(shape3, F32)
    for j in range(CHUNK // SUB):
        a_j = cum[:, j * SUB + SUB - 1:j * SUB + SUB, :]
        q_j = q * jnp.exp(jnp.where(row >= j * SUB, cum - a_j, 0.0))
        in_grp = (row >= j * SUB) & (row < (j + 1) * SUB)
        scores = scores + _mm(q_j, jnp.where(in_grp, k_t, 0.0), _BNT)
    o_intra = _mm(jnp.where(tri, scores, 0.0), v, _BNN)
    last = cum[:, CHUNK - 1:CHUNK, :]
    n_mat = _mm(v, k * jnp.exp(last - cum), _BTN)
    q_dec = q * jnp.exp(cum)
    dec_last = jnp.exp(last)
    st = st_ref[...]
    o_inter = []
    for c in range(nc):
        o_inter.append(_mm(_chunk_of(q_dec, nh, c), st, _BNT))
        st = st * _chunk_of(dec_last, nh, c) + _chunk_of(n_mat, nh, c)
    st_ref[...] = st
    o = o_intra + jnp.stack(o_inter, axis=1).reshape(shape3)
    o = o * lax.rsqrt(jnp.mean(o * o, axis=-1, keepdims=True) + LN_EPS) * nw_ref[...]
    g = hb[:, 3 * GLA_W:4 * GLA_W]
    o_ref[0] = _merge_heads(o, nh) * (g * _sigmoid(g))


def _gla(h_gla, a_up, a_bias, norm_w):
    b, t, c = h_gla.shape
    vec = lambda a: pl.BlockSpec(a.shape, lambda i, s: (0, 0))
    return pl.pallas_call(
        _gla_kernel,
        grid=(b, t // GLA_TIME_BLOCK),
        in_specs=[pl.BlockSpec((1, GLA_TIME_BLOCK, c), lambda i, s: (i, s, 0)), vec(a_up), vec(a_bias),
                  vec(norm_w)],
        out_specs=pl.BlockSpec((1, GLA_TIME_BLOCK, GLA_W), lambda i, s: (i, s, 0)),
        out_shape=jax.ShapeDtypeStruct((b, t, GLA_W), F32),
        scratch_shapes=[pltpu.VMEM((GLA_HEADS, HEAD_DIM, HEAD_DIM), F32)],
        compiler_params=pltpu.CompilerParams(
            dimension_semantics=("parallel", "arbitrary"), vmem_limit_bytes=VMEM_LIMIT),
    )(h_gla, a_up, a_bias, norm_w)


def _block_penalty(q_t, kbar, own):
    gate = _mm(kbar, q_t, na=2, nb=2)
    blk = _iota2(gate.shape, 0)
    nb = gate.shape[0]
    gate = jnp.where(blk < own, gate, NEG_INF)
    allowed = blk == own
    for _ in range(min(MOBA_TOPK, nb)):
        best = jnp.max(gate, axis=0, keepdims=True)
        first = jnp.min(jnp.where(gate == best, blk, nb), axis=0, keepdims=True)
        hit = blk == first
        allowed = allowed | (hit & (blk < own))
        gate = jnp.where(hit, NEG_INF, gate)
    return jnp.where(allowed, 0.0, -MASK_BIG).astype(BF16)


def _moba_attn_kernel(qt_ref, kbar_ref, k_ref, vt_ref, gt_ref, bias_ref, o_ref, *scratch):
    hps = MOBA_HEADS_PER_STEP
    own = pl.program_id(2)
    nb = kbar_ref.shape[1]
    n_far = jnp.maximum(own + 1 - NEAR_BLOCKS, 0)
    near_groups = (jnp.minimum(own + 1, NEAR_BLOCKS) + KV_GROUP - 1) // KV_GROUP
    n_groups = near_groups + (n_far + KV_GROUP - 1) // KV_GROUP
    head_rows = lambda hh: slice(hh * HEAD_DIM, (hh + 1) * HEAD_DIM)
    q = [(qt_ref[0, head_rows(hh), :] * (HEAD_DIM ** -0.5)).astype(BF16) for hh in range(hps)]
    pen = [_block_penalty(qt_ref[0, head_rows(hh), :], kbar_ref[0][:, head_rows(hh)], own) for hh in range(hps)]
    blk = _iota2(pen[0].shape, 0)
    sa, sb, ma, mb = (scratch[i * hps:(i + 1) * hps] for i in range(4))

    def first_block(u):
        hi_near = own + 1 - u * KV_GROUP
        j0 = jnp.where(u < near_groups, hi_near - KV_GROUP, (u - near_groups) * KV_GROUP)
        return jnp.clip(j0, 0, nb - KV_GROUP), hi_near

    tile_rows = lambda i: slice(i * MOBA_BLOCK, (i + 1) * MOBA_BLOCK)

    def score_operand(u, hh):
        is_near = u < near_groups
        j0, hi_near = first_block(u)
        hi = jnp.where(is_near, hi_near, n_far)
        lo = jnp.where(is_near, hi_near - KV_GROUP, 0)
        in_range = (blk >= lo) & (blk < hi)
        return j0, jnp.concatenate([q[hh], jnp.where(in_range, pen[hh], jnp.asarray(-MASK_BIG, BF16))], axis=0)

    def score_tile(hh, j, q_aug):
        rows = pl.ds(pl.multiple_of(j * MOBA_BLOCK, MOBA_BLOCK), MOBA_BLOCK)
        tile = jnp.clip(own - j, 0, N_BIAS_TILES)
        return _mm(k_ref[0, hh, rows, :], q_aug).astype(BF16) + bias_ref[hh, tile]

    def scores(u, hh, dst, dst_max):
        j0, q_aug = score_operand(u, hh)
        top = None
        for i in range(KV_GROUP):
            s = score_tile(hh, j0 + i, q_aug)
            dst[tile_rows(i), :] = s
            tile_top = jnp.max(s, axis=0, keepdims=True)
            top = tile_top if top is None else jnp.maximum(top, tile_top)
        dst_max[...] = top.astype(F32)

    def consume(u, hh, src, src_max, carry):
        m, acc = carry
        j0, _ = first_block(u)
        ref_b = jnp.maximum(m, src_max[...]).astype(BF16)
        m_new = ref_b.astype(F32)
        acc = acc * jnp.exp(m - m_new)
        for i in range(KV_GROUP):
            acc = acc + _mm(vt_ref[0, hh, j0 + i], jnp.exp(src[tile_rows(i), :] - ref_b))
        return m_new, acc

    def staggered(u, src, src_max, dst, dst_max, carry):
        carry = list(carry)
        for hh in range(hps):
            scores(u + 1, hh, dst[hh], dst_max[hh])
            carry[hh] = consume(u, hh, src[hh], src_max[hh], carry[hh])
        return tuple(carry)

    carry = tuple((jnp.full((1, MOBA_BLOCK), -MASK_BIG, F32), jnp.zeros((vt_ref.shape[3], MOBA_BLOCK), F32))
                  for _ in range(hps))
    for hh in range(hps):
        scores(0, hh, sa[hh], ma[hh])

    def pair(v, carry):
        u = 2 * v
        carry = staggered(u, sa, ma, sb, mb, carry)
        return staggered(u + 1, sb, mb, sa, ma, carry)

    n_pairs = (n_groups - 1) // 2
    carry = lax.fori_loop(0, n_pairs, pair, carry)
    u_last = 2 * n_pairs

    def last_two(carry):
        carry = staggered(u_last, sa, ma, sb, mb, carry)
        return tuple(consume(u_last + 1, hh, sb[hh], mb[hh], carry[hh]) for hh in range(hps))

    def last_one(carry):
        return tuple(consume(u_last, hh, sa[hh], ma[hh], carry[hh]) for hh in range(hps))

    carry = lax.cond(n_groups - u_last == 2, last_two, last_one, carry)
    for hh in range(hps):
        _, acc = carry[hh]
        g = gt_ref[0, head_rows(hh), :]
        o_ref[0, head_rows(hh), :] = acc[:HEAD_DIM] / acc[HEAD_DIM:HEAD_DIM + 1] * (g * _sigmoid(g))


def _moba_attn(ht, kbar, k_aug, vt_aug, bias):
    b, _, t = ht.shape
    h, nb = k_aug.shape[1], kbar.shape[1]
    hps = MOBA_HEADS_PER_STEP
    rows = hps * HEAD_DIM
    qs = pl.BlockSpec((1, rows, MOBA_BLOCK), lambda i, j, s: (i, j, s))
    gs = pl.BlockSpec((1, rows, MOBA_BLOCK), lambda i, j, s: (i, h // hps + j, s))
    once = pl.Buffered(1)
    heads = lambda a: pl.BlockSpec((1, hps) + a.shape[2:], lambda i, j, s: (i, j) + (0,) * (a.ndim - 2),
                                   pipeline_mode=once)
    return pl.pallas_call(
        _moba_attn_kernel,
        grid=(b, h // hps, nb),
        in_specs=[qs, pl.BlockSpec((1, nb, rows), lambda i, j, s: (i, 0, j)), heads(k_aug), heads(vt_aug), gs,
                  pl.BlockSpec((hps,) + bias.shape[1:], lambda i, j, s: (j, 0, 0, 0), pipeline_mode=once)],
        out_specs=qs,
        out_shape=jax.ShapeDtypeStruct((b, h * HEAD_DIM, t), F32),
        scratch_shapes=[pltpu.VMEM((KV_GROUP * MOBA_BLOCK, MOBA_BLOCK), BF16)] * (2 * hps)
        + [pltpu.VMEM((1, MOBA_BLOCK), F32)] * (2 * hps),
        compiler_params=pltpu.CompilerParams(
            dimension_semantics=("parallel", "parallel", "arbitrary"), vmem_limit_bytes=VMEM_LIMIT),
    )(ht, kbar, k_aug, vt_aug, ht, bias)


def _t5_bucket(rel):
    rel = jnp.maximum(rel, 0)
    max_exact = N_BUCKETS // 2
    rel_f = jnp.maximum(rel, 1).astype(F32)
    large = max_exact + (jnp.log(rel_f / max_exact) / math.log(MAX_DISTANCE / max_exact)
                         * (N_BUCKETS - max_exact)).astype(jnp.int32)
    large = jnp.minimum(large, N_BUCKETS - 1)
    return jnp.where(rel < max_exact, rel, large)


def _bias_tiles(rel_bias):
    blk = MOBA_BLOCK
    length = (N_BIAS_TILES + 1) * blk
    rel = jnp.arange(length) - (blk - 1)
    bucket = _t5_bucket(rel)
    tab = rel_bias.T.astype(F32)
    heads = tab.shape[0]
    line = jnp.zeros((heads, length), F32)
    for bk in range(N_BUCKETS):
        line = jnp.where(bucket[None] == bk, tab[:, bk][:, None], line)
    line = jnp.where(rel[None] < 0, -MASK_BIG, line).astype(BF16)
    skew = jnp.broadcast_to(line[:, None, :], (heads, blk, length)).reshape(heads, blk * length)
    skew = skew[:, :blk * (length - 1)].reshape(heads, blk, length - 1)
    tiles = [skew[:, :, blk - 1 + d * blk:2 * blk - 1 + d * blk] for d in range(N_BIAS_TILES)]
    far = jnp.broadcast_to(tab[:, N_BUCKETS - 1].astype(BF16)[:, None, None], (heads, blk, blk))
    return jnp.stack(tiles + [far], axis=1)


def _rwkv_kernel(h_ref, mu_ref, wup_ref, w0_ref, aup_ref, a0_ref, kk_ref, ka_ref, rk_ref, gw_ref, gb_ref,
                 o_ref, st_ref, prev_ref):
    @pl.when(pl.program_id(1) == 0)
    def _():
        st_ref[...] = jnp.zeros_like(st_ref)
        prev_ref[...] = jnp.zeros_like(prev_ref)

    nc = RWKV_TIME_BLOCK // CHUNK
    nh = RWKV_HEADS
    w = RWKV_W
    p = h_ref[0]
    shifted = pltpu.roll(p, 1, axis=0)
    p_prev = jnp.where(_iota2(p.shape, 0) == 0, prev_ref[...], shifted)
    prev_ref[...] = p[RWKV_TIME_BLOCK - 1:RWKV_TIME_BLOCK, :]
    p = p + (p_prev - p) * mu_ref[...]
    r, k, v, g = p[:, 0:w], p[:, w:2 * w], p[:, 2 * w:3 * w], p[:, 3 * w:4 * w]
    w_dn = p[:, 4 * w:4 * w + RWKV_LORA]
    a_dn = p[:, 4 * w + RWKV_LORA:4 * w + 2 * RWKV_LORA]
    d = w0_ref[...] + _mm(jnp.tanh(w_dn), wup_ref[...], na=2, nb=2)
    lw = -jnp.exp(-_softplus(-d) - 0.5)
    a = _sigmoid(a0_ref[...] + _mm(a_dn, aup_ref[...], na=2, nb=2))
    same_head = (_iota2((w, w), 0) // HEAD_DIM == _iota2((w, w), 1) // HEAD_DIM).astype(BF16)
    kk = k * kk_ref[...]
    kk = kk / jnp.maximum(jnp.sqrt(_mm(kk * kk, same_head, na=3)), 1e-12)
    k = k * (1.0 + (a - 1.0) * ka_ref[...])
    bonus = _mm(r * k * rk_ref[...], same_head, na=3) * v

    shape3 = (nh * nc, CHUNK, CHUNK)
    row = _iota2(shape3, 1)
    col = _iota2(shape3, 2)
    tri_b = (row >= col).astype(BF16)
    eye = row == col
    hp = dict()
    carry_prec = dict(na=2, nb=2)
    r, lw, k, v = (_split_heads(x, nc) for x in (r, lw, k, v))
    a_vec = _split_heads(-kk, nc)
    b_vec = _split_heads(kk * a, nc)
    cum = _mm(tri_b, lw, _BNN, nb=3)
    last = cum[:, CHUNK - 1:CHUNK, :]
    inv = jnp.exp(-cum)
    to_end = jnp.exp(last - cum)
    r_t = r * jnp.exp(cum)
    a_t = a_vec * jnp.exp(cum - lw)
    k_t = k * inv
    b_t = b_vec * inv
    a_ab = jnp.where(row > col, _mm(a_t, b_t, _BNT, **hp), 0.0)
    a_ak = jnp.where(row > col, _mm(a_t, k_t, _BNT, **hp), 0.0)
    a_rb = jnp.where(row >= col, _mm(r_t, b_t, _BNT, **hp), 0.0)
    a_rk = jnp.where(row >= col, _mm(r_t, k_t, _BNT, **hp), 0.0)
    inv_t = jnp.where(eye, 1.0, a_ab)
    power = a_ab
    for _ in range(int(math.log2(CHUNK)) - 1):
        power = _mm(power, power, _BNN, **hp)
        inv_t = inv_t + _mm(inv_t, power, _BNN, **hp)
    w_mat = _mm(inv_t, a_t, _BNN, **hp)
    u0 = _mm(inv_t, _mm(a_ak, v, _BNN, **hp), _BNN, **hp)
    y_w = r_t + _mm(a_rb, w_mat, _BNN, **hp)
    y0 = _mm(a_rb, u0, _BNN, **hp) + _mm(a_rk, v, _BNN, **hp)
    b_end = b_vec * to_end
    k_end = k * to_end
    m_mat = jnp.where(eye, jnp.exp(last), 0.0) + _mm(b_end, w_mat, _BTN, **hp)
    n_mat = _mm(b_end, u0, _BTN, **hp) + _mm(k_end, v, _BTN, **hp)
    st = st_ref[...]
    ys = []
    for c in range(nc):
        ys.append(_mm(_chunk_of(y_w, nh, c), st, _BNN, **hp))
        st = _mm(_chunk_of(m_mat, nh, c), st, _BNN, **carry_prec) + _chunk_of(n_mat, nh, c)
    st_ref[...] = st
    y = y0 + jnp.stack(ys, axis=1).reshape(shape3)
    mean = jnp.mean(y, axis=-1, keepdims=True)
    var = jnp.mean(jnp.square(y - mean), axis=-1, keepdims=True)
    y = _merge_heads((y - mean) * lax.rsqrt(var + RWKV_GN_EPS), nh) * gw_ref[...] + gb_ref[...]
    o_ref[0] = (y + bonus) * (g * _sigmoid(g))


def _rwkv(h_rwkv, mu, w_up, w0, a_up, a0, k_k, k_a, r_k, gn_w, gn_b):
    b, t, c = h_rwkv.shape
    full = lambda a: pl.BlockSpec(a.shape, lambda i, s: (0, 0))
    params = (mu, w_up, w0, a_up, a0, k_k, k_a, r_k, gn_w, gn_b)
    return pl.pallas_call(
        _rwkv_kernel,
        grid=(b, t // RWKV_TIME_BLOCK),
        in_specs=[pl.BlockSpec((1, RWKV_TIME_BLOCK, c), lambda i, s: (i, s, 0))] + [full(a) for a in params],
        out_specs=pl.BlockSpec((1, RWKV_TIME_BLOCK, RWKV_W), lambda i, s: (i, s, 0)),
        out_shape=jax.ShapeDtypeStruct((b, t, RWKV_W), F32),
        scratch_shapes=[pltpu.VMEM((RWKV_HEADS, HEAD_DIM, HEAD_DIM), F32), pltpu.VMEM((1, c), F32)],
        compiler_params=pltpu.CompilerParams(
            dimension_semantics=("parallel", "arbitrary"), vmem_limit_bytes=VMEM_LIMIT),
    )(h_rwkv, *params)


def _proj_out_kernel(alpha, og_ref, ot_ref, or_ref, w_ref, x_ref, lw_ref, lb_ref, y_ref):
    y = _mm(og_ref[0].astype(BF16), w_ref[0:GLA_W, :])
    y = y + _mm(ot_ref[0].astype(BF16), w_ref[GLA_W:GLA_W + MOBA_W, :], _TN)
    y = y + _mm(or_ref[0].astype(BF16), w_ref[GLA_W + MOBA_W:, :])
    z = alpha * x_ref[0] + y
    mu = jnp.mean(z, axis=-1, keepdims=True)
    var = jnp.mean(jnp.square(z - mu), axis=-1, keepdims=True)
    y_ref[0] = (z - mu) * lax.rsqrt(var + LN_EPS) * lw_ref[...] + lb_ref[...]


def _proj_out(o_gla, ot_moba, o_rwkv, w, x, ln_w, ln_b, alpha):
    b, t, d = x.shape
    rs = lambda n: pl.BlockSpec((1, ROW_BLOCK, n), lambda i, s: (i, s, 0))
    vs = pl.BlockSpec((1, d), lambda i, s: (0, 0))
    return pl.pallas_call(
        functools.partial(_proj_out_kernel, alpha),
        grid=(b, t // ROW_BLOCK),
        in_specs=[rs(GLA_W), pl.BlockSpec((1, MOBA_W, ROW_BLOCK), lambda i, s: (i, 0, s)), rs(RWKV_W),
                  pl.BlockSpec(w.shape, lambda i, s: (0, 0)), rs(d), vs, vs],
        out_specs=rs(d),
        out_shape=jax.ShapeDtypeStruct((b, t, d), F32),
        compiler_params=pltpu.CompilerParams(
            dimension_semantics=("parallel", "parallel"), vmem_limit_bytes=VMEM_LIMIT),
    )(o_gla, ot_moba, o_rwkv, w, x, ln_w, ln_b)


def _pad_cols(a, n):
    return jnp.pad(a, ((0, 0), (0, n - a.shape[1])))


def _split_w_in(w_l):
    gla, moba = w_l[:, :GLA_COLS], w_l[:, GLA_COLS:GLA_COLS + MOBA_COLS]
    rwkv = w_l[:, GLA_COLS + MOBA_COLS:]
    mq, mk, mv, mg = (moba[:, i * MOBA_W:(i + 1) * MOBA_W] for i in range(4))
    w_t = jnp.concatenate([mq, mg], axis=1).T
    return tuple(a.astype(BF16) for a in (_pad_cols(gla, GLA_PAD), _pad_cols(rwkv, RWKV_PAD), mk, w_t, mv.T))


def _moba_branch(ht, k_aug, kbar, vt_aug, bias):
    return _moba_attn(ht, kbar.reshape(kbar.shape[0], -1, MOBA_W), k_aug, vt_aug, bias)


def _gla_branch(h_gla, a_up, a_bias, norm_w):
    return _gla(h_gla, a_up, a_bias.reshape(1, GLA_W), norm_w.reshape(1, HEAD_DIM))


def _rwkv_branch(h_rwkv, mu, w_up, w0, a_up, a0, k_k, k_a, r_k, gn_w, gn_b):
    row = lambda p: p.reshape(1, -1)
    return _rwkv(h_rwkv, _pad_cols(row(mu), h_rwkv.shape[2]), w_up, row(w0), a_up, row(a0), row(k_k), row(k_a),
                 row(r_k), row(gn_w), row(gn_b))


def kernel(x, w_in, w_out, gla_a_up, gla_a_bias, gla_norm_w, moba_rel_bias, rwkv_mu, rwkv_w_up, rwkv_w0,
           rwkv_a_up, rwkv_a0, rwkv_k_k, rwkv_k_a, rwkv_r_k, rwkv_gn_w, rwkv_gn_b, ln_w, ln_b):
    bsz, t, d = x.shape
    depth = w_in.shape[0]
    alpha = (2.0 * depth) ** 0.25
    bias = _bias_tiles(moba_rel_bias)
    for l in range(depth):
        h_gla, h_rwkv, ht, k_aug, kbar, vt_aug = _proj_in(x, *_split_w_in(w_in[l]))
        o_gla = _gla_branch(h_gla, gla_a_up[l], gla_a_bias[l], gla_norm_w[l])
        ot_moba = _moba_branch(ht, k_aug, kbar, vt_aug, bias)
        o_rwkv = _rwkv_branch(h_rwkv, rwkv_mu[l], rwkv_w_up[l], rwkv_w0[l], rwkv_a_up[l], rwkv_a0[l],
                              rwkv_k_k[l], rwkv_k_a[l], rwkv_r_k[l], rwkv_gn_w[l], rwkv_gn_b[l])
        x = _proj_out(o_gla, ot_moba, o_rwkv, w_out[l].astype(BF16), x,
                      ln_w[l].reshape(1, d), ln_b[l].reshape(1, d), alpha)
    return x
```

```python
import functools
import math

import jax
import jax.numpy as jnp
from jax import lax
from jax.experimental import pallas as pl
from jax.experimental.pallas import tpu as pltpu

HEAD_DIM = 64
GLA_HEADS = 4
MOBA_HEADS = 8
RWKV_HEADS = 4
GLA_W = GLA_HEADS * HEAD_DIM
MOBA_W = MOBA_HEADS * HEAD_DIM
RWKV_W = RWKV_HEADS * HEAD_DIM
GLA_LOWRANK = 16
GLA_TAU = 16.0
RWKV_LORA = 32
RWKV_GN_EPS = 64e-5
MOBA_BLOCK = 256
MOBA_TOPK = 3
N_BUCKETS = 32
MAX_DISTANCE = 4096
LN_EPS = 1e-5
GLA_COLS = 4 * GLA_W + GLA_LOWRANK
MOBA_COLS = 4 * MOBA_W
RWKV_COLS = 4 * RWKV_W + 2 * RWKV_LORA

LANE = 128
SUBLANE_BF16 = 16
GLA_PAD = -(-GLA_COLS // LANE) * LANE
RWKV_PAD = -(-RWKV_COLS // LANE) * LANE
CHUNK = 64
SUB = 16
GLA_TIME_BLOCK = 1024
RWKV_TIME_BLOCK = 512
ROW_BLOCK = 512
N_BIAS_TILES = (MAX_DISTANCE + MOBA_BLOCK - 1) // MOBA_BLOCK + 1
KV_GROUP = 4
MOBA_HEADS_PER_STEP = 4
NEAR_BLOCKS = -(-N_BIAS_TILES // KV_GROUP) * KV_GROUP
VMEM_LIMIT = 56 * 1024 * 1024

F32 = jnp.float32
BF16 = jnp.bfloat16
NEG_INF = float("-inf")
MASK_BIG = 2.0 ** 100


def _parts(a, n):
    out, r = [], a
    for i in range(n):
        p = r.astype(BF16)
        out.append(p)
        if i + 1 < n:
            r = r - p.astype(F32)
    return out


def _mm(a, b, dims=(((1,), (0,)), ((), ())), na=1, nb=1):
    ap = [a] if a.dtype == BF16 else _parts(a, na)
    bp = [b] if b.dtype == BF16 else _parts(b, nb)
    n = max(len(ap), len(bp))
    acc = None
    for i, x in enumerate(ap):
        for j, y in enumerate(bp):
            if i + j < n:
                t = lax.dot_general(x, y, dims, preferred_element_type=F32)
                acc = t if acc is None else acc + t
    return acc


_NT = (((1,), (1,)), ((), ()))
_TN = (((0,), (0,)), ((), ()))
_BNN = (((2,), (1,)), ((0,), (0,)))
_BNT = (((2,), (2,)), ((0,), (0,)))
_BTN = (((1,), (1,)), ((0,), (0,)))


def _softplus(x):
    return jnp.maximum(x, 0.0) + jnp.log(1.0 + jnp.exp(-jnp.abs(x)))


def _sigmoid(x):
    return 1.0 / (1.0 + jnp.exp(-x))


def _iota2(shape, dim):
    return lax.broadcasted_iota(jnp.int32, shape, dim)


def _split_heads(x, nc):
    n = x.shape[1] // HEAD_DIM
    return jnp.concatenate(
        [x[:, h * HEAD_DIM:(h + 1) * HEAD_DIM].reshape(nc, CHUNK, HEAD_DIM) for h in range(n)], axis=0)


def _merge_heads(x, n):
    nc = x.shape[0] // n
    return jnp.concatenate([x[h * nc:(h + 1) * nc].reshape(nc * CHUNK, HEAD_DIM) for h in range(n)], axis=1)


def _chunk_of(x, n, c):
    return x.reshape((n, x.shape[0] // n) + x.shape[1:])[:, c]


def _proj_in_kernel(x_ref, wg_ref, wr_ref, wk_ref, wt_ref, wv_ref,
                    hg_ref, hr_ref, ht_ref, kaug_ref, kbar_ref, vt_ref):
    nb = kaug_ref.shape[3] - HEAD_DIM
    blocks = ROW_BLOCK // MOBA_BLOCK
    xb = x_ref[0].astype(BF16)
    hg_ref[0] = jnp.dot(xb, wg_ref[...], preferred_element_type=F32)
    hr_ref[0] = jnp.dot(xb, wr_ref[...], preferred_element_type=F32)
    ht_ref[0] = lax.dot_general(wt_ref[...], xb, _NT, preferred_element_type=F32)
    k = jnp.dot(xb, wk_ref[...], preferred_element_type=F32)
    first = pl.program_id(1) * blocks
    onehot = (_iota2((ROW_BLOCK, nb), 1) == first + _iota2((ROW_BLOCK, nb), 0) // MOBA_BLOCK).astype(BF16)
    for h in range(MOBA_HEADS):
        kaug_ref[0, h] = jnp.concatenate([k[:, h * HEAD_DIM:(h + 1) * HEAD_DIM].astype(BF16), onehot], axis=1)
    for i in range(blocks):
        kbar_ref[0, 0, i:i + 1, :] = jnp.sum(k[i * MOBA_BLOCK:(i + 1) * MOBA_BLOCK], axis=0,
                                             keepdims=True) * (1.0 / MOBA_BLOCK)
    vt = lax.dot_general(wv_ref[...], xb, _NT, preferred_element_type=F32)
    pad = jnp.concatenate([jnp.ones((1, MOBA_BLOCK), F32), jnp.zeros((SUBLANE_BF16 - 1, MOBA_BLOCK), F32)], axis=0)
    for h in range(MOBA_HEADS):
        for i in range(blocks):
            tile = vt[h * HEAD_DIM:(h + 1) * HEAD_DIM, i * MOBA_BLOCK:(i + 1) * MOBA_BLOCK]
            vt_ref[0, h, i] = jnp.concatenate([tile, pad], axis=0).astype(BF16)


def _proj_in(x, w_gla, w_rwkv, w_k, w_t, w_v):
    b, t, d = x.shape
    nb = t // MOBA_BLOCK
    blocks = ROW_BLOCK // MOBA_BLOCK
    full = lambda w: pl.BlockSpec(w.shape, lambda i, s: (0, 0))
    rows = lambda n: pl.BlockSpec((1, ROW_BLOCK, n), lambda i, s: (i, s, 0))
    nt = w_t.shape[0]
    vrows = HEAD_DIM + SUBLANE_BF16
    return pl.pallas_call(
        _proj_in_kernel,
        grid=(b, t // ROW_BLOCK),
        in_specs=[rows(d), full(w_gla), full(w_rwkv), full(w_k), full(w_t), full(w_v)],
        out_specs=[rows(w_gla.shape[1]), rows(w_rwkv.shape[1]),
                   pl.BlockSpec((1, nt, ROW_BLOCK), lambda i, s: (i, 0, s)),
                   pl.BlockSpec((1, MOBA_HEADS, ROW_BLOCK, HEAD_DIM + nb), lambda i, s: (i, 0, s, 0)),
                   pl.BlockSpec((1, 1, blocks, MOBA_W), lambda i, s: (i, s, 0, 0)),
                   pl.BlockSpec((1, MOBA_HEADS, blocks, vrows, MOBA_BLOCK), lambda i, s: (i, 0, s, 0, 0))],
        out_shape=[jax.ShapeDtypeStruct((b, t, w_gla.shape[1]), F32),
                   jax.ShapeDtypeStruct((b, t, w_rwkv.shape[1]), F32),
                   jax.ShapeDtypeStruct((b, nt, t), F32),
                   jax.ShapeDtypeStruct((b, MOBA_HEADS, t, HEAD_DIM + nb), BF16),
                   jax.ShapeDtypeStruct((b, nb // blocks, blocks, MOBA_W), F32),
                   jax.ShapeDtypeStruct((b, MOBA_HEADS, nb, vrows, MOBA_BLOCK), BF16)],
        compiler_params=pltpu.CompilerParams(
            dimension_semantics=("parallel", "parallel"), vmem_limit_bytes=VMEM_LIMIT),
    )(x, w_gla, w_rwkv, w_k, w_t, w_v)


def _gla_kernel(h_ref, aup_ref, ab_ref, nw_ref, o_ref, st_ref):
    @pl.when(pl.program_id(1) == 0)
    def _():
        st_ref[...] = jnp.zeros_like(st_ref)

    nc = GLA_TIME_BLOCK // CHUNK
    nh = GLA_HEADS
    shape3 = (nh * nc, CHUNK, CHUNK)
    hb = h_ref[0]
    z = _mm(hb[:, 4 * GLA_W:4 * GLA_W + GLA_LOWRANK], aup_ref[...], na=2, nb=2) + ab_ref[...]
    la = _split_heads(-_softplus(-z) * (1.0 / GLA_TAU), nc)
    row = _iota2(shape3, 1)
    col = _iota2(shape3, 2)
    tri = (row >= col)
    anchor_b = (col <= (row // SUB) * SUB + (SUB - 1)).astype(BF16)
    q = _split_heads(hb[:, 0:GLA_W], nc) * (HEAD_DIM ** -0.5)
    k = _split_heads(hb[:, GLA_W:2 * GLA_W], nc)
    v = _split_heads(hb[:, 2 * GLA_W:3 * GLA_W], nc)
    cum = _mm(tri.astype(BF16), la, _BNN, nb=3)
    k_anchor = _mm(anchor_b, la, _BNN, nb=3)
    k_t = k * jnp.exp(k_anchor - cum)
    scores = jnp.zeros(shape3, F32)
    for j in range(CHUNK // SUB):
        a_j = cum[:, j * SUB + SUB - 1:j * SUB + SUB, :]
        q_j = q * jnp.exp(jnp.where(row >= j * SUB, cum - a_j, 0.0))
        in_grp = (row >= j * SUB) & (row < (j + 1) * SUB)
        scores = scores + _mm(q_j, jnp.where(in_grp, k_t, 0.0), _BNT)
    o_intra = _mm(jnp.where(tri, scores, 0.0), v, _BNN)
    last = cum[:, CHUNK - 1:CHUNK, :]
    n_mat = _mm(v, k * jnp.exp(last - cum), _BTN)
    q_dec = q * jnp.exp(cum)
    dec_last = jnp.exp(last)
    st = st_ref[...]
    o_inter = []
    for c in range(nc):
        o_inter.append(_mm(_chunk_of(q_dec, nh, c), st, _BNT))
        st = st * _chunk_of(dec_last, nh, c) + _chunk_of(n_mat, nh, c)
    st_ref[...] = st
    o = o_intra + jnp.stack(o_inter, axis=1).reshape(shape3)
    o = o * lax.rsqrt(jnp.mean(o * o, axis=-1, keepdims=True) + LN_EPS) * nw_ref[...]
    g = hb[:, 3 * GLA_W:4 * GLA_W]
    o_ref[0] = _merge_heads(o, nh) * (g * _sigmoid(g))


def _gla(h_gla, a_up, a_bias, norm_w):
    b, t, c = h_gla.shape
    vec = lambda a: pl.BlockSpec(a.shape, lambda i, s: (0, 0))
    return pl.pallas_call(
        _gla_kernel,
        grid=(b, t // GLA_TIME_BLOCK),
        in_specs=[pl.BlockSpec((1, GLA_TIME_BLOCK, c), lambda i, s: (i, s, 0)), vec(a_up), vec(a_bias),
                  vec(norm_w)],
        out_specs=pl.BlockSpec((1, GLA_TIME_BLOCK, GLA_W), lambda i, s: (i, s, 0)),
        out_shape=jax.ShapeDtypeStruct((b, t, GLA_W), F32),
        scratch_shapes=[pltpu.VMEM((GLA_HEADS, HEAD_DIM, HEAD_DIM), F32)],
        compiler_params=pltpu.CompilerParams(
            dimension_semantics=("parallel", "arbitrary"), vmem_limit_bytes=VMEM_LIMIT),
    )(h_gla, a_up, a_bias, norm_w)


def _block_penalty(q_t, kbar, own):
    gate = _mm(kbar, q_t, na=2, nb=2)
    blk = _iota2(gate.shape, 0)
    nb = gate.shape[0]
    gate = jnp.where(blk < own, gate, NEG_INF)
    allowed = blk == own
    for _ in range(min(MOBA_TOPK, nb)):
        best = jnp.max(gate, axis=0, keepdims=True)
        first = jnp.min(jnp.where(gate == best, blk, nb), axis=0, keepdims=True)
        hit = blk == first
        allowed = allowed | (hit & (blk < own))
        gate = jnp.where(hit, NEG_INF, gate)
    return jnp.where(allowed, 0.0, -MASK_BIG).astype(BF16)


def _moba_attn_kernel(qt_ref, kbar_ref, k_ref, vt_ref, gt_ref, bias_ref, o_ref, *scratch):
    hps = MOBA_HEADS_PER_STEP
    own = pl.program_id(2)
    nb = kbar_ref.shape[1]
    n_far = jnp.maximum(own + 1 - NEAR_BLOCKS, 0)
    near_groups = (jnp.minimum(own + 1, NEAR_BLOCKS) + KV_GROUP - 1) // KV_GROUP
    n_groups = near_groups + (n_far + KV_GROUP - 1) // KV_GROUP
    head_rows = lambda hh: slice(hh * HEAD_DIM, (hh + 1) * HEAD_DIM)
    q = [(qt_ref[0, head_rows(hh), :] * (HEAD_DIM ** -0.5)).astype(BF16) for hh in range(hps)]
    pen = [_block_penalty(qt_ref[0, head_rows(hh), :], kbar_ref[0][:, head_rows(hh)], own) for hh in range(hps)]
    blk = _iota2(pen[0].shape, 0)
    sa, sb, ma, mb = (scratch[i * hps:(i + 1) * hps] for i in range(4))

    def first_block(u):
        hi_near = own + 1 - u * KV_GROUP
        j0 = jnp.where(u < near_groups, hi_near - KV_GROUP, (u - near_groups) * KV_GROUP)
        return jnp.clip(j0, 0, nb - KV_GROUP), hi_near

    tile_rows = lambda i: slice(i * MOBA_BLOCK, (i + 1) * MOBA_BLOCK)

    def score_operand(u, hh):
        is_near = u < near_groups
        j0, hi_near = first_block(u)
        hi = jnp.where(is_near, hi_near, n_far)
        lo = jnp.where(is_near, hi_near - KV_GROUP, 0)
        in_range = (blk >= lo) & (blk < hi)
        return j0, jnp.concatenate([q[hh], jnp.where(in_range, pen[hh], jnp.asarray(-MASK_BIG, BF16))], axis=0)

    def score_tile(hh, j, q_aug):
        rows = pl.ds(pl.multiple_of(j * MOBA_BLOCK, MOBA_BLOCK), MOBA_BLOCK)
        tile = jnp.clip(own - j, 0, N_BIAS_TILES)
        return _mm(k_ref[0, hh, rows, :], q_aug).astype(BF16) + bias_ref[hh, tile]

    def scores(u, hh, dst, dst_max):
        j0, q_aug = score_operand(u, hh)
        top = None
        for i in range(KV_GROUP):
            s = score_tile(hh, j0 + i, q_aug)
            dst[tile_rows(i), :] = s
            tile_top = jnp.max(s, axis=0, keepdims=True)
            top = tile_top if top is None else jnp.maximum(top, tile_top)
        dst_max[...] = top.astype(F32)

    def consume(u, hh, src, src_max, carry):
        m, acc = carry
        j0, _ = first_block(u)
        ref_b = jnp.maximum(m, src_max[...]).astype(BF16)
        m_new = ref_b.astype(F32)
        acc = acc * jnp.exp(m - m_new)
        for i in range(KV_GROUP):
            acc = acc + _mm(vt_ref[0, hh, j0 + i], jnp.exp(src[tile_rows(i), :] - ref_b))
        return m_new, acc

    def staggered(u, src, src_max, dst, dst_max, carry):
        carry = list(carry)
        for hh in range(hps):
            scores(u + 1, hh, dst[hh], dst_max[hh])
            carry[hh] = consume(u, hh, src[hh], src_max[hh], carry[hh])
        return tuple(carry)

    carry = tuple((jnp.full((1, MOBA_BLOCK), -MASK_BIG, F32), jnp.zeros((vt_ref.shape[3], MOBA_BLOCK), F32))
                  for _ in range(hps))
    for hh in range(hps):
        scores(0, hh, sa[hh], ma[hh])

    def pair(v, carry):
        u = 2 * v
        carry = staggered(u, sa, ma, sb, mb, carry)
        return staggered(u + 1, sb, mb, sa, ma, carry)

    n_pairs = (n_groups - 1) // 2
    carry = lax.fori_loop(0, n_pairs, pair, carry)
    u_last = 2 * n_pairs

    def last_two(carry):
        carry = staggered(u_last, sa, ma, sb, mb, carry)
        return tuple(consume(u_last + 1, hh, sb[hh], mb[hh], carry[hh]) for hh in range(hps))

    def last_one(carry):
        return tuple(consume(u_last, hh, sa[hh], ma[hh], carry[hh]) for hh in range(hps))

    carry = lax.cond(n_groups - u_last == 2, last_two, last_one, carry)
    for hh in range(hps):
        _, acc = carry[hh]
        g = gt_ref[0, head_rows(hh), :]
        o_ref[0, head_rows(hh), :] = acc[:HEAD_DIM] / acc[HEAD_DIM:HEAD_DIM + 1] * (g * _sigmoid(g))


def _moba_attn(ht, kbar, k_aug, vt_aug, bias):
    b, _, t = ht.shape
    h, nb = k_aug.shape[1], kbar.shape[1]
    hps = MOBA_HEADS_PER_STEP
    rows = hps * HEAD_DIM
    qs = pl.BlockSpec((1, rows, MOBA_BLOCK), lambda i, j, s: (i, j, s))
    gs = pl.BlockSpec((1, rows, MOBA_BLOCK), lambda i, j, s: (i, h // hps + j, s))
    once = pl.Buffered(1)
    heads = lambda a: pl.BlockSpec((1, hps) + a.shape[2:], lambda i, j, s: (i, j) + (0,) * (a.ndim - 2),
                                   pipeline_mode=once)
    return pl.pallas_call(
        _moba_attn_kernel,
        grid=(b, h // hps, nb),
        in_specs=[qs, pl.BlockSpec((1, nb, rows), lambda i, j, s: (i, 0, j)), heads(k_aug), heads(vt_aug), gs,
                  pl.BlockSpec((hps,) + bias.shape[1:], lambda i, j, s: (j, 0, 0, 0), pipeline_mode=once)],
        out_specs=qs,
        out_shape=jax.ShapeDtypeStruct((b, h * HEAD_DIM, t), F32),
        scratch_shapes=[pltpu.VMEM((KV_GROUP * MOBA_BLOCK, MOBA_BLOCK), BF16)] * (2 * hps)
        + [pltpu.VMEM((1, MOBA_BLOCK), F32)] * (2 * hps),
        compiler_params=pltpu.CompilerParams(
            dimension_semantics=("parallel", "parallel", "arbitrary"), vmem_limit_bytes=VMEM_LIMIT),
    )(ht, kbar, k_aug, vt_aug, ht, bias)


def _t5_bucket(rel):
    rel = jnp.maximum(rel, 0)
    max_exact = N_BUCKETS // 2
    rel_f = jnp.maximum(rel, 1).astype(F32)
    large = max_exact + (jnp.log(rel_f / max_exact) / math.log(MAX_DISTANCE / max_exact)
                         * (N_BUCKETS - max_exact)).astype(jnp.int32)
    large = jnp.minimum(large, N_BUCKETS - 1)
    return jnp.where(rel < max_exact, rel, large)


def _bias_tiles(rel_bias):
    d = jnp.arange(N_BIAS_TILES)[:, None, None] * MOBA_BLOCK
    rel = d + jnp.arange(MOBA_BLOCK)[None, None, :] - jnp.arange(MOBA_BLOCK)[None, :, None]
    bucket = _t5_bucket(rel)
    tab = rel_bias.T.astype(F32)
    out = jnp.zeros((tab.shape[0],) + bucket.shape, F32)
    for bk in range(N_BUCKETS):
        out = jnp.where(bucket[None] == bk, tab[:, bk][:, None, None, None], out)
    out = jnp.where(rel[None] < 0, -MASK_BIG, out)
    far = jnp.broadcast_to(tab[:, N_BUCKETS - 1][:, None, None, None], out[:, :1].shape)
    return jnp.concatenate([out, far], axis=1).astype(BF16)


def _rwkv_kernel(h_ref, mu_ref, wup_ref, w0_ref, aup_ref, a0_ref, kk_ref, ka_ref, rk_ref, gw_ref, gb_ref,
                 o_ref, st_ref, prev_ref):
    @pl.when(pl.program_id(1) == 0)
    def _():
        st_ref[...] = jnp.zeros_like(st_ref)
        prev_ref[...] = jnp.zeros_like(prev_ref)

    nc = RWKV_TIME_BLOCK // CHUNK
    nh = RWKV_HEADS
    w = RWKV_W
    p = h_ref[0]
    shifted = pltpu.roll(p, 1, axis=0)
    p_prev = jnp.where(_iota2(p.shape, 0) == 0, prev_ref[...], shifted)
    prev_ref[...] = p[RWKV_TIME_BLOCK - 1:RWKV_TIME_BLOCK, :]
    p = p + (p_prev - p) * mu_ref[...]
    r, k, v, g = p[:, 0:w], p[:, w:2 * w], p[:, 2 * w:3 * w], p[:, 3 * w:4 * w]
    w_dn = p[:, 4 * w:4 * w + RWKV_LORA]
    a_dn = p[:, 4 * w + RWKV_LORA:4 * w + 2 * RWKV_LORA]
    d = w0_ref[...] + _mm(jnp.tanh(w_dn), wup_ref[...], na=2, nb=2)
    lw = -jnp.exp(-_softplus(-d) - 0.5)
    a = _sigmoid(a0_ref[...] + _mm(a_dn, aup_ref[...], na=2, nb=2))
    same_head = (_iota2((w, w), 0) // HEAD_DIM == _iota2((w, w), 1) // HEAD_DIM).astype(BF16)
    kk = k * kk_ref[...]
    kk = kk / jnp.maximum(jnp.sqrt(_mm(kk * kk, same_head, na=3)), 1e-12)
    k = k * (1.0 + (a - 1.0) * ka_ref[...])
    bonus = _mm(r * k * rk_ref[...], same_head, na=3) * v

    shape3 = (nh * nc, CHUNK, CHUNK)
    row = _iota2(shape3, 1)
    col = _iota2(shape3, 2)
    tri_b = (row >= col).astype(BF16)
    eye = row == col
    hp = dict()
    carry_prec = dict(na=2, nb=2)
    r, lw, k, v = (_split_heads(x, nc) for x in (r, lw, k, v))
    a_vec = _split_heads(-kk, nc)
    b_vec = _split_heads(kk * a, nc)
    cum = _mm(tri_b, lw, _BNN, nb=3)
    last = cum[:, CHUNK - 1:CHUNK, :]
    inv = jnp.exp(-cum)
    to_end = jnp.exp(last - cum)
    r_t = r * jnp.exp(cum)
    a_t = a_vec * jnp.exp(cum - lw)
    k_t = k * inv
    b_t = b_vec * inv
    gram = _mm(jnp.concatenate([a_t, r_t], axis=1), jnp.concatenate([b_t, k_t], axis=1), _BNT, **hp)
    a_ab = jnp.where(row > col, gram[:, :CHUNK, :CHUNK], 0.0)
    a_ak = jnp.where(row > col, gram[:, :CHUNK, CHUNK:], 0.0)
    a_rb = jnp.where(row >= col, gram[:, CHUNK:, :CHUNK], 0.0)
    a_rk = jnp.where(row >= col, gram[:, CHUNK:, CHUNK:], 0.0)
    inv_t = jnp.where(eye, 1.0, a_ab)
    power = a_ab
    for _ in range(int(math.log2(CHUNK)) - 1):
        power = _mm(power, power, _BNN, **hp)
        inv_t = inv_t + _mm(inv_t, power, _BNN, **hp)
    w_mat = _mm(inv_t, a_t, _BNN, **hp)
    u0 = _mm(inv_t, _mm(a_ak, v, _BNN, **hp), _BNN, **hp)
    y_w = r_t + _mm(a_rb, w_mat, _BNN, **hp)
    y0 = _mm(a_rb, u0, _BNN, **hp) + _mm(a_rk, v, _BNN, **hp)
    b_end = b_vec * to_end
    k_end = k * to_end
    m_mat = jnp.where(eye, jnp.exp(last), 0.0) + _mm(b_end, w_mat, _BTN, **hp)
    n_mat = _mm(b_end, u0, _BTN, **hp) + _mm(k_end, v, _BTN, **hp)
    st = st_ref[...]
    ys = []
    for c in range(nc):
        ys.append(_mm(_chunk_of(y_w, nh, c), st, _BNN, **hp))
        st = _mm(_chunk_of(m_mat, nh, c), st, _BNN, **carry_prec) + _chunk_of(n_mat, nh, c)
    st_ref[...] = st
    y = y0 + jnp.stack(ys, axis=1).reshape(shape3)
    mean = jnp.mean(y, axis=-1, keepdims=True)
    var = jnp.mean(jnp.square(y - mean), axis=-1, keepdims=True)
    y = _merge_heads((y - mean) * lax.rsqrt(var + RWKV_GN_EPS), nh) * gw_ref[...] + gb_ref[...]
    o_ref[0] = (y + bonus) * (g * _sigmoid(g))


def _rwkv(h_rwkv, mu, w_up, w0, a_up, a0, k_k, k_a, r_k, gn_w, gn_b):
    b, t, c = h_rwkv.shape
    full = lambda a: pl.BlockSpec(a.shape, lambda i, s: (0, 0))
    params = (mu, w_up, w0, a_up, a0, k_k, k_a, r_k, gn_w, gn_b)
    return pl.pallas_call(
        _rwkv_kernel,
        grid=(b, t // RWKV_TIME_BLOCK),
        in_specs=[pl.BlockSpec((1, RWKV_TIME_BLOCK, c), lambda i, s: (i, s, 0))] + [full(a) for a in params],
        out_specs=pl.BlockSpec((1, RWKV_TIME_BLOCK, RWKV_W), lambda i, s: (i, s, 0)),
        out_shape=jax.ShapeDtypeStruct((b, t, RWKV_W), F32),
        scratch_shapes=[pltpu.VMEM((RWKV_HEADS, HEAD_DIM, HEAD_DIM), F32), pltpu.VMEM((1, c), F32)],
        compiler_params=pltpu.CompilerParams(
            dimension_semantics=("parallel", "arbitrary"), vmem_limit_bytes=VMEM_LIMIT),
    )(h_rwkv, *params)


def _proj_out_kernel(alpha, og_ref, ot_ref, or_ref, w_ref, x_ref, lw_ref, lb_ref, y_ref):
    y = _mm(og_ref[0].astype(BF16), w_ref[0:GLA_W, :])
    y = y + _mm(ot_ref[0].astype(BF16), w_ref[GLA_W:GLA_W + MOBA_W, :], _TN)
    y = y + _mm(or_ref[0].astype(BF16), w_ref[GLA_W + MOBA_W:, :])
    z = alpha * x_ref[0] + y
    mu = jnp.mean(z, axis=-1, keepdims=True)
    var = jnp.mean(jnp.square(z - mu), axis=-1, keepdims=True)
    y_ref[0] = (z - mu) * lax.rsqrt(var + LN_EPS) * lw_ref[...] + lb_ref[...]


def _proj_out(o_gla, ot_moba, o_rwkv, w, x, ln_w, ln_b, alpha):
    b, t, d = x.shape
    rs = lambda n: pl.BlockSpec((1, ROW_BLOCK, n), lambda i, s: (i, s, 0))
    vs = pl.BlockSpec((1, d), lambda i, s: (0, 0))
    return pl.pallas_call(
        functools.partial(_proj_out_kernel, alpha),
        grid=(b, t // ROW_BLOCK),
        in_specs=[rs(GLA_W), pl.BlockSpec((1, MOBA_W, ROW_BLOCK), lambda i, s: (i, 0, s)), rs(RWKV_W),
                  pl.BlockSpec(w.shape, lambda i, s: (0, 0)), rs(d), vs, vs],
        out_specs=rs(d),
        out_shape=jax.ShapeDtypeStruct((b, t, d), F32),
        compiler_params=pltpu.CompilerParams(
            dimension_semantics=("parallel", "parallel"), vmem_limit_bytes=VMEM_LIMIT),
    )(o_gla, ot_moba, o_rwkv, w, x, ln_w, ln_b)


def _pad_cols(a, n):
    return jnp.pad(a, ((0, 0), (0, n - a.shape[1])))


def _split_w_in(w_l):
    gla, moba = w_l[:, :GLA_COLS], w_l[:, GLA_COLS:GLA_COLS + MOBA_COLS]
    rwkv = w_l[:, GLA_COLS + MOBA_COLS:]
    mq, mk, mv, mg = (moba[:, i * MOBA_W:(i + 1) * MOBA_W] for i in range(4))
    w_t = jnp.concatenate([mq, mg], axis=1).T
    return tuple(a.astype(BF16) for a in (_pad_cols(gla, GLA_PAD), _pad_cols(rwkv, RWKV_PAD), mk, w_t, mv.T))


def _moba_branch(ht, k_aug, kbar, vt_aug, bias):
    return _moba_attn(ht, kbar.reshape(kbar.shape[0], -1, MOBA_W), k_aug, vt_aug, bias)


def _gla_branch(h_gla, a_up, a_bias, norm_w):
    return _gla(h_gla, a_up, a_bias.reshape(1, GLA_W), norm_w.reshape(1, HEAD_DIM))


def _rwkv_branch(h_rwkv, mu, w_up, w0, a_up, a0, k_k, k_a, r_k, gn_w, gn_b):
    row = lambda p: p.reshape(1, -1)
    return _rwkv(h_rwkv, _pad_cols(row(mu), h_rwkv.shape[2]), w_up, row(w0), a_up, row(a0), row(k_k), row(k_a),
                 row(r_k), row(gn_w), row(gn_b))


def kernel(x, w_in, w_out, gla_a_up, gla_a_bias, gla_norm_w, moba_rel_bias, rwkv_mu, rwkv_w_up, rwkv_w0,
           rwkv_a_up, rwkv_a0, rwkv_k_k, rwkv_k_a, rwkv_r_k, rwkv_gn_w, rwkv_gn_b, ln_w, ln_b):
    bsz, t, d = x.shape
    depth = w_in.shape[0]
    alpha = (2.0 * depth) ** 0.25
    bias = _bias_tiles(moba_rel_bias)
    for l in range(depth):
        h_gla, h_rwkv, ht, k_aug, kbar, vt_aug = _proj_in(x, *_split_w_in(w_in[l]))
        o_gla = _gla_branch(h_gla, gla_a_up[l], gla_a_bias[l], gla_norm_w[l])
        ot_moba = _moba_branch(ht, k_aug, kbar, vt_aug, bias)
        o_rwkv = _rwkv_branch(h_rwkv, rwkv_mu[l], rwkv_w_up[l], rwkv_w0[l], rwkv_a_up[l], rwkv_a0[l],
                              rwkv_k_k[l], rwkv_k_a[l], rwkv_r_k[l], rwkv_gn_w[l], rwkv_gn_b[l])
        x = _proj_out(o_gla, ot_moba, o_rwkv, w_out[l].astype(BF16), x,
                      ln_w[l].reshape(1, d), ln_b[l].reshape(1, d), alpha)
    return x
```

```python
import functools
import math

import jax
import jax.numpy as jnp
from jax import lax
from jax.experimental import pallas as pl
from jax.experimental.pallas import tpu as pltpu

HEAD_DIM = 64
GLA_HEADS = 4
MOBA_HEADS = 8
RWKV_HEADS = 4
GLA_W = GLA_HEADS * HEAD_DIM
MOBA_W = MOBA_HEADS * HEAD_DIM
RWKV_W = RWKV_HEADS * HEAD_DIM
GLA_LOWRANK = 16
GLA_TAU = 16.0
RWKV_LORA = 32
RWKV_GN_EPS = 64e-5
MOBA_BLOCK = 256
MOBA_TOPK = 3
N_BUCKETS = 32
MAX_DISTANCE = 4096
LN_EPS = 1e-5
GLA_COLS = 4 * GLA_W + GLA_LOWRANK
MOBA_COLS = 4 * MOBA_W
RWKV_COLS = 4 * RWKV_W + 2 * RWKV_LORA

LANE = 128
SUBLANE_BF16 = 16
GLA_PAD = -(-GLA_COLS // LANE) * LANE
RWKV_PAD = -(-RWKV_COLS // LANE) * LANE
CHUNK = 64
SUB = 16
GLA_TIME_BLOCK = 1024
RWKV_TIME_BLOCK = 512
ROW_BLOCK = 512
OUT_ROW_BLOCK = 1024
N_BIAS_TILES = (MAX_DISTANCE + MOBA_BLOCK - 1) // MOBA_BLOCK + 1
KV_GROUP = 4
MOBA_HEADS_PER_STEP = 4
VMEM_LIMIT = 56 * 1024 * 1024

F32 = jnp.float32
BF16 = jnp.bfloat16
NEG_INF = float("-inf")
MASK_BIG = 2.0 ** 100


def _parts(a, n):
    out, r = [], a
    for i in range(n):
        p = r.astype(BF16)
        out.append(p)
        if i + 1 < n:
            r = r - p.astype(F32)
    return out


def _mm(a, b, dims=(((1,), (0,)), ((), ())), na=1, nb=1):
    ap = [a] if a.dtype == BF16 else _parts(a, na)
    bp = [b] if b.dtype == BF16 else _parts(b, nb)
    n = max(len(ap), len(bp))
    acc = None
    for i, x in enumerate(ap):
        for j, y in enumerate(bp):
            if i + j < n:
                t = lax.dot_general(x, y, dims, preferred_element_type=F32)
                acc = t if acc is None else acc + t
    return acc


_NT = (((1,), (1,)), ((), ()))
_TN = (((0,), (0,)), ((), ()))
_BNN = (((2,), (1,)), ((0,), (0,)))
_BNT = (((2,), (2,)), ((0,), (0,)))
_BTN = (((1,), (1,)), ((0,), (0,)))


def _softplus(x):
    return jnp.maximum(x, 0.0) + jnp.log(1.0 + jnp.exp(-jnp.abs(x)))


def _sigmoid(x):
    return 1.0 / (1.0 + jnp.exp(-x))


def _iota2(shape, dim):
    return lax.broadcasted_iota(jnp.int32, shape, dim)


def _split_heads(x, nc):
    n = x.shape[1] // HEAD_DIM
    return jnp.concatenate(
        [x[:, h * HEAD_DIM:(h + 1) * HEAD_DIM].reshape(nc, CHUNK, HEAD_DIM) for h in range(n)], axis=0)


def _merge_heads(x, n):
    nc = x.shape[0] // n
    return jnp.concatenate([x[h * nc:(h + 1) * nc].reshape(nc * CHUNK, HEAD_DIM) for h in range(n)], axis=1)


def _chunk_of(x, n, c):
    return x.reshape((n, x.shape[0] // n) + x.shape[1:])[:, c]


def _proj_in_kernel(x_ref, wg_ref, wr_ref, wk_ref, wt_ref, wv_ref,
                    hg_ref, hr_ref, ht_ref, kaug_ref, kbar_ref, vt_ref):
    nb = kaug_ref.shape[3] - HEAD_DIM
    blocks = ROW_BLOCK // MOBA_BLOCK
    xb = x_ref[0].astype(BF16)
    hg_ref[0] = jnp.dot(xb, wg_ref[...], preferred_element_type=F32)
    hr_ref[0] = jnp.dot(xb, wr_ref[...], preferred_element_type=F32)
    ht_ref[0] = lax.dot_general(wt_ref[...], xb, _NT, preferred_element_type=F32)
    k = jnp.dot(xb, wk_ref[...], preferred_element_type=F32)
    first = pl.program_id(1) * blocks
    onehot = (_iota2((ROW_BLOCK, nb), 1) == first + _iota2((ROW_BLOCK, nb), 0) // MOBA_BLOCK).astype(BF16)
    for h in range(MOBA_HEADS):
        kaug_ref[0, h] = jnp.concatenate([k[:, h * HEAD_DIM:(h + 1) * HEAD_DIM].astype(BF16), onehot], axis=1)
    for i in range(blocks):
        kbar_ref[0, 0, i:i + 1, :] = jnp.sum(k[i * MOBA_BLOCK:(i + 1) * MOBA_BLOCK], axis=0,
                                             keepdims=True) * (1.0 / MOBA_BLOCK)
    vt = lax.dot_general(wv_ref[...], xb, _NT, preferred_element_type=F32)
    pad = jnp.concatenate([jnp.ones((1, MOBA_BLOCK), F32), jnp.zeros((SUBLANE_BF16 - 1, MOBA_BLOCK), F32)], axis=0)
    for h in range(MOBA_HEADS):
        for i in range(blocks):
            tile = vt[h * HEAD_DIM:(h + 1) * HEAD_DIM, i * MOBA_BLOCK:(i + 1) * MOBA_BLOCK]
            vt_ref[0, h, i] = jnp.concatenate([tile, pad], axis=0).astype(BF16)


def _proj_in(x, w_gla, w_rwkv, w_k, w_t, w_v):
    b, t, d = x.shape
    nb = t // MOBA_BLOCK
    blocks = ROW_BLOCK // MOBA_BLOCK
    full = lambda w: pl.BlockSpec(w.shape, lambda i, s: (0, 0))
    rows = lambda n: pl.BlockSpec((1, ROW_BLOCK, n), lambda i, s: (i, s, 0))
    nt = w_t.shape[0]
    vrows = HEAD_DIM + SUBLANE_BF16
    return pl.pallas_call(
        _proj_in_kernel,
        grid=(b, t // ROW_BLOCK),
        in_specs=[rows(d), full(w_gla), full(w_rwkv), full(w_k), full(w_t), full(w_v)],
        out_specs=[rows(w_gla.shape[1]), rows(w_rwkv.shape[1]),
                   pl.BlockSpec((1, nt, ROW_BLOCK), lambda i, s: (i, 0, s)),
                   pl.BlockSpec((1, MOBA_HEADS, ROW_BLOCK, HEAD_DIM + nb), lambda i, s: (i, 0, s, 0)),
                   pl.BlockSpec((1, 1, blocks, MOBA_W), lambda i, s: (i, s, 0, 0)),
                   pl.BlockSpec((1, MOBA_HEADS, blocks, vrows, MOBA_BLOCK), lambda i, s: (i, 0, s, 0, 0))],
        out_shape=[jax.ShapeDtypeStruct((b, t, w_gla.shape[1]), F32),
                   jax.ShapeDtypeStruct((b, t, w_rwkv.shape[1]), F32),
                   jax.ShapeDtypeStruct((b, nt, t), F32),
                   jax.ShapeDtypeStruct((b, MOBA_HEADS, t, HEAD_DIM + nb), BF16),
                   jax.ShapeDtypeStruct((b, nb // blocks, blocks, MOBA_W), F32),
                   jax.ShapeDtypeStruct((b, MOBA_HEADS, nb, vrows, MOBA_BLOCK), BF16)],
        compiler_params=pltpu.CompilerParams(
            dimension_semantics=("parallel", "parallel"), vmem_limit_bytes=VMEM_LIMIT),
    )(x, w_gla, w_rwkv, w_k, w_t, w_v)


def _gla_kernel(h_ref, aup_ref, ab_ref, nw_ref, o_ref, st_ref):
    @pl.when(pl.program_id(1) == 0)
    def _():
        st_ref[...] = jnp.zeros_like(st_ref)

    nc = GLA_TIME_BLOCK // CHUNK
    nh = GLA_HEADS
    shape3 = (nh * nc, CHUNK, CHUNK)
    hb = h_ref[0]
    z = _mm(hb[:, 4 * GLA_W:4 * GLA_W + GLA_LOWRANK], aup_ref[...], na=2, nb=2) + ab_ref[...]
    la = _split_heads(-_softplus(-z) * (1.0 / GLA_TAU), nc)
    row = _iota2(shape3, 1)
    col = _iota2(shape3, 2)
    tri = (row >= col)
    anchor_b = (col <= (row // SUB) * SUB + (SUB - 1)).astype(BF16)
    q = _split_heads(hb[:, 0:GLA_W], nc) * (HEAD_DIM ** -0.5)
    k = _split_heads(hb[:, GLA_W:2 * GLA_W], nc)
    v = _split_heads(hb[:, 2 * GLA_W:3 * GLA_W], nc)
    cum = _mm(tri.astype(BF16), la, _BNN, nb=3)
    k_anchor = _mm(anchor_b, la, _BNN, nb=3)
    k_t = k * jnp.exp(k_anchor - cum)
    scores = jnp.zeros(shape3, F32)
    for j in range(CHUNK // SUB):
        a_j = cum[:, j * SUB + SUB - 1:j * SUB + SUB, :]
        q_j = q * jnp.exp(jnp.where(row >= j * SUB, cum - a_j, 0.0))
        in_grp = (row >= j * SUB) & (row < (j + 1) * SUB)
        scores = scores + _mm(q_j, jnp.where(in_grp, k_t, 0.0), _BNT)
    o_intra = _mm(jnp.where(tri, scores, 0.0), v, _BNN)
    last = cum[:, CHUNK - 1:CHUNK, :]
    n_mat = _mm(v, k * jnp.exp(last - cum), _BTN)
    q_dec = q * jnp.exp(cum)
    dec_last = jnp.exp(last)
    st = st_ref[...]
    o_inter = []
    for c in range(nc):
        o_inter.append(_mm(_chunk_of(q_dec, nh, c), st, _BNT))
        st = st * _chunk_of(dec_last, nh, c) + _chunk_of(n_mat, nh, c)
    st_ref[...] = st
    o = o_intra + jnp.stack(o_inter, axis=1).reshape(shape3)
    o = o * lax.rsqrt(jnp.mean(o * o, axis=-1, keepdims=True) + LN_EPS) * nw_ref[...]
    g = hb[:, 3 * GLA_W:4 * GLA_W]
    o_ref[0] = _merge_heads(o, nh) * (g * _sigmoid(g))


def _gla(h_gla, a_up, a_bias, norm_w):
    b, t, c = h_gla.shape
    vec = lambda a: pl.BlockSpec(a.shape, lambda i, s: (0, 0))
    return pl.pallas_call(
        _gla_kernel,
        grid=(b, t // GLA_TIME_BLOCK),
        in_specs=[pl.BlockSpec((1, GLA_TIME_BLOCK, c), lambda i, s: (i, s, 0)), vec(a_up), vec(a_bias),
                  vec(norm_w)],
        out_specs=pl.BlockSpec((1, GLA_TIME_BLOCK, GLA_W), lambda i, s: (i, s, 0)),
        out_shape=jax.ShapeDtypeStruct((b, t, GLA_W), F32),
        scratch_shapes=[pltpu.VMEM((GLA_HEADS, HEAD_DIM, HEAD_DIM), F32)],
        compiler_params=pltpu.CompilerParams(
            dimension_semantics=("parallel", "arbitrary"), vmem_limit_bytes=VMEM_LIMIT),
    )(h_gla, a_up, a_bias, norm_w)


def _block_penalty(q_t, kbar, own):
    gate = _mm(kbar, q_t, na=2, nb=2)
    blk = _iota2(gate.shape, 0)
    nb = gate.shape[0]
    gate = jnp.where(blk < own, gate, NEG_INF)
    allowed = blk == own
    for _ in range(min(MOBA_TOPK, nb)):
        best = jnp.max(gate, axis=0, keepdims=True)
        first = jnp.min(jnp.where(gate == best, blk, nb), axis=0, keepdims=True)
        hit = blk == first
        allowed = allowed | (hit & (blk < own))
        gate = jnp.where(hit, NEG_INF, gate)
    return jnp.where(allowed, 0.0, -MASK_BIG).astype(BF16)


def _moba_attn_kernel(qt_ref, kbar_ref, k_ref, vt_ref, gt_ref, bias_ref, o_ref, *scratch):
    hps = MOBA_HEADS_PER_STEP
    own = pl.program_id(2)
    nb = kbar_ref.shape[1]
    n_groups = (own + KV_GROUP) // KV_GROUP
    head_rows = lambda hh: slice(hh * HEAD_DIM, (hh + 1) * HEAD_DIM)
    q = [(qt_ref[0, head_rows(hh), :] * (HEAD_DIM ** -0.5)).astype(BF16) for hh in range(hps)]
    pen = [_block_penalty(qt_ref[0, head_rows(hh), :], kbar_ref[0][:, head_rows(hh)], own) for hh in range(hps)]
    blk = _iota2(pen[0].shape, 0)
    sa, sb, ma, mb = (scratch[i * hps:(i + 1) * hps] for i in range(4))

    def first_block(u):
        lo = own + 1 - (u + 1) * KV_GROUP
        return jnp.clip(lo, 0, nb - KV_GROUP), lo

    tile_rows = lambda i: slice(i * MOBA_BLOCK, (i + 1) * MOBA_BLOCK)

    def score_operand(u, hh):
        j0, lo = first_block(u)
        in_range = (blk >= lo) & (blk < lo + KV_GROUP)
        return j0, jnp.concatenate([q[hh], jnp.where(in_range, pen[hh], jnp.asarray(-MASK_BIG, BF16))], axis=0)

    def score_tile(hh, j, q_aug):
        rows = pl.ds(pl.multiple_of(j * MOBA_BLOCK, MOBA_BLOCK), MOBA_BLOCK)
        tile = jnp.clip(own - j, 0, N_BIAS_TILES)
        return _mm(k_ref[0, hh, rows, :], q_aug).astype(BF16) + bias_ref[hh, tile]

    def scores(u, hh, dst, dst_max):
        j0, q_aug = score_operand(u, hh)
        top = None
        for i in range(KV_GROUP):
            s = score_tile(hh, j0 + i, q_aug)
            dst[tile_rows(i), :] = s
            tile_top = jnp.max(s, axis=0, keepdims=True)
            top = tile_top if top is None else jnp.maximum(top, tile_top)
        dst_max[...] = top.astype(F32)

    def consume(u, hh, src, src_max, carry):
        m, acc = carry
        j0, _ = first_block(u)
        ref_b = jnp.maximum(m, src_max[...]).astype(BF16)
        m_new = ref_b.astype(F32)
        acc = acc * jnp.exp(m - m_new)
        for i in range(KV_GROUP):
            acc = acc + _mm(vt_ref[0, hh, j0 + i], jnp.exp(src[tile_rows(i), :] - ref_b))
        return m_new, acc

    def staggered(u, src, src_max, dst, dst_max, carry):
        carry = list(carry)
        for hh in range(hps):
            scores(u + 1, hh, dst[hh], dst_max[hh])
            carry[hh] = consume(u, hh, src[hh], src_max[hh], carry[hh])
        return tuple(carry)

    carry = tuple((jnp.full((1, MOBA_BLOCK), -MASK_BIG, F32), jnp.zeros((vt_ref.shape[3], MOBA_BLOCK), F32))
                  for _ in range(hps))
    for hh in range(hps):
        scores(0, hh, sa[hh], ma[hh])

    def pair(v, carry):
        u = 2 * v
        carry = staggered(u, sa, ma, sb, mb, carry)
        return staggered(u + 1, sb, mb, sa, ma, carry)

    n_pairs = (n_groups - 1) // 2
    carry = lax.fori_loop(0, n_pairs, pair, carry)
    u_last = 2 * n_pairs

    def last_two(carry):
        carry = staggered(u_last, sa, ma, sb, mb, carry)
        return tuple(consume(u_last + 1, hh, sb[hh], mb[hh], carry[hh]) for hh in range(hps))

    def last_one(carry):
        return tuple(consume(u_last, hh, sa[hh], ma[hh], carry[hh]) for hh in range(hps))

    carry = lax.cond(n_groups - u_last == 2, last_two, last_one, carry)
    for hh in range(hps):
        _, acc = carry[hh]
        g = gt_ref[0, head_rows(hh), :]
        o_ref[0, head_rows(hh), :] = acc[:HEAD_DIM] / acc[HEAD_DIM:HEAD_DIM + 1] * (g * _sigmoid(g))


def _moba_attn(ht, kbar, k_aug, vt_aug, bias):
    b, _, t = ht.shape
    h, nb = k_aug.shape[1], kbar.shape[1]
    hps = MOBA_HEADS_PER_STEP
    rows = hps * HEAD_DIM
    qs = pl.BlockSpec((1, rows, MOBA_BLOCK), lambda i, j, s: (i, j, s))
    gs = pl.BlockSpec((1, rows, MOBA_BLOCK), lambda i, j, s: (i, h // hps + j, s))
    once = pl.Buffered(1)
    heads = lambda a: pl.BlockSpec((1, hps) + a.shape[2:], lambda i, j, s: (i, j) + (0,) * (a.ndim - 2),
                                   pipeline_mode=once)
    return pl.pallas_call(
        _moba_attn_kernel,
        grid=(b, h // hps, nb),
        in_specs=[qs, pl.BlockSpec((1, nb, rows), lambda i, j, s: (i, 0, j)), heads(k_aug), heads(vt_aug), gs,
                  pl.BlockSpec((hps,) + bias.shape[1:], lambda i, j, s: (j, 0, 0, 0), pipeline_mode=once)],
        out_specs=qs,
        out_shape=jax.ShapeDtypeStruct((b, h * HEAD_DIM, t), F32),
        scratch_shapes=[pltpu.VMEM((KV_GROUP * MOBA_BLOCK, MOBA_BLOCK), BF16)] * (2 * hps)
        + [pltpu.VMEM((1, MOBA_BLOCK), F32)] * (2 * hps),
        compiler_params=pltpu.CompilerParams(
            dimension_semantics=("parallel", "parallel", "arbitrary"), vmem_limit_bytes=VMEM_LIMIT),
    )(ht, kbar, k_aug, vt_aug, ht, bias)


def _t5_bucket(rel):
    rel = jnp.maximum(rel, 0)
    max_exact = N_BUCKETS // 2
    rel_f = jnp.maximum(rel, 1).astype(F32)
    large = max_exact + (jnp.log(rel_f / max_exact) / math.log(MAX_DISTANCE / max_exact)
                         * (N_BUCKETS - max_exact)).astype(jnp.int32)
    large = jnp.minimum(large, N_BUCKETS - 1)
    return jnp.where(rel < max_exact, rel, large)


def _bias_tiles(rel_bias):
    d = jnp.arange(N_BIAS_TILES)[:, None, None] * MOBA_BLOCK
    rel = d + jnp.arange(MOBA_BLOCK)[None, None, :] - jnp.arange(MOBA_BLOCK)[None, :, None]
    bucket = _t5_bucket(rel)
    tab = rel_bias.T.astype(F32)
    out = jnp.zeros((tab.shape[0],) + bucket.shape, F32)
    for bk in range(N_BUCKETS):
        out = jnp.where(bucket[None] == bk, tab[:, bk][:, None, None, None], out)
    out = jnp.where(rel[None] < 0, -MASK_BIG, out)
    far = jnp.broadcast_to(tab[:, N_BUCKETS - 1][:, None, None, None], out[:, :1].shape)
    return jnp.concatenate([out, far], axis=1).astype(BF16)


def _rwkv_kernel(h_ref, mu_ref, wup_ref, w0_ref, aup_ref, a0_ref, kk_ref, ka_ref, rk_ref, gw_ref, gb_ref,
                 o_ref, st_ref, prev_ref):
    @pl.when(pl.program_id(1) == 0)
    def _():
        st_ref[...] = jnp.zeros_like(st_ref)
        prev_ref[...] = jnp.zeros_like(prev_ref)

    nc = RWKV_TIME_BLOCK // CHUNK
    nh = RWKV_HEADS
    w = RWKV_W
    p = h_ref[0]
    shifted = pltpu.roll(p, 1, axis=0)
    p_prev = jnp.where(_iota2(p.shape, 0) == 0, prev_ref[...], shifted)
    prev_ref[...] = p[RWKV_TIME_BLOCK - 1:RWKV_TIME_BLOCK, :]
    p = p + (p_prev - p) * mu_ref[...]
    r, k, v, g = p[:, 0:w], p[:, w:2 * w], p[:, 2 * w:3 * w], p[:, 3 * w:4 * w]
    w_dn = p[:, 4 * w:4 * w + RWKV_LORA]
    a_dn = p[:, 4 * w + RWKV_LORA:4 * w + 2 * RWKV_LORA]
    d = w0_ref[...] + _mm(jnp.tanh(w_dn), wup_ref[...], na=2, nb=2)
    lw = -jnp.exp(-_softplus(-d) - 0.5)
    a = _sigmoid(a0_ref[...] + _mm(a_dn, aup_ref[...], na=2, nb=2))
    same_head = (_iota2((w, w), 0) // HEAD_DIM == _iota2((w, w), 1) // HEAD_DIM).astype(BF16)
    kk = k * kk_ref[...]
    kk = kk / jnp.maximum(jnp.sqrt(_mm(kk * kk, same_head, na=3)), 1e-12)
    k = k * (1.0 + (a - 1.0) * ka_ref[...])
    bonus = _mm(r * k * rk_ref[...], same_head, na=3) * v

    shape3 = (nh * nc, CHUNK, CHUNK)
    row = _iota2(shape3, 1)
    col = _iota2(shape3, 2)
    tri_b = (row >= col).astype(BF16)
    eye = row == col
    hp = dict()
    carry_prec = dict(na=2, nb=2)
    r, lw, k, v = (_split_heads(x, nc) for x in (r, lw, k, v))
    a_vec = _split_heads(-kk, nc)
    b_vec = _split_heads(kk * a, nc)
    cum = _mm(tri_b, lw, _BNN, nb=3)
    last = cum[:, CHUNK - 1:CHUNK, :]
    inv = jnp.exp(-cum)
    to_end = jnp.exp(last - cum)
    r_t = r * jnp.exp(cum)
    a_t = a_vec * jnp.exp(cum - lw)
    k_t = k * inv
    b_t = b_vec * inv
    gram = _mm(jnp.concatenate([a_t, r_t], axis=1), jnp.concatenate([b_t, k_t], axis=1), _BNT, **hp)
    a_ab = jnp.where(row > col, gram[:, :CHUNK, :CHUNK], 0.0)
    a_ak = jnp.where(row > col, gram[:, :CHUNK, CHUNK:], 0.0)
    a_rb = jnp.where(row >= col, gram[:, CHUNK:, :CHUNK], 0.0)
    a_rk = jnp.where(row >= col, gram[:, CHUNK:, CHUNK:], 0.0)
    inv_t = jnp.where(eye, 1.0, a_ab)
    power = a_ab
    for _ in range(int(math.log2(CHUNK)) - 1):
        power = _mm(power, power, _BNN, **hp)
        inv_t = inv_t + _mm(inv_t, power, _BNN, **hp)
    w_mat = _mm(inv_t, a_t, _BNN, **hp)
    u0 = _mm(inv_t, _mm(a_ak, v, _BNN, **hp), _BNN, **hp)
    y_w = r_t + _mm(a_rb, w_mat, _BNN, **hp)
    y0 = _mm(a_rb, u0, _BNN, **hp) + _mm(a_rk, v, _BNN, **hp)
    b_end = b_vec * to_end
    k_end = k * to_end
    m_mat = jnp.where(eye, jnp.exp(last), 0.0) + _mm(b_end, w_mat, _BTN, **hp)
    n_mat = _mm(b_end, u0, _BTN, **hp) + _mm(k_end, v, _BTN, **hp)
    st = st_ref[...]
    ys = []
    for c in range(nc):
        ys.append(_mm(_chunk_of(y_w, nh, c), st, _BNN, **hp))
        st = _mm(_chunk_of(m_mat, nh, c), st, _BNN, **carry_prec) + _chunk_of(n_mat, nh, c)
    st_ref[...] = st
    y = y0 + jnp.stack(ys, axis=1).reshape(shape3)
    mean = jnp.mean(y, axis=-1, keepdims=True)
    var = jnp.mean(jnp.square(y - mean), axis=-1, keepdims=True)
    y = _merge_heads((y - mean) * lax.rsqrt(var + RWKV_GN_EPS), nh) * gw_ref[...] + gb_ref[...]
    o_ref[0] = (y + bonus) * (g * _sigmoid(g))


def _rwkv(h_rwkv, mu, w_up, w0, a_up, a0, k_k, k_a, r_k, gn_w, gn_b):
    b, t, c = h_rwkv.shape
    full = lambda a: pl.BlockSpec(a.shape, lambda i, s: (0, 0))
    params = (mu, w_up, w0, a_up, a0, k_k, k_a, r_k, gn_w, gn_b)
    return pl.pallas_call(
        _rwkv_kernel,
        grid=(b, t // RWKV_TIME_BLOCK),
        in_specs=[pl.BlockSpec((1, RWKV_TIME_BLOCK, c), lambda i, s: (i, s, 0))] + [full(a) for a in params],
        out_specs=pl.BlockSpec((1, RWKV_TIME_BLOCK, RWKV_W), lambda i, s: (i, s, 0)),
        out_shape=jax.ShapeDtypeStruct((b, t, RWKV_W), F32),
        scratch_shapes=[pltpu.VMEM((RWKV_HEADS, HEAD_DIM, HEAD_DIM), F32), pltpu.VMEM((1, c), F32)],
        compiler_params=pltpu.CompilerParams(
            dimension_semantics=("parallel", "arbitrary"), vmem_limit_bytes=VMEM_LIMIT),
    )(h_rwkv, *params)


def _proj_out_kernel(alpha, og_ref, ot_ref, or_ref, w_ref, x_ref, lw_ref, lb_ref, y_ref):
    y = _mm(og_ref[0].astype(BF16), w_ref[0:GLA_W, :])
    y = y + _mm(ot_ref[0].astype(BF16), w_ref[GLA_W:GLA_W + MOBA_W, :], _TN)
    y = y + _mm(or_ref[0].astype(BF16), w_ref[GLA_W + MOBA_W:, :])
    z = alpha * x_ref[0] + y
    mu = jnp.mean(z, axis=-1, keepdims=True)
    var = jnp.mean(jnp.square(z - mu), axis=-1, keepdims=True)
    y_ref[0] = (z - mu) * lax.rsqrt(var + LN_EPS) * lw_ref[...] + lb_ref[...]


def _proj_out(o_gla, ot_moba, o_rwkv, w, x, ln_w, ln_b, alpha):
    b, t, d = x.shape
    rs = lambda n: pl.BlockSpec((1, OUT_ROW_BLOCK, n), lambda i, s: (i, s, 0))
    vs = pl.BlockSpec((1, d), lambda i, s: (0, 0))
    return pl.pallas_call(
        functools.partial(_proj_out_kernel, alpha),
        grid=(b, t // OUT_ROW_BLOCK),
        in_specs=[rs(GLA_W), pl.BlockSpec((1, MOBA_W, OUT_ROW_BLOCK), lambda i, s: (i, 0, s)), rs(RWKV_W),
                  pl.BlockSpec(w.shape, lambda i, s: (0, 0)), rs(d), vs, vs],
        out_specs=rs(d),
        out_shape=jax.ShapeDtypeStruct((b, t, d), F32),
        compiler_params=pltpu.CompilerParams(
            dimension_semantics=("parallel", "parallel"), vmem_limit_bytes=VMEM_LIMIT),
    )(o_gla, ot_moba, o_rwkv, w, x, ln_w, ln_b)


def _pad_cols(a, n):
    return jnp.pad(a, ((0, 0), (0, n - a.shape[1])))


def _split_w_in(w_l):
    gla, moba = w_l[:, :GLA_COLS], w_l[:, GLA_COLS:GLA_COLS + MOBA_COLS]
    rwkv = w_l[:, GLA_COLS + MOBA_COLS:]
    mq, mk, mv, mg = (moba[:, i * MOBA_W:(i + 1) * MOBA_W] for i in range(4))
    w_t = jnp.concatenate([mq, mg], axis=1).T
    return tuple(a.astype(BF16) for a in (_pad_cols(gla, GLA_PAD), _pad_cols(rwkv, RWKV_PAD), mk, w_t, mv.T))


def _moba_branch(ht, k_aug, kbar, vt_aug, bias):
    return _moba_attn(ht, kbar.reshape(kbar.shape[0], -1, MOBA_W), k_aug, vt_aug, bias)


def _gla_branch(h_gla, a_up, a_bias, norm_w):
    return _gla(h_gla, a_up, a_bias.reshape(1, GLA_W), norm_w.reshape(1, HEAD_DIM))


def _rwkv_branch(h_rwkv, mu, w_up, w0, a_up, a0, k_k, k_a, r_k, gn_w, gn_b):
    row = lambda p: p.reshape(1, -1)
    return _rwkv(h_rwkv, _pad_cols(row(mu), h_rwkv.shape[2]), w_up, row(w0), a_up, row(a0), row(k_k), row(k_a),
                 row(r_k), row(gn_w), row(gn_b))


def kernel(x, w_in, w_out, gla_a_up, gla_a_bias, gla_norm_w, moba_rel_bias, rwkv_mu, rwkv_w_up, rwkv_w0,
           rwkv_a_up, rwkv_a0, rwkv_k_k, rwkv_k_a, rwkv_r_k, rwkv_gn_w, rwkv_gn_b, ln_w, ln_b):
    bsz, t, d = x.shape
    depth = w_in.shape[0]
    alpha = (2.0 * depth) ** 0.25
    bias = _bias_tiles(moba_rel_bias)
    for l in range(depth):
        h_gla, h_rwkv, ht, k_aug, kbar, vt_aug = _proj_in(x, *_split_w_in(w_in[l]))
        o_gla = _gla_branch(h_gla, gla_a_up[l], gla_a_bias[l], gla_norm_w[l])
        ot_moba = _moba_branch(ht, k_aug, kbar, vt_aug, bias)
        o_rwkv = _rwkv_branch(h_rwkv, rwkv_mu[l], rwkv_w_up[l], rwkv_w0[l], rwkv_a_up[l], rwkv_a0[l],
                              rwkv_k_k[l], rwkv_k_a[l], rwkv_r_k[l], rwkv_gn_w[l], rwkv_gn_b[l])
        x = _proj_out(o_gla, ot_moba, o_rwkv, w_out[l].astype(BF16), x,
                      ln_w[l].reshape(1, d), ln_b[l].reshape(1, d), alpha)
    return x
```

```python
import functools
import math

import jax
import jax.numpy as jnp
from jax import lax
from jax.experimental import pallas as pl
from jax.experimental.pallas import tpu as pltpu

HEAD_DIM = 64
GLA_HEADS = 4
MOBA_HEADS = 8
RWKV_HEADS = 4
GLA_W = GLA_HEADS * HEAD_DIM
MOBA_W = MOBA_HEADS * HEAD_DIM
RWKV_W = RWKV_HEADS * HEAD_DIM
GLA_LOWRANK = 16
GLA_TAU = 16.0
RWKV_LORA = 32
RWKV_GN_EPS = 64e-5
MOBA_BLOCK = 256
MOBA_TOPK = 3
N_BUCKETS = 32
MAX_DISTANCE = 4096
LN_EPS = 1e-5
GLA_COLS = 4 * GLA_W + GLA_LOWRANK
MOBA_COLS = 4 * MOBA_W
RWKV_COLS = 4 * RWKV_W + 2 * RWKV_LORA

LANE = 128
SUBLANE_BF16 = 16
GLA_PAD = -(-GLA_COLS // LANE) * LANE
RWKV_PAD = -(-RWKV_COLS // LANE) * LANE
CHUNK = 64
SUB = 16
GLA_TIME_BLOCK = 1024
RWKV_TIME_BLOCK = 512
ROW_BLOCK = 512
OUT_ROW_BLOCK = 1024
N_BIAS_TILES = (MAX_DISTANCE + MOBA_BLOCK - 1) // MOBA_BLOCK + 1
KV_GROUP = 4
MOBA_HEADS_PER_STEP = 4
VMEM_LIMIT = 56 * 1024 * 1024

F32 = jnp.float32
BF16 = jnp.bfloat16
NEG_INF = float("-inf")
MASK_BIG = 2.0 ** 100


def _parts(a, n):
    out, r = [], a
    for i in range(n):
        p = r.astype(BF16)
        out.append(p)
        if i + 1 < n:
            r = r - p.astype(F32)
    return out


def _mm(a, b, dims=(((1,), (0,)), ((), ())), na=1, nb=1):
    ap = [a] if a.dtype == BF16 else _parts(a, na)
    bp = [b] if b.dtype == BF16 else _parts(b, nb)
    n = max(len(ap), len(bp))
    acc = None
    for i, x in enumerate(ap):
        for j, y in enumerate(bp):
            if i + j < n:
                t = lax.dot_general(x, y, dims, preferred_element_type=F32)
                acc = t if acc is None else acc + t
    return acc


_NT = (((1,), (1,)), ((), ()))
_TN = (((0,), (0,)), ((), ()))
_BNN = (((2,), (1,)), ((0,), (0,)))
_BNT = (((2,), (2,)), ((0,), (0,)))
_BTN = (((1,), (1,)), ((0,), (0,)))


def _softplus(x):
    return jnp.maximum(x, 0.0) + jnp.log(1.0 + jnp.exp(-jnp.abs(x)))


def _sigmoid(x):
    return 1.0 / (1.0 + jnp.exp(-x))


def _iota2(shape, dim):
    return lax.broadcasted_iota(jnp.int32, shape, dim)


def _split_heads(x, nc):
    n = x.shape[1] // HEAD_DIM
    return jnp.concatenate(
        [x[:, h * HEAD_DIM:(h + 1) * HEAD_DIM].reshape(nc, CHUNK, HEAD_DIM) for h in range(n)], axis=0)


def _merge_heads(x, n):
    nc = x.shape[0] // n
    return jnp.concatenate([x[h * nc:(h + 1) * nc].reshape(nc * CHUNK, HEAD_DIM) for h in range(n)], axis=1)


def _chunk_of(x, n, c):
    return x.reshape((n, x.shape[0] // n) + x.shape[1:])[:, c]


def _proj_in_kernel(x_ref, wg_ref, wr_ref, wk_ref, wt_ref, wv_ref,
                    hg_ref, hr_ref, ht_ref, kaug_ref, kbar_ref, vt_ref):
    nb = kaug_ref.shape[3] - HEAD_DIM
    blocks = ROW_BLOCK // MOBA_BLOCK
    xb = x_ref[0].astype(BF16)
    hg_ref[0] = jnp.dot(xb, wg_ref[...], preferred_element_type=F32)
    hr_ref[0] = jnp.dot(xb, wr_ref[...], preferred_element_type=F32)
    ht_ref[0] = lax.dot_general(wt_ref[...], xb, _NT, preferred_element_type=F32)
    k = jnp.dot(xb, wk_ref[...], preferred_element_type=F32)
    first = pl.program_id(1) * blocks
    onehot = (_iota2((ROW_BLOCK, nb), 1) == first + _iota2((ROW_BLOCK, nb), 0) // MOBA_BLOCK).astype(BF16)
    for h in range(MOBA_HEADS):
        kaug_ref[0, h] = jnp.concatenate([k[:, h * HEAD_DIM:(h + 1) * HEAD_DIM].astype(BF16), onehot], axis=1)
    for i in range(blocks):
        kbar_ref[0, 0, i:i + 1, :] = jnp.sum(k[i * MOBA_BLOCK:(i + 1) * MOBA_BLOCK], axis=0,
                                             keepdims=True) * (1.0 / MOBA_BLOCK)
    vt = lax.dot_general(wv_ref[...], xb, _NT, preferred_element_type=F32)
    pad = jnp.concatenate([jnp.ones((1, MOBA_BLOCK), F32), jnp.zeros((SUBLANE_BF16 - 1, MOBA_BLOCK), F32)], axis=0)
    for h in range(MOBA_HEADS):
        for i in range(blocks):
            tile = vt[h * HEAD_DIM:(h + 1) * HEAD_DIM, i * MOBA_BLOCK:(i + 1) * MOBA_BLOCK]
            vt_ref[0, h, i] = jnp.concatenate([tile, pad], axis=0).astype(BF16)


def _proj_in(x, w_gla, w_rwkv, w_k, w_t, w_v):
    b, t, d = x.shape
    nb = t // MOBA_BLOCK
    blocks = ROW_BLOCK // MOBA_BLOCK
    full = lambda w: pl.BlockSpec(w.shape, lambda i, s: (0, 0))
    rows = lambda n: pl.BlockSpec((1, ROW_BLOCK, n), lambda i, s: (i, s, 0))
    nt = w_t.shape[0]
    vrows = HEAD_DIM + SUBLANE_BF16
    return pl.pallas_call(
        _proj_in_kernel,
        grid=(b, t // ROW_BLOCK),
        in_specs=[rows(d), full(w_gla), full(w_rwkv), full(w_k), full(w_t), full(w_v)],
        out_specs=[rows(w_gla.shape[1]), rows(w_rwkv.shape[1]),
                   pl.BlockSpec((1, nt, ROW_BLOCK), lambda i, s: (i, 0, s)),
                   pl.BlockSpec((1, MOBA_HEADS, ROW_BLOCK, HEAD_DIM + nb), lambda i, s: (i, 0, s, 0)),
                   pl.BlockSpec((1, 1, blocks, MOBA_W), lambda i, s: (i, s, 0, 0)),
                   pl.BlockSpec((1, MOBA_HEADS, blocks, vrows, MOBA_BLOCK), lambda i, s: (i, 0, s, 0, 0))],
        out_shape=[jax.ShapeDtypeStruct((b, t, w_gla.shape[1]), F32),
                   jax.ShapeDtypeStruct((b, t, w_rwkv.shape[1]), F32),
                   jax.ShapeDtypeStruct((b, nt, t), F32),
                   jax.ShapeDtypeStruct((b, MOBA_HEADS, t, HEAD_DIM + nb), BF16),
                   jax.ShapeDtypeStruct((b, nb // blocks, blocks, MOBA_W), F32),
                   jax.ShapeDtypeStruct((b, MOBA_HEADS, nb, vrows, MOBA_BLOCK), BF16)],
        compiler_params=pltpu.CompilerParams(
            dimension_semantics=("parallel", "parallel"), vmem_limit_bytes=VMEM_LIMIT),
    )(x, w_gla, w_rwkv, w_k, w_t, w_v)


def _gla_kernel(h_ref, aup_ref, ab_ref, nw_ref, o_ref, st_ref):
    @pl.when(pl.program_id(1) == 0)
    def _():
        st_ref[...] = jnp.zeros_like(st_ref)

    nc = GLA_TIME_BLOCK // CHUNK
    nh = GLA_HEADS
    shape3 = (nh * nc, CHUNK, CHUNK)
    hb = h_ref[0]
    z = _mm(hb[:, 4 * GLA_W:4 * GLA_W + GLA_LOWRANK], aup_ref[...], na=2, nb=2) + ab_ref[...]
    la = _split_heads(-_softplus(-z) * (1.0 / GLA_TAU), nc)
    row = _iota2((1, CHUNK, CHUNK), 1)
    col = _iota2((1, CHUNK, CHUNK), 2)
    tri = (row >= col)
    every = lambda mask: jnp.broadcast_to(mask.astype(BF16), shape3)
    q = _split_heads(hb[:, 0:GLA_W], nc) * (HEAD_DIM ** -0.5)
    k = _split_heads(hb[:, GLA_W:2 * GLA_W], nc)
    v = _split_heads(hb[:, 2 * GLA_W:3 * GLA_W], nc)
    cum = _mm(every(tri), la, _BNN, nb=3)
    k_anchor = _mm(every(col <= (row // SUB) * SUB + (SUB - 1)), la, _BNN, nb=3)
    k_t = k * jnp.exp(k_anchor - cum)
    scores = jnp.zeros(shape3, F32)
    for j in range(CHUNK // SUB - 1):
        a_j = cum[:, j * SUB + SUB - 1:j * SUB + SUB, :]
        q_j = q * jnp.exp(jnp.where(row >= (j + 1) * SUB, cum - a_j, NEG_INF))
        in_grp = (row >= j * SUB) & (row < (j + 1) * SUB)
        scores = scores + _mm(q_j, jnp.where(in_grp, k_t, 0.0), _BNT)
    half = SUB // 2
    while half >= 1:
        pair_end = (row // (2 * half)) * (2 * half) + (half - 1)
        anchor = _mm(every(col <= pair_end), la, _BNN)
        upper = (row // half) % 2 == 1
        decay = jnp.exp(jnp.where(upper, cum - anchor, anchor - cum))
        q_h = jnp.where(upper, q * decay, 0.0)
        k_h = jnp.where(upper, 0.0, k * decay)
        same_pair = row // (2 * half) == col // (2 * half)
        scores = scores + jnp.where(same_pair, _mm(q_h, k_h, _BNT), 0.0)
        half //= 2
    diag = _mm(q * k, every(row >= 0), _BNN)
    scores = scores + jnp.where(row == col, diag, 0.0)
    o_intra = _mm(jnp.where(tri, scores, 0.0), v, _BNN)
    last = cum[:, CHUNK - 1:CHUNK, :]
    n_mat = _mm(v, k * jnp.exp(last - cum), _BTN)
    q_dec = q * jnp.exp(cum)
    dec_last = jnp.exp(last)
    st = st_ref[...]
    o_inter = []
    for c in range(nc):
        o_inter.append(_mm(_chunk_of(q_dec, nh, c), st, _BNT))
        st = st * _chunk_of(dec_last, nh, c) + _chunk_of(n_mat, nh, c)
    st_ref[...] = st
    o = o_intra + jnp.stack(o_inter, axis=1).reshape(shape3)
    o = o * lax.rsqrt(jnp.mean(o * o, axis=-1, keepdims=True) + LN_EPS) * nw_ref[...]
    g = hb[:, 3 * GLA_W:4 * GLA_W]
    o_ref[0] = _merge_heads(o, nh) * (g * _sigmoid(g))


def _gla(h_gla, a_up, a_bias, norm_w):
    b, t, c = h_gla.shape
    vec = lambda a: pl.BlockSpec(a.shape, lambda i, s: (0, 0))
    return pl.pallas_call(
        _gla_kernel,
        grid=(b, t // GLA_TIME_BLOCK),
        in_specs=[pl.BlockSpec((1, GLA_TIME_BLOCK, c), lambda i, s: (i, s, 0)), vec(a_up), vec(a_bias),
                  vec(norm_w)],
        out_specs=pl.BlockSpec((1, GLA_TIME_BLOCK, GLA_W), lambda i, s: (i, s, 0)),
        out_shape=jax.ShapeDtypeStruct((b, t, GLA_W), F32),
        scratch_shapes=[pltpu.VMEM((GLA_HEADS, HEAD_DIM, HEAD_DIM), F32)],
        compiler_params=pltpu.CompilerParams(
            dimension_semantics=("parallel", "arbitrary"), vmem_limit_bytes=VMEM_LIMIT),
    )(h_gla, a_up, a_bias, norm_w)


def _block_penalty(q_t, kbar, own):
    gate = _mm(kbar, q_t, na=2, nb=2)
    blk = _iota2(gate.shape, 0)
    nb = gate.shape[0]
    gate = jnp.where(blk < own, gate, NEG_INF)
    allowed = blk == own
    for _ in range(min(MOBA_TOPK, nb)):
        best = jnp.max(gate, axis=0, keepdims=True)
        first = jnp.min(jnp.where(gate == best, blk, nb), axis=0, keepdims=True)
        hit = blk == first
        allowed = allowed | (hit & (blk < own))
        gate = jnp.where(hit, NEG_INF, gate)
    return jnp.where(allowed, 0.0, -MASK_BIG).astype(BF16)


def _moba_attn_kernel(qt_ref, kbar_ref, k_ref, vt_ref, gt_ref, bias_ref, o_ref, *scratch):
    hps = MOBA_HEADS_PER_STEP
    own = pl.program_id(2)
    nb = kbar_ref.shape[1]
    n_groups = (own + KV_GROUP) // KV_GROUP
    head_rows = lambda hh: slice(hh * HEAD_DIM, (hh + 1) * HEAD_DIM)
    q = [(qt_ref[0, head_rows(hh), :] * (HEAD_DIM ** -0.5)).astype(BF16) for hh in range(hps)]
    pen = [_block_penalty(qt_ref[0, head_rows(hh), :], kbar_ref[0][:, head_rows(hh)], own) for hh in range(hps)]
    blk = _iota2(pen[0].shape, 0)
    sa, sb, ma, mb = (scratch[i * hps:(i + 1) * hps] for i in range(4))

    def first_block(u):
        lo = own + 1 - (u + 1) * KV_GROUP
        return jnp.clip(lo, 0, nb - KV_GROUP), lo

    tile_rows = lambda i: slice(i * MOBA_BLOCK, (i + 1) * MOBA_BLOCK)

    def score_operand(u, hh):
        j0, lo = first_block(u)
        in_range = (blk >= lo) & (blk < lo + KV_GROUP)
        return j0, jnp.concatenate([q[hh], jnp.where(in_range, pen[hh], jnp.asarray(-MASK_BIG, BF16))], axis=0)

    def score_tile(hh, j, q_aug):
        rows = pl.ds(pl.multiple_of(j * MOBA_BLOCK, MOBA_BLOCK), MOBA_BLOCK)
        tile = jnp.clip(own - j, 0, N_BIAS_TILES)
        return _mm(k_ref[0, hh, rows, :], q_aug).astype(BF16) + bias_ref[hh, tile]

    def scores(u, hh, dst, dst_max):
        j0, q_aug = score_operand(u, hh)
        top = None
        for i in range(KV_GROUP):
            s = score_tile(hh, j0 + i, q_aug)
            dst[tile_rows(i), :] = s
            tile_top = jnp.max(s, axis=0, keepdims=True)
            top = tile_top if top is None else jnp.maximum(top, tile_top)
        dst_max[...] = top.astype(F32)

    def consume(u, hh, src, src_max, carry):
        m, acc = carry
        j0, _ = first_block(u)
        ref_b = jnp.maximum(m, src_max[...]).astype(BF16)
        m_new = ref_b.astype(F32)
        acc = acc * jnp.exp(m - m_new)
        for i in range(KV_GROUP):
            acc = acc + _mm(vt_ref[0, hh, j0 + i], jnp.exp(src[tile_rows(i), :] - ref_b))
        return m_new, acc

    def staggered(u, src, src_max, dst, dst_max, carry):
        carry = list(carry)
        for hh in range(hps):
            scores(u + 1, hh, dst[hh], dst_max[hh])
            carry[hh] = consume(u, hh, src[hh], src_max[hh], carry[hh])
        return tuple(carry)

    carry = tuple((jnp.full((1, MOBA_BLOCK), -MASK_BIG, F32), jnp.zeros((vt_ref.shape[3], MOBA_BLOCK), F32))
                  for _ in range(hps))
    for hh in range(hps):
        scores(0, hh, sa[hh], ma[hh])

    def pair(v, carry):
        u = 2 * v
        carry = staggered(u, sa, ma, sb, mb, carry)
        return staggered(u + 1, sb, mb, sa, ma, carry)

    n_pairs = (n_groups - 1) // 2
    carry = lax.fori_loop(0, n_pairs, pair, carry)
    u_last = 2 * n_pairs

    def last_two(carry):
        carry = staggered(u_last, sa, ma, sb, mb, carry)
        return tuple(consume(u_last + 1, hh, sb[hh], mb[hh], carry[hh]) for hh in range(hps))

    def last_one(carry):
        return tuple(consume(u_last, hh, sa[hh], ma[hh], carry[hh]) for hh in range(hps))

    carry = lax.cond(n_groups - u_last == 2, last_two, last_one, carry)
    for hh in range(hps):
        _, acc = carry[hh]
        g = gt_ref[0, head_rows(hh), :]
        o_ref[0, head_rows(hh), :] = acc[:HEAD_DIM] / acc[HEAD_DIM:HEAD_DIM + 1] * (g * _sigmoid(g))


def _moba_attn(ht, kbar, k_aug, vt_aug, bias):
    b, _, t = ht.shape
    h, nb = k_aug.shape[1], kbar.shape[1]
    hps = MOBA_HEADS_PER_STEP
    rows = hps * HEAD_DIM
    qs = pl.BlockSpec((1, rows, MOBA_BLOCK), lambda i, j, s: (i, j, s))
    gs = pl.BlockSpec((1, rows, MOBA_BLOCK), lambda i, j, s: (i, h // hps + j, s))
    once = pl.Buffered(1)
    heads = lambda a: pl.BlockSpec((1, hps) + a.shape[2:], lambda i, j, s: (i, j) + (0,) * (a.ndim - 2),
                                   pipeline_mode=once)
    return pl.pallas_call(
        _moba_attn_kernel,
        grid=(b, h // hps, nb),
        in_specs=[qs, pl.BlockSpec((1, nb, rows), lambda i, j, s: (i, 0, j)), heads(k_aug), heads(vt_aug), gs,
                  pl.BlockSpec((hps,) + bias.shape[1:], lambda i, j, s: (j, 0, 0, 0), pipeline_mode=once)],
        out_specs=qs,
        out_shape=jax.ShapeDtypeStruct((b, h * HEAD_DIM, t), F32),
        scratch_shapes=[pltpu.VMEM((KV_GROUP * MOBA_BLOCK, MOBA_BLOCK), BF16)] * (2 * hps)
        + [pltpu.VMEM((1, MOBA_BLOCK), F32)] * (2 * hps),
        compiler_params=pltpu.CompilerParams(
            dimension_semantics=("parallel", "parallel", "arbitrary"), vmem_limit_bytes=VMEM_LIMIT),
    )(ht, kbar, k_aug, vt_aug, ht, bias)


def _t5_bucket(rel):
    rel = jnp.maximum(rel, 0)
    max_exact = N_BUCKETS // 2
    rel_f = jnp.maximum(rel, 1).astype(F32)
    large = max_exact + (jnp.log(rel_f / max_exact) / math.log(MAX_DISTANCE / max_exact)
                         * (N_BUCKETS - max_exact)).astype(jnp.int32)
    large = jnp.minimum(large, N_BUCKETS - 1)
    return jnp.where(rel < max_exact, rel, large)


def _bias_tiles(rel_bias):
    d = jnp.arange(N_BIAS_TILES)[:, None, None] * MOBA_BLOCK
    rel = d + jnp.arange(MOBA_BLOCK)[None, None, :] - jnp.arange(MOBA_BLOCK)[None, :, None]
    bucket = _t5_bucket(rel)
    tab = rel_bias.T.astype(F32)
    out = jnp.zeros((tab.shape[0],) + bucket.shape, F32)
    for bk in range(N_BUCKETS):
        out = jnp.where(bucket[None] == bk, tab[:, bk][:, None, None, None], out)
    out = jnp.where(rel[None] < 0, -MASK_BIG, out)
    far = jnp.broadcast_to(tab[:, N_BUCKETS - 1][:, None, None, None], out[:, :1].shape)
    return jnp.concatenate([out, far], axis=1).astype(BF16)


def _rwkv_kernel(h_ref, mu_ref, wup_ref, w0_ref, aup_ref, a0_ref, kk_ref, ka_ref, rk_ref, gw_ref, gb_ref,
                 o_ref, st_ref, prev_ref):
    @pl.when(pl.program_id(1) == 0)
    def _():
        st_ref[...] = jnp.zeros_like(st_ref)
        prev_ref[...] = jnp.zeros_like(prev_ref)

    nc = RWKV_TIME_BLOCK // CHUNK
    nh = RWKV_HEADS
    w = RWKV_W
    p = h_ref[0]
    shifted = pltpu.roll(p, 1, axis=0)
    p_prev = jnp.where(_iota2(p.shape, 0) == 0, prev_ref[...], shifted)
    prev_ref[...] = p[RWKV_TIME_BLOCK - 1:RWKV_TIME_BLOCK, :]
    p = p + (p_prev - p) * mu_ref[...]
    r, k, v, g = p[:, 0:w], p[:, w:2 * w], p[:, 2 * w:3 * w], p[:, 3 * w:4 * w]
    w_dn = p[:, 4 * w:4 * w + RWKV_LORA]
    a_dn = p[:, 4 * w + RWKV_LORA:4 * w + 2 * RWKV_LORA]
    d = w0_ref[...] + _mm(jnp.tanh(w_dn), wup_ref[...], na=2, nb=2)
    lw = -jnp.exp(-_softplus(-d) - 0.5)
    a = _sigmoid(a0_ref[...] + _mm(a_dn, aup_ref[...], na=2, nb=2))
    same_head = (_iota2((w, w), 0) // HEAD_DIM == _iota2((w, w), 1) // HEAD_DIM).astype(BF16)
    kk = k * kk_ref[...]
    kk = kk / jnp.maximum(jnp.sqrt(_mm(kk * kk, same_head, na=3)), 1e-12)
    k = k * (1.0 + (a - 1.0) * ka_ref[...])
    bonus = _mm(r * k * rk_ref[...], same_head, na=3) * v

    shape3 = (nh * nc, CHUNK, CHUNK)
    row = _iota2(shape3, 1)
    col = _iota2(shape3, 2)
    tri_b = (row >= col).astype(BF16)
    eye = row == col
    hp = dict()
    carry_prec = dict(na=2, nb=2)
    r, lw, k, v = (_split_heads(x, nc) for x in (r, lw, k, v))
    a_vec = _split_heads(-kk, nc)
    b_vec = _split_heads(kk * a, nc)
    cum = _mm(tri_b, lw, _BNN, nb=3)
    last = cum[:, CHUNK - 1:CHUNK, :]
    inv = jnp.exp(-cum)
    to_end = jnp.exp(last - cum)
    r_t = r * jnp.exp(cum)
    a_t = a_vec * jnp.exp(cum - lw)
    k_t = k * inv
    b_t = b_vec * inv
    gram = _mm(jnp.concatenate([a_t, r_t], axis=1), jnp.concatenate([b_t, k_t], axis=1), _BNT, **hp)
    a_ab = jnp.where(row > col, gram[:, :CHUNK, :CHUNK], 0.0)
    a_ak = jnp.where(row > col, gram[:, :CHUNK, CHUNK:], 0.0)
    a_rb = jnp.where(row >= col, gram[:, CHUNK:, :CHUNK], 0.0)
    a_rk = jnp.where(row >= col, gram[:, CHUNK:, CHUNK:], 0.0)
    inv_t = jnp.where(eye, 1.0, a_ab)
    power = a_ab
    for _ in range(int(math.log2(CHUNK)) - 1):
        power = _mm(power, power, _BNN, **hp)
        inv_t = inv_t + _mm(inv_t, power, _BNN, **hp)
    w_mat = _mm(inv_t, a_t, _BNN, **hp)
    u0 = _mm(inv_t, _mm(a_ak, v, _BNN, **hp), _BNN, **hp)
    y_w = r_t + _mm(a_rb, w_mat, _BNN, **hp)
    y0 = _mm(a_rb, u0, _BNN, **hp) + _mm(a_rk, v, _BNN, **hp)
    b_end = b_vec * to_end
    k_end = k * to_end
    m_mat = jnp.where(eye, jnp.exp(last), 0.0) + _mm(b_end, w_mat, _BTN, **hp)
    n_mat = _mm(b_end, u0, _BTN, **hp) + _mm(k_end, v, _BTN, **hp)
    st = st_ref[...]
    ys = []
    for c in range(nc):
        ys.append(_mm(_chunk_of(y_w, nh, c), st, _BNN, **hp))
        st = _mm(_chunk_of(m_mat, nh, c), st, _BNN, **carry_prec) + _chunk_of(n_mat, nh, c)
    st_ref[...] = st
    y = y0 + jnp.stack(ys, axis=1).reshape(shape3)
    mean = jnp.mean(y, axis=-1, keepdims=True)
    var = jnp.mean(jnp.square(y - mean), axis=-1, keepdims=True)
    y = _merge_heads((y - mean) * lax.rsqrt(var + RWKV_GN_EPS), nh) * gw_ref[...] + gb_ref[...]
    o_ref[0] = (y + bonus) * (g * _sigmoid(g))


def _rwkv(h_rwkv, mu, w_up, w0, a_up, a0, k_k, k_a, r_k, gn_w, gn_b):
    b, t, c = h_rwkv.shape
    full = lambda a: pl.BlockSpec(a.shape, lambda i, s: (0, 0))
    params = (mu, w_up, w0, a_up, a0, k_k, k_a, r_k, gn_w, gn_b)
    return pl.pallas_call(
        _rwkv_kernel,
        grid=(b, t // RWKV_TIME_BLOCK),
        in_specs=[pl.BlockSpec((1, RWKV_TIME_BLOCK, c), lambda i, s: (i, s, 0))] + [full(a) for a in params],
        out_specs=pl.BlockSpec((1, RWKV_TIME_BLOCK, RWKV_W), lambda i, s: (i, s, 0)),
        out_shape=jax.ShapeDtypeStruct((b, t, RWKV_W), F32),
        scratch_shapes=[pltpu.VMEM((RWKV_HEADS, HEAD_DIM, HEAD_DIM), F32), pltpu.VMEM((1, c), F32)],
        compiler_params=pltpu.CompilerParams(
            dimension_semantics=("parallel", "arbitrary"), vmem_limit_bytes=VMEM_LIMIT),
    )(h_rwkv, *params)


def _proj_out_kernel(alpha, og_ref, ot_ref, or_ref, w_ref, x_ref, lw_ref, lb_ref, y_ref):
    y = _mm(og_ref[0].astype(BF16), w_ref[0:GLA_W, :])
    y = y + _mm(ot_ref[0].astype(BF16), w_ref[GLA_W:GLA_W + MOBA_W, :], _TN)
    y = y + _mm(or_ref[0].astype(BF16), w_ref[GLA_W + MOBA_W:, :])
    z = alpha * x_ref[0] + y
    mu = jnp.mean(z, axis=-1, keepdims=True)
    var = jnp.mean(jnp.square(z - mu), axis=-1, keepdims=True)
    y_ref[0] = (z - mu) * lax.rsqrt(var + LN_EPS) * lw_ref[...] + lb_ref[...]


def _proj_out(o_gla, ot_moba, o_rwkv, w, x, ln_w, ln_b, alpha):
    b, t, d = x.shape
    rs = lambda n: pl.BlockSpec((1, OUT_ROW_BLOCK, n), lambda i, s: (i, s, 0))
    vs = pl.BlockSpec((1, d), lambda i, s: (0, 0))
    return pl.pallas_call(
        functools.partial(_proj_out_kernel, alpha),
        grid=(b, t // OUT_ROW_BLOCK),
        in_specs=[rs(GLA_W), pl.BlockSpec((1, MOBA_W, OUT_ROW_BLOCK), lambda i, s: (i, 0, s)), rs(RWKV_W),
                  pl.BlockSpec(w.shape, lambda i, s: (0, 0)), rs(d), vs, vs],
        out_specs=rs(d),
        out_shape=jax.ShapeDtypeStruct((b, t, d), F32),
        compiler_params=pltpu.CompilerParams(
            dimension_semantics=("parallel", "parallel"), vmem_limit_bytes=VMEM_LIMIT),
    )(o_gla, ot_moba, o_rwkv, w, x, ln_w, ln_b)


def _pad_cols(a, n):
    return jnp.pad(a, ((0, 0), (0, n - a.shape[1])))


def _split_w_in(w_l):
    gla, moba = w_l[:, :GLA_COLS], w_l[:, GLA_COLS:GLA_COLS + MOBA_COLS]
    rwkv = w_l[:, GLA_COLS + MOBA_COLS:]
    mq, mk, mv, mg = (moba[:, i * MOBA_W:(i + 1) * MOBA_W] for i in range(4))
    w_t = jnp.concatenate([mq, mg], axis=1).T
    return tuple(a.astype(BF16) for a in (_pad_cols(gla, GLA_PAD), _pad_cols(rwkv, RWKV_PAD), mk, w_t, mv.T))


def _moba_branch(ht, k_aug, kbar, vt_aug, bias):
    return _moba_attn(ht, kbar.reshape(kbar.shape[0], -1, MOBA_W), k_aug, vt_aug, bias)


def _gla_branch(h_gla, a_up, a_bias, norm_w):
    return _gla(h_gla, a_up, a_bias.reshape(1, GLA_W), norm_w.reshape(1, HEAD_DIM))


def _rwkv_branch(h_rwkv, mu, w_up, w0, a_up, a0, k_k, k_a, r_k, gn_w, gn_b):
    row = lambda p: p.reshape(1, -1)
    return _rwkv(h_rwkv, _pad_cols(row(mu), h_rwkv.shape[2]), w_up, row(w0), a_up, row(a0), row(k_k), row(k_a),
                 row(r_k), row(gn_w), row(gn_b))


def kernel(x, w_in, w_out, gla_a_up, gla_a_bias, gla_norm_w, moba_rel_bias, rwkv_mu, rwkv_w_up, rwkv_w0,
           rwkv_a_up, rwkv_a0, rwkv_k_k, rwkv_k_a, rwkv_r_k, rwkv_gn_w, rwkv_gn_b, ln_w, ln_b):
    bsz, t, d = x.shape
    depth = w_in.shape[0]
    alpha = (2.0 * depth) ** 0.25
    bias = _bias_tiles(moba_rel_bias)
    for l in range(depth):
        h_gla, h_rwkv, ht, k_aug, kbar, vt_aug = _proj_in(x, *_split_w_in(w_in[l]))
        o_gla = _gla_branch(h_gla, gla_a_up[l], gla_a_bias[l], gla_norm_w[l])
        ot_moba = _moba_branch(ht, k_aug, kbar, vt_aug, bias)
        o_rwkv = _rwkv_branch(h_rwkv, rwkv_mu[l], rwkv_w_up[l], rwkv_w0[l], rwkv_a_up[l], rwkv_a0[l],
                              rwkv_k_k[l], rwkv_k_a[l], rwkv_r_k[l], rwkv_gn_w[l], rwkv_gn_b[l])
        x = _proj_out(o_gla, ot_moba, o_rwkv, w_out[l].astype(BF16), x,
                      ln_w[l].reshape(1, d), ln_b[l].reshape(1, d), alpha)
    return x
```

```python
import functools
import math

import jax
import jax.numpy as jnp
from jax import lax
from jax.experimental import pallas as pl
from jax.experimental.pallas import tpu as pltpu

HEAD_DIM = 64
GLA_HEADS = 4
MOBA_HEADS = 8
RWKV_HEADS = 4
GLA_W = GLA_HEADS * HEAD_DIM
MOBA_W = MOBA_HEADS * HEAD_DIM
RWKV_W = RWKV_HEADS * HEAD_DIM
GLA_LOWRANK = 16
GLA_TAU = 16.0
RWKV_LORA = 32
RWKV_GN_EPS = 64e-5
MOBA_BLOCK = 256
MOBA_TOPK = 3
N_BUCKETS = 32
MAX_DISTANCE = 4096
LN_EPS = 1e-5
GLA_COLS = 4 * GLA_W + GLA_LOWRANK
MOBA_COLS = 4 * MOBA_W
RWKV_COLS = 4 * RWKV_W + 2 * RWKV_LORA

LANE = 128
SUBLANE_BF16 = 16
GLA_PAD = -(-GLA_COLS // LANE) * LANE
RWKV_PAD = -(-RWKV_COLS // LANE) * LANE
CHUNK = 64
SUB = 16
INV_BASE = 8
GLA_TIME_BLOCK = 1024
RWKV_TIME_BLOCK = 512
ROW_BLOCK = 512
OUT_ROW_BLOCK = 1024
N_BIAS_TILES = (MAX_DISTANCE + MOBA_BLOCK - 1) // MOBA_BLOCK + 1
KV_GROUP = 4
MOBA_HEADS_PER_STEP = 4
VMEM_LIMIT = 56 * 1024 * 1024

F32 = jnp.float32
BF16 = jnp.bfloat16
NEG_INF = float("-inf")
MASK_BIG = 2.0 ** 100


def _parts(a, n):
    out, r = [], a
    for i in range(n):
        p = r.astype(BF16)
        out.append(p)
        if i + 1 < n:
            r = r - p.astype(F32)
    return out


def _mm(a, b, dims=(((1,), (0,)), ((), ())), na=1, nb=1):
    ap = [a] if a.dtype == BF16 else _parts(a, na)
    bp = [b] if b.dtype == BF16 else _parts(b, nb)
    n = max(len(ap), len(bp))
    acc = None
    for i, x in enumerate(ap):
        for j, y in enumerate(bp):
            if i + j < n:
                t = lax.dot_general(x, y, dims, preferred_element_type=F32)
                acc = t if acc is None else acc + t
    return acc


_NT = (((1,), (1,)), ((), ()))
_TN = (((0,), (0,)), ((), ()))
_BNN = (((2,), (1,)), ((0,), (0,)))
_BNT = (((2,), (2,)), ((0,), (0,)))
_BTN = (((1,), (1,)), ((0,), (0,)))


def _softplus(x):
    return jnp.maximum(x, 0.0) + jnp.log(1.0 + jnp.exp(-jnp.abs(x)))


def _sigmoid(x):
    return 1.0 / (1.0 + jnp.exp(-x))


def _iota2(shape, dim):
    return lax.broadcasted_iota(jnp.int32, shape, dim)


def _split_heads(x, nc):
    n = x.shape[1] // HEAD_DIM
    return jnp.concatenate(
        [x[:, h * HEAD_DIM:(h + 1) * HEAD_DIM].reshape(nc, CHUNK, HEAD_DIM) for h in range(n)], axis=0)


def _merge_heads(x, n):
    nc = x.shape[0] // n
    return jnp.concatenate([x[h * nc:(h + 1) * nc].reshape(nc * CHUNK, HEAD_DIM) for h in range(n)], axis=1)


def _chunk_of(x, n, c):
    return x.reshape((n, x.shape[0] // n) + x.shape[1:])[:, c]


def _proj_in_kernel(x_ref, wg_ref, wr_ref, wk_ref, wt_ref, wv_ref,
                    hg_ref, hr_ref, ht_ref, kaug_ref, kbar_ref, vt_ref):
    nb = kaug_ref.shape[3] - HEAD_DIM
    blocks = ROW_BLOCK // MOBA_BLOCK
    xb = x_ref[0].astype(BF16)
    hg_ref[0] = jnp.dot(xb, wg_ref[...], preferred_element_type=F32)
    hr_ref[0] = jnp.dot(xb, wr_ref[...], preferred_element_type=F32)
    ht_ref[0] = lax.dot_general(wt_ref[...], xb, _NT, preferred_element_type=F32)
    k = jnp.dot(xb, wk_ref[...], preferred_element_type=F32)
    first = pl.program_id(1) * blocks
    onehot = (_iota2((ROW_BLOCK, nb), 1) == first + _iota2((ROW_BLOCK, nb), 0) // MOBA_BLOCK).astype(BF16)
    for h in range(MOBA_HEADS):
        kaug_ref[0, h] = jnp.concatenate([k[:, h * HEAD_DIM:(h + 1) * HEAD_DIM].astype(BF16), onehot], axis=1)
    for i in range(blocks):
        kbar_ref[0, 0, i:i + 1, :] = jnp.sum(k[i * MOBA_BLOCK:(i + 1) * MOBA_BLOCK], axis=0,
                                             keepdims=True) * (1.0 / MOBA_BLOCK)
    vt = lax.dot_general(wv_ref[...], xb, _NT, preferred_element_type=F32)
    pad = jnp.concatenate([jnp.ones((1, MOBA_BLOCK), F32), jnp.zeros((SUBLANE_BF16 - 1, MOBA_BLOCK), F32)], axis=0)
    for h in range(MOBA_HEADS):
        for i in range(blocks):
            tile = vt[h * HEAD_DIM:(h + 1) * HEAD_DIM, i * MOBA_BLOCK:(i + 1) * MOBA_BLOCK]
            vt_ref[0, h, i] = jnp.concatenate([tile, pad], axis=0).astype(BF16)


def _proj_in(x, w_gla, w_rwkv, w_k, w_t, w_v):
    b, t, d = x.shape
    nb = t // MOBA_BLOCK
    blocks = ROW_BLOCK // MOBA_BLOCK
    full = lambda w: pl.BlockSpec(w.shape, lambda i, s: (0, 0))
    rows = lambda n: pl.BlockSpec((1, ROW_BLOCK, n), lambda i, s: (i, s, 0))
    nt = w_t.shape[0]
    vrows = HEAD_DIM + SUBLANE_BF16
    return pl.pallas_call(
        _proj_in_kernel,
        grid=(b, t // ROW_BLOCK),
        in_specs=[rows(d), full(w_gla), full(w_rwkv), full(w_k), full(w_t), full(w_v)],
        out_specs=[rows(w_gla.shape[1]), rows(w_rwkv.shape[1]),
                   pl.BlockSpec((1, nt, ROW_BLOCK), lambda i, s: (i, 0, s)),
                   pl.BlockSpec((1, MOBA_HEADS, ROW_BLOCK, HEAD_DIM + nb), lambda i, s: (i, 0, s, 0)),
                   pl.BlockSpec((1, 1, blocks, MOBA_W), lambda i, s: (i, s, 0, 0)),
                   pl.BlockSpec((1, MOBA_HEADS, blocks, vrows, MOBA_BLOCK), lambda i, s: (i, 0, s, 0, 0))],
        out_shape=[jax.ShapeDtypeStruct((b, t, w_gla.shape[1]), F32),
                   jax.ShapeDtypeStruct((b, t, w_rwkv.shape[1]), F32),
                   jax.ShapeDtypeStruct((b, nt, t), F32),
                   jax.ShapeDtypeStruct((b, MOBA_HEADS, t, HEAD_DIM + nb), BF16),
                   jax.ShapeDtypeStruct((b, nb // blocks, blocks, MOBA_W), F32),
                   jax.ShapeDtypeStruct((b, MOBA_HEADS, nb, vrows, MOBA_BLOCK), BF16)],
        compiler_params=pltpu.CompilerParams(
            dimension_semantics=("parallel", "parallel"), vmem_limit_bytes=VMEM_LIMIT),
    )(x, w_gla, w_rwkv, w_k, w_t, w_v)


def _gla_kernel(h_ref, aup_ref, ab_ref, nw_ref, o_ref, st_ref):
    @pl.when(pl.program_id(1) == 0)
    def _():
        st_ref[...] = jnp.zeros_like(st_ref)

    nc = GLA_TIME_BLOCK // CHUNK
    nh = GLA_HEADS
    shape3 = (nh * nc, CHUNK, CHUNK)
    hb = h_ref[0]
    z = _mm(hb[:, 4 * GLA_W:4 * GLA_W + GLA_LOWRANK], aup_ref[...], na=2, nb=2) + ab_ref[...]
    la = _split_heads(-_softplus(-z) * (1.0 / GLA_TAU), nc)
    row = _iota2((1, CHUNK, CHUNK), 1)
    col = _iota2((1, CHUNK, CHUNK), 2)
    tri = (row >= col)
    every = lambda mask: jnp.broadcast_to(mask.astype(BF16), shape3)
    q = _split_heads(hb[:, 0:GLA_W], nc) * (HEAD_DIM ** -0.5)
    k = _split_heads(hb[:, GLA_W:2 * GLA_W], nc)
    v = _split_heads(hb[:, 2 * GLA_W:3 * GLA_W], nc)
    cum = _mm(every(tri), la, _BNN, nb=3)
    k_anchor = _mm(every(col <= (row // SUB) * SUB + (SUB - 1)), la, _BNN, nb=3)
    k_t = k * jnp.exp(k_anchor - cum)
    scores = jnp.zeros(shape3, F32)
    for j in range(CHUNK // SUB - 1):
        a_j = cum[:, j * SUB + SUB - 1:j * SUB + SUB, :]
        q_j = q * jnp.exp(jnp.where(row >= (j + 1) * SUB, cum - a_j, NEG_INF))
        in_grp = (row >= j * SUB) & (row < (j + 1) * SUB)
        scores = scores + _mm(q_j, jnp.where(in_grp, k_t, 0.0), _BNT)
    half = SUB // 2
    while half >= 1:
        pair_end = (row // (2 * half)) * (2 * half) + (half - 1)
        anchor = _mm(every(col <= pair_end), la, _BNN)
        upper = (row // half) % 2 == 1
        decay = jnp.exp(jnp.where(upper, cum - anchor, anchor - cum))
        q_h = jnp.where(upper, q * decay, 0.0)
        k_h = jnp.where(upper, 0.0, k * decay)
        same_pair = row // (2 * half) == col // (2 * half)
        scores = scores + jnp.where(same_pair, _mm(q_h, k_h, _BNT), 0.0)
        half //= 2
    diag = _mm(q * k, every(row >= 0), _BNN)
    scores = scores + jnp.where(row == col, diag, 0.0)
    o_intra = _mm(jnp.where(tri, scores, 0.0), v, _BNN)
    last = cum[:, CHUNK - 1:CHUNK, :]
    n_mat = _mm(v, k * jnp.exp(last - cum), _BTN)
    q_dec = q * jnp.exp(cum)
    dec_last = jnp.exp(last)
    st = st_ref[...]
    o_inter = []
    for c in range(nc):
        o_inter.append(_mm(_chunk_of(q_dec, nh, c), st, _BNT))
        st = st * _chunk_of(dec_last, nh, c) + _chunk_of(n_mat, nh, c)
    st_ref[...] = st
    o = o_intra + jnp.stack(o_inter, axis=1).reshape(shape3)
    o = o * lax.rsqrt(jnp.mean(o * o, axis=-1, keepdims=True) + LN_EPS) * nw_ref[...]
    g = hb[:, 3 * GLA_W:4 * GLA_W]
    o_ref[0] = _merge_heads(o, nh) * (g * _sigmoid(g))


def _gla(h_gla, a_up, a_bias, norm_w):
    b, t, c = h_gla.shape
    vec = lambda a: pl.BlockSpec(a.shape, lambda i, s: (0, 0))
    return pl.pallas_call(
        _gla_kernel,
        grid=(b, t // GLA_TIME_BLOCK),
        in_specs=[pl.BlockSpec((1, GLA_TIME_BLOCK, c), lambda i, s: (i, s, 0)), vec(a_up), vec(a_bias),
                  vec(norm_w)],
        out_specs=pl.BlockSpec((1, GLA_TIME_BLOCK, GLA_W), lambda i, s: (i, s, 0)),
        out_shape=jax.ShapeDtypeStruct((b, t, GLA_W), F32),
        scratch_shapes=[pltpu.VMEM((GLA_HEADS, HEAD_DIM, HEAD_DIM), F32)],
        compiler_params=pltpu.CompilerParams(
            dimension_semantics=("parallel", "arbitrary"), vmem_limit_bytes=VMEM_LIMIT),
    )(h_gla, a_up, a_bias, norm_w)


def _block_penalty(q_t, kbar, own):
    gate = _mm(kbar, q_t, na=2, nb=2)
    blk = _iota2(gate.shape, 0)
    nb = gate.shape[0]
    gate = jnp.where(blk < own, gate, NEG_INF)
    allowed = blk == own
    for _ in range(min(MOBA_TOPK, nb)):
        best = jnp.max(gate, axis=0, keepdims=True)
        first = jnp.min(jnp.where(gate == best, blk, nb), axis=0, keepdims=True)
        hit = blk == first
        allowed = allowed | (hit & (blk < own))
        gate = jnp.where(hit, NEG_INF, gate)
    return jnp.where(allowed, 0.0, -MASK_BIG).astype(BF16)


def _moba_attn_kernel(qt_ref, kbar_ref, k_ref, vt_ref, gt_ref, bias_ref, o_ref, *scratch):
    hps = MOBA_HEADS_PER_STEP
    own = pl.program_id(2)
    nb = kbar_ref.shape[1]
    n_groups = (own + KV_GROUP) // KV_GROUP
    head_rows = lambda hh: slice(hh * HEAD_DIM, (hh + 1) * HEAD_DIM)
    q = [(qt_ref[0, head_rows(hh), :] * (HEAD_DIM ** -0.5)).astype(BF16) for hh in range(hps)]
    pen = [_block_penalty(qt_ref[0, head_rows(hh), :], kbar_ref[0][:, head_rows(hh)], own) for hh in range(hps)]
    blk = _iota2(pen[0].shape, 0)
    sa, sb, ma, mb = (scratch[i * hps:(i + 1) * hps] for i in range(4))

    def first_block(u):
        lo = own + 1 - (u + 1) * KV_GROUP
        return jnp.clip(lo, 0, nb - KV_GROUP), lo

    tile_rows = lambda i: slice(i * MOBA_BLOCK, (i + 1) * MOBA_BLOCK)

    def score_operand(u, hh):
        j0, lo = first_block(u)
        in_range = (blk >= lo) & (blk < lo + KV_GROUP)
        return j0, jnp.concatenate([q[hh], jnp.where(in_range, pen[hh], jnp.asarray(-MASK_BIG, BF16))], axis=0)

    def score_tile(hh, j, q_aug):
        rows = pl.ds(pl.multiple_of(j * MOBA_BLOCK, MOBA_BLOCK), MOBA_BLOCK)
        tile = jnp.clip(own - j, 0, N_BIAS_TILES)
        return _mm(k_ref[0, hh, rows, :], q_aug).astype(BF16) + bias_ref[hh, tile]

    def scores(u, hh, dst, dst_max):
        j0, q_aug = score_operand(u, hh)
        top = None
        for i in range(KV_GROUP):
            s = score_tile(hh, j0 + i, q_aug)
            dst[tile_rows(i), :] = s
            tile_top = jnp.max(s, axis=0, keepdims=True)
            top = tile_top if top is None else jnp.maximum(top, tile_top)
        dst_max[...] = top.astype(F32)

    def consume(u, hh, src, src_max, carry):
        m, acc = carry
        j0, _ = first_block(u)
        ref_b = jnp.maximum(m, src_max[...]).astype(BF16)
        m_new = ref_b.astype(F32)
        acc = acc * jnp.exp(m - m_new)
        for i in range(KV_GROUP):
            acc = acc + _mm(vt_ref[0, hh, j0 + i], jnp.exp(src[tile_rows(i), :] - ref_b))
        return m_new, acc

    def staggered(u, src, src_max, dst, dst_max, carry):
        carry = list(carry)
        for hh in range(hps):
            scores(u + 1, hh, dst[hh], dst_max[hh])
            carry[hh] = consume(u, hh, src[hh], src_max[hh], carry[hh])
        return tuple(carry)

    carry = tuple((jnp.full((1, MOBA_BLOCK), -MASK_BIG, F32), jnp.zeros((vt_ref.shape[3], MOBA_BLOCK), F32))
                  for _ in range(hps))
    for hh in range(hps):
        scores(0, hh, sa[hh], ma[hh])

    def pair(v, carry):
        u = 2 * v
        carry = staggered(u, sa, ma, sb, mb, carry)
        return staggered(u + 1, sb, mb, sa, ma, carry)

    n_pairs = (n_groups - 1) // 2
    carry = lax.fori_loop(0, n_pairs, pair, carry)
    u_last = 2 * n_pairs

    def last_two(carry):
        carry = staggered(u_last, sa, ma, sb, mb, carry)
        return tuple(consume(u_last + 1, hh, sb[hh], mb[hh], carry[hh]) for hh in range(hps))

    def last_one(carry):
        return tuple(consume(u_last, hh, sa[hh], ma[hh], carry[hh]) for hh in range(hps))

    carry = lax.cond(n_groups - u_last == 2, last_two, last_one, carry)
    for hh in range(hps):
        _, acc = carry[hh]
        g = gt_ref[0, head_rows(hh), :]
        o_ref[0, head_rows(hh), :] = acc[:HEAD_DIM] / acc[HEAD_DIM:HEAD_DIM + 1] * (g * _sigmoid(g))


def _moba_attn(ht, kbar, k_aug, vt_aug, bias):
    b, _, t = ht.shape
    h, nb = k_aug.shape[1], kbar.shape[1]
    hps = MOBA_HEADS_PER_STEP
    rows = hps * HEAD_DIM
    qs = pl.BlockSpec((1, rows, MOBA_BLOCK), lambda i, j, s: (i, j, s))
    gs = pl.BlockSpec((1, rows, MOBA_BLOCK), lambda i, j, s: (i, h // hps + j, s))
    once = pl.Buffered(1)
    heads = lambda a: pl.BlockSpec((1, hps) + a.shape[2:], lambda i, j, s: (i, j) + (0,) * (a.ndim - 2),
                                   pipeline_mode=once)
    return pl.pallas_call(
        _moba_attn_kernel,
        grid=(b, h // hps, nb),
        in_specs=[qs, pl.BlockSpec((1, nb, rows), lambda i, j, s: (i, 0, j)), heads(k_aug), heads(vt_aug), gs,
                  pl.BlockSpec((hps,) + bias.shape[1:], lambda i, j, s: (j, 0, 0, 0), pipeline_mode=once)],
        out_specs=qs,
        out_shape=jax.ShapeDtypeStruct((b, h * HEAD_DIM, t), F32),
        scratch_shapes=[pltpu.VMEM((KV_GROUP * MOBA_BLOCK, MOBA_BLOCK), BF16)] * (2 * hps)
        + [pltpu.VMEM((1, MOBA_BLOCK), F32)] * (2 * hps),
        compiler_params=pltpu.CompilerParams(
            dimension_semantics=("parallel", "parallel", "arbitrary"), vmem_limit_bytes=VMEM_LIMIT),
    )(ht, kbar, k_aug, vt_aug, ht, bias)


def _t5_bucket(rel):
    rel = jnp.maximum(rel, 0)
    max_exact = N_BUCKETS // 2
    rel_f = jnp.maximum(rel, 1).astype(F32)
    large = max_exact + (jnp.log(rel_f / max_exact) / math.log(MAX_DISTANCE / max_exact)
                         * (N_BUCKETS - max_exact)).astype(jnp.int32)
    large = jnp.minimum(large, N_BUCKETS - 1)
    return jnp.where(rel < max_exact, rel, large)


def _bias_tiles(rel_bias):
    d = jnp.arange(N_BIAS_TILES)[:, None, None] * MOBA_BLOCK
    rel = d + jnp.arange(MOBA_BLOCK)[None, None, :] - jnp.arange(MOBA_BLOCK)[None, :, None]
    bucket = _t5_bucket(rel)
    tab = rel_bias.T.astype(F32)
    out = jnp.zeros((tab.shape[0],) + bucket.shape, F32)
    for bk in range(N_BUCKETS):
        out = jnp.where(bucket[None] == bk, tab[:, bk][:, None, None, None], out)
    out = jnp.where(rel[None] < 0, -MASK_BIG, out)
    far = jnp.broadcast_to(tab[:, N_BUCKETS - 1][:, None, None, None], out[:, :1].shape)
    return jnp.concatenate([out, far], axis=1).astype(BF16)


def _rwkv_kernel(h_ref, mu_ref, wup_ref, w0_ref, aup_ref, a0_ref, kk_ref, ka_ref, rk_ref, gw_ref, gb_ref,
                 o_ref, st_ref, prev_ref):
    @pl.when(pl.program_id(1) == 0)
    def _():
        st_ref[...] = jnp.zeros_like(st_ref)
        prev_ref[...] = jnp.zeros_like(prev_ref)

    nc = RWKV_TIME_BLOCK // CHUNK
    nh = RWKV_HEADS
    w = RWKV_W
    p = h_ref[0]
    shifted = pltpu.roll(p, 1, axis=0)
    p_prev = jnp.where(_iota2(p.shape, 0) == 0, prev_ref[...], shifted)
    prev_ref[...] = p[RWKV_TIME_BLOCK - 1:RWKV_TIME_BLOCK, :]
    p = p + (p_prev - p) * mu_ref[...]
    r, k, v, g = p[:, 0:w], p[:, w:2 * w], p[:, 2 * w:3 * w], p[:, 3 * w:4 * w]
    w_dn = p[:, 4 * w:4 * w + RWKV_LORA]
    a_dn = p[:, 4 * w + RWKV_LORA:4 * w + 2 * RWKV_LORA]
    d = w0_ref[...] + _mm(jnp.tanh(w_dn), wup_ref[...], na=2, nb=2)
    lw = -jnp.exp(-_softplus(-d) - 0.5)
    a = _sigmoid(a0_ref[...] + _mm(a_dn, aup_ref[...], na=2, nb=2))
    same_head = (_iota2((w, w), 0) // HEAD_DIM == _iota2((w, w), 1) // HEAD_DIM).astype(BF16)
    kk = k * kk_ref[...]
    kk = kk / jnp.maximum(jnp.sqrt(_mm(kk * kk, same_head, na=3)), 1e-12)
    k = k * (1.0 + (a - 1.0) * ka_ref[...])
    bonus = _mm(r * k * rk_ref[...], same_head, na=3) * v

    shape3 = (nh * nc, CHUNK, CHUNK)
    row = _iota2(shape3, 1)
    col = _iota2(shape3, 2)
    tri_b = (row >= col).astype(BF16)
    eye = row == col
    hp = dict()
    carry_prec = dict(na=2, nb=2)
    r, lw, k, v = (_split_heads(x, nc) for x in (r, lw, k, v))
    a_vec = _split_heads(-kk, nc)
    b_vec = _split_heads(kk * a, nc)
    cum = _mm(tri_b, lw, _BNN, nb=3)
    last = cum[:, CHUNK - 1:CHUNK, :]
    inv = jnp.exp(-cum)
    to_end = jnp.exp(last - cum)
    r_t = r * jnp.exp(cum)
    a_t = a_vec * jnp.exp(cum - lw)
    k_t = k * inv
    b_t = b_vec * inv
    gram = _mm(jnp.concatenate([a_t, r_t], axis=1), jnp.concatenate([b_t, k_t], axis=1), _BNT, **hp)
    a_ab = jnp.where(row > col, gram[:, :CHUNK, :CHUNK], 0.0)
    a_ak = jnp.where(row > col, gram[:, :CHUNK, CHUNK:], 0.0)
    a_rb = jnp.where(row >= col, gram[:, CHUNK:, :CHUNK], 0.0)
    a_rk = jnp.where(row >= col, gram[:, CHUNK:, CHUNK:], 0.0)
    same_block = lambda size: row // size == col // size
    power = jnp.where(same_block(INV_BASE), a_ab, 0.0)
    inv_t = jnp.where(eye, 1.0, power)
    for _ in range(int(math.log2(INV_BASE)) - 1):
        power = _mm(power, power, _BNN, **hp)
        inv_t = inv_t + _mm(inv_t, power, _BNN, **hp)
    size = INV_BASE
    while size < CHUNK:
        coupling = jnp.where(same_block(2 * size) & ~same_block(size), a_ab, 0.0)
        inv_t = inv_t + _mm(inv_t, _mm(coupling, inv_t, _BNN, **hp), _BNN, **hp)
        size *= 2
    w_mat = _mm(inv_t, a_t, _BNN, **hp)
    u0 = _mm(inv_t, _mm(a_ak, v, _BNN, **hp), _BNN, **hp)
    y_w = r_t + _mm(a_rb, w_mat, _BNN, **hp)
    y0 = _mm(a_rb, u0, _BNN, **hp) + _mm(a_rk, v, _BNN, **hp)
    b_end = b_vec * to_end
    k_end = k * to_end
    m_mat = jnp.where(eye, jnp.exp(last), 0.0) + _mm(b_end, w_mat, _BTN, **hp)
    n_mat = _mm(b_end, u0, _BTN, **hp) + _mm(k_end, v, _BTN, **hp)
    st = st_ref[...]
    ys = []
    for c in range(nc):
        ys.append(_mm(_chunk_of(y_w, nh, c), st, _BNN, **hp))
        st = _mm(_chunk_of(m_mat, nh, c), st, _BNN, **carry_prec) + _chunk_of(n_mat, nh, c)
    st_ref[...] = st
    y = y0 + jnp.stack(ys, axis=1).reshape(shape3)
    mean = jnp.mean(y, axis=-1, keepdims=True)
    var = jnp.mean(jnp.square(y - mean), axis=-1, keepdims=True)
    y = _merge_heads((y - mean) * lax.rsqrt(var + RWKV_GN_EPS), nh) * gw_ref[...] + gb_ref[...]
    o_ref[0] = (y + bonus) * (g * _sigmoid(g))


def _rwkv(h_rwkv, mu, w_up, w0, a_up, a0, k_k, k_a, r_k, gn_w, gn_b):
    b, t, c = h_rwkv.shape
    full = lambda a: pl.BlockSpec(a.shape, lambda i, s: (0, 0))
    params = (mu, w_up, w0, a_up, a0, k_k, k_a, r_k, gn_w, gn_b)
    return pl.pallas_call(
        _rwkv_kernel,
        grid=(b, t // RWKV_TIME_BLOCK),
        in_specs=[pl.BlockSpec((1, RWKV_TIME_BLOCK, c), lambda i, s: (i, s, 0))] + [full(a) for a in params],
        out_specs=pl.BlockSpec((1, RWKV_TIME_BLOCK, RWKV_W), lambda i, s: (i, s, 0)),
        out_shape=jax.ShapeDtypeStruct((b, t, RWKV_W), F32),
        scratch_shapes=[pltpu.VMEM((RWKV_HEADS, HEAD_DIM, HEAD_DIM), F32), pltpu.VMEM((1, c), F32)],
        compiler_params=pltpu.CompilerParams(
            dimension_semantics=("parallel", "arbitrary"), vmem_limit_bytes=VMEM_LIMIT),
    )(h_rwkv, *params)


def _proj_out_kernel(alpha, og_ref, ot_ref, or_ref, w_ref, x_ref, lw_ref, lb_ref, y_ref):
    y = _mm(og_ref[0].astype(BF16), w_ref[0:GLA_W, :])
    y = y + _mm(ot_ref[0].astype(BF16), w_ref[GLA_W:GLA_W + MOBA_W, :], _TN)
    y = y + _mm(or_ref[0].astype(BF16), w_ref[GLA_W + MOBA_W:, :])
    z = alpha * x_ref[0] + y
    mu = jnp.mean(z, axis=-1, keepdims=True)
    var = jnp.mean(jnp.square(z - mu), axis=-1, keepdims=True)
    y_ref[0] = (z - mu) * lax.rsqrt(var + LN_EPS) * lw_ref[...] + lb_ref[...]


def _proj_out(o_gla, ot_moba, o_rwkv, w, x, ln_w, ln_b, alpha):
    b, t, d = x.shape
    rs = lambda n: pl.BlockSpec((1, OUT_ROW_BLOCK, n), lambda i, s: (i, s, 0))
    vs = pl.BlockSpec((1, d), lambda i, s: (0, 0))
    return pl.pallas_call(
        functools.partial(_proj_out_kernel, alpha),
        grid=(b, t // OUT_ROW_BLOCK),
        in_specs=[rs(GLA_W), pl.BlockSpec((1, MOBA_W, OUT_ROW_BLOCK), lambda i, s: (i, 0, s)), rs(RWKV_W),
                  pl.BlockSpec(w.shape, lambda i, s: (0, 0)), rs(d), vs, vs],
        out_specs=rs(d),
        out_shape=jax.ShapeDtypeStruct((b, t, d), F32),
        compiler_params=pltpu.CompilerParams(
            dimension_semantics=("parallel", "parallel"), vmem_limit_bytes=VMEM_LIMIT),
    )(o_gla, ot_moba, o_rwkv, w, x, ln_w, ln_b)


def _pad_cols(a, n):
    return jnp.pad(a, ((0, 0), (0, n - a.shape[1])))


def _split_w_in(w_l):
    gla, moba = w_l[:, :GLA_COLS], w_l[:, GLA_COLS:GLA_COLS + MOBA_COLS]
    rwkv = w_l[:, GLA_COLS + MOBA_COLS:]
    mq, mk, mv, mg = (moba[:, i * MOBA_W:(i + 1) * MOBA_W] for i in range(4))
    w_t = jnp.concatenate([mq, mg], axis=1).T
    return tuple(a.astype(BF16) for a in (_pad_cols(gla, GLA_PAD), _pad_cols(rwkv, RWKV_PAD), mk, w_t, mv.T))


def _moba_branch(ht, k_aug, kbar, vt_aug, bias):
    return _moba_attn(ht, kbar.reshape(kbar.shape[0], -1, MOBA_W), k_aug, vt_aug, bias)


def _gla_branch(h_gla, a_up, a_bias, norm_w):
    return _gla(h_gla, a_up, a_bias.reshape(1, GLA_W), norm_w.reshape(1, HEAD_DIM))


def _rwkv_branch(h_rwkv, mu, w_up, w0, a_up, a0, k_k, k_a, r_k, gn_w, gn_b):
    row = lambda p: p.reshape(1, -1)
    return _rwkv(h_rwkv, _pad_cols(row(mu), h_rwkv.shape[2]), w_up, row(w0), a_up, row(a0), row(k_k), row(k_a),
                 row(r_k), row(gn_w), row(gn_b))


def kernel(x, w_in, w_out, gla_a_up, gla_a_bias, gla_norm_w, moba_rel_bias, rwkv_mu, rwkv_w_up, rwkv_w0,
           rwkv_a_up, rwkv_a0, rwkv_k_k, rwkv_k_a, rwkv_r_k, rwkv_gn_w, rwkv_gn_b, ln_w, ln_b):
    bsz, t, d = x.shape
    depth = w_in.shape[0]
    alpha = (2.0 * depth) ** 0.25
    bias = _bias_tiles(moba_rel_bias)
    for l in range(depth):
        h_gla, h_rwkv, ht, k_aug, kbar, vt_aug = _proj_in(x, *_split_w_in(w_in[l]))
        o_gla = _gla_branch(h_gla, gla_a_up[l], gla_a_bias[l], gla_norm_w[l])
        ot_moba = _moba_branch(ht, k_aug, kbar, vt_aug, bias)
        o_rwkv = _rwkv_branch(h_rwkv, rwkv_mu[l], rwkv_w_up[l], rwkv_w0[l], rwkv_a_up[l], rwkv_a0[l],
                              rwkv_k_k[l], rwkv_k_a[l], rwkv_r_k[l], rwkv_gn_w[l], rwkv_gn_b[l])
        x = _proj_out(o_gla, ot_moba, o_rwkv, w_out[l].astype(BF16), x,
                      ln_w[l].reshape(1, d), ln_b[l].reshape(1, d), alpha)
    return x
```

```python
import functools
import itertools
import math

import jax
import jax.numpy as jnp
from jax import lax
from jax.experimental import pallas as pl
from jax.experimental.pallas import tpu as pltpu

HEAD_DIM = 64
GLA_HEADS = 4
MOBA_HEADS = 8
RWKV_HEADS = 4
GLA_W = GLA_HEADS * HEAD_DIM
MOBA_W = MOBA_HEADS * HEAD_DIM
RWKV_W = RWKV_HEADS * HEAD_DIM
GLA_LOWRANK = 16
GLA_TAU = 16.0
RWKV_LORA = 32
RWKV_GN_EPS = 64e-5
MOBA_BLOCK = 256
MOBA_TOPK = 3
N_BUCKETS = 32
MAX_DISTANCE = 4096
LN_EPS = 1e-5
GLA_COLS = 4 * GLA_W + GLA_LOWRANK
MOBA_COLS = 4 * MOBA_W
RWKV_COLS = 4 * RWKV_W + 2 * RWKV_LORA

LANE = 128
SUBLANE_BF16 = 16
GLA_PAD = -(-GLA_COLS // LANE) * LANE
RWKV_PAD = -(-RWKV_COLS // LANE) * LANE
CHUNK = 64
SUB = 16
INV_BASE = 8
RWKV_HEAD_GROUP = 4
GLA_TIME_BLOCK = 512
RWKV_TIME_BLOCK = GLA_TIME_BLOCK
ROW_BLOCK = 512
OUT_ROW_BLOCK = 1024
N_BIAS_TILES = (MAX_DISTANCE + MOBA_BLOCK - 1) // MOBA_BLOCK + 1
KV_GROUP = 4
MOBA_HEADS_PER_STEP = 4
VMEM_LIMIT = 56 * 1024 * 1024

F32 = jnp.float32
BF16 = jnp.bfloat16
NEG_INF = float("-inf")
MASK_BIG = 2.0 ** 100


def _parts(a, n):
    out, r = [], a
    for i in range(n):
        p = r.astype(BF16)
        out.append(p)
        if i + 1 < n:
            r = r - p.astype(F32)
    return out


def _mm(a, b, dims=(((1,), (0,)), ((), ())), na=1, nb=1):
    ap = [a] if a.dtype == BF16 else _parts(a, na)
    bp = [b] if b.dtype == BF16 else _parts(b, nb)
    n = max(len(ap), len(bp))
    acc = None
    for i, x in enumerate(ap):
        for j, y in enumerate(bp):
            if i + j < n:
                t = lax.dot_general(x, y, dims, preferred_element_type=F32)
                acc = t if acc is None else acc + t
    return acc


_NT = (((1,), (1,)), ((), ()))
_TN = (((0,), (0,)), ((), ()))
_BNN = (((2,), (1,)), ((0,), (0,)))
_BNT = (((2,), (2,)), ((0,), (0,)))
_BTN = (((1,), (1,)), ((0,), (0,)))


def _softplus(x):
    return jnp.maximum(x, 0.0) + jnp.log(1.0 + jnp.exp(-jnp.abs(x)))


def _sigmoid(x):
    return 1.0 / (1.0 + jnp.exp(-x))


def _iota2(shape, dim):
    return lax.broadcasted_iota(jnp.int32, shape, dim)


def _split_heads(x, nc):
    n = x.shape[1] // HEAD_DIM
    return jnp.concatenate(
        [x[:, h * HEAD_DIM:(h + 1) * HEAD_DIM].reshape(nc, CHUNK, HEAD_DIM) for h in range(n)], axis=0)


def _merge_heads(x, n):
    nc = x.shape[0] // n
    return jnp.concatenate([x[h * nc:(h + 1) * nc].reshape(nc * CHUNK, HEAD_DIM) for h in range(n)], axis=1)


def _chunk_of(x, n, c):
    return x.reshape((n, x.shape[0] // n) + x.shape[1:])[:, c]


def _proj_in_kernel(x_ref, wg_ref, wr_ref, wk_ref, wt_ref, wv_ref,
                    hg_ref, hr_ref, ht_ref, kaug_ref, kbar_ref, vt_ref):
    nb = kaug_ref.shape[3] - HEAD_DIM
    blocks = ROW_BLOCK // MOBA_BLOCK
    xb = x_ref[0].astype(BF16)
    hg_ref[0] = jnp.dot(xb, wg_ref[...], preferred_element_type=F32)
    hr_ref[0] = jnp.dot(xb, wr_ref[...], preferred_element_type=F32)
    ht_ref[0] = lax.dot_general(wt_ref[...], xb, _NT, preferred_element_type=F32)
    k = jnp.dot(xb, wk_ref[...], preferred_element_type=F32)
    first = pl.program_id(1) * blocks
    onehot = (_iota2((ROW_BLOCK, nb), 1) == first + _iota2((ROW_BLOCK, nb), 0) // MOBA_BLOCK).astype(BF16)
    for h in range(MOBA_HEADS):
        kaug_ref[0, h] = jnp.concatenate([k[:, h * HEAD_DIM:(h + 1) * HEAD_DIM].astype(BF16), onehot], axis=1)
    for i in range(blocks):
        kbar_ref[0, 0, i:i + 1, :] = jnp.sum(k[i * MOBA_BLOCK:(i + 1) * MOBA_BLOCK], axis=0,
                                             keepdims=True) * (1.0 / MOBA_BLOCK)
    vt = lax.dot_general(wv_ref[...], xb, _NT, preferred_element_type=F32)
    pad = jnp.concatenate([jnp.ones((1, MOBA_BLOCK), F32), jnp.zeros((SUBLANE_BF16 - 1, MOBA_BLOCK), F32)], axis=0)
    for h in range(MOBA_HEADS):
        for i in range(blocks):
            tile = vt[h * HEAD_DIM:(h + 1) * HEAD_DIM, i * MOBA_BLOCK:(i + 1) * MOBA_BLOCK]
            vt_ref[0, h, i] = jnp.concatenate([tile, pad], axis=0).astype(BF16)


def _proj_in(x, w_gla, w_rwkv, w_k, w_t, w_v):
    b, t, d = x.shape
    nb = t // MOBA_BLOCK
    blocks = ROW_BLOCK // MOBA_BLOCK
    full = lambda w: pl.BlockSpec(w.shape, lambda i, s: (0, 0))
    rows = lambda n: pl.BlockSpec((1, ROW_BLOCK, n), lambda i, s: (i, s, 0))
    nt = w_t.shape[0]
    vrows = HEAD_DIM + SUBLANE_BF16
    return pl.pallas_call(
        _proj_in_kernel,
        grid=(b, t // ROW_BLOCK),
        in_specs=[rows(d), full(w_gla), full(w_rwkv), full(w_k), full(w_t), full(w_v)],
        out_specs=[rows(w_gla.shape[1]), rows(w_rwkv.shape[1]),
                   pl.BlockSpec((1, nt, ROW_BLOCK), lambda i, s: (i, 0, s)),
                   pl.BlockSpec((1, MOBA_HEADS, ROW_BLOCK, HEAD_DIM + nb), lambda i, s: (i, 0, s, 0)),
                   pl.BlockSpec((1, 1, blocks, MOBA_W), lambda i, s: (i, s, 0, 0)),
                   pl.BlockSpec((1, MOBA_HEADS, blocks, vrows, MOBA_BLOCK), lambda i, s: (i, 0, s, 0, 0))],
        out_shape=[jax.ShapeDtypeStruct((b, t, w_gla.shape[1]), F32),
                   jax.ShapeDtypeStruct((b, t, w_rwkv.shape[1]), F32),
                   jax.ShapeDtypeStruct((b, nt, t), F32),
                   jax.ShapeDtypeStruct((b, MOBA_HEADS, t, HEAD_DIM + nb), BF16),
                   jax.ShapeDtypeStruct((b, nb // blocks, blocks, MOBA_W), F32),
                   jax.ShapeDtypeStruct((b, MOBA_HEADS, nb, vrows, MOBA_BLOCK), BF16)],
        compiler_params=pltpu.CompilerParams(
            dimension_semantics=("parallel", "parallel"), vmem_limit_bytes=VMEM_LIMIT),
    )(x, w_gla, w_rwkv, w_k, w_t, w_v)


def _gla_steps(h_ref, aup_ref, ab_ref, nw_ref, o_ref, st_ref):
    @pl.when(pl.program_id(1) == 0)
    def _():
        st_ref[...] = jnp.zeros_like(st_ref)

    nc = GLA_TIME_BLOCK // CHUNK
    nh = GLA_HEADS
    shape3 = (nh * nc, CHUNK, CHUNK)
    hb = h_ref[0]
    z = _mm(hb[:, 4 * GLA_W:4 * GLA_W + GLA_LOWRANK], aup_ref[...], na=2, nb=2) + ab_ref[...]
    la = _split_heads(-_softplus(-z) * (1.0 / GLA_TAU), nc)
    yield
    row = _iota2((1, CHUNK, CHUNK), 1)
    col = _iota2((1, CHUNK, CHUNK), 2)
    tri = (row >= col)
    every = lambda mask: jnp.broadcast_to(mask.astype(BF16), shape3)
    q = _split_heads(hb[:, 0:GLA_W], nc) * (HEAD_DIM ** -0.5)
    k = _split_heads(hb[:, GLA_W:2 * GLA_W], nc)
    v = _split_heads(hb[:, 2 * GLA_W:3 * GLA_W], nc)
    cum = _mm(every(tri), la, _BNN, nb=3)
    k_anchor = _mm(every(col <= (row // SUB) * SUB + (SUB - 1)), la, _BNN, nb=3)
    k_t = k * jnp.exp(k_anchor - cum)
    yield
    scores = jnp.zeros(shape3, F32)
    for j in range(CHUNK // SUB - 1):
        a_j = cum[:, j * SUB + SUB - 1:j * SUB + SUB, :]
        q_j = q * jnp.exp(jnp.where(row >= (j + 1) * SUB, cum - a_j, NEG_INF))
        in_grp = (row >= j * SUB) & (row < (j + 1) * SUB)
        scores = scores + _mm(q_j, jnp.where(in_grp, k_t, 0.0), _BNT)
        yield
    half = SUB // 2
    while half >= 1:
        pair_end = (row // (2 * half)) * (2 * half) + (half - 1)
        anchor = _mm(every(col <= pair_end), la, _BNN)
        upper = (row // half) % 2 == 1
        decay = jnp.exp(jnp.where(upper, cum - anchor, anchor - cum))
        q_h = jnp.where(upper, q * decay, 0.0)
        k_h = jnp.where(upper, 0.0, k * decay)
        same_pair = row // (2 * half) == col // (2 * half)
        scores = scores + jnp.where(same_pair, _mm(q_h, k_h, _BNT), 0.0)
        half //= 2
        yield
    diag = _mm(q * k, every(row >= 0), _BNN)
    scores = scores + jnp.where(row == col, diag, 0.0)
    o_intra = _mm(jnp.where(tri, scores, 0.0), v, _BNN)
    yield
    last = cum[:, CHUNK - 1:CHUNK, :]
    n_mat = _mm(v, k * jnp.exp(last - cum), _BTN)
    q_dec = q * jnp.exp(cum)
    dec_last = jnp.exp(last)
    yield
    st = st_ref[...]
    o_inter = []
    for c in range(nc):
        o_inter.append(_mm(_chunk_of(q_dec, nh, c), st, _BNT))
        st = st * _chunk_of(dec_last, nh, c) + _chunk_of(n_mat, nh, c)
    st_ref[...] = st
    o = o_intra + jnp.stack(o_inter, axis=1).reshape(shape3)
    o = o * lax.rsqrt(jnp.mean(o * o, axis=-1, keepdims=True) + LN_EPS) * nw_ref[...]
    g = hb[:, 3 * GLA_W:4 * GLA_W]
    yield
    o_ref[0] = _merge_heads(o, nh) * (g * _sigmoid(g))


def _block_penalty(q_t, kbar, own):
    gate = _mm(kbar, q_t, na=2, nb=2)
    blk = _iota2(gate.shape, 0)
    nb = gate.shape[0]
    gate = jnp.where(blk < own, gate, NEG_INF)
    allowed = blk == own
    for _ in range(min(MOBA_TOPK, nb)):
        best = jnp.max(gate, axis=0, keepdims=True)
        first = jnp.min(jnp.where(gate == best, blk, nb), axis=0, keepdims=True)
        hit = blk == first
        allowed = allowed | (hit & (blk < own))
        gate = jnp.where(hit, NEG_INF, gate)
    return jnp.where(allowed, 0.0, -MASK_BIG).astype(BF16)


def _moba_attn_kernel(qt_ref, kbar_ref, k_ref, vt_ref, gt_ref, bias_ref, o_ref, *scratch):
    hps = MOBA_HEADS_PER_STEP
    own = pl.program_id(2)
    nb = kbar_ref.shape[1]
    n_groups = (own + KV_GROUP) // KV_GROUP
    head_rows = lambda hh: slice(hh * HEAD_DIM, (hh + 1) * HEAD_DIM)
    q = [(qt_ref[0, head_rows(hh), :] * (HEAD_DIM ** -0.5)).astype(BF16) for hh in range(hps)]
    pen = [_block_penalty(qt_ref[0, head_rows(hh), :], kbar_ref[0][:, head_rows(hh)], own) for hh in range(hps)]
    blk = _iota2(pen[0].shape, 0)
    sa, sb, ma, mb = (scratch[i * hps:(i + 1) * hps] for i in range(4))

    def first_block(u):
        lo = own + 1 - (u + 1) * KV_GROUP
        return jnp.clip(lo, 0, nb - KV_GROUP), lo

    tile_rows = lambda i: slice(i * MOBA_BLOCK, (i + 1) * MOBA_BLOCK)

    def score_operand(u, hh):
        j0, lo = first_block(u)
        in_range = (blk >= lo) & (blk < lo + KV_GROUP)
        return j0, jnp.concatenate([q[hh], jnp.where(in_range, pen[hh], jnp.asarray(-MASK_BIG, BF16))], axis=0)

    def score_tile(hh, j, q_aug):
        rows = pl.ds(pl.multiple_of(j * MOBA_BLOCK, MOBA_BLOCK), MOBA_BLOCK)
        tile = jnp.clip(own - j, 0, N_BIAS_TILES)
        return _mm(k_ref[0, hh, rows, :], q_aug).astype(BF16) + bias_ref[hh, tile]

    def scores(u, hh, dst, dst_max):
        j0, q_aug = score_operand(u, hh)
        top = None
        for i in range(KV_GROUP):
            s = score_tile(hh, j0 + i, q_aug)
            dst[tile_rows(i), :] = s
            tile_top = jnp.max(s, axis=0, keepdims=True)
            top = tile_top if top is None else jnp.maximum(top, tile_top)
        dst_max[...] = top.astype(F32)

    def consume(u, hh, src, src_max, carry):
        m, acc = carry
        j0, _ = first_block(u)
        ref_b = jnp.maximum(m, src_max[...]).astype(BF16)
        m_new = ref_b.astype(F32)
        acc = acc * jnp.exp(m - m_new)
        for i in range(KV_GROUP):
            acc = acc + _mm(vt_ref[0, hh, j0 + i], jnp.exp(src[tile_rows(i), :] - ref_b))
        return m_new, acc

    def staggered(u, src, src_max, dst, dst_max, carry):
        carry = list(carry)
        for hh in range(hps):
            scores(u + 1, hh, dst[hh], dst_max[hh])
            carry[hh] = consume(u, hh, src[hh], src_max[hh], carry[hh])
        return tuple(carry)

    carry = tuple((jnp.full((1, MOBA_BLOCK), -MASK_BIG, F32), jnp.zeros((vt_ref.shape[3], MOBA_BLOCK), F32))
                  for _ in range(hps))
    for hh in range(hps):
        scores(0, hh, sa[hh], ma[hh])

    def pair(v, carry):
        u = 2 * v
        carry = staggered(u, sa, ma, sb, mb, carry)
        return staggered(u + 1, sb, mb, sa, ma, carry)

    n_pairs = (n_groups - 1) // 2
    carry = lax.fori_loop(0, n_pairs, pair, carry)
    u_last = 2 * n_pairs

    def last_two(carry):
        carry = staggered(u_last, sa, ma, sb, mb, carry)
        return tuple(consume(u_last + 1, hh, sb[hh], mb[hh], carry[hh]) for hh in range(hps))

    def last_one(carry):
        return tuple(consume(u_last, hh, sa[hh], ma[hh], carry[hh]) for hh in range(hps))

    carry = lax.cond(n_groups - u_last == 2, last_two, last_one, carry)
    for hh in range(hps):
        _, acc = carry[hh]
        g = gt_ref[0, head_rows(hh), :]
        o_ref[0, head_rows(hh), :] = acc[:HEAD_DIM] / acc[HEAD_DIM:HEAD_DIM + 1] * (g * _sigmoid(g))


def _moba_attn(ht, kbar, k_aug, vt_aug, bias):
    b, _, t = ht.shape
    h, nb = k_aug.shape[1], kbar.shape[1]
    hps = MOBA_HEADS_PER_STEP
    rows = hps * HEAD_DIM
    qs = pl.BlockSpec((1, rows, MOBA_BLOCK), lambda i, j, s: (i, j, s))
    gs = pl.BlockSpec((1, rows, MOBA_BLOCK), lambda i, j, s: (i, h // hps + j, s))
    once = pl.Buffered(1)
    heads = lambda a: pl.BlockSpec((1, hps) + a.shape[2:], lambda i, j, s: (i, j) + (0,) * (a.ndim - 2),
                                   pipeline_mode=once)
    return pl.pallas_call(
        _moba_attn_kernel,
        grid=(b, h // hps, nb),
        in_specs=[qs, pl.BlockSpec((1, nb, rows), lambda i, j, s: (i, 0, j)), heads(k_aug), heads(vt_aug), gs,
                  pl.BlockSpec((hps,) + bias.shape[1:], lambda i, j, s: (j, 0, 0, 0), pipeline_mode=once)],
        out_specs=qs,
        out_shape=jax.ShapeDtypeStruct((b, h * HEAD_DIM, t), F32),
        scratch_shapes=[pltpu.VMEM((KV_GROUP * MOBA_BLOCK, MOBA_BLOCK), BF16)] * (2 * hps)
        + [pltpu.VMEM((1, MOBA_BLOCK), F32)] * (2 * hps),
        compiler_params=pltpu.CompilerParams(
            dimension_semantics=("parallel", "parallel", "arbitrary"), vmem_limit_bytes=VMEM_LIMIT),
    )(ht, kbar, k_aug, vt_aug, ht, bias)


def _t5_bucket(rel):
    rel = jnp.maximum(rel, 0)
    max_exact = N_BUCKETS // 2
    rel_f = jnp.maximum(rel, 1).astype(F32)
    large = max_exact + (jnp.log(rel_f / max_exact) / math.log(MAX_DISTANCE / max_exact)
                         * (N_BUCKETS - max_exact)).astype(jnp.int32)
    large = jnp.minimum(large, N_BUCKETS - 1)
    return jnp.where(rel < max_exact, rel, large)


def _bias_tiles(rel_bias):
    d = jnp.arange(N_BIAS_TILES)[:, None, None] * MOBA_BLOCK
    rel = d + jnp.arange(MOBA_BLOCK)[None, None, :] - jnp.arange(MOBA_BLOCK)[None, :, None]
    bucket = _t5_bucket(rel)
    tab = rel_bias.T.astype(F32)
    out = jnp.zeros((tab.shape[0],) + bucket.shape, F32)
    for bk in range(N_BUCKETS):
        out = jnp.where(bucket[None] == bk, tab[:, bk][:, None, None, None], out)
    out = jnp.where(rel[None] < 0, -MASK_BIG, out)
    far = jnp.broadcast_to(tab[:, N_BUCKETS - 1][:, None, None, None], out[:, :1].shape)
    return jnp.concatenate([out, far], axis=1).astype(BF16)


def _rwkv_steps(h_ref, mu_ref, wup_ref, w0_ref, aup_ref, a0_ref, kk_ref, ka_ref, rk_ref, gw_ref, gb_ref,
                o_ref, st_ref, prev_ref):
    @pl.when(pl.program_id(1) == 0)
    def _():
        st_ref[...] = jnp.zeros_like(st_ref)
        prev_ref[...] = jnp.zeros_like(prev_ref)

    nc = RWKV_TIME_BLOCK // CHUNK
    nh = RWKV_HEADS
    w = RWKV_W
    p = h_ref[0]
    shifted = pltpu.roll(p, 1, axis=0)
    p_prev = jnp.where(_iota2(p.shape, 0) == 0, prev_ref[...], shifted)
    prev_ref[...] = p[RWKV_TIME_BLOCK - 1:RWKV_TIME_BLOCK, :]
    p = p + (p_prev - p) * mu_ref[...]
    r, k, v, g = p[:, 0:w], p[:, w:2 * w], p[:, 2 * w:3 * w], p[:, 3 * w:4 * w]
    w_dn = p[:, 4 * w:4 * w + RWKV_LORA]
    a_dn = p[:, 4 * w + RWKV_LORA:4 * w + 2 * RWKV_LORA]
    d = w0_ref[...] + _mm(jnp.tanh(w_dn), wup_ref[...], na=2, nb=2)
    lw = -jnp.exp(-_softplus(-d) - 0.5)
    a = _sigmoid(a0_ref[...] + _mm(a_dn, aup_ref[...], na=2, nb=2))
    same_head = (_iota2((w, w), 0) // HEAD_DIM == _iota2((w, w), 1) // HEAD_DIM).astype(BF16)
    kk = k * kk_ref[...]
    kk = kk / jnp.maximum(jnp.sqrt(_mm(kk * kk, same_head, na=3)), 1e-12)
    k = k * (1.0 + (a - 1.0) * ka_ref[...])
    bonus = _mm(r * k * rk_ref[...], same_head, na=3) * v
    yield

    hg = RWKV_HEAD_GROUP

    def scan_group(first, r, lw, k, v, kk, a):
        shape3 = (hg * nc, CHUNK, CHUNK)
        row = _iota2(shape3, 1)
        col = _iota2(shape3, 2)
        tri_b = (row >= col).astype(BF16)
        eye = row == col
        hp = dict()
        carry_prec = dict(na=2, nb=2)
        r, lw, k, v = (_split_heads(x, nc) for x in (r, lw, k, v))
        a_vec = _split_heads(-kk, nc)
        b_vec = _split_heads(kk * a, nc)
        cum = _mm(tri_b, lw, _BNN, nb=3)
        last = cum[:, CHUNK - 1:CHUNK, :]
        inv = jnp.exp(-cum)
        to_end = jnp.exp(last - cum)
        r_t = r * jnp.exp(cum)
        a_t = a_vec * jnp.exp(cum - lw)
        k_t = k * inv
        b_t = b_vec * inv
        yield
        gram = _mm(jnp.concatenate([a_t, r_t], axis=1), jnp.concatenate([b_t, k_t], axis=1), _BNT, **hp)
        a_ab = jnp.where(row > col, gram[:, :CHUNK, :CHUNK], 0.0)
        a_ak = jnp.where(row > col, gram[:, :CHUNK, CHUNK:], 0.0)
        a_rb = jnp.where(row >= col, gram[:, CHUNK:, :CHUNK], 0.0)
        a_rk = jnp.where(row >= col, gram[:, CHUNK:, CHUNK:], 0.0)
        yield
        same_block = lambda size: row // size == col // size
        power = jnp.where(same_block(INV_BASE), a_ab, 0.0)
        inv_t = jnp.where(eye, 1.0, power)
        for _ in range(int(math.log2(INV_BASE)) - 1):
            power = _mm(power, power, _BNN, **hp)
            inv_t = inv_t + _mm(inv_t, power, _BNN, **hp)
            yield
        size = INV_BASE
        while size < CHUNK:
            coupling = jnp.where(same_block(2 * size) & ~same_block(size), a_ab, 0.0)
            inv_t = inv_t + _mm(inv_t, _mm(coupling, inv_t, _BNN, **hp), _BNN, **hp)
            size *= 2
            yield
        w_mat = _mm(inv_t, a_t, _BNN, **hp)
        u0 = _mm(inv_t, _mm(a_ak, v, _BNN, **hp), _BNN, **hp)
        yield
        y_w = r_t + _mm(a_rb, w_mat, _BNN, **hp)
        y0 = _mm(a_rb, u0, _BNN, **hp) + _mm(a_rk, v, _BNN, **hp)
        yield
        b_end = b_vec * to_end
        k_end = k * to_end
        m_mat = jnp.where(eye, jnp.exp(last), 0.0) + _mm(b_end, w_mat, _BTN, **hp)
        n_mat = _mm(b_end, u0, _BTN, **hp) + _mm(k_end, v, _BTN, **hp)
        yield
        st = st_ref[first:first + hg]
        ys = []
        for c in range(nc):
            ys.append(_mm(_chunk_of(y_w, hg, c), st, _BNN, **hp))
            st = _mm(_chunk_of(m_mat, hg, c), st, _BNN, **carry_prec) + _chunk_of(n_mat, hg, c)
        st_ref[first:first + hg] = st
        y = y0 + jnp.stack(ys, axis=1).reshape(shape3)
        mean = jnp.mean(y, axis=-1, keepdims=True)
        var = jnp.mean(jnp.square(y - mean), axis=-1, keepdims=True)
        return _merge_heads((y - mean) * lax.rsqrt(var + RWKV_GN_EPS), hg)

    cols = lambda x, f: x[:, f * HEAD_DIM:(f + hg) * HEAD_DIM]
    parts = []
    for f in range(0, nh, hg):
        parts.append((yield from scan_group(f, *(cols(x, f) for x in (r, lw, k, v, kk, a)))))
    y = jnp.concatenate(parts, axis=1) * gw_ref[...] + gb_ref[...]
    o_ref[0] = (y + bonus) * (g * _sigmoid(g))


def _gla_rwkv_kernel(hg_ref, gaup_ref, gab_ref, gnw_ref, hr_ref, *rest):
    rwkv_params, (og_ref, or_ref, gst_ref, rst_ref, prev_ref) = rest[:-5], rest[-5:]
    gla = _gla_steps(hg_ref, gaup_ref, gab_ref, gnw_ref, og_ref, gst_ref)
    rwkv = _rwkv_steps(hr_ref, *rwkv_params, or_ref, rst_ref, prev_ref)
    for _ in itertools.zip_longest(gla, rwkv):
        pass


def _gla_rwkv(h_gla, gla_params, h_rwkv, rwkv_params):
    b, t, cg = h_gla.shape
    cr = h_rwkv.shape[2]
    tb = GLA_TIME_BLOCK
    full = lambda a: pl.BlockSpec(a.shape, lambda i, s: (0, 0))
    rows = lambda n: pl.BlockSpec((1, tb, n), lambda i, s: (i, s, 0))
    return pl.pallas_call(
        _gla_rwkv_kernel,
        grid=(b, t // tb),
        in_specs=[rows(cg)] + [full(a) for a in gla_params] + [rows(cr)] + [full(a) for a in rwkv_params],
        out_specs=[rows(GLA_W), rows(RWKV_W)],
        out_shape=[jax.ShapeDtypeStruct((b, t, GLA_W), F32), jax.ShapeDtypeStruct((b, t, RWKV_W), F32)],
        scratch_shapes=[pltpu.VMEM((GLA_HEADS, HEAD_DIM, HEAD_DIM), F32),
                        pltpu.VMEM((RWKV_HEADS, HEAD_DIM, HEAD_DIM), F32), pltpu.VMEM((1, cr), F32)],
        compiler_params=pltpu.CompilerParams(
            dimension_semantics=("parallel", "arbitrary"), vmem_limit_bytes=VMEM_LIMIT),
    )(h_gla, *gla_params, h_rwkv, *rwkv_params)


def _proj_out_kernel(alpha, og_ref, ot_ref, or_ref, w_ref, x_ref, lw_ref, lb_ref, y_ref):
    y = _mm(og_ref[0].astype(BF16), w_ref[0:GLA_W, :])
    y = y + _mm(ot_ref[0].astype(BF16), w_ref[GLA_W:GLA_W + MOBA_W, :], _TN)
    y = y + _mm(or_ref[0].astype(BF16), w_ref[GLA_W + MOBA_W:, :])
    z = alpha * x_ref[0] + y
    mu = jnp.mean(z, axis=-1, keepdims=True)
    var = jnp.mean(jnp.square(z - mu), axis=-1, keepdims=True)
    y_ref[0] = (z - mu) * lax.rsqrt(var + LN_EPS) * lw_ref[...] + lb_ref[...]


def _proj_out(o_gla, ot_moba, o_rwkv, w, x, ln_w, ln_b, alpha):
    b, t, d = x.shape
    rs = lambda n: pl.BlockSpec((1, OUT_ROW_BLOCK, n), lambda i, s: (i, s, 0))
    vs = pl.BlockSpec((1, d), lambda i, s: (0, 0))
    return pl.pallas_call(
        functools.partial(_proj_out_kernel, alpha),
        grid=(b, t // OUT_ROW_BLOCK),
        in_specs=[rs(GLA_W), pl.BlockSpec((1, MOBA_W, OUT_ROW_BLOCK), lambda i, s: (i, 0, s)), rs(RWKV_W),
                  pl.BlockSpec(w.shape, lambda i, s: (0, 0)), rs(d), vs, vs],
        out_specs=rs(d),
        out_shape=jax.ShapeDtypeStruct((b, t, d), F32),
        compiler_params=pltpu.CompilerParams(
            dimension_semantics=("parallel", "parallel"), vmem_limit_bytes=VMEM_LIMIT),
    )(o_gla, ot_moba, o_rwkv, w, x, ln_w, ln_b)


def _pad_cols(a, n):
    return jnp.pad(a, ((0, 0), (0, n - a.shape[1])))


def _split_w_in(w_l):
    gla, moba = w_l[:, :GLA_COLS], w_l[:, GLA_COLS:GLA_COLS + MOBA_COLS]
    rwkv = w_l[:, GLA_COLS + MOBA_COLS:]
    mq, mk, mv, mg = (moba[:, i * MOBA_W:(i + 1) * MOBA_W] for i in range(4))
    w_t = jnp.concatenate([mq, mg], axis=1).T
    return tuple(a.astype(BF16) for a in (_pad_cols(gla, GLA_PAD), _pad_cols(rwkv, RWKV_PAD), mk, w_t, mv.T))


def _moba_branch(ht, k_aug, kbar, vt_aug, bias):
    return _moba_attn(ht, kbar.reshape(kbar.shape[0], -1, MOBA_W), k_aug, vt_aug, bias)


def _gla_rwkv_branch(h_gla, a_up, a_bias, norm_w, h_rwkv, mu, w_up, w0, r_a_up, a0, k_k, k_a, r_k, gn_w, gn_b):
    row = lambda p: p.reshape(1, -1)
    gla_params = (a_up, a_bias.reshape(1, GLA_W), norm_w.reshape(1, HEAD_DIM))
    rwkv_params = (_pad_cols(row(mu), h_rwkv.shape[2]), w_up, row(w0), r_a_up, row(a0), row(k_k), row(k_a), row(r_k),
                   row(gn_w), row(gn_b))
    return _gla_rwkv(h_gla, gla_params, h_rwkv, rwkv_params)


def kernel(x, w_in, w_out, gla_a_up, gla_a_bias, gla_norm_w, moba_rel_bias, rwkv_mu, rwkv_w_up, rwkv_w0,
           rwkv_a_up, rwkv_a0, rwkv_k_k, rwkv_k_a, rwkv_r_k, rwkv_gn_w, rwkv_gn_b, ln_w, ln_b):
    bsz, t, d = x.shape
    depth = w_in.shape[0]
    alpha = (2.0 * depth) ** 0.25
    bias = _bias_tiles(moba_rel_bias)
    for l in range(depth):
        h_gla, h_rwkv, ht, k_aug, kbar, vt_aug = _proj_in(x, *_split_w_in(w_in[l]))
        ot_moba = _moba_branch(ht, k_aug, kbar, vt_aug, bias)
        o_gla, o_rwkv = _gla_rwkv_branch(
            h_gla, gla_a_up[l], gla_a_bias[l], gla_norm_w[l], h_rwkv, rwkv_mu[l], rwkv_w_up[l], rwkv_w0[l],
            rwkv_a_up[l], rwkv_a0[l], rwkv_k_k[l], rwkv_k_a[l], rwkv_r_k[l], rwkv_gn_w[l], rwkv_gn_b[l])
        x = _proj_out(o_gla, ot_moba, o_rwkv, w_out[l].astype(BF16), x,
                      ln_w[l].reshape(1, d), ln_b[l].reshape(1, d), alpha)
    return x
```

```python
import functools
import itertools
import math

import jax
import jax.numpy as jnp
from jax import lax
from jax.experimental import pallas as pl
from jax.experimental.pallas import tpu as pltpu

HEAD_DIM = 64
GLA_HEADS = 4
MOBA_HEADS = 8
RWKV_HEADS = 4
GLA_W = GLA_HEADS * HEAD_DIM
MOBA_W = MOBA_HEADS * HEAD_DIM
RWKV_W = RWKV_HEADS * HEAD_DIM
GLA_LOWRANK = 16
GLA_TAU = 16.0
RWKV_LORA = 32
RWKV_GN_EPS = 64e-5
MOBA_BLOCK = 256
MOBA_TOPK = 3
N_BUCKETS = 32
MAX_DISTANCE = 4096
LN_EPS = 1e-5
GLA_COLS = 4 * GLA_W + GLA_LOWRANK
MOBA_COLS = 4 * MOBA_W
RWKV_COLS = 4 * RWKV_W + 2 * RWKV_LORA

LANE = 128
SUBLANE_BF16 = 16
GLA_PAD = -(-GLA_COLS // LANE) * LANE
RWKV_PAD = -(-RWKV_COLS // LANE) * LANE
CHUNK = 64
SUB = 64
INV_BASE = 8
RWKV_HEAD_GROUP = 4
GLA_TIME_BLOCK = 512
RWKV_TIME_BLOCK = GLA_TIME_BLOCK
ROW_BLOCK = 512
OUT_ROW_BLOCK = 1024
N_BIAS_TILES = (MAX_DISTANCE + MOBA_BLOCK - 1) // MOBA_BLOCK + 1
KV_GROUP = 4
MOBA_HEADS_PER_STEP = 4
VMEM_LIMIT = 56 * 1024 * 1024

F32 = jnp.float32
BF16 = jnp.bfloat16
NEG_INF = float("-inf")
MASK_BIG = 2.0 ** 100


def _parts(a, n):
    out, r = [], a
    for i in range(n):
        p = r.astype(BF16)
        out.append(p)
        if i + 1 < n:
            r = r - p.astype(F32)
    return out


def _mm(a, b, dims=(((1,), (0,)), ((), ())), na=1, nb=1):
    ap = [a] if a.dtype == BF16 else _parts(a, na)
    bp = [b] if b.dtype == BF16 else _parts(b, nb)
    n = max(len(ap), len(bp))
    acc = None
    for i, x in enumerate(ap):
        for j, y in enumerate(bp):
            if i + j < n:
                t = lax.dot_general(x, y, dims, preferred_element_type=F32)
                acc = t if acc is None else acc + t
    return acc


_NT = (((1,), (1,)), ((), ()))
_TN = (((0,), (0,)), ((), ()))
_BNN = (((2,), (1,)), ((0,), (0,)))
_BNT = (((2,), (2,)), ((0,), (0,)))
_BTN = (((1,), (1,)), ((0,), (0,)))


def _softplus(x):
    return jnp.maximum(x, 0.0) + jnp.log(1.0 + jnp.exp(-jnp.abs(x)))


def _sigmoid(x):
    return 1.0 / (1.0 + jnp.exp(-x))


def _iota2(shape, dim):
    return lax.broadcasted_iota(jnp.int32, shape, dim)


def _split_heads(x, nc):
    n = x.shape[1] // HEAD_DIM
    return jnp.concatenate(
        [x[:, h * HEAD_DIM:(h + 1) * HEAD_DIM].reshape(nc, CHUNK, HEAD_DIM) for h in range(n)], axis=0)


def _merge_heads(x, n):
    nc = x.shape[0] // n
    return jnp.concatenate([x[h * nc:(h + 1) * nc].reshape(nc * CHUNK, HEAD_DIM) for h in range(n)], axis=1)


def _chunk_of(x, n, c):
    return x.reshape((n, x.shape[0] // n) + x.shape[1:])[:, c]


def _proj_in_kernel(x_ref, wg_ref, wr_ref, wk_ref, wt_ref, wv_ref,
                    hg_ref, hr_ref, ht_ref, kaug_ref, kbar_ref, vt_ref):
    nb = kaug_ref.shape[3] - HEAD_DIM
    blocks = ROW_BLOCK // MOBA_BLOCK
    xb = x_ref[0].astype(BF16)
    hg_ref[0] = jnp.dot(xb, wg_ref[...], preferred_element_type=F32)
    hr_ref[0] = jnp.dot(xb, wr_ref[...], preferred_element_type=F32)
    ht_ref[0] = lax.dot_general(wt_ref[...], xb, _NT, preferred_element_type=F32)
    k = jnp.dot(xb, wk_ref[...], preferred_element_type=F32)
    first = pl.program_id(1) * blocks
    onehot = (_iota2((ROW_BLOCK, nb), 1) == first + _iota2((ROW_BLOCK, nb), 0) // MOBA_BLOCK).astype(BF16)
    for h in range(MOBA_HEADS):
        kaug_ref[0, h] = jnp.concatenate([k[:, h * HEAD_DIM:(h + 1) * HEAD_DIM].astype(BF16), onehot], axis=1)
    for i in range(blocks):
        kbar_ref[0, 0, i:i + 1, :] = jnp.sum(k[i * MOBA_BLOCK:(i + 1) * MOBA_BLOCK], axis=0,
                                             keepdims=True) * (1.0 / MOBA_BLOCK)
    vt = lax.dot_general(wv_ref[...], xb, _NT, preferred_element_type=F32)
    pad = jnp.concatenate([jnp.ones((1, MOBA_BLOCK), F32), jnp.zeros((SUBLANE_BF16 - 1, MOBA_BLOCK), F32)], axis=0)
    for h in range(MOBA_HEADS):
        for i in range(blocks):
            tile = vt[h * HEAD_DIM:(h + 1) * HEAD_DIM, i * MOBA_BLOCK:(i + 1) * MOBA_BLOCK]
            vt_ref[0, h, i] = jnp.concatenate([tile, pad], axis=0).astype(BF16)


def _proj_in(x, w_gla, w_rwkv, w_k, w_t, w_v):
    b, t, d = x.shape
    nb = t // MOBA_BLOCK
    blocks = ROW_BLOCK // MOBA_BLOCK
    full = lambda w: pl.BlockSpec(w.shape, lambda i, s: (0, 0))
    rows = lambda n: pl.BlockSpec((1, ROW_BLOCK, n), lambda i, s: (i, s, 0))
    nt = w_t.shape[0]
    vrows = HEAD_DIM + SUBLANE_BF16
    return pl.pallas_call(
        _proj_in_kernel,
        grid=(b, t // ROW_BLOCK),
        in_specs=[rows(d), full(w_gla), full(w_rwkv), full(w_k), full(w_t), full(w_v)],
        out_specs=[rows(w_gla.shape[1]), rows(w_rwkv.shape[1]),
                   pl.BlockSpec((1, nt, ROW_BLOCK), lambda i, s: (i, 0, s)),
                   pl.BlockSpec((1, MOBA_HEADS, ROW_BLOCK, HEAD_DIM + nb), lambda i, s: (i, 0, s, 0)),
                   pl.BlockSpec((1, 1, blocks, MOBA_W), lambda i, s: (i, s, 0, 0)),
                   pl.BlockSpec((1, MOBA_HEADS, blocks, vrows, MOBA_BLOCK), lambda i, s: (i, 0, s, 0, 0))],
        out_shape=[jax.ShapeDtypeStruct((b, t, w_gla.shape[1]), F32),
                   jax.ShapeDtypeStruct((b, t, w_rwkv.shape[1]), F32),
                   jax.ShapeDtypeStruct((b, nt, t), F32),
                   jax.ShapeDtypeStruct((b, MOBA_HEADS, t, HEAD_DIM + nb), BF16),
                   jax.ShapeDtypeStruct((b, nb // blocks, blocks, MOBA_W), F32),
                   jax.ShapeDtypeStruct((b, MOBA_HEADS, nb, vrows, MOBA_BLOCK), BF16)],
        compiler_params=pltpu.CompilerParams(
            dimension_semantics=("parallel", "parallel"), vmem_limit_bytes=VMEM_LIMIT),
    )(x, w_gla, w_rwkv, w_k, w_t, w_v)


def _gla_steps(h_ref, aup_ref, ab_ref, nw_ref, o_ref, st_ref):
    @pl.when(pl.program_id(1) == 0)
    def _():
        st_ref[...] = jnp.zeros_like(st_ref)

    nc = GLA_TIME_BLOCK // CHUNK
    nh = GLA_HEADS
    shape3 = (nh * nc, CHUNK, CHUNK)
    hb = h_ref[0]
    z = _mm(hb[:, 4 * GLA_W:4 * GLA_W + GLA_LOWRANK], aup_ref[...], na=2, nb=2) + ab_ref[...]
    la = _split_heads(-_softplus(-z) * (1.0 / GLA_TAU), nc)
    yield
    row = _iota2((1, CHUNK, CHUNK), 1)
    col = _iota2((1, CHUNK, CHUNK), 2)
    tri = (row >= col)
    every = lambda mask: jnp.broadcast_to(mask.astype(BF16), shape3)
    q = _split_heads(hb[:, 0:GLA_W], nc) * (HEAD_DIM ** -0.5)
    k = _split_heads(hb[:, GLA_W:2 * GLA_W], nc)
    v = _split_heads(hb[:, 2 * GLA_W:3 * GLA_W], nc)
    cum = _mm(every(tri), la, _BNN, nb=3)
    k_anchor = _mm(every(col <= (row // SUB) * SUB + (SUB - 1)), la, _BNN, nb=3)
    k_t = k * jnp.exp(k_anchor - cum)
    yield
    scores = jnp.zeros(shape3, F32)
    for j in range(CHUNK // SUB - 1):
        a_j = cum[:, j * SUB + SUB - 1:j * SUB + SUB, :]
        q_j = q * jnp.exp(jnp.where(row >= (j + 1) * SUB, cum - a_j, NEG_INF))
        in_grp = (row >= j * SUB) & (row < (j + 1) * SUB)
        scores = scores + _mm(q_j, jnp.where(in_grp, k_t, 0.0), _BNT)
        yield
    half = SUB // 2
    while half >= 1:
        pair_end = (row // (2 * half)) * (2 * half) + (half - 1)
        anchor = _mm(every(col <= pair_end), la, _BNN)
        upper = (row // half) % 2 == 1
        decay = jnp.exp(jnp.where(upper, cum - anchor, anchor - cum))
        q_h = jnp.where(upper, q * decay, 0.0)
        k_h = jnp.where(upper, 0.0, k * decay)
        same_pair = row // (2 * half) == col // (2 * half)
        scores = scores + jnp.where(same_pair, _mm(q_h, k_h, _BNT), 0.0)
        half //= 2
        yield
    diag = _mm(q * k, every(row >= 0), _BNN)
    scores = scores + jnp.where(row == col, diag, 0.0)
    o_intra = _mm(jnp.where(tri, scores, 0.0), v, _BNN)
    yield
    last = cum[:, CHUNK - 1:CHUNK, :]
    n_mat = _mm(v, k * jnp.exp(last - cum), _BTN)
    q_dec = q * jnp.exp(cum)
    dec_last = jnp.exp(last)
    yield
    st = st_ref[...]
    o_inter = []
    for c in range(nc):
        o_inter.append(_mm(_chunk_of(q_dec, nh, c), st, _BNT))
        st = st * _chunk_of(dec_last, nh, c) + _chunk_of(n_mat, nh, c)
    st_ref[...] = st
    o = o_intra + jnp.stack(o_inter, axis=1).reshape(shape3)
    o = o * lax.rsqrt(jnp.mean(o * o, axis=-1, keepdims=True) + LN_EPS) * nw_ref[...]
    g = hb[:, 3 * GLA_W:4 * GLA_W]
    yield
    o_ref[0] = _merge_heads(o, nh) * (g * _sigmoid(g))


def _block_penalty(q_t, kbar, own):
    gate = _mm(kbar, q_t, na=2, nb=2)
    blk = _iota2(gate.shape, 0)
    nb = gate.shape[0]
    gate = jnp.where(blk < own, gate, NEG_INF)
    allowed = blk == own
    for _ in range(min(MOBA_TOPK, nb)):
        best = jnp.max(gate, axis=0, keepdims=True)
        first = jnp.min(jnp.where(gate == best, blk, nb), axis=0, keepdims=True)
        hit = blk == first
        allowed = allowed | (hit & (blk < own))
        gate = jnp.where(hit, NEG_INF, gate)
    return jnp.where(allowed, 0.0, -MASK_BIG).astype(BF16)


def _moba_attn_kernel(qt_ref, kbar_ref, k_ref, vt_ref, gt_ref, bias_ref, o_ref, *scratch):
    hps = MOBA_HEADS_PER_STEP
    own = pl.program_id(2)
    nb = kbar_ref.shape[1]
    n_groups = (own + KV_GROUP) // KV_GROUP
    head_rows = lambda hh: slice(hh * HEAD_DIM, (hh + 1) * HEAD_DIM)
    q = [(qt_ref[0, head_rows(hh), :] * (HEAD_DIM ** -0.5)).astype(BF16) for hh in range(hps)]
    pen = [_block_penalty(qt_ref[0, head_rows(hh), :], kbar_ref[0][:, head_rows(hh)], own) for hh in range(hps)]
    blk = _iota2(pen[0].shape, 0)
    sa, sb, ma, mb = (scratch[i * hps:(i + 1) * hps] for i in range(4))

    def first_block(u):
        lo = own + 1 - (u + 1) * KV_GROUP
        return jnp.clip(lo, 0, nb - KV_GROUP), lo

    tile_rows = lambda i: slice(i * MOBA_BLOCK, (i + 1) * MOBA_BLOCK)

    def score_operand(u, hh):
        j0, lo = first_block(u)
        in_range = (blk >= lo) & (blk < lo + KV_GROUP)
        return j0, jnp.concatenate([q[hh], jnp.where(in_range, pen[hh], jnp.asarray(-MASK_BIG, BF16))], axis=0)

    def score_tile(hh, j, q_aug):
        rows = pl.ds(pl.multiple_of(j * MOBA_BLOCK, MOBA_BLOCK), MOBA_BLOCK)
        tile = jnp.clip(own - j, 0, N_BIAS_TILES)
        return _mm(k_ref[0, hh, rows, :], q_aug).astype(BF16) + bias_ref[hh, tile]

    def scores(u, hh, dst, dst_max):
        j0, q_aug = score_operand(u, hh)
        top = None
        for i in range(KV_GROUP):
            s = score_tile(hh, j0 + i, q_aug)
            dst[tile_rows(i), :] = s
            tile_top = jnp.max(s, axis=0, keepdims=True)
            top = tile_top if top is None else jnp.maximum(top, tile_top)
        dst_max[...] = top.astype(F32)

    def consume(u, hh, src, src_max, carry):
        m, acc = carry
        j0, _ = first_block(u)
        ref_b = jnp.maximum(m, src_max[...]).astype(BF16)
        m_new = ref_b.astype(F32)
        acc = acc * jnp.exp(m - m_new)
        for i in range(KV_GROUP):
            acc = acc + _mm(vt_ref[0, hh, j0 + i], jnp.exp(src[tile_rows(i), :] - ref_b))
        return m_new, acc

    def staggered(u, src, src_max, dst, dst_max, carry):
        carry = list(carry)
        for hh in range(hps):
            scores(u + 1, hh, dst[hh], dst_max[hh])
            carry[hh] = consume(u, hh, src[hh], src_max[hh], carry[hh])
        return tuple(carry)

    carry = tuple((jnp.full((1, MOBA_BLOCK), -MASK_BIG, F32), jnp.zeros((vt_ref.shape[3], MOBA_BLOCK), F32))
                  for _ in range(hps))
    for hh in range(hps):
        scores(0, hh, sa[hh], ma[hh])

    def pair(v, carry):
        u = 2 * v
        carry = staggered(u, sa, ma, sb, mb, carry)
        return staggered(u + 1, sb, mb, sa, ma, carry)

    n_pairs = (n_groups - 1) // 2
    carry = lax.fori_loop(0, n_pairs, pair, carry)
    u_last = 2 * n_pairs

    def last_two(carry):
        carry = staggered(u_last, sa, ma, sb, mb, carry)
        return tuple(consume(u_last + 1, hh, sb[hh], mb[hh], carry[hh]) for hh in range(hps))

    def last_one(carry):
        return tuple(consume(u_last, hh, sa[hh], ma[hh], carry[hh]) for hh in range(hps))

    carry = lax.cond(n_groups - u_last == 2, last_two, last_one, carry)
    for hh in range(hps):
        _, acc = carry[hh]
        g = gt_ref[0, head_rows(hh), :]
        o_ref[0, head_rows(hh), :] = acc[:HEAD_DIM] / acc[HEAD_DIM:HEAD_DIM + 1] * (g * _sigmoid(g))


def _moba_attn(ht, kbar, k_aug, vt_aug, bias):
    b, _, t = ht.shape
    h, nb = k_aug.shape[1], kbar.shape[1]
    hps = MOBA_HEADS_PER_STEP
    rows = hps * HEAD_DIM
    qs = pl.BlockSpec((1, rows, MOBA_BLOCK), lambda i, j, s: (i, j, s))
    gs = pl.BlockSpec((1, rows, MOBA_BLOCK), lambda i, j, s: (i, h // hps + j, s))
    once = pl.Buffered(1)
    heads = lambda a: pl.BlockSpec((1, hps) + a.shape[2:], lambda i, j, s: (i, j) + (0,) * (a.ndim - 2),
                                   pipeline_mode=once)
    return pl.pallas_call(
        _moba_attn_kernel,
        grid=(b, h // hps, nb),
        in_specs=[qs, pl.BlockSpec((1, nb, rows), lambda i, j, s: (i, 0, j)), heads(k_aug), heads(vt_aug), gs,
                  pl.BlockSpec((hps,) + bias.shape[1:], lambda i, j, s: (j, 0, 0, 0), pipeline_mode=once)],
        out_specs=qs,
        out_shape=jax.ShapeDtypeStruct((b, h * HEAD_DIM, t), F32),
        scratch_shapes=[pltpu.VMEM((KV_GROUP * MOBA_BLOCK, MOBA_BLOCK), BF16)] * (2 * hps)
        + [pltpu.VMEM((1, MOBA_BLOCK), F32)] * (2 * hps),
        compiler_params=pltpu.CompilerParams(
            dimension_semantics=("parallel", "parallel", "arbitrary"), vmem_limit_bytes=VMEM_LIMIT),
    )(ht, kbar, k_aug, vt_aug, ht, bias)


def _t5_bucket(rel):
    rel = jnp.maximum(rel, 0)
    max_exact = N_BUCKETS // 2
    rel_f = jnp.maximum(rel, 1).astype(F32)
    large = max_exact + (jnp.log(rel_f / max_exact) / math.log(MAX_DISTANCE / max_exact)
                         * (N_BUCKETS - max_exact)).astype(jnp.int32)
    large = jnp.minimum(large, N_BUCKETS - 1)
    return jnp.where(rel < max_exact, rel, large)


def _bias_tiles(rel_bias):
    d = jnp.arange(N_BIAS_TILES)[:, None, None] * MOBA_BLOCK
    rel = d + jnp.arange(MOBA_BLOCK)[None, None, :] - jnp.arange(MOBA_BLOCK)[None, :, None]
    bucket = _t5_bucket(rel)
    tab = rel_bias.T.astype(F32)
    out = jnp.zeros((tab.shape[0],) + bucket.shape, F32)
    for bk in range(N_BUCKETS):
        out = jnp.where(bucket[None] == bk, tab[:, bk][:, None, None, None], out)
    out = jnp.where(rel[None] < 0, -MASK_BIG, out)
    far = jnp.broadcast_to(tab[:, N_BUCKETS - 1][:, None, None, None], out[:, :1].shape)
    return jnp.concatenate([out, far], axis=1).astype(BF16)


def _rwkv_steps(h_ref, mu_ref, wup_ref, w0_ref, aup_ref, a0_ref, kk_ref, ka_ref, rk_ref, gw_ref, gb_ref,
                o_ref, st_ref, prev_ref):
    @pl.when(pl.program_id(1) == 0)
    def _():
        st_ref[...] = jnp.zeros_like(st_ref)
        prev_ref[...] = jnp.zeros_like(prev_ref)

    nc = RWKV_TIME_BLOCK // CHUNK
    nh = RWKV_HEADS
    w = RWKV_W
    p = h_ref[0]
    shifted = pltpu.roll(p, 1, axis=0)
    p_prev = jnp.where(_iota2(p.shape, 0) == 0, prev_ref[...], shifted)
    prev_ref[...] = p[RWKV_TIME_BLOCK - 1:RWKV_TIME_BLOCK, :]
    p = p + (p_prev - p) * mu_ref[...]
    r, k, v, g = p[:, 0:w], p[:, w:2 * w], p[:, 2 * w:3 * w], p[:, 3 * w:4 * w]
    w_dn = p[:, 4 * w:4 * w + RWKV_LORA]
    a_dn = p[:, 4 * w + RWKV_LORA:4 * w + 2 * RWKV_LORA]
    d = w0_ref[...] + _mm(jnp.tanh(w_dn), wup_ref[...], na=2, nb=2)
    lw = -jnp.exp(-_softplus(-d) - 0.5)
    a = _sigmoid(a0_ref[...] + _mm(a_dn, aup_ref[...], na=2, nb=2))
    same_head = (_iota2((w, w), 0) // HEAD_DIM == _iota2((w, w), 1) // HEAD_DIM).astype(BF16)
    kk = k * kk_ref[...]
    kk = kk / jnp.maximum(jnp.sqrt(_mm(kk * kk, same_head, na=3)), 1e-12)
    k = k * (1.0 + (a - 1.0) * ka_ref[...])
    bonus = _mm(r * k * rk_ref[...], same_head, na=3) * v
    yield

    hg = RWKV_HEAD_GROUP

    def scan_group(first, r, lw, k, v, kk, a):
        shape3 = (hg * nc, CHUNK, CHUNK)
        row = _iota2(shape3, 1)
        col = _iota2(shape3, 2)
        tri_b = (row >= col).astype(BF16)
        eye = row == col
        hp = dict()
        carry_prec = dict(na=2, nb=2)
        r, lw, k, v = (_split_heads(x, nc) for x in (r, lw, k, v))
        a_vec = _split_heads(-kk, nc)
        b_vec = _split_heads(kk * a, nc)
        cum = _mm(tri_b, lw, _BNN, nb=3)
        last = cum[:, CHUNK - 1:CHUNK, :]
        inv = jnp.exp(-cum)
        to_end = jnp.exp(last - cum)
        r_t = r * jnp.exp(cum)
        a_t = a_vec * jnp.exp(cum - lw)
        k_t = k * inv
        b_t = b_vec * inv
        yield
        gram = _mm(jnp.concatenate([a_t, r_t], axis=1), jnp.concatenate([b_t, k_t], axis=1), _BNT, **hp)
        a_ab = jnp.where(row > col, gram[:, :CHUNK, :CHUNK], 0.0)
        a_ak = jnp.where(row > col, gram[:, :CHUNK, CHUNK:], 0.0)
        a_rb = jnp.where(row >= col, gram[:, CHUNK:, :CHUNK], 0.0)
        a_rk = jnp.where(row >= col, gram[:, CHUNK:, CHUNK:], 0.0)
        yield
        same_block = lambda size: row // size == col // size
        power = jnp.where(same_block(INV_BASE), a_ab, 0.0)
        inv_t = jnp.where(eye, 1.0, power)
        for _ in range(int(math.log2(INV_BASE)) - 1):
            power = _mm(power, power, _BNN, **hp)
            inv_t = inv_t + _mm(inv_t, power, _BNN, **hp)
            yield
        size = INV_BASE
        while size < CHUNK:
            coupling = jnp.where(same_block(2 * size) & ~same_block(size), a_ab, 0.0)
            inv_t = inv_t + _mm(inv_t, _mm(coupling, inv_t, _BNN, **hp), _BNN, **hp)
            size *= 2
            yield
        w_mat = _mm(inv_t, a_t, _BNN, **hp)
        u0 = _mm(inv_t, _mm(a_ak, v, _BNN, **hp), _BNN, **hp)
        yield
        y_w = r_t + _mm(a_rb, w_mat, _BNN, **hp)
        y0 = _mm(a_rb, u0, _BNN, **hp) + _mm(a_rk, v, _BNN, **hp)
        yield
        b_end = b_vec * to_end
        k_end = k * to_end
        m_mat = jnp.where(eye, jnp.exp(last), 0.0) + _mm(b_end, w_mat, _BTN, **hp)
        n_mat = _mm(b_end, u0, _BTN, **hp) + _mm(k_end, v, _BTN, **hp)
        yield
        st = st_ref[first:first + hg]
        ys = []
        for c in range(nc):
            ys.append(_mm(_chunk_of(y_w, hg, c), st, _BNN, **hp))
            st = _mm(_chunk_of(m_mat, hg, c), st, _BNN, **carry_prec) + _chunk_of(n_mat, hg, c)
        st_ref[first:first + hg] = st
        y = y0 + jnp.stack(ys, axis=1).reshape(shape3)
        mean = jnp.mean(y, axis=-1, keepdims=True)
        var = jnp.mean(jnp.square(y - mean), axis=-1, keepdims=True)
        return _merge_heads((y - mean) * lax.rsqrt(var + RWKV_GN_EPS), hg)

    cols = lambda x, f: x[:, f * HEAD_DIM:(f + hg) * HEAD_DIM]
    parts = []
    for f in range(0, nh, hg):
        parts.append((yield from scan_group(f, *(cols(x, f) for x in (r, lw, k, v, kk, a)))))
    y = jnp.concatenate(parts, axis=1) * gw_ref[...] + gb_ref[...]
    o_ref[0] = (y + bonus) * (g * _sigmoid(g))


def _gla_rwkv_kernel(hg_ref, gaup_ref, gab_ref, gnw_ref, hr_ref, *rest):
    rwkv_params, (og_ref, or_ref, gst_ref, rst_ref, prev_ref) = rest[:-5], rest[-5:]
    gla = _gla_steps(hg_ref, gaup_ref, gab_ref, gnw_ref, og_ref, gst_ref)
    rwkv = _rwkv_steps(hr_ref, *rwkv_params, or_ref, rst_ref, prev_ref)
    for _ in itertools.zip_longest(gla, rwkv):
        pass


def _gla_rwkv(h_gla, gla_params, h_rwkv, rwkv_params):
    b, t, cg = h_gla.shape
    cr = h_rwkv.shape[2]
    tb = GLA_TIME_BLOCK
    full = lambda a: pl.BlockSpec(a.shape, lambda i, s: (0, 0))
    rows = lambda n: pl.BlockSpec((1, tb, n), lambda i, s: (i, s, 0))
    return pl.pallas_call(
        _gla_rwkv_kernel,
        grid=(b, t // tb),
        in_specs=[rows(cg)] + [full(a) for a in gla_params] + [rows(cr)] + [full(a) for a in rwkv_params],
        out_specs=[rows(GLA_W), rows(RWKV_W)],
        out_shape=[jax.ShapeDtypeStruct((b, t, GLA_W), F32), jax.ShapeDtypeStruct((b, t, RWKV_W), F32)],
        scratch_shapes=[pltpu.VMEM((GLA_HEADS, HEAD_DIM, HEAD_DIM), F32),
                        pltpu.VMEM((RWKV_HEADS, HEAD_DIM, HEAD_DIM), F32), pltpu.VMEM((1, cr), F32)],
        compiler_params=pltpu.CompilerParams(
            dimension_semantics=("parallel", "arbitrary"), vmem_limit_bytes=VMEM_LIMIT),
    )(h_gla, *gla_params, h_rwkv, *rwkv_params)


def _proj_out_kernel(alpha, og_ref, ot_ref, or_ref, w_ref, x_ref, lw_ref, lb_ref, y_ref):
    y = _mm(og_ref[0].astype(BF16), w_ref[0:GLA_W, :])
    y = y + _mm(ot_ref[0].astype(BF16), w_ref[GLA_W:GLA_W + MOBA_W, :], _TN)
    y = y + _mm(or_ref[0].astype(BF16), w_ref[GLA_W + MOBA_W:, :])
    z = alpha * x_ref[0] + y
    mu = jnp.mean(z, axis=-1, keepdims=True)
    var = jnp.mean(jnp.square(z - mu), axis=-1, keepdims=True)
    y_ref[0] = (z - mu) * lax.rsqrt(var + LN_EPS) * lw_ref[...] + lb_ref[...]


def _proj_out(o_gla, ot_moba, o_rwkv, w, x, ln_w, ln_b, alpha):
    b, t, d = x.shape
    rs = lambda n: pl.BlockSpec((1, OUT_ROW_BLOCK, n), lambda i, s: (i, s, 0))
    vs = pl.BlockSpec((1, d), lambda i, s: (0, 0))
    return pl.pallas_call(
        functools.partial(_proj_out_kernel, alpha),
        grid=(b, t // OUT_ROW_BLOCK),
        in_specs=[rs(GLA_W), pl.BlockSpec((1, MOBA_W, OUT_ROW_BLOCK), lambda i, s: (i, 0, s)), rs(RWKV_W),
                  pl.BlockSpec(w.shape, lambda i, s: (0, 0)), rs(d), vs, vs],
        out_specs=rs(d),
        out_shape=jax.ShapeDtypeStruct((b, t, d), F32),
        compiler_params=pltpu.CompilerParams(
            dimension_semantics=("parallel", "parallel"), vmem_limit_bytes=VMEM_LIMIT),
    )(o_gla, ot_moba, o_rwkv, w, x, ln_w, ln_b)


def _pad_cols(a, n):
    return jnp.pad(a, ((0, 0), (0, n - a.shape[1])))


def _split_w_in(w_l):
    gla, moba = w_l[:, :GLA_COLS], w_l[:, GLA_COLS:GLA_COLS + MOBA_COLS]
    rwkv = w_l[:, GLA_COLS + MOBA_COLS:]
    mq, mk, mv, mg = (moba[:, i * MOBA_W:(i + 1) * MOBA_W] for i in range(4))
    w_t = jnp.concatenate([mq, mg], axis=1).T
    return tuple(a.astype(BF16) for a in (_pad_cols(gla, GLA_PAD), _pad_cols(rwkv, RWKV_PAD), mk, w_t, mv.T))


def _moba_branch(ht, k_aug, kbar, vt_aug, bias):
    return _moba_attn(ht, kbar.reshape(kbar.shape[0], -1, MOBA_W), k_aug, vt_aug, bias)


def _gla_rwkv_branch(h_gla, a_up, a_bias, norm_w, h_rwkv, mu, w_up, w0, r_a_up, a0, k_k, k_a, r_k, gn_w, gn_b):
    row = lambda p: p.reshape(1, -1)
    gla_params = (a_up, a_bias.reshape(1, GLA_W), norm_w.reshape(1, HEAD_DIM))
    rwkv_params = (_pad_cols(row(mu), h_rwkv.shape[2]), w_up, row(w0), r_a_up, row(a0), row(k_k), row(k_a), row(r_k),
                   row(gn_w), row(gn_b))
    return _gla_rwkv(h_gla, gla_params, h_rwkv, rwkv_params)


def kernel(x, w_in, w_out, gla_a_up, gla_a_bias, gla_norm_w, moba_rel_bias, rwkv_mu, rwkv_w_up, rwkv_w0,
           rwkv_a_up, rwkv_a0, rwkv_k_k, rwkv_k_a, rwkv_r_k, rwkv_gn_w, rwkv_gn_b, ln_w, ln_b):
    bsz, t, d = x.shape
    depth = w_in.shape[0]
    alpha = (2.0 * depth) ** 0.25
    bias = _bias_tiles(moba_rel_bias)
    for l in range(depth):
        h_gla, h_rwkv, ht, k_aug, kbar, vt_aug = _proj_in(x, *_split_w_in(w_in[l]))
        ot_moba = _moba_branch(ht, k_aug, kbar, vt_aug, bias)
        o_gla, o_rwkv = _gla_rwkv_branch(
            h_gla, gla_a_up[l], gla_a_bias[l], gla_norm_w[l], h_rwkv, rwkv_mu[l], rwkv_w_up[l], rwkv_w0[l],
            rwkv_a_up[l], rwkv_a0[l], rwkv_k_k[l], rwkv_k_a[l], rwkv_r_k[l], rwkv_gn_w[l], rwkv_gn_b[l])
        x = _proj_out(o_gla, ot_moba, o_rwkv, w_out[l].astype(BF16), x,
                      ln_w[l].reshape(1, d), ln_b[l].reshape(1, d), alpha)
    return x
```
